```python
import math
import jax, jax.numpy as jnp
from jax import lax
import numpy as np

D_MODEL = 1024
BATCH = 4
SEQ = 8192
DEPTH = 1

GRID_W = 64
DA_HEADS = 4
DA_HEAD_DIM = 64
DA_V_DIM = 2 * DA_HEAD_DIM
DA_WIDTH = DA_HEADS * DA_V_DIM
Q_BLOCK = 128
ROPE_THETA = 10000.0
NA_HEADS = 8
NA_HEAD_DIM = 64
NA_WIDTH = NA_HEADS * NA_HEAD_DIM
NA_KH_MAX = 8
NA_KW = 16
N_BRANCHES = 2
IN_COLS = 3 * DA_WIDTH + 3 * NA_WIDTH + N_BRANCHES * D_MODEL
IN_SPLITS = (DA_WIDTH, 2 * DA_WIDTH, 3 * DA_WIDTH,
             3 * DA_WIDTH + NA_WIDTH, 3 * DA_WIDTH + 2 * NA_WIDTH,
             3 * DA_WIDTH + 3 * NA_WIDTH)
N_EXPERTS = 32
TOP_K = 4
D_EXPERT = 1024
SWIGLU_ALPHA = 1.702
SWIGLU_LIMIT = 7.0
MOE_BLOCK = 256
LN_EPS = 1e-5
RMS_EPS = 1e-5
DEEPNORM_ALPHA = (2 * DEPTH) ** 0.25
DEEPNORM_BETA = (8 * DEPTH) ** -0.25

kernel_name = "hybrid_diffattn_natten_moe_deepnorm"


def layer_norm(x, g, b):
    xf = x.astype(jnp.float32)
    mu = jnp.mean(xf, axis=-1, keepdims=True)
    var = jnp.mean(jnp.square(xf - mu), axis=-1, keepdims=True)
    return ((xf - mu) * lax.rsqrt(var + LN_EPS) * g + b).astype(x.dtype)


def rope_tables(seq, dim):
    pos = jnp.arange(seq, dtype=jnp.float32)
    inv = ROPE_THETA ** (-jnp.arange(0, dim, 2, dtype=jnp.float32) / dim)
    ang = pos[:, None] * inv[None, :]
    ang = jnp.concatenate([ang, ang], axis=-1)
    return jnp.cos(ang), jnp.sin(ang)


def apply_rope(x, cos, sin):
    c = cos[None, :, None, None, :]
    s = sin[None, :, None, None, :]
    x1, x2 = jnp.split(x, 2, axis=-1)
    rot = jnp.concatenate([-x2, x1], axis=-1)
    return (x.astype(jnp.float32) * c + rot.astype(jnp.float32) * s).astype(x.dtype)


def diff_attention(q, k, v, lam, lam_init, subln_g):
    B, S, H, _, d = q.shape
    nb = S // Q_BLOCK
    qb = (q * d ** -0.5).reshape(B, nb, Q_BLOCK, H, 2, d).transpose(1, 0, 2, 3, 4, 5)

    def block(q_blk):
        s = jnp.einsum('bqhcd,bkhcd->bhcqk', q_blk, k).astype(jnp.float32)
        p = jax.nn.softmax(s, axis=-1)
        w = p[:, :, 0] - lam * p[:, :, 1]
        return jnp.einsum('bhqk,bkhe->bqhe', w.astype(v.dtype), v)

    o = lax.map(block, qb)
    o = o.transpose(1, 0, 2, 3, 4).reshape(B, S, H, 2 * d).astype(jnp.float32)
    o = o * lax.rsqrt(jnp.mean(jnp.square(o), axis=-1, keepdims=True) + RMS_EPS) * subln_g
    return (o * (1.0 - lam_init)).reshape(B, S, H * 2 * d).astype(q.dtype)


def neighbourhood_attention(q, k, v, rpb):
    B, S, H, hd = q.shape
    rows = S // GRID_W
    kh = min(NA_KH_MAX, rows)
    q = (q * hd ** -0.5).reshape(B, rows, GRID_W, H, hd)
    k = k.reshape(B, rows, GRID_W, H, hd)
    v = v.reshape(B, rows, GRID_W, H, hd)
    cols = jnp.arange(GRID_W)
    col_start = jnp.clip(cols - NA_KW // 2, 0, GRID_W - NA_KW)
    col_idx = col_start[:, None] + jnp.arange(NA_KW)[None, :]
    col_off = col_idx - cols[:, None] + (NA_KW - 1)

    def row(r):
        rs = jnp.clip(r - kh // 2, 0, rows - kh)
        q_r = lax.dynamic_index_in_dim(q, r, axis=1, keepdims=False)
        k_rows = lax.dynamic_slice_in_dim(k, rs, kh, axis=1)
        v_rows = lax.dynamic_slice_in_dim(v, rs, kh, axis=1)
        k_win = k_rows[:, :, col_idx]
        v_win = v_rows[:, :, col_idx]
        row_off = rs + jnp.arange(kh) - r + (NA_KH_MAX - 1)
        bias = rpb[:, row_off[:, None, None], col_off[None, :, :]]
        s = jnp.einsum('bchd,bicjhd->bhcij', q_r, k_win).astype(jnp.float32)
        s = s + bias.transpose(0, 2, 1, 3)[None].astype(jnp.float32)
        p = jax.nn.softmax(s.reshape(B, H, GRID_W, kh * NA_KW), axis=-1).reshape(s.shape)
        return jnp.einsum('bhcij,bicjhd->bchd', p.astype(v.dtype), v_win)

    o = lax.map(row, jnp.arange(rows))
    return o.transpose(1, 0, 2, 3, 4).reshape(B, S, H * hd)


def hybrid_mixer(x, cos, sin, lam_init, w_in, b_in, lambda_q1, lambda_k1, lambda_q2,
                 lambda_k2, subln_g, rpb, w_branch_da, w_branch_na, w_out):
    B, S, D = x.shape
    proj = jnp.einsum('bsd,de->bse', x, w_in) + b_in
    q_da, k_da, v_da, q_na, k_na, v_na, gate_pre = jnp.split(proj, IN_SPLITS, axis=-1)
    q_da = apply_rope(q_da.reshape(B, S, DA_HEADS, 2, DA_HEAD_DIM), cos, sin)
    k_da = apply_rope(k_da.reshape(B, S, DA_HEADS, 2, DA_HEAD_DIM), cos, sin)
    v_da = v_da.reshape(B, S, DA_HEADS, DA_V_DIM)
    f32 = jnp.float32
    lam = (jnp.exp(jnp.sum(lambda_q1.astype(f32) * lambda_k1.astype(f32)))
           - jnp.exp(jnp.sum(lambda_q2.astype(f32) * lambda_k2.astype(f32))) + lam_init)
    a = diff_attention(q_da, k_da, v_da, lam, lam_init, subln_g)
    nb = neighbourhood_attention(q_na.reshape(B, S, NA_HEADS, NA_HEAD_DIM),
                                 k_na.reshape(B, S, NA_HEADS, NA_HEAD_DIM),
                                 v_na.reshape(B, S, NA_HEADS, NA_HEAD_DIM), rpb)
    ya = jnp.einsum('bse,ed->bsd', a, w_branch_da)
    yb = jnp.einsum('bse,ed->bsd', nb, w_branch_na)
    g = jax.nn.sigmoid(gate_pre.astype(f32)).reshape(B, S, N_BRANCHES, D)
    merged = (g[:, :, 0] * ya + g[:, :, 1] * yb).astype(x.dtype)
    return jnp.einsum('bsd,de->bse', merged, w_out)


def moe(x2, w_router, b_router, w_mlp1, b_mlp1, w_mlp2, b_mlp2):
    T, D = x2.shape
    TK = T * TOP_K
    logits = (x2 @ w_router + b_router).astype(jnp.float32)
    top_v, top_i = lax.top_k(logits, TOP_K)
    gates = jax.nn.softmax(top_v, axis=-1)
    flat_e = top_i.reshape(-1)
    flat_tok = jnp.arange(TK, dtype=jnp.int32) // TOP_K
    order = jnp.argsort(flat_e, stable=True)
    sorted_e = flat_e[order]
    counts = jnp.bincount(flat_e, length=N_EXPERTS)
    padded = ((counts + MOE_BLOCK - 1) // MOE_BLOCK) * MOE_BLOCK
    padded_end = jnp.cumsum(padded)
    padded_start = padded_end - padded
    group_start = jnp.cumsum(counts) - counts
    rank = jnp.arange(TK, dtype=jnp.int32) - group_start[sorted_e]
    dest = (padded_start[sorted_e] + rank).astype(jnp.int32)
    n_blocks = -(-TK // MOE_BLOCK) + N_EXPERTS
    P = n_blocks * MOE_BLOCK
    buf_tok = jnp.full((P,), T, dtype=jnp.int32).at[dest].set(flat_tok[order])
    x_pad = jnp.concatenate([x2, jnp.zeros((1, D), x2.dtype)], axis=0)
    xs = jnp.take(x_pad, buf_tok, axis=0).reshape(n_blocks, MOE_BLOCK, D)
    block_expert = jnp.clip(jnp.searchsorted(padded_end, jnp.arange(n_blocks) * MOE_BLOCK,
                                             side='right'), 0, N_EXPERTS - 1)

    def expert_block(args):
        xb, e = args
        h = xb @ w_mlp1[e] + b_mlp1[e]
        x_glu = jnp.minimum(h[:, ::2], SWIGLU_LIMIT)
        x_lin = jnp.clip(h[:, 1::2], -SWIGLU_LIMIT, SWIGLU_LIMIT)
        act = x_glu * jax.nn.sigmoid(SWIGLU_ALPHA * x_glu) * (x_lin + 1.0)
        return act @ w_mlp2[e] + b_mlp2[e]

    ys = lax.map(expert_block, (xs, block_expert)).reshape(P, D)
    dest_orig = jnp.zeros((TK,), jnp.int32).at[order].set(dest)
    y_slots = jnp.take(ys, dest_orig, axis=0).reshape(T, TOP_K, D)
    return jnp.sum(y_slots * gates[..., None].astype(ys.dtype), axis=1)


def setup_inputs(seed: int = 0) -> dict:
    key = jax.random.key(seed)
    ks = jax.random.split(key, 24)
    L = DEPTH
    f32 = jnp.float32

    def nrm(k, shape, s):
        return jax.random.normal(k, shape, f32) * s

    col_scale = jnp.concatenate([
        jnp.ones((2 * DA_WIDTH,), f32), jnp.full((DA_WIDTH,), DEEPNORM_BETA, f32),
        jnp.ones((2 * NA_WIDTH,), f32), jnp.full((NA_WIDTH,), DEEPNORM_BETA, f32),
        jnp.ones((N_BRANCHES * D_MODEL,), f32)])
    return {
        "x": nrm(ks[0], (BATCH, SEQ, D_MODEL), 1.0),
        "w_in": nrm(ks[1], (L, D_MODEL, IN_COLS), D_MODEL ** -0.5) * col_scale,
        "b_in": nrm(ks[2], (L, IN_COLS), 0.02),
        "lambda_q1": nrm(ks[3], (L, DA_HEAD_DIM), 0.1),
        "lambda_k1": nrm(ks[4], (L, DA_HEAD_DIM), 0.1),
        "lambda_q2": nrm(ks[5], (L, DA_HEAD_DIM), 0.1),
        "lambda_k2": nrm(ks[6], (L, DA_HEAD_DIM), 0.1),
        "subln_g": 1.0 + nrm(ks[7], (L, DA_V_DIM), 0.02),
        "rpb": nrm(ks[8], (L, NA_HEADS, 2 * NA_KH_MAX - 1, 2 * NA_KW - 1), 0.02),
        "w_branch_da": nrm(ks[9], (L, DA_WIDTH, D_MODEL), DA_WIDTH ** -0.5),
        "w_branch_na": nrm(ks[10], (L, NA_WIDTH, D_MODEL), NA_WIDTH ** -0.5),
        "w_out": nrm(ks[11], (L, D_MODEL, D_MODEL), D_MODEL ** -0.5) * DEEPNORM_BETA,
        "ln1_g": 1.0 + nrm(ks[12], (L, D_MODEL), 0.02),
        "ln1_b": nrm(ks[13], (L, D_MODEL), 0.02),
        "w_router": nrm(ks[14], (L, D_MODEL, N_EXPERTS), D_MODEL ** -0.5),
        "b_router": nrm(ks[15], (L, N_EXPERTS), 0.01),
        "w_mlp1": nrm(ks[16], (L, N_EXPERTS, D_MODEL, 2 * D_EXPERT), D_MODEL ** -0.5),
        "b_mlp1": nrm(ks[17], (L, N_EXPERTS, 2 * D_EXPERT), 0.02),
        "w_mlp2": nrm(ks[18], (L, N_EXPERTS, D_EXPERT, D_MODEL), D_EXPERT ** -0.5) * DEEPNORM_BETA,
        "b_mlp2": nrm(ks[19], (L, N_EXPERTS, D_MODEL), 0.02),
        "ln2_g": 1.0 + nrm(ks[20], (L, D_MODEL), 0.02),
        "ln2_b": nrm(ks[21], (L, D_MODEL), 0.02),
    }


def reference(x, w_in, b_in, lambda_q1, lambda_k1, lambda_q2, lambda_k2, subln_g, rpb,
              w_branch_da, w_branch_na, w_out, ln1_g, ln1_b, w_router, b_router,
              w_mlp1, b_mlp1, w_mlp2, b_mlp2, ln2_g, ln2_b):
    B, S, D = x.shape
    cos, sin = rope_tables(S, DA_HEAD_DIM)
    for l in range(DEPTH):
        lam_init = 0.8 - 0.6 * math.exp(-0.3 * l)
        mix = hybrid_mixer(x, cos, sin, lam_init, w_in[l], b_in[l], lambda_q1[l], lambda_k1[l],
                           lambda_q2[l], lambda_k2[l], subln_g[l], rpb[l],
                           w_branch_da[l], w_branch_na[l], w_out[l])
        x = layer_norm(DEEPNORM_ALPHA * x + mix, ln1_g[l], ln1_b[l])
        ffn = moe(x.reshape(B * S, D), w_router[l], b_router[l], w_mlp1[l], b_mlp1[l],
                  w_mlp2[l], b_mlp2[l]).reshape(B, S, D)
        x = layer_norm(DEEPNORM_ALPHA * x + ffn, ln2_g[l], ln2_b[l])
    return x
```

```python
import functools
import math

import jax
import jax.numpy as jnp
from jax import lax
from jax.experimental import pallas as pl
from jax.experimental.pallas import tpu as pltpu

F32 = jnp.float32
BF16 = jnp.bfloat16

GRID_W = 64
DA_HEADS = 4
DA_HEAD_DIM = 64
DA_V_DIM = 2 * DA_HEAD_DIM
DA_WIDTH = DA_HEADS * DA_V_DIM
ROPE_THETA = 10000.0
NA_HEADS = 8
NA_HEAD_DIM = 64
NA_WIDTH = NA_HEADS * NA_HEAD_DIM
NA_KH = 8
NA_KW = 16
TOP_K = 4
SWIGLU_ALPHA = 1.702
SWIGLU_LIMIT = 7.0
MOE_BLOCK = 256
LN_EPS = 1e-5
RMS_EPS = 1e-5

LANES = 128
VMEM_LIMIT_BYTES = 56 * 1024 * 1024

LOG2E = math.log2(math.e)
NEG_BIG = -1e30

PROJ_TM = 512
DA_TQ = 256
DA_TK = 512
NA_ROWS = 8
NA_BAND = 16


def _cparams(sem):
    return pltpu.CompilerParams(dimension_semantics=sem, vmem_limit_bytes=VMEM_LIMIT_BYTES)


def _in_proj_kernel(x_ref, w_ref, b_ref, cos_ref, sin_ref,
                    qT_ref, k_ref, vT_ref, qn_ref, kn_ref, vn_ref, g_ref):
    xb = x_ref[...].astype(BF16)

    def seg(lo, hi):
        return jnp.dot(xb, w_ref[:, lo:hi], preferred_element_type=F32) + b_ref[:, lo:hi]

    cos = cos_ref[...]
    sin = sin_ref[...]
    lane = lax.broadcasted_iota(jnp.int32, cos.shape, 1)
    first_half = (lane % DA_HEAD_DIM) < (DA_HEAD_DIM // 2)

    def rope(y):
        outs = []
        for h in range(DA_HEADS):
            yh = y[:, h * LANES:(h + 1) * LANES]
            partner = jnp.where(first_half,
                                pltpu.roll(yh, LANES - DA_HEAD_DIM // 2, 1),
                                pltpu.roll(yh, DA_HEAD_DIM // 2, 1))
            outs.append(yh * cos + partner * sin)
        return jnp.concatenate(outs, axis=1)

    w = DA_WIDTH
    q = rope(seg(0, w)) * (DA_HEAD_DIM ** -0.5 * LOG2E)
    qT_ref[...] = q.T.astype(BF16)
    k_ref[...] = rope(seg(w, 2 * w)).astype(BF16)
    vT_ref[...] = seg(2 * w, 3 * w).T.astype(BF16)
    o = 3 * w
    qn_ref[...] = (seg(o, o + NA_WIDTH) * (NA_HEAD_DIM ** -0.5)).astype(BF16)
    kn_ref[...] = seg(o + NA_WIDTH, o + 2 * NA_WIDTH).astype(BF16)
    vn_ref[...] = seg(o + 2 * NA_WIDTH, o + 3 * NA_WIDTH).astype(BF16)
    g0 = o + 3 * NA_WIDTH
    gate_pre = seg(g0, w_ref.shape[1])
    g_ref[...] = (1.0 / (1.0 + jnp.exp(-gate_pre))).astype(BF16)


def _in_proj(x, w_in, b_in, cos, sin):
    B, S, D = x.shape
    tm = min(PROJ_TM, S)
    n_cols = w_in.shape[1]
    n_gate = n_cols - 3 * DA_WIDTH - 3 * NA_WIDTH
    tok = lambda width: pl.BlockSpec((None, tm, width), lambda b, i: (b, i, 0))
    tr = pl.BlockSpec((None, DA_WIDTH, tm), lambda b, i: (b, 0, i))
    const = lambda shape: pl.BlockSpec(shape, lambda b, i: (0,) * len(shape))
    out_shape = (
        jax.ShapeDtypeStruct((B, DA_WIDTH, S), BF16),
        jax.ShapeDtypeStruct((B, S, DA_WIDTH), BF16),
        jax.ShapeDtypeStruct((B, DA_WIDTH, S), BF16),
        jax.ShapeDtypeStruct((B, S, NA_WIDTH), BF16),
        jax.ShapeDtypeStruct((B, S, NA_WIDTH), BF16),
        jax.ShapeDtypeStruct((B, S, NA_WIDTH), BF16),
        jax.ShapeDtypeStruct((B, S, n_gate), BF16),
    )
    return pl.pallas_call(
        _in_proj_kernel,
        grid=(B, S // tm),
        in_specs=[tok(D), const((D, n_cols)), const((1, n_cols)),
                  pl.BlockSpec((tm, LANES), lambda b, i: (i, 0)),
                  pl.BlockSpec((tm, LANES), lambda b, i: (i, 0))],
        out_specs=(tr, tok(DA_WIDTH), tr, tok(NA_WIDTH), tok(NA_WIDTH), tok(NA_WIDTH), tok(n_gate)),
        out_shape=out_shape,
        compiler_params=_cparams(("parallel", "parallel")),
        name="in_proj",
    )(x, w_in, b_in, cos, sin)


def _diff_attn_kernel(qT_ref, k_ref, vT_ref, lq1_ref, lk1_ref, lq2_ref, lk2_ref, g_ref,
                      o_ref, m_sc, l_sc, acc_sc, *, tk, lam_init):
    qT = qT_ref[...]
    tq = qT.shape[1]
    row = lax.broadcasted_iota(jnp.int32, qT.shape, 0)
    zero = jnp.zeros_like(qT)
    qbd = jnp.concatenate([jnp.where(row < DA_HEAD_DIM, qT, zero),
                           jnp.where(row < DA_HEAD_DIM, zero, qT)], axis=1)

    m_sc[...] = jnp.full(m_sc.shape, -jnp.inf, F32)
    l_sc[...] = jnp.zeros(l_sc.shape, F32)
    acc_sc[...] = jnp.zeros(acc_sc.shape, F32)

    def body(j, carry):
        off = pl.multiple_of(j * tk, tk)
        kb = k_ref[pl.ds(off, tk), :]
        s = jnp.dot(kb, qbd, preferred_element_type=F32)
        m_prev = m_sc[...]
        m_new = jnp.maximum(m_prev, jnp.max(s, axis=0, keepdims=True))
        alpha = jnp.exp2(m_prev - m_new)
        p = jnp.exp2(s - m_new)
        l_sc[...] = alpha * l_sc[...] + jnp.sum(p, axis=0, keepdims=True)
        vb = vT_ref[:, pl.ds(off, tk)]
        acc_sc[...] = alpha * acc_sc[...] + jnp.dot(vb, p.astype(BF16), preferred_element_type=F32)
        m_sc[...] = m_new
        return carry

    lax.fori_loop(0, k_ref.shape[0] // tk, body, 0)

    lam = (jnp.exp(jnp.sum(lq1_ref[...] * lk1_ref[...], axis=1, keepdims=True))
           - jnp.exp(jnp.sum(lq2_ref[...] * lk2_ref[...], axis=1, keepdims=True)) + lam_init)
    on = acc_sc[...] / l_sc[...]
    o = on[:, :tq] - lam * on[:, tq:]
    ms = jnp.mean(o * o, axis=0, keepdims=True)
    o = o * lax.rsqrt(ms + RMS_EPS) * g_ref[...]
    o = o * (1.0 - lam_init)
    o_ref[...] = o.T.astype(BF16)


def _diff_attn(qT, k, vT, lq1, lk1, lq2, lk2, subln_g, lam_init):
    B, S, _ = k.shape
    tq = min(DA_TQ, S)
    tk = min(DA_TK, S)
    vec = pl.BlockSpec((1, DA_HEAD_DIM), lambda b, h, i: (0, 0))
    kernel = functools.partial(_diff_attn_kernel, tk=tk, lam_init=lam_init)
    return pl.pallas_call(
        kernel,
        grid=(B, DA_HEADS, S // tq),
        in_specs=[pl.BlockSpec((None, DA_V_DIM, tq), lambda b, h, i: (b, h, i)),
                  pl.BlockSpec((None, S, DA_V_DIM), lambda b, h, i: (b, 0, h)),
                  pl.BlockSpec((None, DA_V_DIM, S), lambda b, h, i: (b, h, 0)),
                  vec, vec, vec, vec,
                  pl.BlockSpec((DA_V_DIM, 1), lambda b, h, i: (0, 0))],
        out_specs=pl.BlockSpec((None, tq, DA_V_DIM), lambda b, h, i: (b, i, h)),
        out_shape=jax.ShapeDtypeStruct((B, S, DA_WIDTH), BF16),
        scratch_shapes=[pltpu.VMEM((1, 2 * tq), F32), pltpu.VMEM((1, 2 * tq), F32),
                        pltpu.VMEM((DA_V_DIM, 2 * tq), F32)],
        compiler_params=_cparams(("parallel", "parallel", "arbitrary")),
        name="diff_attn",
    )(qT, k, vT, lq1, lk1, lq2, lk2, subln_g)


def _na_band_start(r0, rows):
    return jnp.clip(r0 - NA_KH // 2, 0, rows - NA_BAND)


def _na_bias(rpb, rows):
    nblk = rows // NA_ROWS
    r0 = jnp.array([0, NA_ROWS * min(1, nblk - 1), NA_ROWS * (nblk - 1)], jnp.int32)
    band = _na_band_start(r0, rows)
    qr = r0[:, None] + jnp.arange(NA_ROWS)[None, :]
    rs = jnp.clip(qr - NA_KH // 2, 0, rows - NA_KH)
    kr = band[:, None] + jnp.arange(NA_BAND)[None, :]
    row_ok = (kr[:, None, :] >= rs[:, :, None]) & (kr[:, None, :] < rs[:, :, None] + NA_KH)
    row_off = jnp.clip(kr[:, None, :] - qr[:, :, None] + (NA_KH - 1), 0, 2 * NA_KH - 2)
    cols = jnp.arange(GRID_W)
    cs = jnp.clip(cols - NA_KW // 2, 0, GRID_W - NA_KW)
    col_ok = (cols[None, :] >= cs[:, None]) & (cols[None, :] < cs[:, None] + NA_KW)
    col_off = jnp.clip(cols[None, :] - cols[:, None] + (NA_KW - 1), 0, 2 * NA_KW - 2)
    vals = rpb[:, row_off[:, :, None, :, None], col_off[None, None, :, None, :]]
    ok = row_ok[:, :, None, :, None] & col_ok[None, None, :, None, :]
    bias = jnp.where(ok[None], vals.astype(F32), NEG_BIG)
    nq, nk = NA_ROWS * GRID_W, NA_BAND * GRID_W
    bias = bias.reshape(NA_HEADS // 2, 2, 3, nq, nk).transpose(2, 0, 3, 1, 4)
    return bias.reshape(3, NA_HEADS // 2, nq, 2 * nk)


def _na_kernel(q_ref, k_ref, v_ref, bias_ref, o_ref, *, rows):
    i = pl.program_id(2)
    nk = NA_BAND * GRID_W
    off = pl.multiple_of(_na_band_start(i * NA_ROWS, rows) * GRID_W, GRID_W)
    kb = k_ref[pl.ds(off, nk), :]
    vb = v_ref[pl.ds(off, nk), :]
    lane = lax.broadcasted_iota(jnp.int32, kb.shape, 1)
    head0 = lane < NA_HEAD_DIM
    zero = jnp.zeros_like(kb)
    kbd = jnp.concatenate([jnp.where(head0, kb, zero), jnp.where(head0, zero, kb)], axis=0)
    vbd = jnp.concatenate([jnp.where(head0, vb, zero), jnp.where(head0, zero, vb)], axis=0)
    q = q_ref[...]
    s = lax.dot_general(q, kbd, (((1,), (1,)), ((), ())), preferred_element_type=F32)
    s = s + bias_ref[...]
    ps, ls = [], []
    for h in range(2):
        sh = s[:, h * nk:(h + 1) * nk]
        ph = jnp.exp(sh - jnp.max(sh, axis=1, keepdims=True))
        ps.append(ph.astype(BF16))
        ls.append(jnp.sum(ph, axis=1, keepdims=True))
    p = jnp.concatenate(ps, axis=1)
    o = jnp.dot(p, vbd, preferred_element_type=F32)
    lane_o = lax.broadcasted_iota(jnp.int32, o.shape, 1)
    denom = jnp.where(lane_o < NA_HEAD_DIM, ls[0], ls[1])
    o_ref[...] = (o / denom).astype(BF16)


def _na_attn(qn, kn, vn, bias):
    B, S, _ = qn.shape
    rows = S // GRID_W
    nblk = rows // NA_ROWS
    nq = NA_ROWS * GRID_W

    def variant(i):
        return jnp.where(i == 0, 0, jnp.where(i == nblk - 1, 2, 1))

    return pl.pallas_call(
        functools.partial(_na_kernel, rows=rows),
        grid=(B, NA_HEADS // 2, nblk),
        in_specs=[pl.BlockSpec((None, nq, LANES), lambda b, h, i: (b, i, h)),
                  pl.BlockSpec((None, S, LANES), lambda b, h, i: (b, 0, h)),
                  pl.BlockSpec((None, S, LANES), lambda b, h, i: (b, 0, h)),
                  pl.BlockSpec((None, None, nq, bias.shape[3]), lambda b, h, i: (variant(i), h, 0, 0))],
        out_specs=pl.BlockSpec((None, nq, LANES), lambda b, h, i: (b, i, h)),
        out_shape=jax.ShapeDtypeStruct((B, S, NA_WIDTH), BF16),
        compiler_params=_cparams(("parallel", "parallel", "arbitrary")),
        name="na_attn",
    )(qn, kn, vn, bias)


def _layer_norm(h, g, b):
    mu = jnp.mean(h, axis=-1, keepdims=True)
    d = h - mu
    var = jnp.mean(d * d, axis=-1, keepdims=True)
    return d * lax.rsqrt(var + LN_EPS) * g + b


def _merge_kernel(x_ref, a_ref, nb_ref, g_ref, wda_ref, wna_ref, wout_ref, lng_ref, lnb_ref,
                  wr_ref, br_ref, x1_ref, x1b_ref, logit_ref, *, alpha):
    d = x_ref.shape[1]
    ya = jnp.dot(a_ref[...], wda_ref[...], preferred_element_type=F32)
    yb = jnp.dot(nb_ref[...], wna_ref[...], preferred_element_type=F32)
    g = g_ref[...].astype(F32)
    merged = g[:, :d] * ya + g[:, d:] * yb
    mix = jnp.dot(merged.astype(BF16), wout_ref[...], preferred_element_type=F32)
    x1 = _layer_norm(alpha * x_ref[...] + mix, lng_ref[...], lnb_ref[...])
    x1_ref[...] = x1
    x1b_ref[...] = x1.astype(BF16)
    logit_ref[...] = jnp.dot(x1, wr_ref[...], preferred_element_type=F32,
                             precision=lax.Precision.HIGHEST) + br_ref[...]


def _merge(x2, a2, nb2, g2, wda, wna, wout, ln_g, ln_b, wr, br, alpha):
    T, D = x2.shape
    tm = min(PROJ_TM, T)
    tok = lambda width: pl.BlockSpec((tm, width), lambda i: (i, 0))
    const = lambda shape: pl.BlockSpec(shape, lambda i: (0,) * len(shape))
    return pl.pallas_call(
        functools.partial(_merge_kernel, alpha=alpha),
        grid=(T // tm,),
        in_specs=[tok(D), tok(a2.shape[1]), tok(nb2.shape[1]), tok(g2.shape[1]),
                  const(wda.shape), const(wna.shape), const(wout.shape),
                  const((1, D)), const((1, D)), const(wr.shape), const((1, LANES))],
        out_specs=(tok(D), tok(D), tok(LANES)),
        out_shape=(jax.ShapeDtypeStruct((T, D), F32), jax.ShapeDtypeStruct((T, D), BF16),
                   jax.ShapeDtypeStruct((T, LANES), F32)),
        compiler_params=_cparams(("parallel",)),
        name="merge",
    )(x2, a2, nb2, g2, wda, wna, wout, ln_g, ln_b, wr, br)


def _moe_kernel(be_ref, nused_ref, xs_ref, w1g_ref, w1l_ref, b1g_ref, b1l_ref, w2_ref, b2_ref, ys_ref):
    i = pl.program_id(0)

    @pl.when(i < nused_ref[0])
    def _():
        xb = xs_ref[...]
        hg = jnp.dot(xb, w1g_ref[...], preferred_element_type=F32) + b1g_ref[...]
        hl = jnp.dot(xb, w1l_ref[...], preferred_element_type=F32) + b1l_ref[...]
        x_glu = jnp.minimum(hg, SWIGLU_LIMIT)
        x_lin = jnp.clip(hl, -SWIGLU_LIMIT, SWIGLU_LIMIT)
        act = x_glu * (1.0 / (1.0 + jnp.exp(-SWIGLU_ALPHA * x_glu))) * (x_lin + 1.0)
        y = jnp.dot(act.astype(BF16), w2_ref[...], preferred_element_type=F32) + b2_ref[...]
        ys_ref[...] = y.astype(ys_ref.dtype)

    @pl.when(i >= nused_ref[0])
    def _():
        ys_ref[...] = jnp.zeros(ys_ref.shape, ys_ref.dtype)


def _moe_ffn(xs, block_expert, n_used, w1g, w1l, b1g, b1l, w2, b2):
    P, D = xs.shape
    n_blocks = P // MOE_BLOCK
    de = w1g.shape[2]
    wspec = lambda r, c: pl.BlockSpec((None, r, c), lambda i, be, nu: (be[i], 0, 0))
    grid_spec = pltpu.PrefetchScalarGridSpec(
        num_scalar_prefetch=2,
        grid=(n_blocks,),
        in_specs=[pl.BlockSpec((MOE_BLOCK, D), lambda i, be, nu: (i, 0)),
                  wspec(D, de), wspec(D, de), wspec(1, de), wspec(1, de), wspec(de, D), wspec(1, D)],
        out_specs=pl.BlockSpec((MOE_BLOCK, D), lambda i, be, nu: (i, 0)),
    )
    return pl.pallas_call(
        _moe_kernel,
        grid_spec=grid_spec,
        out_shape=jax.ShapeDtypeStruct((P, D), BF16),
        compiler_params=_cparams(("arbitrary",)),
        name="moe_ffn",
    )(block_expert, n_used, xs, w1g, w1l, b1g, b1l, w2, b2)


def _combine_kernel(x1_ref, ys_ref, gate_ref, lng_ref, lnb_ref, o_ref, *, alpha):
    d = x1_ref.shape[1]
    gates = gate_ref[...]
    ffn = jnp.zeros(x1_ref.shape, F32)
    for k in range(TOP_K):
        ffn = ffn + ys_ref[:, k * d:(k + 1) * d].astype(F32) * gates[:, k:k + 1]
    o_ref[...] = _layer_norm(alpha * x1_ref[...] + ffn, lng_ref[...], lnb_ref[...])


def _combine(x1, y_slots, gates, ln_g, ln_b, alpha):
    T, D = x1.shape
    tm = min(PROJ_TM, T)
    tok = lambda width: pl.BlockSpec((tm, width), lambda i: (i, 0))
    const = lambda shape: pl.BlockSpec(shape, lambda i: (0,) * len(shape))
    return pl.pallas_call(
        functools.partial(_combine_kernel, alpha=alpha),
        grid=(T // tm,),
        in_specs=[tok(D), tok(TOP_K * D), tok(LANES), const((1, D)), const((1, D))],
        out_specs=tok(D),
        out_shape=jax.ShapeDtypeStruct((T, D), F32),
        compiler_params=_cparams(("parallel",)),
        name="combine",
    )(x1, y_slots, gates, ln_g, ln_b)


def _route(logits, n_experts):
    T = logits.shape[0]
    TK = T * TOP_K
    top_v, top_i = lax.top_k(logits, TOP_K)
    gates = jax.nn.softmax(top_v, axis=-1)
    flat_e = top_i.reshape(-1)
    flat_tok = jnp.arange(TK, dtype=jnp.int32) // TOP_K
    order = jnp.argsort(flat_e, stable=True)
    sorted_e = flat_e[order]
    counts = jnp.bincount(flat_e, length=n_experts)
    padded = ((counts + MOE_BLOCK - 1) // MOE_BLOCK) * MOE_BLOCK
    padded_end = jnp.cumsum(padded)
    padded_start = padded_end - padded
    group_start = jnp.cumsum(counts) - counts
    rank = jnp.arange(TK, dtype=jnp.int32) - group_start[sorted_e]
    dest = (padded_start[sorted_e] + rank).astype(jnp.int32)
    n_blocks = -(-TK // MOE_BLOCK) + n_experts
    P = n_blocks * MOE_BLOCK
    buf_tok = jnp.full((P,), T, dtype=jnp.int32).at[dest].set(flat_tok[order])
    block_expert = jnp.clip(jnp.searchsorted(padded_end, jnp.arange(n_blocks) * MOE_BLOCK, side='right'),
                            0, n_experts - 1).astype(jnp.int32)
    n_used = (padded_end[-1] // MOE_BLOCK).astype(jnp.int32).reshape(1)
    dest_orig = jnp.zeros((TK,), jnp.int32).at[order].set(dest)
    return gates, buf_tok, block_expert, n_used, dest_orig


def _rope_tables(seq):
    pos = jnp.arange(seq, dtype=F32)
    inv = ROPE_THETA ** (-jnp.arange(0, DA_HEAD_DIM, 2, dtype=F32) / DA_HEAD_DIM)
    ang = pos[:, None] * inv[None, :]
    ang = jnp.concatenate([ang, ang], axis=-1)
    cos, sin = jnp.cos(ang), jnp.sin(ang)
    half = DA_HEAD_DIM // 2
    sin_signed = jnp.concatenate([-sin[:, :half], sin[:, half:]], axis=-1)
    return jnp.tile(cos, (1, 2)), jnp.tile(sin_signed, (1, 2))


def kernel(x, w_in, b_in, lambda_q1, lambda_k1, lambda_q2, lambda_k2, subln_g, rpb, w_branch_da, w_branch_na, w_out, ln1_g, ln1_b, w_router, b_router, w_mlp1, b_mlp1, w_mlp2, b_mlp2, ln2_g, ln2_b):
    B, S, D = x.shape
    depth = w_in.shape[0]
    n_experts = w_router.shape[2]
    T = B * S
    rows = S // GRID_W
    assert S % GRID_W == 0 and rows % NA_ROWS == 0 and rows >= NA_BAND
    alpha = (2 * depth) ** 0.25
    cos, sin = _rope_tables(S)
    row = lambda v: v.reshape(1, -1)

    for l in range(depth):
        lam_init = 0.8 - 0.6 * math.exp(-0.3 * l)
        qT, k, vT, qn, kn, vn, gates_br = _in_proj(x, w_in[l].astype(BF16), row(b_in[l]), cos, sin)
        a = _diff_attn(qT, k, vT, row(lambda_q1[l]), row(lambda_k1[l]), row(lambda_q2[l]),
                       row(lambda_k2[l]), subln_g[l].reshape(-1, 1), lam_init)
        nb = _na_attn(qn, kn, vn, _na_bias(rpb[l], rows))

        wr = jnp.zeros((D, LANES), F32).at[:, :n_experts].set(w_router[l])
        br = jnp.zeros((1, LANES), F32).at[0, :n_experts].set(b_router[l])
        x1, x1b, logits = _merge(
            x.reshape(T, D), a.reshape(T, -1), nb.reshape(T, -1), gates_br.reshape(T, -1),
            w_branch_da[l].astype(BF16), w_branch_na[l].astype(BF16), w_out[l].astype(BF16),
            row(ln1_g[l]), row(ln1_b[l]), wr, br, alpha)

        gates, buf_tok, block_expert, n_used, dest_orig = _route(logits[:, :n_experts], n_experts)
        xs = jnp.take(jnp.concatenate([x1b, jnp.zeros((1, D), BF16)], axis=0), buf_tok, axis=0)
        ys = _moe_ffn(xs, block_expert, n_used,
                      w_mlp1[l][:, :, 0::2].astype(BF16), w_mlp1[l][:, :, 1::2].astype(BF16),
                      b_mlp1[l][:, None, 0::2], b_mlp1[l][:, None, 1::2],
                      w_mlp2[l].astype(BF16), b_mlp2[l][:, None, :])
        y_slots = jnp.take(ys, dest_orig, axis=0).reshape(T, TOP_K * D)
        gates_pad = jnp.zeros((T, LANES), F32).at[:, :TOP_K].set(gates)
        x = _combine(x1, y_slots, gates_pad, row(ln2_g[l]), row(ln2_b[l]), alpha).reshape(B, S, D)
    return x
```

```python
import functools
import math

import jax
import jax.numpy as jnp
from jax import lax
from jax.experimental import pallas as pl
from jax.experimental.pallas import tpu as pltpu

F32 = jnp.float32
BF16 = jnp.bfloat16

GRID_W = 64
DA_HEADS = 4
DA_HEAD_DIM = 64
DA_V_DIM = 2 * DA_HEAD_DIM
DA_WIDTH = DA_HEADS * DA_V_DIM
ROPE_THETA = 10000.0
NA_HEADS = 8
NA_HEAD_DIM = 64
NA_WIDTH = NA_HEADS * NA_HEAD_DIM
NA_KH = 8
NA_KW = 16
TOP_K = 4
SWIGLU_ALPHA = 1.702
SWIGLU_LIMIT = 7.0
MOE_BLOCK = 256
LN_EPS = 1e-5
RMS_EPS = 1e-5

LANES = 128
VMEM_LIMIT_BYTES = 56 * 1024 * 1024

LOG2E = math.log2(math.e)
NEG_BIG = -1e30

PROJ_TM = 512
DA_TQ = 256
DA_TK = 512
NA_ROWS = 8
NA_BAND = 16


def _cparams(sem):
    return pltpu.CompilerParams(dimension_semantics=sem, vmem_limit_bytes=VMEM_LIMIT_BYTES)


def _in_proj_kernel(x_ref, w_ref, b_ref, cos_ref, sin_ref,
                    qT_ref, k_ref, vT_ref, qn_ref, kn_ref, vn_ref, g_ref):
    xb = x_ref[...].astype(BF16)

    def seg(lo, hi):
        return jnp.dot(xb, w_ref[:, lo:hi], preferred_element_type=F32) + b_ref[:, lo:hi]

    cos = cos_ref[...]
    sin = sin_ref[...]
    lane = lax.broadcasted_iota(jnp.int32, cos.shape, 1)
    first_half = (lane % DA_HEAD_DIM) < (DA_HEAD_DIM // 2)

    def rope(y):
        outs = []
        for h in range(DA_HEADS):
            yh = y[:, h * LANES:(h + 1) * LANES]
            partner = jnp.where(first_half,
                                pltpu.roll(yh, LANES - DA_HEAD_DIM // 2, 1),
                                pltpu.roll(yh, DA_HEAD_DIM // 2, 1))
            outs.append(yh * cos + partner * sin)
        return jnp.concatenate(outs, axis=1)

    w = DA_WIDTH
    q = rope(seg(0, w)) * (DA_HEAD_DIM ** -0.5 * LOG2E)
    qT_ref[...] = q.T.astype(BF16)
    k_ref[...] = rope(seg(w, 2 * w)).astype(BF16)
    vT_ref[...] = seg(2 * w, 3 * w).T.astype(BF16)
    o = 3 * w
    qn_ref[...] = (seg(o, o + NA_WIDTH) * (NA_HEAD_DIM ** -0.5)).astype(BF16)
    kn_ref[...] = seg(o + NA_WIDTH, o + 2 * NA_WIDTH).astype(BF16)
    vn_ref[...] = seg(o + 2 * NA_WIDTH, o + 3 * NA_WIDTH).astype(BF16)
    g0 = o + 3 * NA_WIDTH
    gate_pre = seg(g0, w_ref.shape[1])
    g_ref[...] = (1.0 / (1.0 + jnp.exp(-gate_pre))).astype(BF16)


def _in_proj(x, w_in, b_in, cos, sin):
    B, S, D = x.shape
    tm = min(PROJ_TM, S)
    n_cols = w_in.shape[1]
    n_gate = n_cols - 3 * DA_WIDTH - 3 * NA_WIDTH
    tok = lambda width: pl.BlockSpec((None, tm, width), lambda b, i: (b, i, 0))
    tr = pl.BlockSpec((None, DA_WIDTH, tm), lambda b, i: (b, 0, i))
    const = lambda shape: pl.BlockSpec(shape, lambda b, i: (0,) * len(shape))
    out_shape = (
        jax.ShapeDtypeStruct((B, DA_WIDTH, S), BF16),
        jax.ShapeDtypeStruct((B, S, DA_WIDTH), BF16),
        jax.ShapeDtypeStruct((B, DA_WIDTH, S), BF16),
        jax.ShapeDtypeStruct((B, S, NA_WIDTH), BF16),
        jax.ShapeDtypeStruct((B, S, NA_WIDTH), BF16),
        jax.ShapeDtypeStruct((B, S, NA_WIDTH), BF16),
        jax.ShapeDtypeStruct((B, S, n_gate), BF16),
    )
    return pl.pallas_call(
        _in_proj_kernel,
        grid=(B, S // tm),
        in_specs=[tok(D), const((D, n_cols)), const((1, n_cols)),
                  pl.BlockSpec((tm, LANES), lambda b, i: (i, 0)),
                  pl.BlockSpec((tm, LANES), lambda b, i: (i, 0))],
        out_specs=(tr, tok(DA_WIDTH), tr, tok(NA_WIDTH), tok(NA_WIDTH), tok(NA_WIDTH), tok(n_gate)),
        out_shape=out_shape,
        compiler_params=_cparams(("parallel", "parallel")),
        name="in_proj",
    )(x, w_in, b_in, cos, sin)


def _diff_attn_kernel(qT_ref, k_ref, vT_ref, lq1_ref, lk1_ref, lq2_ref, lk2_ref, g_ref,
                      o_ref, m_sc, l_sc, acc_sc, s_sc, *, tk, lam_init):
    qT = qT_ref[...]
    tq = qT.shape[1]
    row = lax.broadcasted_iota(jnp.int32, qT.shape, 0)
    zero = jnp.zeros_like(qT)
    qbd = jnp.concatenate([jnp.where(row < DA_HEAD_DIM, qT, zero),
                           jnp.where(row < DA_HEAD_DIM, zero, qT)], axis=1)

    m_sc[...] = jnp.full(m_sc.shape, -jnp.inf, F32)
    l_sc[...] = jnp.zeros(l_sc.shape, F32)
    acc_sc[...] = jnp.zeros(acc_sc.shape, F32)

    nk = k_ref.shape[0] // tk

    def scores(j):
        off = pl.multiple_of(j * tk, tk)
        return jnp.dot(k_ref[pl.ds(off, tk), :], qbd, preferred_element_type=F32)

    def accumulate(j, s):
        off = pl.multiple_of(j * tk, tk)
        m_prev = m_sc[...]
        m_new = jnp.maximum(m_prev, jnp.max(s, axis=0, keepdims=True))
        alpha = jnp.exp2(m_prev - m_new)
        p = jnp.exp2(s - m_new)
        l_sc[...] = alpha * l_sc[...] + jnp.sum(p, axis=0, keepdims=True)
        vb = vT_ref[:, pl.ds(off, tk)]
        acc_sc[...] = alpha * acc_sc[...] + jnp.dot(vb, p.astype(BF16), preferred_element_type=F32)
        m_sc[...] = m_new

    s_sc[0] = scores(0)

    def body(jj, carry):
        j = 2 * jj
        s_sc[1] = scores(j + 1)
        accumulate(j, s_sc[0])
        s_sc[0] = scores(jnp.minimum(j + 2, nk - 1))
        accumulate(j + 1, s_sc[1])
        return carry

    lax.fori_loop(0, nk // 2, body, 0)

    lam = (jnp.exp(jnp.sum(lq1_ref[...] * lk1_ref[...], axis=1, keepdims=True))
           - jnp.exp(jnp.sum(lq2_ref[...] * lk2_ref[...], axis=1, keepdims=True)) + lam_init)
    on = acc_sc[...] / l_sc[...]
    o = on[:, :tq] - lam * on[:, tq:]
    ms = jnp.mean(o * o, axis=0, keepdims=True)
    o = o * lax.rsqrt(ms + RMS_EPS) * g_ref[...]
    o = o * (1.0 - lam_init)
    o_ref[...] = o.T.astype(BF16)


def _diff_attn(qT, k, vT, lq1, lk1, lq2, lk2, subln_g, lam_init):
    B, S, _ = k.shape
    tq = min(DA_TQ, S)
    tk = min(DA_TK, S // 2)
    assert S % (2 * tk) == 0 and S % tq == 0
    vec =pl.BlockSpec((1, DA_HEAD_DIM), lambda b, h, i: (0, 0))
    kernel = functools.partial(_diff_attn_kernel, tk=tk, lam_init=lam_init)
    return pl.pallas_call(
        kernel,
        grid=(B, DA_HEADS, S // tq),
        in_specs=[pl.BlockSpec((None, DA_V_DIM, tq), lambda b, h, i: (b, h, i)),
                  pl.BlockSpec((None, S, DA_V_DIM), lambda b, h, i: (b, 0, h)),
                  pl.BlockSpec((None, DA_V_DIM, S), lambda b, h, i: (b, h, 0)),
                  vec, vec, vec, vec,
                  pl.BlockSpec((DA_V_DIM, 1), lambda b, h, i: (0, 0))],
        out_specs=pl.BlockSpec((None, tq, DA_V_DIM), lambda b, h, i: (b, i, h)),
        out_shape=jax.ShapeDtypeStruct((B, S, DA_WIDTH), BF16),
        scratch_shapes=[pltpu.VMEM((1, 2 * tq), F32), pltpu.VMEM((1, 2 * tq), F32),
                        pltpu.VMEM((DA_V_DIM, 2 * tq), F32), pltpu.VMEM((2, tk, 2 * tq), F32)],
        compiler_params=_cparams(("parallel", "parallel", "arbitrary")),
        name="diff_attn",
    )(qT, k, vT, lq1, lk1, lq2, lk2, subln_g)


def _na_band_start(r0, rows):
    return jnp.clip(r0 - NA_KH // 2, 0, rows - NA_BAND)


def _na_bias(rpb, rows):
    nblk = rows // NA_ROWS
    r0 = jnp.array([0, NA_ROWS * min(1, nblk - 1), NA_ROWS * (nblk - 1)], jnp.int32)
    band = _na_band_start(r0, rows)
    qr = r0[:, None] + jnp.arange(NA_ROWS)[None, :]
    rs = jnp.clip(qr - NA_KH // 2, 0, rows - NA_KH)
    kr = band[:, None] + jnp.arange(NA_BAND)[None, :]
    row_ok = (kr[:, None, :] >= rs[:, :, None]) & (kr[:, None, :] < rs[:, :, None] + NA_KH)
    row_off = jnp.clip(kr[:, None, :] - qr[:, :, None] + (NA_KH - 1), 0, 2 * NA_KH - 2)
    cols = jnp.arange(GRID_W)
    cs = jnp.clip(cols - NA_KW // 2, 0, GRID_W - NA_KW)
    col_ok = (cols[None, :] >= cs[:, None]) & (cols[None, :] < cs[:, None] + NA_KW)
    col_off = jnp.clip(cols[None, :] - cols[:, None] + (NA_KW - 1), 0, 2 * NA_KW - 2)
    tiles = jnp.where(col_ok[None, None], rpb.astype(F32)[:, :, col_off], NEG_BIG)
    sel = jnp.where(row_ok[None, :, :, :, None, None], tiles[:, row_off], NEG_BIG)
    nq, nk = NA_ROWS * GRID_W, NA_BAND * GRID_W
    sel = sel.reshape(NA_HEADS // 2, 2, 3, NA_ROWS, NA_BAND, GRID_W, GRID_W)
    sel = sel.transpose(2, 0, 3, 5, 1, 4, 6)
    return sel.reshape(3, NA_HEADS // 2, nq, 2 * nk)


def _na_kernel(q_ref, k_ref, v_ref, bias_ref, o_ref, *, rows):
    i = pl.program_id(2)
    nk = NA_BAND * GRID_W
    off = pl.multiple_of(_na_band_start(i * NA_ROWS, rows) * GRID_W, GRID_W)
    kb = k_ref[pl.ds(off, nk), :]
    vb = v_ref[pl.ds(off, nk), :]
    lane = lax.broadcasted_iota(jnp.int32, kb.shape, 1)
    head0 = lane < NA_HEAD_DIM
    zero = jnp.zeros_like(kb)
    kbd = jnp.concatenate([jnp.where(head0, kb, zero), jnp.where(head0, zero, kb)], axis=0)
    vbd = jnp.concatenate([jnp.where(head0, vb, zero), jnp.where(head0, zero, vb)], axis=0)
    q = q_ref[...]
    s = lax.dot_general(q, kbd, (((1,), (1,)), ((), ())), preferred_element_type=F32)
    s = s + bias_ref[...]
    ps, ls = [], []
    for h in range(2):
        sh = s[:, h * nk:(h + 1) * nk]
        ph = jnp.exp(sh - jnp.max(sh, axis=1, keepdims=True))
        ps.append(ph.astype(BF16))
        ls.append(jnp.sum(ph, axis=1, keepdims=True))
    p = jnp.concatenate(ps, axis=1)
    o = jnp.dot(p, vbd, preferred_element_type=F32)
    lane_o = lax.broadcasted_iota(jnp.int32, o.shape, 1)
    denom = jnp.where(lane_o < NA_HEAD_DIM, ls[0], ls[1])
    o_ref[...] = (o / denom).astype(BF16)


def _na_attn(qn, kn, vn, bias):
    B, S, _ = qn.shape
    rows = S // GRID_W
    nblk = rows // NA_ROWS
    nq = NA_ROWS * GRID_W

    def variant(i):
        return jnp.where(i == 0, 0, jnp.where(i == nblk - 1, 2, 1))

    return pl.pallas_call(
        functools.partial(_na_kernel, rows=rows),
        grid=(B, NA_HEADS // 2, nblk),
        in_specs=[pl.BlockSpec((None, nq, LANES), lambda b, h, i: (b, i, h)),
                  pl.BlockSpec((None, S, LANES), lambda b, h, i: (b, 0, h)),
                  pl.BlockSpec((None, S, LANES), lambda b, h, i: (b, 0, h)),
                  pl.BlockSpec((None, None, nq, bias.shape[3]), lambda b, h, i: (variant(i), h, 0, 0))],
        out_specs=pl.BlockSpec((None, nq, LANES), lambda b, h, i: (b, i, h)),
        out_shape=jax.ShapeDtypeStruct((B, S, NA_WIDTH), BF16),
        compiler_params=_cparams(("parallel", "parallel", "arbitrary")),
        name="na_attn",
    )(qn, kn, vn, bias)


def _layer_norm(h, g, b):
    mu = jnp.mean(h, axis=-1, keepdims=True)
    d = h - mu
    var = jnp.mean(d * d, axis=-1, keepdims=True)
    return d * lax.rsqrt(var + LN_EPS) * g + b


def _merge_kernel(x_ref, a_ref, nb_ref, g_ref, wda_ref, wna_ref, wout_ref, lng_ref, lnb_ref,
                  wr_ref, br_ref, x1_ref, x1b_ref, logit_ref, *, alpha):
    d = x_ref.shape[1]
    ya = jnp.dot(a_ref[...], wda_ref[...], preferred_element_type=F32)
    yb = jnp.dot(nb_ref[...], wna_ref[...], preferred_element_type=F32)
    g = g_ref[...].astype(F32)
    merged = g[:, :d] * ya + g[:, d:] * yb
    mix = jnp.dot(merged.astype(BF16), wout_ref[...], preferred_element_type=F32)
    x1 = _layer_norm(alpha * x_ref[...] + mix, lng_ref[...], lnb_ref[...])
    x1_ref[...] = x1
    x1b_ref[...] = x1.astype(BF16)
    logit_ref[...] = jnp.dot(x1, wr_ref[...], preferred_element_type=F32,
                             precision=lax.Precision.HIGHEST) + br_ref[...]


def _merge(x2, a2, nb2, g2, wda, wna, wout, ln_g, ln_b, wr, br, alpha):
    T, D = x2.shape
    tm = min(PROJ_TM, T)
    tok = lambda width: pl.BlockSpec((tm, width), lambda i: (i, 0))
    const = lambda shape: pl.BlockSpec(shape, lambda i: (0,) * len(shape))
    return pl.pallas_call(
        functools.partial(_merge_kernel, alpha=alpha),
        grid=(T // tm,),
        in_specs=[tok(D), tok(a2.shape[1]), tok(nb2.shape[1]), tok(g2.shape[1]),
                  const(wda.shape), const(wna.shape), const(wout.shape),
                  const((1, D)), const((1, D)), const(wr.shape), const((1, LANES))],
        out_specs=(tok(D), tok(D), tok(LANES)),
        out_shape=(jax.ShapeDtypeStruct((T, D), F32), jax.ShapeDtypeStruct((T, D), BF16),
                   jax.ShapeDtypeStruct((T, LANES), F32)),
        compiler_params=_cparams(("parallel",)),
        name="merge",
    )(x2, a2, nb2, g2, wda, wna, wout, ln_g, ln_b, wr, br)


def _glu_group_bias(b1):
    e, n = b1.shape
    return b1.reshape(e, n // (2 * LANES), LANES, 2).transpose(0, 1, 3, 2).reshape(e, 1, n)


def _moe_kernel(be_ref, nused_ref, xs_ref, w1_ref, b1_ref, w2_ref, b2_ref, ys_ref, w1g_sc, w2b_sc):
    i = pl.program_id(0)
    used = i < nused_ref[0]
    fresh = jnp.logical_or(i == 0, be_ref[i] != be_ref[jnp.maximum(i - 1, 0)])
    grp = 2 * LANES

    @pl.when(jnp.logical_and(used, fresh))
    def _():
        r = lax.broadcasted_iota(jnp.int32, (grp, grp), 0)
        c = lax.broadcasted_iota(jnp.int32, (grp, grp), 1)
        src = jnp.where(c < LANES, 2 * c, 2 * (c - LANES) + 1)
        perm = jnp.where(r == src, 1.0, 0.0).astype(BF16)
        for g in range(w1_ref.shape[1] // grp):
            blk = w1_ref[:, g * grp:(g + 1) * grp].astype(BF16)
            w1g_sc[:, g * grp:(g + 1) * grp] = jnp.dot(blk, perm, preferred_element_type=F32).astype(BF16)
        w2b_sc[...] = w2_ref[...].astype(BF16)

    @pl.when(used)
    def _():
        xb = xs_ref[...]
        h = jnp.dot(xb, w1g_sc[...], preferred_element_type=F32) + b1_ref[...]
        acts = []
        for g in range(h.shape[1] // grp):
            x_glu = jnp.minimum(h[:, g * grp:g * grp + LANES], SWIGLU_LIMIT)
            x_lin = jnp.clip(h[:, g * grp + LANES:(g + 1) * grp], -SWIGLU_LIMIT, SWIGLU_LIMIT)
            acts.append(x_glu * (1.0 / (1.0 + jnp.exp(-SWIGLU_ALPHA * x_glu))) * (x_lin + 1.0))
        act = jnp.concatenate(acts, axis=1).astype(BF16)
        y = jnp.dot(act, w2b_sc[...], preferred_element_type=F32) + b2_ref[...]
        ys_ref[...] = y.astype(ys_ref.dtype)

    @pl.when(jnp.logical_not(used))
    def _():
        ys_ref[...] = jnp.zeros(ys_ref.shape, ys_ref.dtype)


def _moe_ffn(xs, block_expert, n_used, w1, b1g, w2, b2):
    P, D = xs.shape
    n_blocks = P // MOE_BLOCK
    de2 = w1.shape[2]
    de = w2.shape[1]
    wspec = lambda r, c: pl.BlockSpec((None, r, c), lambda i, be, nu: (be[i], 0, 0))
    grid_spec = pltpu.PrefetchScalarGridSpec(
        num_scalar_prefetch=2,
        grid=(n_blocks,),
        in_specs=[pl.BlockSpec((MOE_BLOCK, D), lambda i, be, nu: (i, 0)),
                  wspec(D, de2), wspec(1, de2), wspec(de, D), wspec(1, D)],
        out_specs=pl.BlockSpec((MOE_BLOCK, D), lambda i, be, nu: (i, 0)),
        scratch_shapes=[pltpu.VMEM((D, de2), BF16), pltpu.VMEM((de, D), BF16)],
    )
    return pl.pallas_call(
        _moe_kernel,
        grid_spec=grid_spec,
        out_shape=jax.ShapeDtypeStruct((P, D), BF16),
        compiler_params=_cparams(("arbitrary",)),
        name="moe_ffn",
    )(block_expert, n_used, xs, w1, b1g, w2, b2)


def _combine_kernel(x1_ref, ys_ref, gate_ref, lng_ref, lnb_ref, o_ref, *, alpha):
    d = x1_ref.shape[1]
    gates = gate_ref[...]
    ffn = jnp.zeros(x1_ref.shape, F32)
    for k in range(TOP_K):
        ffn = ffn + ys_ref[:, k * d:(k + 1) * d].astype(F32) * gates[:, k:k + 1]
    o_ref[...] = _layer_norm(alpha * x1_ref[...] + ffn, lng_ref[...], lnb_ref[...])


def _combine(x1, y_slots, gates, ln_g, ln_b, alpha):
    T, D = x1.shape
    tm = min(PROJ_TM, T)
    tok = lambda width: pl.BlockSpec((tm, width), lambda i: (i, 0))
    const = lambda shape: pl.BlockSpec(shape, lambda i: (0,) * len(shape))
    return pl.pallas_call(
        functools.partial(_combine_kernel, alpha=alpha),
        grid=(T // tm,),
        in_specs=[tok(D), tok(TOP_K * D), tok(LANES), const((1, D)), const((1, D))],
        out_specs=tok(D),
        out_shape=jax.ShapeDtypeStruct((T, D), F32),
        compiler_params=_cparams(("parallel",)),
        name="combine",
    )(x1, y_slots, gates, ln_g, ln_b)


def _route(logits, n_experts):
    T = logits.shape[0]
    TK = T * TOP_K
    top_v, top_i = lax.top_k(logits, TOP_K)
    gates = jax.nn.softmax(top_v, axis=-1)
    flat_e = top_i.reshape(-1)
    flat_tok = jnp.arange(TK, dtype=jnp.int32) // TOP_K
    order = jnp.argsort(flat_e, stable=True)
    sorted_e = flat_e[order]
    counts = jnp.bincount(flat_e, length=n_experts)
    padded = ((counts + MOE_BLOCK - 1) // MOE_BLOCK) * MOE_BLOCK
    padded_end = jnp.cumsum(padded)
    padded_start = padded_end - padded
    group_start = jnp.cumsum(counts) - counts
    rank = jnp.arange(TK, dtype=jnp.int32) - group_start[sorted_e]
    dest = (padded_start[sorted_e] + rank).astype(jnp.int32)
    n_blocks = -(-TK // MOE_BLOCK) + n_experts
    P = n_blocks * MOE_BLOCK
    buf_tok = jnp.full((P,), T, dtype=jnp.int32).at[dest].set(flat_tok[order])
    block_expert = jnp.clip(jnp.searchsorted(padded_end, jnp.arange(n_blocks) * MOE_BLOCK, side='right'),
                            0, n_experts - 1).astype(jnp.int32)
    n_used = (padded_end[-1] // MOE_BLOCK).astype(jnp.int32).reshape(1)
    dest_orig = jnp.zeros((TK,), jnp.int32).at[order].set(dest)
    return gates, buf_tok, block_expert, n_used, dest_orig


def _rope_tables(seq):
    pos = jnp.arange(seq, dtype=F32)
    inv = ROPE_THETA ** (-jnp.arange(0, DA_HEAD_DIM, 2, dtype=F32) / DA_HEAD_DIM)
    ang = pos[:, None] * inv[None, :]
    ang = jnp.concatenate([ang, ang], axis=-1)
    cos, sin = jnp.cos(ang), jnp.sin(ang)
    half = DA_HEAD_DIM // 2
    sin_signed = jnp.concatenate([-sin[:, :half], sin[:, half:]], axis=-1)
    return jnp.tile(cos, (1, 2)), jnp.tile(sin_signed, (1, 2))


def kernel(x, w_in, b_in, lambda_q1, lambda_k1, lambda_q2, lambda_k2, subln_g, rpb, w_branch_da, w_branch_na, w_out, ln1_g, ln1_b, w_router, b_router, w_mlp1, b_mlp1, w_mlp2, b_mlp2, ln2_g, ln2_b):
    B, S, D = x.shape
    depth = w_in.shape[0]
    n_experts = w_router.shape[2]
    T = B * S
    rows = S // GRID_W
    assert S % GRID_W == 0 and rows % NA_ROWS == 0 and rows >= NA_BAND
    alpha = (2 * depth) ** 0.25
    cos, sin = _rope_tables(S)
    row = lambda v: v.reshape(1, -1)

    for l in range(depth):
        lam_init = 0.8 - 0.6 * math.exp(-0.3 * l)
        qT, k, vT, qn, kn, vn, gates_br = _in_proj(x, w_in[l].astype(BF16), row(b_in[l]), cos, sin)
        a = _diff_attn(qT, k, vT, row(lambda_q1[l]), row(lambda_k1[l]), row(lambda_q2[l]),
                       row(lambda_k2[l]), subln_g[l].reshape(-1, 1), lam_init)
        nb = _na_attn(qn, kn, vn, _na_bias(rpb[l], rows))

        wr = jnp.zeros((D, LANES), F32).at[:, :n_experts].set(w_router[l])
        br = jnp.zeros((1, LANES), F32).at[0, :n_experts].set(b_router[l])
        x1, x1b, logits = _merge(
            x.reshape(T, D), a.reshape(T, -1), nb.reshape(T, -1), gates_br.reshape(T, -1),
            w_branch_da[l].astype(BF16), w_branch_na[l].astype(BF16), w_out[l].astype(BF16),
            row(ln1_g[l]), row(ln1_b[l]), wr, br, alpha)

        gates, buf_tok, block_expert, n_used, dest_orig = _route(logits[:, :n_experts], n_experts)
        xs = jnp.take(jnp.concatenate([x1b, jnp.zeros((1, D), BF16)], axis=0), buf_tok, axis=0)
        ys = _moe_ffn(xs, block_expert, n_used, w_mlp1[l], _glu_group_bias(b_mlp1[l]),
                      w_mlp2[l], b_mlp2[l][:, None, :])
        y_slots = jnp.take(ys, dest_orig, axis=0).reshape(T, TOP_K * D)
        gates_pad = jnp.zeros((T, LANES), F32).at[:, :TOP_K].set(gates)
        x = _combine(x1, y_slots, gates_pad, row(ln2_g[l]), row(ln2_b[l]), alpha).reshape(B, S, D)
    return x
```

```python
import functools
import math

import jax
import jax.numpy as jnp
from jax import lax
from jax.experimental import pallas as pl
from jax.experimental.pallas import tpu as pltpu

F32 = jnp.float32
BF16 = jnp.bfloat16

GRID_W = 64
DA_HEADS = 4
DA_HEAD_DIM = 64
DA_V_DIM = 2 * DA_HEAD_DIM
DA_WIDTH = DA_HEADS * DA_V_DIM
ROPE_THETA = 10000.0
NA_HEADS = 8
NA_HEAD_DIM = 64
NA_WIDTH = NA_HEADS * NA_HEAD_DIM
NA_KH = 8
NA_KW = 16
TOP_K = 4
SWIGLU_ALPHA = 1.702
SWIGLU_LIMIT = 7.0
MOE_BLOCK = 256
LN_EPS = 1e-5
RMS_EPS = 1e-5

LANES = 128
VMEM_LIMIT_BYTES = 56 * 1024 * 1024

LOG2E = math.log2(math.e)
NEG_BIG = -1e30

PROJ_TM = 512
DA_TQ = 256
DA_TK = 512
NA_ROWS = 8
NA_BAND = 16


def _cparams(sem):
    return pltpu.CompilerParams(dimension_semantics=sem, vmem_limit_bytes=VMEM_LIMIT_BYTES)


def _in_proj_kernel(x_ref, w_ref, b_ref, cos_ref, sin_ref,
                    qT_ref, k_ref, vT_ref, qn_ref, kn_ref, vn_ref, g_ref):
    xb = x_ref[...].astype(BF16)

    def seg(lo, hi):
        return jnp.dot(xb, w_ref[:, lo:hi], preferred_element_type=F32) + b_ref[:, lo:hi]

    cos = cos_ref[...]
    sin = sin_ref[...]
    lane = lax.broadcasted_iota(jnp.int32, cos.shape, 1)
    first_half = (lane % DA_HEAD_DIM) < (DA_HEAD_DIM // 2)

    def rope(y):
        outs = []
        for h in range(DA_HEADS):
            yh = y[:, h * LANES:(h + 1) * LANES]
            partner = jnp.where(first_half,
                                pltpu.roll(yh, LANES - DA_HEAD_DIM // 2, 1),
                                pltpu.roll(yh, DA_HEAD_DIM // 2, 1))
            outs.append(yh * cos + partner * sin)
        return jnp.concatenate(outs, axis=1)

    w = DA_WIDTH
    q = rope(seg(0, w)) * (DA_HEAD_DIM ** -0.5 * LOG2E)
    qT_ref[...] = q.T.astype(BF16)
    k_ref[...] = rope(seg(w, 2 * w)).astype(BF16)
    vT_ref[...] = seg(2 * w, 3 * w).T.astype(BF16)
    o = 3 * w
    qn_ref[...] = (seg(o, o + NA_WIDTH) * (NA_HEAD_DIM ** -0.5)).astype(BF16)
    kn_ref[...] = seg(o + NA_WIDTH, o + 2 * NA_WIDTH).astype(BF16)
    vn_ref[...] = seg(o + 2 * NA_WIDTH, o + 3 * NA_WIDTH).astype(BF16)
    g0 = o + 3 * NA_WIDTH
    gate_pre = seg(g0, w_ref.shape[1])
    g_ref[...] = (1.0 / (1.0 + jnp.exp(-gate_pre))).astype(BF16)


def _in_proj(x, w_in, b_in, cos, sin):
    B, S, D = x.shape
    tm = min(PROJ_TM, S)
    n_cols = w_in.shape[1]
    n_gate = n_cols - 3 * DA_WIDTH - 3 * NA_WIDTH
    tok = lambda width: pl.BlockSpec((None, tm, width), lambda b, i: (b, i, 0))
    tr = pl.BlockSpec((None, DA_WIDTH, tm), lambda b, i: (b, 0, i))
    const = lambda shape: pl.BlockSpec(shape, lambda b, i: (0,) * len(shape))
    out_shape = (
        jax.ShapeDtypeStruct((B, DA_WIDTH, S), BF16),
        jax.ShapeDtypeStruct((B, S, DA_WIDTH), BF16),
        jax.ShapeDtypeStruct((B, DA_WIDTH, S), BF16),
        jax.ShapeDtypeStruct((B, S, NA_WIDTH), BF16),
        jax.ShapeDtypeStruct((B, S, NA_WIDTH), BF16),
        jax.ShapeDtypeStruct((B, S, NA_WIDTH), BF16),
        jax.ShapeDtypeStruct((B, S, n_gate), BF16),
    )
    return pl.pallas_call(
        _in_proj_kernel,
        grid=(B, S // tm),
        in_specs=[tok(D), const((D, n_cols)), const((1, n_cols)),
                  pl.BlockSpec((tm, LANES), lambda b, i: (i, 0)),
                  pl.BlockSpec((tm, LANES), lambda b, i: (i, 0))],
        out_specs=(tr, tok(DA_WIDTH), tr, tok(NA_WIDTH), tok(NA_WIDTH), tok(NA_WIDTH), tok(n_gate)),
        out_shape=out_shape,
        compiler_params=_cparams(("parallel", "parallel")),
        name="in_proj",
    )(x, w_in, b_in, cos, sin)


def _diff_attn_kernel(qT_ref, k_ref, vT_ref, lq1_ref, lk1_ref, lq2_ref, lk2_ref, g_ref,
                      o_ref, m_sc, l_sc, acc_sc, s_sc, *, tk, lam_init):
    qT = qT_ref[...]
    tq = qT.shape[1]
    row = lax.broadcasted_iota(jnp.int32, qT.shape, 0)
    zero = jnp.zeros_like(qT)
    qbd = jnp.concatenate([jnp.where(row < DA_HEAD_DIM, qT, zero),
                           jnp.where(row < DA_HEAD_DIM, zero, qT)], axis=1)

    m_sc[...] = jnp.full(m_sc.shape, -jnp.inf, F32)
    l_sc[...] = jnp.zeros(l_sc.shape, F32)
    acc_sc[...] = jnp.zeros(acc_sc.shape, F32)

    nk = k_ref.shape[0] // tk

    def scores(j):
        off = pl.multiple_of(j * tk, tk)
        return jnp.dot(k_ref[pl.ds(off, tk), :], qbd, preferred_element_type=F32)

    def accumulate(j, s):
        off = pl.multiple_of(j * tk, tk)
        m_prev = m_sc[...]
        m_new = jnp.maximum(m_prev, jnp.max(s, axis=0, keepdims=True))
        alpha = jnp.exp2(m_prev - m_new)
        p = jnp.exp2(s - m_new)
        l_sc[...] = alpha * l_sc[...] + jnp.sum(p, axis=0, keepdims=True)
        vb = vT_ref[:, pl.ds(off, tk)]
        acc_sc[...] = alpha * acc_sc[...] + jnp.dot(vb, p.astype(BF16), preferred_element_type=F32)
        m_sc[...] = m_new

    s_sc[0] = scores(0)

    def body(jj, carry):
        j = 2 * jj
        s_sc[1] = scores(j + 1)
        accumulate(j, s_sc[0])
        s_sc[0] = scores(jnp.minimum(j + 2, nk - 1))
        accumulate(j + 1, s_sc[1])
        return carry

    lax.fori_loop(0, nk // 2, body, 0)

    lam = (jnp.exp(jnp.sum(lq1_ref[...] * lk1_ref[...], axis=1, keepdims=True))
           - jnp.exp(jnp.sum(lq2_ref[...] * lk2_ref[...], axis=1, keepdims=True)) + lam_init)
    on = acc_sc[...] / l_sc[...]
    o = on[:, :tq] - lam * on[:, tq:]
    ms = jnp.mean(o * o, axis=0, keepdims=True)
    o = o * lax.rsqrt(ms + RMS_EPS) * g_ref[...]
    o = o * (1.0 - lam_init)
    o_ref[...] = o.T.astype(BF16)


def _diff_attn(qT, k, vT, lq1, lk1, lq2, lk2, subln_g, lam_init):
    B, S, _ = k.shape
    tq = min(DA_TQ, S)
    tk = min(DA_TK, S // 2)
    assert S % (2 * tk) == 0 and S % tq == 0
    vec =pl.BlockSpec((1, DA_HEAD_DIM), lambda b, h, i: (0, 0))
    kernel = functools.partial(_diff_attn_kernel, tk=tk, lam_init=lam_init)
    return pl.pallas_call(
        kernel,
        grid=(B, DA_HEADS, S // tq),
        in_specs=[pl.BlockSpec((None, DA_V_DIM, tq), lambda b, h, i: (b, h, i)),
                  pl.BlockSpec((None, S, DA_V_DIM), lambda b, h, i: (b, 0, h)),
                  pl.BlockSpec((None, DA_V_DIM, S), lambda b, h, i: (b, h, 0)),
                  vec, vec, vec, vec,
                  pl.BlockSpec((DA_V_DIM, 1), lambda b, h, i: (0, 0))],
        out_specs=pl.BlockSpec((None, tq, DA_V_DIM), lambda b, h, i: (b, i, h)),
        out_shape=jax.ShapeDtypeStruct((B, S, DA_WIDTH), BF16),
        scratch_shapes=[pltpu.VMEM((1, 2 * tq), F32), pltpu.VMEM((1, 2 * tq), F32),
                        pltpu.VMEM((DA_V_DIM, 2 * tq), F32), pltpu.VMEM((2, tk, 2 * tq), F32)],
        compiler_params=_cparams(("parallel", "parallel", "arbitrary")),
        name="diff_attn",
    )(qT, k, vT, lq1, lk1, lq2, lk2, subln_g)


def _na_band_start(r0, rows):
    return jnp.clip(r0 - NA_KH // 2, 0, rows - NA_BAND)


def _na_bias(rpb, rows):
    nblk = rows // NA_ROWS
    r0 = jnp.array([0, NA_ROWS * min(1, nblk - 1), NA_ROWS * (nblk - 1)], jnp.int32)
    band = _na_band_start(r0, rows)
    qr = r0[:, None] + jnp.arange(NA_ROWS)[None, :]
    rs = jnp.clip(qr - NA_KH // 2, 0, rows - NA_KH)
    kr = band[:, None] + jnp.arange(NA_BAND)[None, :]
    row_ok = (kr[:, None, :] >= rs[:, :, None]) & (kr[:, None, :] < rs[:, :, None] + NA_KH)
    row_off = jnp.clip(kr[:, None, :] - qr[:, :, None] + (NA_KH - 1), 0, 2 * NA_KH - 2)
    cols = jnp.arange(GRID_W)
    cs = jnp.clip(cols - NA_KW // 2, 0, GRID_W - NA_KW)
    col_ok = (cols[None, :] >= cs[:, None]) & (cols[None, :] < cs[:, None] + NA_KW)
    col_off = jnp.clip(cols[None, :] - cols[:, None] + (NA_KW - 1), 0, 2 * NA_KW - 2)
    tiles = jnp.where(col_ok[None, None], rpb.astype(F32)[:, :, col_off], NEG_BIG)
    sel = jnp.where(row_ok[None, :, :, :, None, None], tiles[:, row_off], NEG_BIG)
    nq, nk = NA_ROWS * GRID_W, NA_BAND * GRID_W
    sel = sel.reshape(NA_HEADS // 2, 2, 3, NA_ROWS, NA_BAND, GRID_W, GRID_W)
    sel = sel.transpose(2, 0, 3, 5, 1, 4, 6)
    return sel.reshape(3, NA_HEADS // 2, nq, 2 * nk)


def _na_kernel(q_ref, k_ref, v_ref, bias_ref, o_ref, *, rows):
    i = pl.program_id(2)
    nk = NA_BAND * GRID_W
    off = pl.multiple_of(_na_band_start(i * NA_ROWS, rows) * GRID_W, GRID_W)
    kb = k_ref[pl.ds(off, nk), :]
    vb = v_ref[pl.ds(off, nk), :]
    lane = lax.broadcasted_iota(jnp.int32, kb.shape, 1)
    head0 = lane < NA_HEAD_DIM
    zero = jnp.zeros_like(kb)
    kbd = jnp.concatenate([jnp.where(head0, kb, zero), jnp.where(head0, zero, kb)], axis=0)
    vbd = jnp.concatenate([jnp.where(head0, vb, zero), jnp.where(head0, zero, vb)], axis=0)
    q = q_ref[...]
    s = lax.dot_general(q, kbd, (((1,), (1,)), ((), ())), preferred_element_type=F32)
    s = s + bias_ref[...]
    ps, ls = [], []
    for h in range(2):
        sh = s[:, h * nk:(h + 1) * nk]
        ph = jnp.exp(sh - jnp.max(sh, axis=1, keepdims=True))
        ps.append(ph.astype(BF16))
        ls.append(jnp.sum(ph, axis=1, keepdims=True))
    p = jnp.concatenate(ps, axis=1)
    o = jnp.dot(p, vbd, preferred_element_type=F32)
    lane_o = lax.broadcasted_iota(jnp.int32, o.shape, 1)
    denom = jnp.where(lane_o < NA_HEAD_DIM, ls[0], ls[1])
    o_ref[...] = (o / denom).astype(BF16)


def _na_attn(qn, kn, vn, bias):
    B, S, _ = qn.shape
    rows = S // GRID_W
    nblk = rows // NA_ROWS
    nq = NA_ROWS * GRID_W

    def variant(i):
        return jnp.where(i == 0, 0, jnp.where(i == nblk - 1, 2, 1))

    return pl.pallas_call(
        functools.partial(_na_kernel, rows=rows),
        grid=(B, NA_HEADS // 2, nblk),
        in_specs=[pl.BlockSpec((None, nq, LANES), lambda b, h, i: (b, i, h)),
                  pl.BlockSpec((None, S, LANES), lambda b, h, i: (b, 0, h)),
                  pl.BlockSpec((None, S, LANES), lambda b, h, i: (b, 0, h)),
                  pl.BlockSpec((None, None, nq, bias.shape[3]), lambda b, h, i: (variant(i), h, 0, 0))],
        out_specs=pl.BlockSpec((None, nq, LANES), lambda b, h, i: (b, i, h)),
        out_shape=jax.ShapeDtypeStruct((B, S, NA_WIDTH), BF16),
        compiler_params=_cparams(("parallel", "parallel", "arbitrary")),
        name="na_attn",
    )(qn, kn, vn, bias)


def _layer_norm(h, g, b):
    mu = jnp.mean(h, axis=-1, keepdims=True)
    d = h - mu
    var = jnp.mean(d * d, axis=-1, keepdims=True)
    return d * lax.rsqrt(var + LN_EPS) * g + b


def _merge_kernel(x_ref, a_ref, nb_ref, g_ref, wda_ref, wna_ref, wout_ref, lng_ref, lnb_ref,
                  wr_ref, br_ref, x1_ref, x1b_ref, topi_ref, gate_ref, *, alpha):
    d = x_ref.shape[1]
    ya = jnp.dot(a_ref[...], wda_ref[...], preferred_element_type=F32)
    yb = jnp.dot(nb_ref[...], wna_ref[...], preferred_element_type=F32)
    g = g_ref[...].astype(F32)
    merged = g[:, :d] * ya + g[:, d:] * yb
    mix = jnp.dot(merged.astype(BF16), wout_ref[...], preferred_element_type=F32)
    x1 = _layer_norm(alpha * x_ref[...] + mix, lng_ref[...], lnb_ref[...])
    x1_ref[...] = x1
    x1b_ref[...] = x1.astype(BF16)
    logits = jnp.dot(x1, wr_ref[...], preferred_element_type=F32,
                     precision=lax.Precision.HIGHEST) + br_ref[...]
    lane = lax.broadcasted_iota(jnp.int32, logits.shape, 1).astype(F32)
    vals, idxs = [], []
    for _ in range(TOP_K):
        m = jnp.max(logits, axis=1, keepdims=True)
        idx = jnp.min(jnp.where(logits == m, lane, float(LANES)), axis=1, keepdims=True)
        vals.append(m)
        idxs.append(idx)
        logits = jnp.where(lane == idx, -jnp.inf, logits)
    es = [jnp.exp(v - vals[0]) for v in vals]
    denom = es[0] + es[1] + es[2] + es[3]
    gates = jnp.zeros(logits.shape, F32)
    topi = jnp.zeros(logits.shape, F32)
    for k in range(TOP_K):
        gates = jnp.where(lane == k, es[k] / denom, gates)
        topi = jnp.where(lane == k, idxs[k], topi)
    gate_ref[...] = gates
    topi_ref[...] = topi.astype(jnp.int32)


def _merge(x2, a2, nb2, g2, wda, wna, wout, ln_g, ln_b, wr, br, alpha):
    T, D = x2.shape
    tm = min(PROJ_TM, T)
    tok = lambda width: pl.BlockSpec((tm, width), lambda i: (i, 0))
    const = lambda shape: pl.BlockSpec(shape, lambda i: (0,) * len(shape))
    return pl.pallas_call(
        functools.partial(_merge_kernel, alpha=alpha),
        grid=(T // tm,),
        in_specs=[tok(D), tok(a2.shape[1]), tok(nb2.shape[1]), tok(g2.shape[1]),
                  const(wda.shape), const(wna.shape), const(wout.shape),
                  const((1, D)), const((1, D)), const(wr.shape), const((1, LANES))],
        out_specs=(tok(D), tok(D), tok(LANES), tok(LANES)),
        out_shape=(jax.ShapeDtypeStruct((T, D), F32), jax.ShapeDtypeStruct((T, D), BF16),
                   jax.ShapeDtypeStruct((T, LANES), jnp.int32), jax.ShapeDtypeStruct((T, LANES), F32)),
        compiler_params=_cparams(("parallel",)),
        name="merge",
    )(x2, a2, nb2, g2, wda, wna, wout, ln_g, ln_b, wr, br)


def _glu_group_bias(b1):
    e, n = b1.shape
    return b1.reshape(e, n // (2 * LANES), LANES, 2).transpose(0, 1, 3, 2).reshape(e, 1, n)


def _moe_kernel(be_ref, nused_ref, xs_ref, w1_ref, b1_ref, w2_ref, b2_ref, ys_ref, w1g_sc, w2b_sc):
    i = pl.program_id(0)
    used = i < nused_ref[0]
    fresh = jnp.logical_or(i == 0, be_ref[i] != be_ref[jnp.maximum(i - 1, 0)])
    grp = 2 * LANES

    @pl.when(jnp.logical_and(used, fresh))
    def _():
        r = lax.broadcasted_iota(jnp.int32, (grp, grp), 0)
        c = lax.broadcasted_iota(jnp.int32, (grp, grp), 1)
        src = jnp.where(c < LANES, 2 * c, 2 * (c - LANES) + 1)
        perm = jnp.where(r == src, 1.0, 0.0).astype(BF16)
        for g in range(w1_ref.shape[1] // grp):
            blk = w1_ref[:, g * grp:(g + 1) * grp].astype(BF16)
            w1g_sc[:, g * grp:(g + 1) * grp] = jnp.dot(blk, perm, preferred_element_type=F32).astype(BF16)
        w2b_sc[...] = w2_ref[...].astype(BF16)

    @pl.when(used)
    def _():
        xb = xs_ref[...]
        h = jnp.dot(xb, w1g_sc[...], preferred_element_type=F32) + b1_ref[...]
        acts = []
        for g in range(h.shape[1] // grp):
            x_glu = jnp.minimum(h[:, g * grp:g * grp + LANES], SWIGLU_LIMIT)
            x_lin = jnp.clip(h[:, g * grp + LANES:(g + 1) * grp], -SWIGLU_LIMIT, SWIGLU_LIMIT)
            acts.append(x_glu * (1.0 / (1.0 + jnp.exp(-SWIGLU_ALPHA * x_glu))) * (x_lin + 1.0))
        act = jnp.concatenate(acts, axis=1).astype(BF16)
        y = jnp.dot(act, w2b_sc[...], preferred_element_type=F32) + b2_ref[...]
        ys_ref[...] = y.astype(ys_ref.dtype)

    @pl.when(jnp.logical_not(used))
    def _():
        ys_ref[...] = jnp.zeros(ys_ref.shape, ys_ref.dtype)


def _moe_ffn(xs, block_expert, n_used, w1, b1g, w2, b2):
    P, D = xs.shape
    n_blocks = P // MOE_BLOCK
    de2 = w1.shape[2]
    de = w2.shape[1]
    wspec = lambda r, c: pl.BlockSpec((None, r, c), lambda i, be, nu: (be[i], 0, 0))
    grid_spec = pltpu.PrefetchScalarGridSpec(
        num_scalar_prefetch=2,
        grid=(n_blocks,),
        in_specs=[pl.BlockSpec((MOE_BLOCK, D), lambda i, be, nu: (i, 0)),
                  wspec(D, de2), wspec(1, de2), wspec(de, D), wspec(1, D)],
        out_specs=pl.BlockSpec((MOE_BLOCK, D), lambda i, be, nu: (i, 0)),
        scratch_shapes=[pltpu.VMEM((D, de2), BF16), pltpu.VMEM((de, D), BF16)],
    )
    return pl.pallas_call(
        _moe_kernel,
        grid_spec=grid_spec,
        out_shape=jax.ShapeDtypeStruct((P, D), BF16),
        compiler_params=_cparams(("arbitrary",)),
        name="moe_ffn",
    )(block_expert, n_used, xs, w1, b1g, w2, b2)


def _combine_kernel(x1_ref, y0_ref, y1_ref, y2_ref, y3_ref, gate_ref, lng_ref, lnb_ref, o_ref, *, alpha):
    gates = gate_ref[...]
    ffn = jnp.zeros(x1_ref.shape, F32)
    for k, y_ref in enumerate((y0_ref, y1_ref, y2_ref, y3_ref)):
        ffn = ffn + y_ref[...].astype(F32) * gates[:, k:k + 1]
    o_ref[...] = _layer_norm(alpha * x1_ref[...] + ffn, lng_ref[...], lnb_ref[...])


def _combine(x1, y_slots, gates, ln_g, ln_b, alpha):
    T, D = x1.shape
    tm = min(PROJ_TM, T)
    tok = lambda width: pl.BlockSpec((tm, width), lambda i: (i, 0))
    const = lambda shape: pl.BlockSpec(shape, lambda i: (0,) * len(shape))
    slot = lambda k: pl.BlockSpec((None, tm, D), lambda i: (k, i, 0))
    return pl.pallas_call(
        functools.partial(_combine_kernel, alpha=alpha),
        grid=(T // tm,),
        in_specs=[tok(D)] + [slot(k) for k in range(TOP_K)] + [tok(LANES), const((1, D)), const((1, D))],
        out_specs=tok(D),
        out_shape=jax.ShapeDtypeStruct((T, D), F32),
        compiler_params=_cparams(("parallel",)),
        name="combine",
    )(x1, y_slots, y_slots, y_slots, y_slots, gates, ln_g, ln_b)


def _route(top_i, n_experts):
    T = top_i.shape[0]
    TK = T * TOP_K
    experts = jnp.arange(n_experts, dtype=jnp.int32)
    chosen = top_i[:, :, None] == experts
    picks = jnp.sum(chosen, axis=1, dtype=jnp.int32)
    csum = jnp.cumsum(picks, axis=0)
    counts = csum[-1]
    padded = ((counts + MOE_BLOCK - 1) // MOE_BLOCK) * MOE_BLOCK
    padded_end = jnp.cumsum(padded)
    padded_start = padded_end - padded
    group_start = jnp.cumsum(counts) - counts
    slot_te = padded_start[None, :] + csum - picks
    dest = jnp.sum(jnp.where(chosen, slot_te[:, None, :], 0), axis=2)
    n_blocks = -(-TK // MOE_BLOCK) + n_experts
    blk_first = jnp.arange(n_blocks, dtype=jnp.int32) * MOE_BLOCK
    block_expert = jnp.minimum(jnp.sum(padded_end[None, :] <= blk_first[:, None], axis=1),
                               n_experts - 1).astype(jnp.int32)
    n_used = (padded_end[-1] // MOE_BLOCK).astype(jnp.int32).reshape(1)
    order = jnp.argsort(top_i.reshape(-1), stable=True).astype(jnp.int32)
    rank0 = blk_first - padded_start[block_expert]
    rank = rank0[:, None] + jnp.arange(MOE_BLOCK, dtype=jnp.int32)[None, :]
    valid = rank < counts[block_expert][:, None]
    sorted_pos = jnp.clip(group_start[block_expert][:, None] + rank, 0, TK - 1)
    src_tok = jnp.where(valid, order[sorted_pos] // TOP_K, 0).reshape(-1)
    return src_tok, block_expert, n_used, dest.T.reshape(-1)


def _rope_tables(seq):
    pos = jnp.arange(seq, dtype=F32)
    inv = ROPE_THETA ** (-jnp.arange(0, DA_HEAD_DIM, 2, dtype=F32) / DA_HEAD_DIM)
    ang = pos[:, None] * inv[None, :]
    ang = jnp.concatenate([ang, ang], axis=-1)
    cos, sin = jnp.cos(ang), jnp.sin(ang)
    half = DA_HEAD_DIM // 2
    sin_signed = jnp.concatenate([-sin[:, :half], sin[:, half:]], axis=-1)
    return jnp.tile(cos, (1, 2)), jnp.tile(sin_signed, (1, 2))


def kernel(x, w_in, b_in, lambda_q1, lambda_k1, lambda_q2, lambda_k2, subln_g, rpb, w_branch_da, w_branch_na, w_out, ln1_g, ln1_b, w_router, b_router, w_mlp1, b_mlp1, w_mlp2, b_mlp2, ln2_g, ln2_b):
    B, S, D = x.shape
    depth = w_in.shape[0]
    n_experts = w_router.shape[2]
    T = B * S
    rows = S // GRID_W
    assert S % GRID_W == 0 and rows % NA_ROWS == 0 and rows >= NA_BAND
    alpha = (2 * depth) ** 0.25
    cos, sin = _rope_tables(S)
    row = lambda v: v.reshape(1, -1)

    for l in range(depth):
        lam_init = 0.8 - 0.6 * math.exp(-0.3 * l)
        qT, k, vT, qn, kn, vn, gates_br = _in_proj(x, w_in[l].astype(BF16), row(b_in[l]), cos, sin)
        a = _diff_attn(qT, k, vT, row(lambda_q1[l]), row(lambda_k1[l]), row(lambda_q2[l]),
                       row(lambda_k2[l]), subln_g[l].reshape(-1, 1), lam_init)
        nb = _na_attn(qn, kn, vn, _na_bias(rpb[l], rows))

        wr = jnp.pad(w_router[l], ((0, 0), (0, LANES - n_experts)))
        br = jnp.pad(row(b_router[l]), ((0, 0), (0, LANES - n_experts)), constant_values=NEG_BIG)
        x1, x1b, topi, gates = _merge(
            x.reshape(T, D), a.reshape(T, -1), nb.reshape(T, -1), gates_br.reshape(T, -1),
            w_branch_da[l].astype(BF16), w_branch_na[l].astype(BF16), w_out[l].astype(BF16),
            row(ln1_g[l]), row(ln1_b[l]), wr, br, alpha)

        src_tok, block_expert, n_used, dest = _route(topi[:, :TOP_K], n_experts)
        xs = jnp.take(x1b, src_tok, axis=0)
        ys = _moe_ffn(xs, block_expert, n_used, w_mlp1[l], _glu_group_bias(b_mlp1[l]),
                      w_mlp2[l], b_mlp2[l][:, None, :])
        y_slots = jnp.take(ys, dest, axis=0).reshape(TOP_K, T, D)
        x = _combine(x1, y_slots, gates, row(ln2_g[l]), row(ln2_b[l]), alpha).reshape(B, S, D)
    return x
```

```python
import functools
import math

import jax
import jax.numpy as jnp
from jax import lax
from jax.experimental import pallas as pl
from jax.experimental.pallas import tpu as pltpu

F32 = jnp.float32
BF16 = jnp.bfloat16

GRID_W = 64
DA_HEADS = 4
DA_HEAD_DIM = 64
DA_V_DIM = 2 * DA_HEAD_DIM
DA_WIDTH = DA_HEADS * DA_V_DIM
ROPE_THETA = 10000.0
NA_HEADS = 8
NA_HEAD_DIM = 64
NA_WIDTH = NA_HEADS * NA_HEAD_DIM
NA_KH = 8
NA_KW = 16
TOP_K = 4
SWIGLU_ALPHA = 1.702
SWIGLU_LIMIT = 7.0
MOE_BLOCK = 256
LN_EPS = 1e-5
RMS_EPS = 1e-5

LANES = 128
VMEM_LIMIT_BYTES = 56 * 1024 * 1024

LOG2E = math.log2(math.e)
NEG_BIG = -1e30

PROJ_TM = 512
DA_TQ = 256
DA_TK = 512
DA_UNROLL = 8
NA_ROWS = 8
NA_BAND = 16


def _cparams(sem):
    return pltpu.CompilerParams(dimension_semantics=sem, vmem_limit_bytes=VMEM_LIMIT_BYTES)


def _in_proj_kernel(x_ref, w_ref, b_ref, cos_ref, sin_ref,
                    qT_ref, k_ref, vT_ref, qn_ref, kn_ref, vn_ref, g_ref):
    xb = x_ref[...].astype(BF16)

    def seg(lo, hi):
        return jnp.dot(xb, w_ref[:, lo:hi], preferred_element_type=F32) + b_ref[:, lo:hi]

    cos = cos_ref[...]
    sin = sin_ref[...]
    lane = lax.broadcasted_iota(jnp.int32, cos.shape, 1)
    first_half = (lane % DA_HEAD_DIM) < (DA_HEAD_DIM // 2)

    def rope(y):
        outs = []
        for h in range(DA_HEADS):
            yh = y[:, h * LANES:(h + 1) * LANES]
            partner = jnp.where(first_half,
                                pltpu.roll(yh, LANES - DA_HEAD_DIM // 2, 1),
                                pltpu.roll(yh, DA_HEAD_DIM // 2, 1))
            outs.append(yh * cos + partner * sin)
        return jnp.concatenate(outs, axis=1)

    w = DA_WIDTH
    q = rope(seg(0, w)) * (DA_HEAD_DIM ** -0.5 * LOG2E)
    qT_ref[...] = q.T.astype(BF16)
    k_ref[...] = rope(seg(w, 2 * w)).astype(BF16)
    vT_ref[...] = seg(2 * w, 3 * w).T.astype(BF16)
    o = 3 * w
    qn_ref[...] = (seg(o, o + NA_WIDTH) * (NA_HEAD_DIM ** -0.5)).astype(BF16)
    kn_ref[...] = seg(o + NA_WIDTH, o + 2 * NA_WIDTH).astype(BF16)
    vn_ref[...] = seg(o + 2 * NA_WIDTH, o + 3 * NA_WIDTH).astype(BF16)
    g0 = o + 3 * NA_WIDTH
    gate_pre = seg(g0, w_ref.shape[1])
    g_ref[...] = (1.0 / (1.0 + jnp.exp(-gate_pre))).astype(BF16)


def _in_proj(x, w_in, b_in, cos, sin):
    B, S, D = x.shape
    tm = min(PROJ_TM, S)
    n_cols = w_in.shape[1]
    n_gate = n_cols - 3 * DA_WIDTH - 3 * NA_WIDTH
    tok = lambda width: pl.BlockSpec((None, tm, width), lambda b, i: (b, i, 0))
    tr = pl.BlockSpec((None, DA_WIDTH, tm), lambda b, i: (b, 0, i))
    const = lambda shape: pl.BlockSpec(shape, lambda b, i: (0,) * len(shape))
    out_shape = (
        jax.ShapeDtypeStruct((B, DA_WIDTH, S), BF16),
        jax.ShapeDtypeStruct((B, S, DA_WIDTH), BF16),
        jax.ShapeDtypeStruct((B, DA_WIDTH, S), BF16),
        jax.ShapeDtypeStruct((B, S, NA_WIDTH), BF16),
        jax.ShapeDtypeStruct((B, S, NA_WIDTH), BF16),
        jax.ShapeDtypeStruct((B, S, NA_WIDTH), BF16),
        jax.ShapeDtypeStruct((B, S, n_gate), BF16),
    )
    return pl.pallas_call(
        _in_proj_kernel,
        grid=(B, S // tm),
        in_specs=[tok(D), const((D, n_cols)), const((1, n_cols)),
                  pl.BlockSpec((tm, LANES), lambda b, i: (i, 0)),
                  pl.BlockSpec((tm, LANES), lambda b, i: (i, 0))],
        out_specs=(tr, tok(DA_WIDTH), tr, tok(NA_WIDTH), tok(NA_WIDTH), tok(NA_WIDTH), tok(n_gate)),
        out_shape=out_shape,
        compiler_params=_cparams(("parallel", "parallel")),
        name="in_proj",
    )(x, w_in, b_in, cos, sin)


def _diff_attn_kernel(qT_ref, k_ref, vT_ref, lq1_ref, lk1_ref, lq2_ref, lk2_ref, g_ref,
                      o_ref, m_sc, l_sc, acc_sc, s_sc, *, tk, unroll, lam_init):
    qT = qT_ref[...]
    tq = qT.shape[1]
    row = lax.broadcasted_iota(jnp.int32, qT.shape, 0)
    zero = jnp.zeros_like(qT)
    qbd = jnp.concatenate([jnp.where(row < DA_HEAD_DIM, qT, zero),
                           jnp.where(row < DA_HEAD_DIM, zero, qT)], axis=1)

    m_sc[...] = jnp.full(m_sc.shape, -jnp.inf, F32)
    l_sc[...] = jnp.zeros(l_sc.shape, F32)
    acc_sc[...] = jnp.zeros(acc_sc.shape, F32)

    nk = k_ref.shape[0] // tk

    def scores(j):
        off = pl.multiple_of(j * tk, tk)
        return jnp.dot(k_ref[pl.ds(off, tk), :], qbd, preferred_element_type=F32)

    def accumulate(j, s):
        off = pl.multiple_of(j * tk, tk)
        m_prev = m_sc[...]
        m_new = jnp.maximum(m_prev, jnp.max(s, axis=0, keepdims=True))
        alpha = jnp.exp2(m_prev - m_new)
        p = jnp.exp2(s - m_new)
        l_sc[...] = alpha * l_sc[...] + jnp.sum(p, axis=0, keepdims=True)
        vb = vT_ref[:, pl.ds(off, tk)]
        acc_sc[...] = alpha * acc_sc[...] + jnp.dot(vb, p.astype(BF16), preferred_element_type=F32)
        m_sc[...] = m_new

    s_sc[0] = scores(0)

    def body(jj, carry):
        for u in range(unroll):
            j = unroll * jj + u
            s_sc[(u + 1) % 2] = scores(jnp.minimum(j + 1, nk - 1))
            accumulate(j, s_sc[u % 2])
        return carry

    lax.fori_loop(0, nk // unroll, body, 0)

    lam = (jnp.exp(jnp.sum(lq1_ref[...] * lk1_ref[...], axis=1, keepdims=True))
           - jnp.exp(jnp.sum(lq2_ref[...] * lk2_ref[...], axis=1, keepdims=True)) + lam_init)
    on = acc_sc[...] / l_sc[...]
    o = on[:, :tq] - lam * on[:, tq:]
    ms = jnp.mean(o * o, axis=0, keepdims=True)
    o = o * lax.rsqrt(ms + RMS_EPS) * g_ref[...]
    o = o * (1.0 - lam_init)
    o_ref[...] = o.T.astype(BF16)


def _diff_attn(qT, k, vT, lq1, lk1, lq2, lk2, subln_g, lam_init):
    B, S, _ = k.shape
    tq = min(DA_TQ, S)
    tk = min(DA_TK, S // 2)
    unroll = math.gcd(DA_UNROLL, S // tk)
    assert unroll % 2 == 0 and S % tk == 0 and S % tq == 0
    vec =pl.BlockSpec((1, DA_HEAD_DIM), lambda b, h, i: (0, 0))
    kernel = functools.partial(_diff_attn_kernel, tk=tk, unroll=unroll, lam_init=lam_init)
    return pl.pallas_call(
        kernel,
        grid=(B, DA_HEADS, S // tq),
        in_specs=[pl.BlockSpec((None, DA_V_DIM, tq), lambda b, h, i: (b, h, i)),
                  pl.BlockSpec((None, S, DA_V_DIM), lambda b, h, i: (b, 0, h)),
                  pl.BlockSpec((None, DA_V_DIM, S), lambda b, h, i: (b, h, 0)),
                  vec, vec, vec, vec,
                  pl.BlockSpec((DA_V_DIM, 1), lambda b, h, i: (0, 0))],
        out_specs=pl.BlockSpec((None, tq, DA_V_DIM), lambda b, h, i: (b, i, h)),
        out_shape=jax.ShapeDtypeStruct((B, S, DA_WIDTH), BF16),
        scratch_shapes=[pltpu.VMEM((1, 2 * tq), F32), pltpu.VMEM((1, 2 * tq), F32),
                        pltpu.VMEM((DA_V_DIM, 2 * tq), F32), pltpu.VMEM((2, tk, 2 * tq), F32)],
        compiler_params=_cparams(("parallel", "parallel", "arbitrary")),
        name="diff_attn",
    )(qT, k, vT, lq1, lk1, lq2, lk2, subln_g)


def _na_band_start(r0, rows):
    return jnp.clip(r0 - NA_KH // 2, 0, rows - NA_BAND)


def _na_bias(rpb, rows):
    nblk = rows // NA_ROWS
    r0 = jnp.array([0, NA_ROWS * min(1, nblk - 1), NA_ROWS * (nblk - 1)], jnp.int32)
    band = _na_band_start(r0, rows)
    qr = r0[:, None] + jnp.arange(NA_ROWS)[None, :]
    rs = jnp.clip(qr - NA_KH // 2, 0, rows - NA_KH)
    kr = band[:, None] + jnp.arange(NA_BAND)[None, :]
    row_ok = (kr[:, None, :] >= rs[:, :, None]) & (kr[:, None, :] < rs[:, :, None] + NA_KH)
    row_off = jnp.clip(kr[:, None, :] - qr[:, :, None] + (NA_KH - 1), 0, 2 * NA_KH - 2)
    cols = jnp.arange(GRID_W)
    cs = jnp.clip(cols - NA_KW // 2, 0, GRID_W - NA_KW)
    col_ok = (cols[None, :] >= cs[:, None]) & (cols[None, :] < cs[:, None] + NA_KW)
    col_off = jnp.clip(cols[None, :] - cols[:, None] + (NA_KW - 1), 0, 2 * NA_KW - 2)
    tiles = jnp.where(col_ok[None, None], rpb.astype(F32)[:, :, col_off], NEG_BIG)
    sel = jnp.where(row_ok[None, :, :, :, None, None], tiles[:, row_off], NEG_BIG)
    nq, nk = NA_ROWS * GRID_W, NA_BAND * GRID_W
    sel = sel.reshape(NA_HEADS // 2, 2, 3, NA_ROWS, NA_BAND, GRID_W, GRID_W)
    sel = sel.transpose(2, 0, 3, 5, 1, 4, 6)
    return sel.reshape(3, NA_HEADS // 2, nq, 2 * nk)


def _na_kernel(q_ref, k_ref, v_ref, bias_ref, o_ref, *, rows):
    i = pl.program_id(2)
    nk = NA_BAND * GRID_W
    off = pl.multiple_of(_na_band_start(i * NA_ROWS, rows) * GRID_W, GRID_W)
    kb = k_ref[pl.ds(off, nk), :]
    vb = v_ref[pl.ds(off, nk), :]
    lane = lax.broadcasted_iota(jnp.int32, kb.shape, 1)
    head0 = lane < NA_HEAD_DIM
    zero = jnp.zeros_like(kb)
    kbd = jnp.concatenate([jnp.where(head0, kb, zero), jnp.where(head0, zero, kb)], axis=0)
    vbd = jnp.concatenate([jnp.where(head0, vb, zero), jnp.where(head0, zero, vb)], axis=0)
    q = q_ref[...]
    s = lax.dot_general(q, kbd, (((1,), (1,)), ((), ())), preferred_element_type=F32)
    s = s + bias_ref[...]
    ps, ls = [], []
    for h in range(2):
        sh = s[:, h * nk:(h + 1) * nk]
        ph = jnp.exp(sh - jnp.max(sh, axis=1, keepdims=True))
        ps.append(ph.astype(BF16))
        ls.append(jnp.sum(ph, axis=1, keepdims=True))
    p = jnp.concatenate(ps, axis=1)
    o = jnp.dot(p, vbd, preferred_element_type=F32)
    lane_o = lax.broadcasted_iota(jnp.int32, o.shape, 1)
    denom = jnp.where(lane_o < NA_HEAD_DIM, ls[0], ls[1])
    o_ref[...] = (o / denom).astype(BF16)


def _na_attn(qn, kn, vn, bias):
    B, S, _ = qn.shape
    rows = S // GRID_W
    nblk = rows // NA_ROWS
    nq = NA_ROWS * GRID_W

    def variant(i):
        return jnp.where(i == 0, 0, jnp.where(i == nblk - 1, 2, 1))

    return pl.pallas_call(
        functools.partial(_na_kernel, rows=rows),
        grid=(B, NA_HEADS // 2, nblk),
        in_specs=[pl.BlockSpec((None, nq, LANES), lambda b, h, i: (b, i, h)),
                  pl.BlockSpec((None, S, LANES), lambda b, h, i: (b, 0, h)),
                  pl.BlockSpec((None, S, LANES), lambda b, h, i: (b, 0, h)),
                  pl.BlockSpec((None, None, nq, bias.shape[3]), lambda b, h, i: (variant(i), h, 0, 0))],
        out_specs=pl.BlockSpec((None, nq, LANES), lambda b, h, i: (b, i, h)),
        out_shape=jax.ShapeDtypeStruct((B, S, NA_WIDTH), BF16),
        compiler_params=_cparams(("parallel", "parallel", "arbitrary")),
        name="na_attn",
    )(qn, kn, vn, bias)


def _layer_norm(h, g, b):
    mu = jnp.mean(h, axis=-1, keepdims=True)
    d = h - mu
    var = jnp.mean(d * d, axis=-1, keepdims=True)
    return d * lax.rsqrt(var + LN_EPS) * g + b


def _merge_kernel(x_ref, a_ref, nb_ref, g_ref, wda_ref, wna_ref, wout_ref, lng_ref, lnb_ref,
                  wr_ref, br_ref, x1_ref, x1b_ref, topi_ref, gate_ref, *, alpha):
    d = x_ref.shape[1]
    ya = jnp.dot(a_ref[...], wda_ref[...], preferred_element_type=F32)
    yb = jnp.dot(nb_ref[...], wna_ref[...], preferred_element_type=F32)
    g = g_ref[...].astype(F32)
    merged = g[:, :d] * ya + g[:, d:] * yb
    mix = jnp.dot(merged.astype(BF16), wout_ref[...], preferred_element_type=F32)
    x1 = _layer_norm(alpha * x_ref[...] + mix, lng_ref[...], lnb_ref[...])
    x1_ref[...] = x1
    x1b_ref[...] = x1.astype(BF16)
    logits = jnp.dot(x1, wr_ref[...], preferred_element_type=F32,
                     precision=lax.Precision.HIGHEST) + br_ref[...]
    lane = lax.broadcasted_iota(jnp.int32, logits.shape, 1).astype(F32)
    vals, idxs = [], []
    for _ in range(TOP_K):
        m = jnp.max(logits, axis=1, keepdims=True)
        idx = jnp.min(jnp.where(logits == m, lane, float(LANES)), axis=1, keepdims=True)
        vals.append(m)
        idxs.append(idx)
        logits = jnp.where(lane == idx, -jnp.inf, logits)
    es = [jnp.exp(v - vals[0]) for v in vals]
    denom = es[0] + es[1] + es[2] + es[3]
    gates = jnp.zeros(logits.shape, F32)
    topi = jnp.zeros(logits.shape, F32)
    for k in range(TOP_K):
        gates = jnp.where(lane == k, es[k] / denom, gates)
        topi = jnp.where(lane == k, idxs[k], topi)
    gate_ref[...] = gates
    topi_ref[...] = topi.astype(jnp.int32)


def _merge(x2, a2, nb2, g2, wda, wna, wout, ln_g, ln_b, wr, br, alpha):
    T, D = x2.shape
    tm = min(PROJ_TM, T)
    tok = lambda width: pl.BlockSpec((tm, width), lambda i: (i, 0))
    const = lambda shape: pl.BlockSpec(shape, lambda i: (0,) * len(shape))
    return pl.pallas_call(
        functools.partial(_merge_kernel, alpha=alpha),
        grid=(T // tm,),
        in_specs=[tok(D), tok(a2.shape[1]), tok(nb2.shape[1]), tok(g2.shape[1]),
                  const(wda.shape), const(wna.shape), const(wout.shape),
                  const((1, D)), const((1, D)), const(wr.shape), const((1, LANES))],
        out_specs=(tok(D), tok(D), tok(LANES), tok(LANES)),
        out_shape=(jax.ShapeDtypeStruct((T, D), F32), jax.ShapeDtypeStruct((T, D), BF16),
                   jax.ShapeDtypeStruct((T, LANES), jnp.int32), jax.ShapeDtypeStruct((T, LANES), F32)),
        compiler_params=_cparams(("parallel",)),
        name="merge",
    )(x2, a2, nb2, g2, wda, wna, wout, ln_g, ln_b, wr, br)


def _glu_group_bias(b1):
    e, n = b1.shape
    return b1.reshape(e, n // (2 * LANES), LANES, 2).transpose(0, 1, 3, 2).reshape(e, 1, n)


def _moe_kernel(be_ref, nused_ref, xs_ref, w1_ref, b1_ref, w2_ref, b2_ref, ys_ref, w1g_sc, w2b_sc):
    i = pl.program_id(0)
    used = i < nused_ref[0]
    fresh = jnp.logical_or(i == 0, be_ref[i] != be_ref[jnp.maximum(i - 1, 0)])
    grp = 2 * LANES

    @pl.when(jnp.logical_and(used, fresh))
    def _():
        r = lax.broadcasted_iota(jnp.int32, (grp, grp), 0)
        c = lax.broadcasted_iota(jnp.int32, (grp, grp), 1)
        src = jnp.where(c < LANES, 2 * c, 2 * (c - LANES) + 1)
        perm = jnp.where(r == src, 1.0, 0.0).astype(BF16)
        for g in range(w1_ref.shape[1] // grp):
            blk = w1_ref[:, g * grp:(g + 1) * grp].astype(BF16)
            w1g_sc[:, g * grp:(g + 1) * grp] = jnp.dot(blk, perm, preferred_element_type=F32).astype(BF16)
        w2b_sc[...] = w2_ref[...].astype(BF16)

    @pl.when(used)
    def _():
        xb = xs_ref[...]
        h = jnp.dot(xb, w1g_sc[...], preferred_element_type=F32) + b1_ref[...]
        acts = []
        for g in range(h.shape[1] // grp):
            x_glu = jnp.minimum(h[:, g * grp:g * grp + LANES], SWIGLU_LIMIT)
            x_lin = jnp.clip(h[:, g * grp + LANES:(g + 1) * grp], -SWIGLU_LIMIT, SWIGLU_LIMIT)
            acts.append(x_glu * (1.0 / (1.0 + jnp.exp(-SWIGLU_ALPHA * x_glu))) * (x_lin + 1.0))
        act = jnp.concatenate(acts, axis=1).astype(BF16)
        y = jnp.dot(act, w2b_sc[...], preferred_element_type=F32) + b2_ref[...]
        ys_ref[...] = y.astype(ys_ref.dtype)

    @pl.when(jnp.logical_not(used))
    def _():
        ys_ref[...] = jnp.zeros(ys_ref.shape, ys_ref.dtype)


def _moe_ffn(xs, block_expert, n_used, w1, b1g, w2, b2):
    P, D = xs.shape
    n_blocks = P // MOE_BLOCK
    de2 = w1.shape[2]
    de = w2.shape[1]
    wspec = lambda r, c: pl.BlockSpec((None, r, c), lambda i, be, nu: (be[i], 0, 0))
    grid_spec = pltpu.PrefetchScalarGridSpec(
        num_scalar_prefetch=2,
        grid=(n_blocks,),
        in_specs=[pl.BlockSpec((MOE_BLOCK, D), lambda i, be, nu: (i, 0)),
                  wspec(D, de2), wspec(1, de2), wspec(de, D), wspec(1, D)],
        out_specs=pl.BlockSpec((MOE_BLOCK, D), lambda i, be, nu: (i, 0)),
        scratch_shapes=[pltpu.VMEM((D, de2), BF16), pltpu.VMEM((de, D), BF16)],
    )
    return pl.pallas_call(
        _moe_kernel,
        grid_spec=grid_spec,
        out_shape=jax.ShapeDtypeStruct((P, D), BF16),
        compiler_params=_cparams(("arbitrary",)),
        name="moe_ffn",
    )(block_expert, n_used, xs, w1, b1g, w2, b2)


def _combine_kernel(x1_ref, y0_ref, y1_ref, y2_ref, y3_ref, gate_ref, lng_ref, lnb_ref, o_ref, *, alpha):
    gates = gate_ref[...]
    ffn = jnp.zeros(x1_ref.shape, F32)
    for k, y_ref in enumerate((y0_ref, y1_ref, y2_ref, y3_ref)):
        ffn = ffn + y_ref[...].astype(F32) * gates[:, k:k + 1]
    o_ref[...] = _layer_norm(alpha * x1_ref[...] + ffn, lng_ref[...], lnb_ref[...])


def _combine(x1, y_slots, gates, ln_g, ln_b, alpha):
    T, D = x1.shape
    tm = min(PROJ_TM, T)
    tok = lambda width: pl.BlockSpec((tm, width), lambda i: (i, 0))
    const = lambda shape: pl.BlockSpec(shape, lambda i: (0,) * len(shape))
    slot = lambda k: pl.BlockSpec((None, tm, D), lambda i: (k, i, 0))
    return pl.pallas_call(
        functools.partial(_combine_kernel, alpha=alpha),
        grid=(T // tm,),
        in_specs=[tok(D)] + [slot(k) for k in range(TOP_K)] + [tok(LANES), const((1, D)), const((1, D))],
        out_specs=tok(D),
        out_shape=jax.ShapeDtypeStruct((T, D), F32),
        compiler_params=_cparams(("parallel",)),
        name="combine",
    )(x1, y_slots, y_slots, y_slots, y_slots, gates, ln_g, ln_b)


def _route(top_i, n_experts):
    T = top_i.shape[0]
    TK = T * TOP_K
    experts = jnp.arange(n_experts, dtype=jnp.int32)
    chosen = top_i[:, :, None] == experts
    picks = jnp.sum(chosen, axis=1, dtype=jnp.int32)
    csum = jnp.cumsum(picks, axis=0)
    counts = csum[-1]
    padded = ((counts + MOE_BLOCK - 1) // MOE_BLOCK) * MOE_BLOCK
    padded_end = jnp.cumsum(padded)
    padded_start = padded_end - padded
    group_start = jnp.cumsum(counts) - counts
    slot_te = padded_start[None, :] + csum - picks
    dest = jnp.sum(jnp.where(chosen, slot_te[:, None, :], 0), axis=2)
    n_blocks = -(-TK // MOE_BLOCK) + n_experts
    blk_first = jnp.arange(n_blocks, dtype=jnp.int32) * MOE_BLOCK
    block_expert = jnp.minimum(jnp.sum(padded_end[None, :] <= blk_first[:, None], axis=1),
                               n_experts - 1).astype(jnp.int32)
    n_used = (padded_end[-1] // MOE_BLOCK).astype(jnp.int32).reshape(1)
    order = jnp.argsort(top_i.reshape(-1), stable=True).astype(jnp.int32)
    rank0 = blk_first - padded_start[block_expert]
    rank = rank0[:, None] + jnp.arange(MOE_BLOCK, dtype=jnp.int32)[None, :]
    valid = rank < counts[block_expert][:, None]
    sorted_pos = jnp.clip(group_start[block_expert][:, None] + rank, 0, TK - 1)
    src_tok = jnp.where(valid, order[sorted_pos] // TOP_K, 0).reshape(-1)
    return src_tok, block_expert, n_used, dest.T.reshape(-1)


def _rope_tables(seq):
    pos = jnp.arange(seq, dtype=F32)
    inv = ROPE_THETA ** (-jnp.arange(0, DA_HEAD_DIM, 2, dtype=F32) / DA_HEAD_DIM)
    ang = pos[:, None] * inv[None, :]
    ang = jnp.concatenate([ang, ang], axis=-1)
    cos, sin = jnp.cos(ang), jnp.sin(ang)
    half = DA_HEAD_DIM // 2
    sin_signed = jnp.concatenate([-sin[:, :half], sin[:, half:]], axis=-1)
    return jnp.tile(cos, (1, 2)), jnp.tile(sin_signed, (1, 2))


def kernel(x, w_in, b_in, lambda_q1, lambda_k1, lambda_q2, lambda_k2, subln_g, rpb, w_branch_da, w_branch_na, w_out, ln1_g, ln1_b, w_router, b_router, w_mlp1, b_mlp1, w_mlp2, b_mlp2, ln2_g, ln2_b):
    B, S, D = x.shape
    depth = w_in.shape[0]
    n_experts = w_router.shape[2]
    T = B * S
    rows = S // GRID_W
    assert S % GRID_W == 0 and rows % NA_ROWS == 0 and rows >= NA_BAND
    alpha = (2 * depth) ** 0.25
    cos, sin = _rope_tables(S)
    row = lambda v: v.reshape(1, -1)

    for l in range(depth):
        lam_init = 0.8 - 0.6 * math.exp(-0.3 * l)
        qT, k, vT, qn, kn, vn, gates_br = _in_proj(x, w_in[l].astype(BF16), row(b_in[l]), cos, sin)
        a = _diff_attn(qT, k, vT, row(lambda_q1[l]), row(lambda_k1[l]), row(lambda_q2[l]),
                       row(lambda_k2[l]), subln_g[l].reshape(-1, 1), lam_init)
        nb = _na_attn(qn, kn, vn, _na_bias(rpb[l], rows))

        wr = jnp.pad(w_router[l], ((0, 0), (0, LANES - n_experts)))
        br = jnp.pad(row(b_router[l]), ((0, 0), (0, LANES - n_experts)), constant_values=NEG_BIG)
        x1, x1b, topi, gates = _merge(
            x.reshape(T, D), a.reshape(T, -1), nb.reshape(T, -1), gates_br.reshape(T, -1),
            w_branch_da[l].astype(BF16), w_branch_na[l].astype(BF16), w_out[l].astype(BF16),
            row(ln1_g[l]), row(ln1_b[l]), wr, br, alpha)

        src_tok, block_expert, n_used, dest = _route(topi[:, :TOP_K], n_experts)
        xs = jnp.take(x1b, src_tok, axis=0, mode="clip")
        ys = _moe_ffn(xs, block_expert, n_used, w_mlp1[l], _glu_group_bias(b_mlp1[l]),
                      w_mlp2[l], b_mlp2[l][:, None, :])
        y_slots = jnp.take(ys, dest, axis=0, mode="clip").reshape(TOP_K, T, D)
        x = _combine(x1, y_slots, gates, row(ln2_g[l]), row(ln2_b[l]), alpha).reshape(B, S, D)
    return x
```

```python
import functools
import math

import jax
import jax.numpy as jnp
from jax import lax
from jax.experimental import pallas as pl
from jax.experimental.pallas import tpu as pltpu

F32 = jnp.float32
BF16 = jnp.bfloat16

GRID_W = 64
DA_HEADS = 4
DA_HEAD_DIM = 64
DA_V_DIM = 2 * DA_HEAD_DIM
DA_WIDTH = DA_HEADS * DA_V_DIM
ROPE_THETA = 10000.0
NA_HEADS = 8
NA_HEAD_DIM = 64
NA_WIDTH = NA_HEADS * NA_HEAD_DIM
NA_KH = 8
NA_KW = 16
TOP_K = 4
SWIGLU_ALPHA = 1.702
SWIGLU_LIMIT = 7.0
MOE_BLOCK = 256
LN_EPS = 1e-5
RMS_EPS = 1e-5

LANES = 128
VMEM_LIMIT_BYTES = 56 * 1024 * 1024

LOG2E = math.log2(math.e)
NEG_BIG = -1e30

PROJ_TM = 512
DA_TQ = 256
DA_TK = 512
DA_UNROLL = 8
NA_ROWS = 8
NA_BAND = 16


def _cparams(sem):
    return pltpu.CompilerParams(dimension_semantics=sem, vmem_limit_bytes=VMEM_LIMIT_BYTES)


def _in_proj_kernel(x_ref, w_ref, b_ref, cos_ref, sin_ref,
                    qT_ref, k_ref, vT_ref, qn_ref, kn_ref, vn_ref, g_ref):
    xb = x_ref[...].astype(BF16)

    def seg(lo, hi):
        return jnp.dot(xb, w_ref[:, lo:hi], preferred_element_type=F32) + b_ref[:, lo:hi]

    cos = cos_ref[...]
    sin = sin_ref[...]
    lane = lax.broadcasted_iota(jnp.int32, cos.shape, 1)
    first_half = (lane % DA_HEAD_DIM) < (DA_HEAD_DIM // 2)

    def rope(y):
        outs = []
        for h in range(DA_HEADS):
            yh = y[:, h * LANES:(h + 1) * LANES]
            partner = jnp.where(first_half,
                                pltpu.roll(yh, LANES - DA_HEAD_DIM // 2, 1),
                                pltpu.roll(yh, DA_HEAD_DIM // 2, 1))
            outs.append(yh * cos + partner * sin)
        return jnp.concatenate(outs, axis=1)

    w = DA_WIDTH
    q = rope(seg(0, w)) * (DA_HEAD_DIM ** -0.5 * LOG2E)
    qT_ref[...] = q.T.astype(BF16)
    k_ref[...] = rope(seg(w, 2 * w)).astype(BF16)
    vT_ref[...] = seg(2 * w, 3 * w).T.astype(BF16)
    o = 3 * w
    qn_ref[...] = (seg(o, o + NA_WIDTH) * (NA_HEAD_DIM ** -0.5)).T.astype(BF16)
    kn_ref[...] = seg(o + NA_WIDTH, o + 2 * NA_WIDTH).astype(BF16)
    vn_ref[...] = seg(o + 2 * NA_WIDTH, o + 3 * NA_WIDTH).T.astype(BF16)
    g0 = o + 3 * NA_WIDTH
    gate_pre = seg(g0, w_ref.shape[1])
    g_ref[...] = (1.0 / (1.0 + jnp.exp(-gate_pre))).astype(BF16)


def _in_proj(x, w_in, b_in, cos, sin):
    B, S, D = x.shape
    tm = min(PROJ_TM, S)
    n_cols = w_in.shape[1]
    n_gate = n_cols - 3 * DA_WIDTH - 3 * NA_WIDTH
    tok = lambda width: pl.BlockSpec((None, tm, width), lambda b, i: (b, i, 0))
    tr = pl.BlockSpec((None, DA_WIDTH, tm), lambda b, i: (b, 0, i))
    const = lambda shape: pl.BlockSpec(shape, lambda b, i: (0,) * len(shape))
    out_shape = (
        jax.ShapeDtypeStruct((B, DA_WIDTH, S), BF16),
        jax.ShapeDtypeStruct((B, S, DA_WIDTH), BF16),
        jax.ShapeDtypeStruct((B, DA_WIDTH, S), BF16),
        jax.ShapeDtypeStruct((B, NA_WIDTH, S), BF16),
        jax.ShapeDtypeStruct((B, S, NA_WIDTH), BF16),
        jax.ShapeDtypeStruct((B, NA_WIDTH, S), BF16),
        jax.ShapeDtypeStruct((B, S, n_gate), BF16),
    )
    assert DA_WIDTH == NA_WIDTH
    return pl.pallas_call(
        _in_proj_kernel,
        grid=(B, S // tm),
        in_specs=[tok(D), const((D, n_cols)), const((1, n_cols)),
                  pl.BlockSpec((tm, LANES), lambda b, i: (i, 0)),
                  pl.BlockSpec((tm, LANES), lambda b, i: (i, 0))],
        out_specs=(tr, tok(DA_WIDTH), tr, tr, tok(NA_WIDTH), tr, tok(n_gate)),
        out_shape=out_shape,
        compiler_params=_cparams(("parallel", "parallel")),
        name="in_proj",
    )(x, w_in, b_in, cos, sin)


def _diff_attn_kernel(qT_ref, k_ref, vT_ref, lq1_ref, lk1_ref, lq2_ref, lk2_ref, g_ref,
                      o_ref, m_sc, l_sc, acc_sc, s_sc, *, tk, unroll, lam_init):
    qT = qT_ref[...]
    tq = qT.shape[1]
    row = lax.broadcasted_iota(jnp.int32, qT.shape, 0)
    zero = jnp.zeros_like(qT)
    qbd = jnp.concatenate([jnp.where(row < DA_HEAD_DIM, qT, zero),
                           jnp.where(row < DA_HEAD_DIM, zero, qT)], axis=1)

    m_sc[...] = jnp.full(m_sc.shape, -jnp.inf, F32)
    l_sc[...] = jnp.zeros(l_sc.shape, F32)
    acc_sc[...] = jnp.zeros(acc_sc.shape, F32)

    nk = k_ref.shape[0] // tk

    def scores(j):
        off = pl.multiple_of(j * tk, tk)
        return jnp.dot(k_ref[pl.ds(off, tk), :], qbd, preferred_element_type=F32)

    def accumulate(j, s):
        off = pl.multiple_of(j * tk, tk)
        m_prev = m_sc[...]
        m_new = jnp.maximum(m_prev, jnp.max(s, axis=0, keepdims=True))
        alpha = jnp.exp2(m_prev - m_new)
        p = jnp.exp2(s - m_new)
        l_sc[...] = alpha * l_sc[...] + jnp.sum(p, axis=0, keepdims=True)
        vb = vT_ref[:, pl.ds(off, tk)]
        acc_sc[...] = alpha * acc_sc[...] + jnp.dot(vb, p.astype(BF16), preferred_element_type=F32)
        m_sc[...] = m_new

    s_sc[0] = scores(0)

    def body(jj, carry):
        for u in range(unroll):
            j = unroll * jj + u
            s_sc[(u + 1) % 2] = scores(jnp.minimum(j + 1, nk - 1))
            accumulate(j, s_sc[u % 2])
        return carry

    lax.fori_loop(0, nk // unroll, body, 0)

    lam = (jnp.exp(jnp.sum(lq1_ref[...] * lk1_ref[...], axis=1, keepdims=True))
           - jnp.exp(jnp.sum(lq2_ref[...] * lk2_ref[...], axis=1, keepdims=True)) + lam_init)
    on = acc_sc[...] / l_sc[...]
    o = on[:, :tq] - lam * on[:, tq:]
    ms = jnp.mean(o * o, axis=0, keepdims=True)
    o = o * lax.rsqrt(ms + RMS_EPS) * g_ref[...]
    o = o * (1.0 - lam_init)
    o_ref[...] = o.T.astype(BF16)


def _diff_attn(qT, k, vT, lq1, lk1, lq2, lk2, subln_g, lam_init):
    B, S, _ = k.shape
    tq = min(DA_TQ, S)
    tk = min(DA_TK, S // 2)
    unroll = math.gcd(DA_UNROLL, S // tk)
    assert unroll % 2 == 0 and S % tk == 0 and S % tq == 0
    vec =pl.BlockSpec((1, DA_HEAD_DIM), lambda b, h, i: (0, 0))
    kernel = functools.partial(_diff_attn_kernel, tk=tk, unroll=unroll, lam_init=lam_init)
    return pl.pallas_call(
        kernel,
        grid=(B, DA_HEADS, S // tq),
        in_specs=[pl.BlockSpec((None, DA_V_DIM, tq), lambda b, h, i: (b, h, i)),
                  pl.BlockSpec((None, S, DA_V_DIM), lambda b, h, i: (b, 0, h)),
                  pl.BlockSpec((None, DA_V_DIM, S), lambda b, h, i: (b, h, 0)),
                  vec, vec, vec, vec,
                  pl.BlockSpec((DA_V_DIM, 1), lambda b, h, i: (0, 0))],
        out_specs=pl.BlockSpec((None, tq, DA_V_DIM), lambda b, h, i: (b, i, h)),
        out_shape=jax.ShapeDtypeStruct((B, S, DA_WIDTH), BF16),
        scratch_shapes=[pltpu.VMEM((1, 2 * tq), F32), pltpu.VMEM((1, 2 * tq), F32),
                        pltpu.VMEM((DA_V_DIM, 2 * tq), F32), pltpu.VMEM((2, tk, 2 * tq), F32)],
        compiler_params=_cparams(("parallel", "parallel", "arbitrary")),
        name="diff_attn",
    )(qT, k, vT, lq1, lk1, lq2, lk2, subln_g)


def _na_band_start(r0, rows):
    return jnp.clip(r0 - NA_KH // 2, 0, rows - NA_BAND)


def _na_bias(rpb, rows):
    nblk = rows // NA_ROWS
    r0 = jnp.array([0, NA_ROWS * min(1, nblk - 1), NA_ROWS * (nblk - 1)], jnp.int32)
    band = _na_band_start(r0, rows)
    qr = r0[:, None] + jnp.arange(NA_ROWS)[None, :]
    rs = jnp.clip(qr - NA_KH // 2, 0, rows - NA_KH)
    kr = band[:, None] + jnp.arange(NA_BAND)[None, :]
    row_ok = (kr[:, None, :] >= rs[:, :, None]) & (kr[:, None, :] < rs[:, :, None] + NA_KH)
    row_off = jnp.clip(kr[:, None, :] - qr[:, :, None] + (NA_KH - 1), 0, 2 * NA_KH - 2)
    cols = jnp.arange(GRID_W)
    cs = jnp.clip(cols - NA_KW // 2, 0, GRID_W - NA_KW)
    col_ok = (cols[None, :] >= cs[:, None]) & (cols[None, :] < cs[:, None] + NA_KW)
    col_off = jnp.clip(cols[None, :] - cols[:, None] + (NA_KW - 1), 0, 2 * NA_KW - 2)
    tiles = jnp.where(col_ok[None, None], rpb.astype(F32)[:, :, col_off], NEG_BIG)
    sel = jnp.where(row_ok[None, :, :, :, None, None], tiles[:, row_off], NEG_BIG)
    nq, nk = NA_ROWS * GRID_W, NA_BAND * GRID_W
    sel = sel.reshape(NA_HEADS // 2, 2, 3, NA_ROWS, NA_BAND, GRID_W, GRID_W)
    sel = sel.transpose(2, 0, 1, 4, 6, 3, 5)
    return sel.reshape(3, NA_HEADS // 2, 2 * nk, nq)


def _na_kernel(qT_ref, k_ref, vT_ref, bias_ref, o_ref, *, rows):
    i = pl.program_id(2)
    nk = NA_BAND * GRID_W
    off = pl.multiple_of(_na_band_start(i * NA_ROWS, rows) * GRID_W, 2 * GRID_W)
    kb = k_ref[pl.ds(off, nk), :]
    vT = vT_ref[:, pl.ds(off, nk)]
    lane = lax.broadcasted_iota(jnp.int32, kb.shape, 1)
    kbd = jnp.concatenate([jnp.where(lane < NA_HEAD_DIM, kb, jnp.zeros_like(kb)),
                           jnp.where(lane < NA_HEAD_DIM, jnp.zeros_like(kb), kb)], axis=0)
    row = lax.broadcasted_iota(jnp.int32, vT.shape, 0)
    vbdT = jnp.concatenate([jnp.where(row < NA_HEAD_DIM, vT, jnp.zeros_like(vT)),
                            jnp.where(row < NA_HEAD_DIM, jnp.zeros_like(vT), vT)], axis=1)
    s = jnp.dot(kbd, qT_ref[...], preferred_element_type=F32) + bias_ref[...]
    ps, ls = [], []
    for h in range(2):
        sh = s[h * nk:(h + 1) * nk]
        ph = jnp.exp(sh - jnp.max(sh, axis=0, keepdims=True))
        ps.append(ph.astype(BF16))
        ls.append(jnp.sum(ph, axis=0, keepdims=True))
    oT = jnp.dot(vbdT, jnp.concatenate(ps, axis=0), preferred_element_type=F32)
    row_o = lax.broadcasted_iota(jnp.int32, oT.shape, 0)
    oT = oT / jnp.where(row_o < NA_HEAD_DIM, ls[0], ls[1])
    o_ref[...] = oT.T.astype(BF16)


def _na_attn(qnT, kn, vnT, bias):
    B, S, _ = kn.shape
    rows = S // GRID_W
    nblk = rows // NA_ROWS
    nq = NA_ROWS * GRID_W

    def variant(i):
        return jnp.where(i == 0, 0, jnp.where(i == nblk - 1, 2, 1))

    return pl.pallas_call(
        functools.partial(_na_kernel, rows=rows),
        grid=(B, NA_HEADS // 2, nblk),
        in_specs=[pl.BlockSpec((None, LANES, nq), lambda b, h, i: (b, h, i)),
                  pl.BlockSpec((None, S, LANES), lambda b, h, i: (b, 0, h)),
                  pl.BlockSpec((None, LANES, S), lambda b, h, i: (b, h, 0)),
                  pl.BlockSpec((None, None, bias.shape[2], nq), lambda b, h, i: (variant(i), h, 0, 0))],
        out_specs=pl.BlockSpec((None, nq, LANES), lambda b, h, i: (b, i, h)),
        out_shape=jax.ShapeDtypeStruct((B, S, NA_WIDTH), BF16),
        compiler_params=_cparams(("parallel", "parallel", "arbitrary")),
        name="na_attn",
    )(qnT, kn, vnT, bias)


def _layer_norm(h, g, b):
    mu = jnp.mean(h, axis=-1, keepdims=True)
    d = h - mu
    var = jnp.mean(d * d, axis=-1, keepdims=True)
    return d * lax.rsqrt(var + LN_EPS) * g + b


def _merge_kernel(x_ref, a_ref, nb_ref, g_ref, wda_ref, wna_ref, wout_ref, lng_ref, lnb_ref,
                  wr_ref, br_ref, x1_ref, x1b_ref, topi_ref, gate_ref, *, alpha):
    d = x_ref.shape[1]
    ya = jnp.dot(a_ref[...], wda_ref[...], preferred_element_type=F32)
    yb = jnp.dot(nb_ref[...], wna_ref[...], preferred_element_type=F32)
    g = g_ref[...].astype(F32)
    merged = g[:, :d] * ya + g[:, d:] * yb
    mix = jnp.dot(merged.astype(BF16), wout_ref[...], preferred_element_type=F32)
    x1 = _layer_norm(alpha * x_ref[...] + mix, lng_ref[...], lnb_ref[...])
    x1_ref[...] = x1
    x1b_ref[...] = x1.astype(BF16)
    xh = x1.astype(BF16)
    xl = (x1 - xh.astype(F32)).astype(BF16)
    logits = jnp.dot(jnp.concatenate([xh, xl, xh], axis=1), wr_ref[...],
                     preferred_element_type=F32) + br_ref[...]
    lane = lax.broadcasted_iota(jnp.int32, logits.shape, 1).astype(F32)
    vals, idxs = [], []
    for _ in range(TOP_K):
        m = jnp.max(logits, axis=1, keepdims=True)
        idx = jnp.min(jnp.where(logits == m, lane, float(LANES)), axis=1, keepdims=True)
        vals.append(m)
        idxs.append(idx)
        logits = jnp.where(lane == idx, -jnp.inf, logits)
    es = [jnp.exp(v - vals[0]) for v in vals]
    denom = es[0] + es[1] + es[2] + es[3]
    gates = jnp.zeros(logits.shape, F32)
    topi = jnp.zeros(logits.shape, F32)
    for k in range(TOP_K):
        gates = jnp.where(lane == k, es[k] / denom, gates)
        topi = jnp.where(lane == k, idxs[k], topi)
    gate_ref[...] = gates
    topi_ref[...] = topi.astype(jnp.int32)


def _merge(x2, a2, nb2, g2, wda, wna, wout, ln_g, ln_b, wr, br, alpha):
    T, D = x2.shape
    tm = min(PROJ_TM, T)
    tok = lambda width: pl.BlockSpec((tm, width), lambda i: (i, 0))
    const = lambda shape: pl.BlockSpec(shape, lambda i: (0,) * len(shape))
    return pl.pallas_call(
        functools.partial(_merge_kernel, alpha=alpha),
        grid=(T // tm,),
        in_specs=[tok(D), tok(a2.shape[1]), tok(nb2.shape[1]), tok(g2.shape[1]),
                  const(wda.shape), const(wna.shape), const(wout.shape),
                  const((1, D)), const((1, D)), const(wr.shape), const((1, LANES))],
        out_specs=(tok(D), tok(D), tok(LANES), tok(LANES)),
        out_shape=(jax.ShapeDtypeStruct((T, D), F32), jax.ShapeDtypeStruct((T, D), BF16),
                   jax.ShapeDtypeStruct((T, LANES), jnp.int32), jax.ShapeDtypeStruct((T, LANES), F32)),
        compiler_params=_cparams(("parallel",)),
        name="merge",
    )(x2, a2, nb2, g2, wda, wna, wout, ln_g, ln_b, wr, br)


def _glu_group_bias(b1):
    e, n = b1.shape
    return b1.reshape(e, n // (2 * LANES), LANES, 2).transpose(0, 1, 3, 2).reshape(e, 1, n)


def _moe_kernel(be_ref, nused_ref, xs_ref, w1_ref, b1_ref, w2_ref, b2_ref, ys_ref, w1g_sc, w2b_sc):
    i = pl.program_id(0)
    used = i < nused_ref[0]
    fresh = jnp.logical_or(i == 0, be_ref[i] != be_ref[jnp.maximum(i - 1, 0)])
    grp = 2 * LANES

    @pl.when(jnp.logical_and(used, fresh))
    def _():
        r = lax.broadcasted_iota(jnp.int32, (grp, grp), 0)
        c = lax.broadcasted_iota(jnp.int32, (grp, grp), 1)
        src = jnp.where(c < LANES, 2 * c, 2 * (c - LANES) + 1)
        perm = jnp.where(r == src, 1.0, 0.0).astype(BF16)
        for g in range(w1_ref.shape[1] // grp):
            blk = w1_ref[:, g * grp:(g + 1) * grp].astype(BF16)
            w1g_sc[:, g * grp:(g + 1) * grp] = jnp.dot(blk, perm, preferred_element_type=F32).astype(BF16)
        w2b_sc[...] = w2_ref[...].astype(BF16)

    @pl.when(used)
    def _():
        xb = xs_ref[...]
        h = jnp.dot(xb, w1g_sc[...], preferred_element_type=F32) + b1_ref[...]
        acts = []
        for g in range(h.shape[1] // grp):
            x_glu = jnp.minimum(h[:, g * grp:g * grp + LANES], SWIGLU_LIMIT)
            x_lin = jnp.clip(h[:, g * grp + LANES:(g + 1) * grp], -SWIGLU_LIMIT, SWIGLU_LIMIT)
            acts.append(x_glu * (1.0 / (1.0 + jnp.exp(-SWIGLU_ALPHA * x_glu))) * (x_lin + 1.0))
        act = jnp.concatenate(acts, axis=1).astype(BF16)
        y = jnp.dot(act, w2b_sc[...], preferred_element_type=F32) + b2_ref[...]
        ys_ref[...] = y.astype(ys_ref.dtype)

    @pl.when(jnp.logical_not(used))
    def _():
        ys_ref[...] = jnp.zeros(ys_ref.shape, ys_ref.dtype)


def _moe_ffn(xs, block_expert, n_used, w1, b1g, w2, b2):
    P, D = xs.shape
    n_blocks = P // MOE_BLOCK
    de2 = w1.shape[2]
    de = w2.shape[1]
    wspec = lambda r, c: pl.BlockSpec((None, r, c), lambda i, be, nu: (be[i], 0, 0))
    grid_spec = pltpu.PrefetchScalarGridSpec(
        num_scalar_prefetch=2,
        grid=(n_blocks,),
        in_specs=[pl.BlockSpec((MOE_BLOCK, D), lambda i, be, nu: (i, 0)),
                  wspec(D, de2), wspec(1, de2), wspec(de, D), wspec(1, D)],
        out_specs=pl.BlockSpec((MOE_BLOCK, D), lambda i, be, nu: (i, 0)),
        scratch_shapes=[pltpu.VMEM((D, de2), BF16), pltpu.VMEM((de, D), BF16)],
    )
    return pl.pallas_call(
        _moe_kernel,
        grid_spec=grid_spec,
        out_shape=jax.ShapeDtypeStruct((P, D), BF16),
        compiler_params=_cparams(("arbitrary",)),
        name="moe_ffn",
    )(block_expert, n_used, xs, w1, b1g, w2, b2)


def _combine_kernel(x1_ref, y0_ref, y1_ref, y2_ref, y3_ref, gate_ref, lng_ref, lnb_ref, o_ref, *, alpha):
    gates = gate_ref[...]
    ffn = jnp.zeros(x1_ref.shape, F32)
    for k, y_ref in enumerate((y0_ref, y1_ref, y2_ref, y3_ref)):
        ffn = ffn + y_ref[...].astype(F32) * gates[:, k:k + 1]
    o_ref[...] = _layer_norm(alpha * x1_ref[...] + ffn, lng_ref[...], lnb_ref[...])


def _combine(x1, y_slots, gates, ln_g, ln_b, alpha):
    T, D = x1.shape
    tm = min(PROJ_TM, T)
    tok = lambda width: pl.BlockSpec((tm, width), lambda i: (i, 0))
    const = lambda shape: pl.BlockSpec(shape, lambda i: (0,) * len(shape))
    slot = lambda k: pl.BlockSpec((None, tm, D), lambda i: (k, i, 0))
    return pl.pallas_call(
        functools.partial(_combine_kernel, alpha=alpha),
        grid=(T // tm,),
        in_specs=[tok(D)] + [slot(k) for k in range(TOP_K)] + [tok(LANES), const((1, D)), const((1, D))],
        out_specs=tok(D),
        out_shape=jax.ShapeDtypeStruct((T, D), F32),
        compiler_params=_cparams(("parallel",)),
        name="combine",
    )(x1, y_slots, y_slots, y_slots, y_slots, gates, ln_g, ln_b)


def _route(top_i, n_experts):
    T = top_i.shape[0]
    TK = T * TOP_K
    experts = jnp.arange(n_experts, dtype=jnp.int32)
    chosen = top_i[:, :, None] == experts
    picks = jnp.sum(chosen, axis=1, dtype=jnp.int32)
    csum = jnp.cumsum(picks, axis=0)
    counts = csum[-1]
    padded = ((counts + MOE_BLOCK - 1) // MOE_BLOCK) * MOE_BLOCK
    padded_end = jnp.cumsum(padded)
    padded_start = padded_end - padded
    group_start = jnp.cumsum(counts) - counts
    slot_te = padded_start[None, :] + csum - picks
    dest = jnp.sum(jnp.where(chosen, slot_te[:, None, :], 0), axis=2)
    n_blocks = -(-TK // MOE_BLOCK) + n_experts
    blk_first = jnp.arange(n_blocks, dtype=jnp.int32) * MOE_BLOCK
    block_expert = jnp.minimum(jnp.sum(padded_end[None, :] <= blk_first[:, None], axis=1),
                               n_experts - 1).astype(jnp.int32)
    n_used = (padded_end[-1] // MOE_BLOCK).astype(jnp.int32).reshape(1)
    order = jnp.argsort(top_i.reshape(-1), stable=True).astype(jnp.int32)
    rank0 = blk_first - padded_start[block_expert]
    rank = rank0[:, None] + jnp.arange(MOE_BLOCK, dtype=jnp.int32)[None, :]
    valid = rank < counts[block_expert][:, None]
    sorted_pos = jnp.clip(group_start[block_expert][:, None] + rank, 0, TK - 1)
    src_tok = jnp.where(valid, order[sorted_pos] // TOP_K, 0).reshape(-1)
    return src_tok, block_expert, n_used, dest.T.reshape(-1)


def _rope_tables(seq):
    pos = jnp.arange(seq, dtype=F32)
    inv = ROPE_THETA ** (-jnp.arange(0, DA_HEAD_DIM, 2, dtype=F32) / DA_HEAD_DIM)
    ang = pos[:, None] * inv[None, :]
    ang = jnp.concatenate([ang, ang], axis=-1)
    cos, sin = jnp.cos(ang), jnp.sin(ang)
    half = DA_HEAD_DIM // 2
    sin_signed = jnp.concatenate([-sin[:, :half], sin[:, half:]], axis=-1)
    return jnp.tile(cos, (1, 2)), jnp.tile(sin_signed, (1, 2))


def kernel(x, w_in, b_in, lambda_q1, lambda_k1, lambda_q2, lambda_k2, subln_g, rpb, w_branch_da, w_branch_na, w_out, ln1_g, ln1_b, w_router, b_router, w_mlp1, b_mlp1, w_mlp2, b_mlp2, ln2_g, ln2_b):
    B, S, D = x.shape
    depth = w_in.shape[0]
    n_experts = w_router.shape[2]
    T = B * S
    rows = S // GRID_W
    assert S % GRID_W == 0 and rows % NA_ROWS == 0 and rows >= NA_BAND
    alpha = (2 * depth) ** 0.25
    cos, sin = _rope_tables(S)
    row = lambda v: v.reshape(1, -1)

    for l in range(depth):
        lam_init = 0.8 - 0.6 * math.exp(-0.3 * l)
        qT, k, vT, qn, kn, vn, gates_br = _in_proj(x, w_in[l].astype(BF16), row(b_in[l]), cos, sin)
        a = _diff_attn(qT, k, vT, row(lambda_q1[l]), row(lambda_k1[l]), row(lambda_q2[l]),
                       row(lambda_k2[l]), subln_g[l].reshape(-1, 1), lam_init)
        nb = _na_attn(qn, kn, vn, _na_bias(rpb[l], rows))

        wr = jnp.pad(w_router[l], ((0, 0), (0, LANES - n_experts)))
        wr_h = wr.astype(BF16)
        wr = jnp.concatenate([wr_h, wr_h, (wr - wr_h.astype(F32)).astype(BF16)], axis=0)
        br = jnp.pad(row(b_router[l]), ((0, 0), (0, LANES - n_experts)), constant_values=NEG_BIG)
        x1, x1b, topi, gates = _merge(
            x.reshape(T, D), a.reshape(T, -1), nb.reshape(T, -1), gates_br.reshape(T, -1),
            w_branch_da[l].astype(BF16), w_branch_na[l].astype(BF16), w_out[l].astype(BF16),
            row(ln1_g[l]), row(ln1_b[l]), wr, br, alpha)

        src_tok, block_expert, n_used, dest = _route(topi[:, :TOP_K], n_experts)
        xs = jnp.take(x1b, src_tok, axis=0, mode="clip")
        ys = _moe_ffn(xs, block_expert, n_used, w_mlp1[l], _glu_group_bias(b_mlp1[l]),
                      w_mlp2[l], b_mlp2[l][:, None, :])
        y_slots = jnp.take(ys, dest, axis=0, mode="clip").reshape(TOP_K, T, D)
        x = _combine(x1, y_slots, gates, row(ln2_g[l]), row(ln2_b[l]), alpha).reshape(B, S, D)
    return x
```

```python
import functools
import math

import jax
import jax.numpy as jnp
from jax import lax
from jax.experimental import pallas as pl
from jax.experimental.pallas import tpu as pltpu

F32 = jnp.float32
BF16 = jnp.bfloat16

GRID_W = 64
DA_HEADS = 4
DA_HEAD_DIM = 64
DA_V_DIM = 2 * DA_HEAD_DIM
DA_WIDTH = DA_HEADS * DA_V_DIM
ROPE_THETA = 10000.0
NA_HEADS = 8
NA_HEAD_DIM = 64
NA_WIDTH = NA_HEADS * NA_HEAD_DIM
NA_KH = 8
NA_KW = 16
TOP_K = 4
SWIGLU_ALPHA = 1.702
SWIGLU_LIMIT = 7.0
MOE_BLOCK = 256
LN_EPS = 1e-5
RMS_EPS = 1e-5

LANES = 128
VMEM_LIMIT_BYTES = 56 * 1024 * 1024

LOG2E = math.log2(math.e)
NEG_BIG = -1e30

PROJ_TM = 512
DA_TQ = 256
DA_TK = 512
DA_UNROLL = 8
NA_ROWS = 8
NA_BAND = 16


def _cparams(sem):
    return pltpu.CompilerParams(dimension_semantics=sem, vmem_limit_bytes=VMEM_LIMIT_BYTES)


def _in_proj_kernel(x_ref, w_ref, b_ref, cos_ref, sin_ref,
                    qT_ref, k_ref, vT_ref, qn_ref, kn_ref, vn_ref, g_ref):
    xb = x_ref[...].astype(BF16)

    def seg(lo, hi):
        return jnp.dot(xb, w_ref[:, lo:hi], preferred_element_type=F32) + b_ref[:, lo:hi]

    cos = cos_ref[...]
    sin = sin_ref[...]
    lane = lax.broadcasted_iota(jnp.int32, cos.shape, 1)
    first_half = (lane % DA_HEAD_DIM) < (DA_HEAD_DIM // 2)

    def rope(y):
        outs = []
        for h in range(DA_HEADS):
            yh = y[:, h * LANES:(h + 1) * LANES]
            partner = jnp.where(first_half,
                                pltpu.roll(yh, LANES - DA_HEAD_DIM // 2, 1),
                                pltpu.roll(yh, DA_HEAD_DIM // 2, 1))
            outs.append(yh * cos + partner * sin)
        return jnp.concatenate(outs, axis=1)

    w = DA_WIDTH
    q = rope(seg(0, w)) * (DA_HEAD_DIM ** -0.5 * LOG2E)
    qT_ref[...] = q.T.astype(BF16)
    k_ref[...] = rope(seg(w, 2 * w)).astype(BF16)
    vT_ref[...] = seg(2 * w, 3 * w).T.astype(BF16)
    o = 3 * w
    qn_ref[...] = (seg(o, o + NA_WIDTH) * (NA_HEAD_DIM ** -0.5)).T.astype(BF16)
    kn_ref[...] = seg(o + NA_WIDTH, o + 2 * NA_WIDTH).astype(BF16)
    vn_ref[...] = seg(o + 2 * NA_WIDTH, o + 3 * NA_WIDTH).T.astype(BF16)
    g0 = o + 3 * NA_WIDTH
    gate_pre = seg(g0, w_ref.shape[1])
    g_ref[...] = (1.0 / (1.0 + jnp.exp(-gate_pre))).astype(BF16)


def _in_proj(x, w_in, b_in, cos, sin):
    B, S, D = x.shape
    tm = min(PROJ_TM, S)
    n_cols = w_in.shape[1]
    n_gate = n_cols - 3 * DA_WIDTH - 3 * NA_WIDTH
    tok = lambda width: pl.BlockSpec((None, tm, width), lambda b, i: (b, i, 0))
    tr = pl.BlockSpec((None, DA_WIDTH, tm), lambda b, i: (b, 0, i))
    const = lambda shape: pl.BlockSpec(shape, lambda b, i: (0,) * len(shape))
    out_shape = (
        jax.ShapeDtypeStruct((B, DA_WIDTH, S), BF16),
        jax.ShapeDtypeStruct((B, S, DA_WIDTH), BF16),
        jax.ShapeDtypeStruct((B, DA_WIDTH, S), BF16),
        jax.ShapeDtypeStruct((B, NA_WIDTH, S), BF16),
        jax.ShapeDtypeStruct((B, S, NA_WIDTH), BF16),
        jax.ShapeDtypeStruct((B, NA_WIDTH, S), BF16),
        jax.ShapeDtypeStruct((B, S, n_gate), BF16),
    )
    assert DA_WIDTH == NA_WIDTH
    return pl.pallas_call(
        _in_proj_kernel,
        grid=(B, S // tm),
        in_specs=[tok(D), const((D, n_cols)), const((1, n_cols)),
                  pl.BlockSpec((tm, LANES), lambda b, i: (i, 0)),
                  pl.BlockSpec((tm, LANES), lambda b, i: (i, 0))],
        out_specs=(tr, tok(DA_WIDTH), tr, tr, tok(NA_WIDTH), tr, tok(n_gate)),
        out_shape=out_shape,
        compiler_params=_cparams(("parallel", "parallel")),
        name="in_proj",
    )(x, w_in, b_in, cos, sin)


def _diff_attn_kernel(qT_ref, k_ref, vT_ref, lq1_ref, lk1_ref, lq2_ref, lk2_ref, g_ref,
                      o_ref, m_sc, l_sc, acc_sc, s_sc, *, tk, unroll, lam_init):
    qT = qT_ref[...]
    tq = qT.shape[1]
    row = lax.broadcasted_iota(jnp.int32, qT.shape, 0)
    zero = jnp.zeros_like(qT)
    qbd = jnp.concatenate([jnp.where(row < DA_HEAD_DIM, qT, zero),
                           jnp.where(row < DA_HEAD_DIM, zero, qT)], axis=1)

    m_sc[...] = jnp.full(m_sc.shape, -jnp.inf, F32)
    l_sc[...] = jnp.zeros(l_sc.shape, F32)
    acc_sc[...] = jnp.zeros(acc_sc.shape, F32)

    nk = k_ref.shape[0] // tk

    def scores(j):
        off = pl.multiple_of(j * tk, tk)
        return jnp.dot(k_ref[pl.ds(off, tk), :], qbd, preferred_element_type=F32)

    def accumulate(j, s):
        off = pl.multiple_of(j * tk, tk)
        m_prev = m_sc[...]
        m_new = jnp.maximum(m_prev, jnp.max(s, axis=0, keepdims=True))
        alpha = jnp.exp2(m_prev - m_new)
        p = jnp.exp2(s - m_new)
        l_sc[...] = alpha * l_sc[...] + jnp.sum(p, axis=0, keepdims=True)
        vb = vT_ref[:, pl.ds(off, tk)]
        acc_sc[...] = alpha * acc_sc[...] + jnp.dot(vb, p.astype(BF16), preferred_element_type=F32)
        m_sc[...] = m_new

    s_sc[0] = scores(0)

    def body(jj, carry):
        for u in range(unroll):
            j = unroll * jj + u
            s_sc[(u + 1) % 2] = scores(jnp.minimum(j + 1, nk - 1))
            accumulate(j, s_sc[u % 2])
        return carry

    lax.fori_loop(0, nk // unroll, body, 0)

    lam = (jnp.exp(jnp.sum(lq1_ref[...] * lk1_ref[...], axis=1, keepdims=True))
           - jnp.exp(jnp.sum(lq2_ref[...] * lk2_ref[...], axis=1, keepdims=True)) + lam_init)
    on = acc_sc[...] / l_sc[...]
    o = on[:, :tq] - lam * on[:, tq:]
    ms = jnp.mean(o * o, axis=0, keepdims=True)
    o = o * lax.rsqrt(ms + RMS_EPS) * g_ref[...]
    o = o * (1.0 - lam_init)
    o_ref[...] = o.T.astype(BF16)


def _diff_attn(qT, k, vT, lq1, lk1, lq2, lk2, subln_g, lam_init):
    B, S, _ = k.shape
    tq = min(DA_TQ, S)
    tk = min(DA_TK, S // 2)
    unroll = math.gcd(DA_UNROLL, S // tk)
    assert unroll % 2 == 0 and S % tk == 0 and S % tq == 0
    vec =pl.BlockSpec((1, DA_HEAD_DIM), lambda b, h, i: (0, 0))
    kernel = functools.partial(_diff_attn_kernel, tk=tk, unroll=unroll, lam_init=lam_init)
    return pl.pallas_call(
        kernel,
        grid=(B, DA_HEADS, S // tq),
        in_specs=[pl.BlockSpec((None, DA_V_DIM, tq), lambda b, h, i: (b, h, i)),
                  pl.BlockSpec((None, S, DA_V_DIM), lambda b, h, i: (b, 0, h)),
                  pl.BlockSpec((None, DA_V_DIM, S), lambda b, h, i: (b, h, 0)),
                  vec, vec, vec, vec,
                  pl.BlockSpec((DA_V_DIM, 1), lambda b, h, i: (0, 0))],
        out_specs=pl.BlockSpec((None, tq, DA_V_DIM), lambda b, h, i: (b, i, h)),
        out_shape=jax.ShapeDtypeStruct((B, S, DA_WIDTH), BF16),
        scratch_shapes=[pltpu.VMEM((1, 2 * tq), F32), pltpu.VMEM((1, 2 * tq), F32),
                        pltpu.VMEM((DA_V_DIM, 2 * tq), F32), pltpu.VMEM((2, tk, 2 * tq), F32)],
        compiler_params=_cparams(("parallel", "parallel", "arbitrary")),
        name="diff_attn",
    )(qT, k, vT, lq1, lk1, lq2, lk2, subln_g)


def _na_band_start(r0, rows):
    return jnp.clip(r0 - NA_KH // 2, 0, rows - NA_BAND)


def _na_bias(rpb, rows):
    nblk = rows // NA_ROWS
    r0 = jnp.array([0, NA_ROWS * min(1, nblk - 1), NA_ROWS * (nblk - 1)], jnp.int32)
    band = _na_band_start(r0, rows)
    qr = r0[:, None] + jnp.arange(NA_ROWS)[None, :]
    rs = jnp.clip(qr - NA_KH // 2, 0, rows - NA_KH)
    kr = band[:, None] + jnp.arange(NA_BAND)[None, :]
    row_ok = (kr[:, None, :] >= rs[:, :, None]) & (kr[:, None, :] < rs[:, :, None] + NA_KH)
    row_off = jnp.clip(kr[:, None, :] - qr[:, :, None] + (NA_KH - 1), 0, 2 * NA_KH - 2)
    cols = jnp.arange(GRID_W)
    cs = jnp.clip(cols - NA_KW // 2, 0, GRID_W - NA_KW)
    col_ok = (cols[None, :] >= cs[:, None]) & (cols[None, :] < cs[:, None] + NA_KW)
    col_off = jnp.clip(cols[None, :] - cols[:, None] + (NA_KW - 1), 0, 2 * NA_KW - 2)
    tiles = jnp.where(col_ok[None, None], rpb.astype(F32)[:, :, col_off], NEG_BIG)
    sel = jnp.where(row_ok[None, :, :, :, None, None], tiles[:, row_off], NEG_BIG)
    nq, nk = NA_ROWS * GRID_W, NA_BAND * GRID_W
    sel = sel.reshape(NA_HEADS // 2, 2, 3, NA_ROWS, NA_BAND, GRID_W, GRID_W)
    sel = sel.transpose(2, 0, 1, 4, 6, 3, 5)
    return sel.reshape(3, NA_HEADS // 2, 2 * nk, nq)


def _na_kernel(qT_ref, k_ref, vT_ref, bias_ref, o_ref, *, rows):
    i = pl.program_id(2)
    nk = NA_BAND * GRID_W
    off = pl.multiple_of(_na_band_start(i * NA_ROWS, rows) * GRID_W, 2 * GRID_W)
    kb = k_ref[pl.ds(off, nk), :]
    vT = vT_ref[:, pl.ds(off, nk)]
    lane = lax.broadcasted_iota(jnp.int32, kb.shape, 1)
    kbd = jnp.concatenate([jnp.where(lane < NA_HEAD_DIM, kb, jnp.zeros_like(kb)),
                           jnp.where(lane < NA_HEAD_DIM, jnp.zeros_like(kb), kb)], axis=0)
    row = lax.broadcasted_iota(jnp.int32, vT.shape, 0)
    vbdT = jnp.concatenate([jnp.where(row < NA_HEAD_DIM, vT, jnp.zeros_like(vT)),
                            jnp.where(row < NA_HEAD_DIM, jnp.zeros_like(vT), vT)], axis=1)
    s = jnp.dot(kbd, qT_ref[...], preferred_element_type=F32) + bias_ref[...]
    ps, ls = [], []
    for h in range(2):
        sh = s[h * nk:(h + 1) * nk]
        ph = jnp.exp(sh - jnp.max(sh, axis=0, keepdims=True))
        ps.append(ph.astype(BF16))
        ls.append(jnp.sum(ph, axis=0, keepdims=True))
    oT = jnp.dot(vbdT, jnp.concatenate(ps, axis=0), preferred_element_type=F32)
    row_o = lax.broadcasted_iota(jnp.int32, oT.shape, 0)
    oT = oT / jnp.where(row_o < NA_HEAD_DIM, ls[0], ls[1])
    o_ref[...] = oT.T.astype(BF16)


def _na_attn(qnT, kn, vnT, bias):
    B, S, _ = kn.shape
    rows = S // GRID_W
    nblk = rows // NA_ROWS
    nq = NA_ROWS * GRID_W

    def variant(i):
        return jnp.where(i == 0, 0, jnp.where(i == nblk - 1, 2, 1))

    return pl.pallas_call(
        functools.partial(_na_kernel, rows=rows),
        grid=(B, NA_HEADS // 2, nblk),
        in_specs=[pl.BlockSpec((None, LANES, nq), lambda b, h, i: (b, h, i)),
                  pl.BlockSpec((None, S, LANES), lambda b, h, i: (b, 0, h)),
                  pl.BlockSpec((None, LANES, S), lambda b, h, i: (b, h, 0)),
                  pl.BlockSpec((None, None, bias.shape[2], nq), lambda b, h, i: (variant(i), h, 0, 0))],
        out_specs=pl.BlockSpec((None, nq, LANES), lambda b, h, i: (b, i, h)),
        out_shape=jax.ShapeDtypeStruct((B, S, NA_WIDTH), BF16),
        compiler_params=_cparams(("parallel", "parallel", "arbitrary")),
        name="na_attn",
    )(qnT, kn, vnT, bias)


def _layer_norm(h, g, b):
    mu = jnp.mean(h, axis=-1, keepdims=True)
    d = h - mu
    var = jnp.mean(d * d, axis=-1, keepdims=True)
    return d * lax.rsqrt(var + LN_EPS) * g + b


def _merge_kernel(x_ref, a_ref, nb_ref, g_ref, wda_ref, wna_ref, wout_ref, lng_ref, lnb_ref,
                  wr_ref, br_ref, x1_ref, x1b_ref, topi_ref, gate_ref, *, alpha):
    d = x_ref.shape[1]
    ya = jnp.dot(a_ref[...], wda_ref[...], preferred_element_type=F32)
    yb = jnp.dot(nb_ref[...], wna_ref[...], preferred_element_type=F32)
    g = g_ref[...].astype(F32)
    merged = g[:, :d] * ya + g[:, d:] * yb
    mix = jnp.dot(merged.astype(BF16), wout_ref[...], preferred_element_type=F32)
    x1 = _layer_norm(alpha * x_ref[...] + mix, lng_ref[...], lnb_ref[...])
    x1_ref[...] = x1
    x1b_ref[...] = x1.astype(BF16)
    xh = x1.astype(BF16)
    xl = (x1 - xh.astype(F32)).astype(BF16)
    logits = jnp.dot(jnp.concatenate([xh, xl, xh], axis=1), wr_ref[...],
                     preferred_element_type=F32) + br_ref[...]
    lane = lax.broadcasted_iota(jnp.int32, logits.shape, 1).astype(F32)
    vals, idxs = [], []
    for _ in range(TOP_K):
        m = jnp.max(logits, axis=1, keepdims=True)
        idx = jnp.min(jnp.where(logits == m, lane, float(LANES)), axis=1, keepdims=True)
        vals.append(m)
        idxs.append(idx)
        logits = jnp.where(lane == idx, -jnp.inf, logits)
    es = [jnp.exp(v - vals[0]) for v in vals]
    denom = es[0] + es[1] + es[2] + es[3]
    gates = jnp.zeros(logits.shape, F32)
    topi = jnp.zeros(logits.shape, F32)
    for k in range(TOP_K):
        gates = jnp.where(lane == k, es[k] / denom, gates)
        topi = jnp.where(lane == k, idxs[k], topi)
    gate_ref[...] = gates
    topi_ref[...] = topi.astype(jnp.int32)


def _merge(x2, a2, nb2, g2, wda, wna, wout, ln_g, ln_b, wr, br, alpha):
    T, D = x2.shape
    tm = min(PROJ_TM, T)
    tok = lambda width: pl.BlockSpec((tm, width), lambda i: (i, 0))
    const = lambda shape: pl.BlockSpec(shape, lambda i: (0,) * len(shape))
    return pl.pallas_call(
        functools.partial(_merge_kernel, alpha=alpha),
        grid=(T // tm,),
        in_specs=[tok(D), tok(a2.shape[1]), tok(nb2.shape[1]), tok(g2.shape[1]),
                  const(wda.shape), const(wna.shape), const(wout.shape),
                  const((1, D)), const((1, D)), const(wr.shape), const((1, LANES))],
        out_specs=(tok(D), tok(D), tok(LANES), tok(LANES)),
        out_shape=(jax.ShapeDtypeStruct((T, D), F32), jax.ShapeDtypeStruct((T, D), BF16),
                   jax.ShapeDtypeStruct((T, LANES), jnp.int32), jax.ShapeDtypeStruct((T, LANES), F32)),
        compiler_params=_cparams(("parallel",)),
        name="merge",
    )(x2, a2, nb2, g2, wda, wna, wout, ln_g, ln_b, wr, br)


def _glu_group_bias(b1):
    e, n = b1.shape
    return b1.reshape(e, n // (2 * LANES), LANES, 2).transpose(0, 1, 3, 2).reshape(e, 1, n)


def _moe_kernel(be_ref, nused_ref, xs_ref, w1_ref, b1_ref, w2_ref, b2_ref, ys_ref, w1g_sc, w2b_sc):
    i = pl.program_id(0)
    used = i < nused_ref[0]
    fresh = jnp.logical_or(i == 0, be_ref[i] != be_ref[jnp.maximum(i - 1, 0)])
    grp = 2 * LANES

    @pl.when(jnp.logical_and(used, fresh))
    def _():
        r = lax.broadcasted_iota(jnp.int32, (grp, grp), 0)
        c = lax.broadcasted_iota(jnp.int32, (grp, grp), 1)
        src = jnp.where(c < LANES, 2 * c, 2 * (c - LANES) + 1)
        perm = jnp.where(r == src, 1.0, 0.0).astype(BF16)
        for g in range(w1_ref.shape[1] // grp):
            blk = w1_ref[:, g * grp:(g + 1) * grp].astype(BF16)
            w1g_sc[:, g * grp:(g + 1) * grp] = jnp.dot(blk, perm, preferred_element_type=F32).astype(BF16)
        w2b_sc[...] = w2_ref[...].astype(BF16)

    @pl.when(used)
    def _():
        xb = xs_ref[...]
        h = jnp.dot(xb, w1g_sc[...], preferred_element_type=F32) + b1_ref[...]
        acts = []
        for g in range(h.shape[1] // grp):
            x_glu = jnp.minimum(h[:, g * grp:g * grp + LANES], SWIGLU_LIMIT)
            x_lin = jnp.clip(h[:, g * grp + LANES:(g + 1) * grp], -SWIGLU_LIMIT, SWIGLU_LIMIT)
            acts.append(x_glu * (1.0 / (1.0 + jnp.exp(-SWIGLU_ALPHA * x_glu))) * (x_lin + 1.0))
        act = jnp.concatenate(acts, axis=1).astype(BF16)
        y = jnp.dot(act, w2b_sc[...], preferred_element_type=F32) + b2_ref[...]
        ys_ref[...] = y.astype(ys_ref.dtype)

    @pl.when(jnp.logical_not(used))
    def _():
        ys_ref[...] = jnp.zeros(ys_ref.shape, ys_ref.dtype)


def _moe_ffn(xs, block_expert, n_used, w1, b1g, w2, b2):
    P, D = xs.shape
    n_blocks = P // MOE_BLOCK
    de2 = w1.shape[2]
    de = w2.shape[1]
    wspec = lambda r, c: pl.BlockSpec((None, r, c), lambda i, be, nu: (be[i], 0, 0))
    grid_spec = pltpu.PrefetchScalarGridSpec(
        num_scalar_prefetch=2,
        grid=(n_blocks,),
        in_specs=[pl.BlockSpec((MOE_BLOCK, D), lambda i, be, nu: (i, 0)),
                  wspec(D, de2), wspec(1, de2), wspec(de, D), wspec(1, D)],
        out_specs=pl.BlockSpec((MOE_BLOCK, D), lambda i, be, nu: (i, 0)),
        scratch_shapes=[pltpu.VMEM((D, de2), BF16), pltpu.VMEM((de, D), BF16)],
    )
    return pl.pallas_call(
        _moe_kernel,
        grid_spec=grid_spec,
        out_shape=jax.ShapeDtypeStruct((P, D), BF16),
        compiler_params=_cparams(("arbitrary",)),
        name="moe_ffn",
    )(block_expert, n_used, xs, w1, b1g, w2, b2)


def _combine_kernel(x1_ref, y0_ref, y1_ref, y2_ref, y3_ref, gate_ref, lng_ref, lnb_ref, o_ref, *, alpha):
    gates = gate_ref[...]
    ffn = jnp.zeros(x1_ref.shape, F32)
    for k, y_ref in enumerate((y0_ref, y1_ref, y2_ref, y3_ref)):
        ffn = ffn + y_ref[...].astype(F32) * gates[:, k:k + 1]
    o_ref[...] = _layer_norm(alpha * x1_ref[...] + ffn, lng_ref[...], lnb_ref[...])


def _combine(x1, y_slots, gates, ln_g, ln_b, alpha):
    T, D = x1.shape
    tm = min(PROJ_TM, T)
    tok = lambda width: pl.BlockSpec((tm, width), lambda i: (i, 0))
    const = lambda shape: pl.BlockSpec(shape, lambda i: (0,) * len(shape))
    slot = lambda k: pl.BlockSpec((None, tm, D), lambda i: (k, i, 0))
    return pl.pallas_call(
        functools.partial(_combine_kernel, alpha=alpha),
        grid=(T // tm,),
        in_specs=[tok(D)] + [slot(k) for k in range(TOP_K)] + [tok(LANES), const((1, D)), const((1, D))],
        out_specs=tok(D),
        out_shape=jax.ShapeDtypeStruct((T, D), F32),
        compiler_params=_cparams(("parallel",)),
        name="combine",
    )(x1, y_slots, y_slots, y_slots, y_slots, gates, ln_g, ln_b)


def _route(top_i, n_experts):
    T = top_i.shape[0]
    TK = T * TOP_K
    experts = jnp.arange(n_experts, dtype=jnp.int32)
    chosen = top_i[:, :, None] == experts
    picks = jnp.sum(chosen, axis=1, dtype=jnp.int32)
    csum = jnp.cumsum(picks, axis=0)
    counts = csum[-1]
    padded = ((counts + MOE_BLOCK - 1) // MOE_BLOCK) * MOE_BLOCK
    padded_end = jnp.cumsum(padded)
    padded_start = padded_end - padded
    group_start = jnp.cumsum(counts) - counts
    slot_te = padded_start[None, :] + csum - picks
    dest = jnp.sum(jnp.where(chosen, slot_te[:, None, :], 0), axis=2)
    n_blocks = -(-TK // MOE_BLOCK) + n_experts
    blk_first = jnp.arange(n_blocks, dtype=jnp.int32) * MOE_BLOCK
    block_expert = jnp.minimum(jnp.sum(padded_end[None, :] <= blk_first[:, None], axis=1),
                               n_experts - 1).astype(jnp.int32)
    n_used = (padded_end[-1] // MOE_BLOCK).astype(jnp.int32).reshape(1)
    order = jnp.argsort(top_i.reshape(-1), stable=True).astype(jnp.int32)
    rank0 = blk_first - padded_start[block_expert]
    rank = rank0[:, None] + jnp.arange(MOE_BLOCK, dtype=jnp.int32)[None, :]
    valid = rank < counts[block_expert][:, None]
    sorted_pos = jnp.clip(group_start[block_expert][:, None] + rank, 0, TK - 1)
    filler = (blk_first[:, None] + jnp.arange(MOE_BLOCK, dtype=jnp.int32)[None, :]) % T
    src_tok = jnp.where(valid, order[sorted_pos] // TOP_K, filler).reshape(-1)
    return src_tok, block_expert, n_used, dest.T.reshape(-1)


def _rope_tables(seq):
    pos = jnp.arange(seq, dtype=F32)
    inv = ROPE_THETA ** (-jnp.arange(0, DA_HEAD_DIM, 2, dtype=F32) / DA_HEAD_DIM)
    ang = pos[:, None] * inv[None, :]
    ang = jnp.concatenate([ang, ang], axis=-1)
    cos, sin = jnp.cos(ang), jnp.sin(ang)
    half = DA_HEAD_DIM // 2
    sin_signed = jnp.concatenate([-sin[:, :half], sin[:, half:]], axis=-1)
    return jnp.tile(cos, (1, 2)), jnp.tile(sin_signed, (1, 2))


def kernel(x, w_in, b_in, lambda_q1, lambda_k1, lambda_q2, lambda_k2, subln_g, rpb, w_branch_da, w_branch_na, w_out, ln1_g, ln1_b, w_router, b_router, w_mlp1, b_mlp1, w_mlp2, b_mlp2, ln2_g, ln2_b):
    B, S, D = x.shape
    depth = w_in.shape[0]
    n_experts = w_router.shape[2]
    T = B * S
    rows = S // GRID_W
    assert S % GRID_W == 0 and rows % NA_ROWS == 0 and rows >= NA_BAND
    alpha = (2 * depth) ** 0.25
    cos, sin = _rope_tables(S)
    row = lambda v: v.reshape(1, -1)

    for l in range(depth):
        lam_init = 0.8 - 0.6 * math.exp(-0.3 * l)
        qT, k, vT, qn, kn, vn, gates_br = _in_proj(x, w_in[l].astype(BF16), row(b_in[l]), cos, sin)
        a = _diff_attn(qT, k, vT, row(lambda_q1[l]), row(lambda_k1[l]), row(lambda_q2[l]),
                       row(lambda_k2[l]), subln_g[l].reshape(-1, 1), lam_init)
        nb = _na_attn(qn, kn, vn, _na_bias(rpb[l], rows))

        wr = jnp.pad(w_router[l], ((0, 0), (0, LANES - n_experts)))
        wr_h = wr.astype(BF16)
        wr = jnp.concatenate([wr_h, wr_h, (wr - wr_h.astype(F32)).astype(BF16)], axis=0)
        br = jnp.pad(row(b_router[l]), ((0, 0), (0, LANES - n_experts)), constant_values=NEG_BIG)
        x1, x1b, topi, gates = _merge(
            x.reshape(T, D), a.reshape(T, -1), nb.reshape(T, -1), gates_br.reshape(T, -1),
            w_branch_da[l].astype(BF16), w_branch_na[l].astype(BF16), w_out[l].astype(BF16),
            row(ln1_g[l]), row(ln1_b[l]), wr, br, alpha)

        src_tok, block_expert, n_used, dest = _route(topi[:, :TOP_K], n_experts)
        xs = jnp.take(x1b, src_tok, axis=0, mode="clip")
        ys = _moe_ffn(xs, block_expert, n_used, w_mlp1[l], _glu_group_bias(b_mlp1[l]),
                      w_mlp2[l], b_mlp2[l][:, None, :])
        y_slots = jnp.take(ys, dest, axis=0, mode="clip").reshape(TOP_K, T, D)
        x = _combine(x1, y_slots, gates, row(ln2_g[l]), row(ln2_b[l]), alpha).reshape(B, S, D)
    return x
```

```python
import functools
import math

import jax
import jax.numpy as jnp
from jax import lax
from jax.experimental import pallas as pl
from jax.experimental.pallas import tpu as pltpu

F32 = jnp.float32
BF16 = jnp.bfloat16

GRID_W = 64
DA_HEADS = 4
DA_HEAD_DIM = 64
DA_V_DIM = 2 * DA_HEAD_DIM
DA_WIDTH = DA_HEADS * DA_V_DIM
ROPE_THETA = 10000.0
NA_HEADS = 8
NA_HEAD_DIM = 64
NA_WIDTH = NA_HEADS * NA_HEAD_DIM
NA_KH = 8
NA_KW = 16
TOP_K = 4
SWIGLU_ALPHA = 1.702
SWIGLU_LIMIT = 7.0
MOE_BLOCK = 512
LN_EPS = 1e-5
RMS_EPS = 1e-5

LANES = 128
VMEM_LIMIT_BYTES = 56 * 1024 * 1024

LOG2E = math.log2(math.e)
NEG_BIG = -1e30

PROJ_TM = 512
DA_TQ = 512
DA_TK = 512
DA_UNROLL = 8
NA_ROWS = 8
NA_BAND = 16


def _cparams(sem):
    return pltpu.CompilerParams(dimension_semantics=sem, vmem_limit_bytes=VMEM_LIMIT_BYTES)


def _in_proj_kernel(x_ref, w_ref, b_ref, cos_ref, sin_ref,
                    qT_ref, k_ref, vT_ref, qn_ref, kn_ref, vn_ref, g_ref):
    xb = x_ref[...].astype(BF16)

    def seg(lo, hi):
        return jnp.dot(xb, w_ref[:, lo:hi], preferred_element_type=F32) + b_ref[:, lo:hi]

    cos = cos_ref[...]
    sin = sin_ref[...]
    lane = lax.broadcasted_iota(jnp.int32, cos.shape, 1)
    first_half = (lane % DA_HEAD_DIM) < (DA_HEAD_DIM // 2)

    def rope(y):
        outs = []
        for h in range(DA_HEADS):
            yh = y[:, h * LANES:(h + 1) * LANES]
            partner = jnp.where(first_half,
                                pltpu.roll(yh, LANES - DA_HEAD_DIM // 2, 1),
                                pltpu.roll(yh, DA_HEAD_DIM // 2, 1))
            outs.append(yh * cos + partner * sin)
        return jnp.concatenate(outs, axis=1)

    w = DA_WIDTH
    q = rope(seg(0, w)) * (DA_HEAD_DIM ** -0.5 * LOG2E)
    qT_ref[...] = q.T.astype(BF16)
    k_ref[...] = rope(seg(w, 2 * w)).astype(BF16)
    vT_ref[...] = seg(2 * w, 3 * w).T.astype(BF16)
    o = 3 * w
    qn_ref[...] = (seg(o, o + NA_WIDTH) * (NA_HEAD_DIM ** -0.5)).T.astype(BF16)
    kn_ref[...] = seg(o + NA_WIDTH, o + 2 * NA_WIDTH).astype(BF16)
    vn_ref[...] = seg(o + 2 * NA_WIDTH, o + 3 * NA_WIDTH).T.astype(BF16)
    g0 = o + 3 * NA_WIDTH
    gate_pre = seg(g0, w_ref.shape[1])
    g_ref[...] = (1.0 / (1.0 + jnp.exp(-gate_pre))).astype(BF16)


def _in_proj(x, w_in, b_in, cos, sin):
    B, S, D = x.shape
    tm = min(PROJ_TM, S)
    n_cols = w_in.shape[1]
    n_gate = n_cols - 3 * DA_WIDTH - 3 * NA_WIDTH
    tok = lambda width: pl.BlockSpec((None, tm, width), lambda b, i: (b, i, 0))
    tr = pl.BlockSpec((None, DA_WIDTH, tm), lambda b, i: (b, 0, i))
    const = lambda shape: pl.BlockSpec(shape, lambda b, i: (0,) * len(shape))
    out_shape = (
        jax.ShapeDtypeStruct((B, DA_WIDTH, S), BF16),
        jax.ShapeDtypeStruct((B, S, DA_WIDTH), BF16),
        jax.ShapeDtypeStruct((B, DA_WIDTH, S), BF16),
        jax.ShapeDtypeStruct((B, NA_WIDTH, S), BF16),
        jax.ShapeDtypeStruct((B, S, NA_WIDTH), BF16),
        jax.ShapeDtypeStruct((B, NA_WIDTH, S), BF16),
        jax.ShapeDtypeStruct((B, S, n_gate), BF16),
    )
    assert DA_WIDTH == NA_WIDTH
    return pl.pallas_call(
        _in_proj_kernel,
        grid=(B, S // tm),
        in_specs=[tok(D), const((D, n_cols)), const((1, n_cols)),
                  pl.BlockSpec((tm, LANES), lambda b, i: (i, 0)),
                  pl.BlockSpec((tm, LANES), lambda b, i: (i, 0))],
        out_specs=(tr, tok(DA_WIDTH), tr, tr, tok(NA_WIDTH), tr, tok(n_gate)),
        out_shape=out_shape,
        compiler_params=_cparams(("parallel", "parallel")),
        name="in_proj",
    )(x, w_in, b_in, cos, sin)


def _diff_attn_kernel(qT_ref, k_ref, vT_ref, lq1_ref, lk1_ref, lq2_ref, lk2_ref, g_ref,
                      o_ref, m_sc, l_sc, acc_sc, s_sc, *, tk, unroll, lam_init):
    qT = qT_ref[...]
    tq = qT.shape[1]
    row = lax.broadcasted_iota(jnp.int32, qT.shape, 0)
    zero = jnp.zeros_like(qT)
    qbd = jnp.concatenate([jnp.where(row < DA_HEAD_DIM, qT, zero),
                           jnp.where(row < DA_HEAD_DIM, zero, qT)], axis=1)

    m_sc[...] = jnp.full(m_sc.shape, -jnp.inf, F32)
    l_sc[...] = jnp.zeros(l_sc.shape, F32)
    acc_sc[...] = jnp.zeros(acc_sc.shape, F32)

    nk = k_ref.shape[0] // tk

    def scores(j):
        off = pl.multiple_of(j * tk, tk)
        return jnp.dot(k_ref[pl.ds(off, tk), :], qbd, preferred_element_type=F32)

    def accumulate(j, s):
        off = pl.multiple_of(j * tk, tk)
        m_prev = m_sc[...]
        m_new = jnp.maximum(m_prev, jnp.max(s, axis=0, keepdims=True))
        alpha = jnp.exp2(m_prev - m_new)
        p = jnp.exp2(s - m_new)
        l_sc[...] = alpha * l_sc[...] + jnp.sum(p, axis=0, keepdims=True)
        vb = vT_ref[:, pl.ds(off, tk)]
        acc_sc[...] = alpha * acc_sc[...] + jnp.dot(vb, p.astype(BF16), preferred_element_type=F32)
        m_sc[...] = m_new

    s_sc[0] = scores(0)

    def body(jj, carry):
        for u in range(unroll):
            j = unroll * jj + u
            s_sc[(u + 1) % 2] = scores(jnp.minimum(j + 1, nk - 1))
            accumulate(j, s_sc[u % 2])
        return carry

    lax.fori_loop(0, nk // unroll, body, 0)

    lam = (jnp.exp(jnp.sum(lq1_ref[...] * lk1_ref[...], axis=1, keepdims=True))
           - jnp.exp(jnp.sum(lq2_ref[...] * lk2_ref[...], axis=1, keepdims=True)) + lam_init)
    on = acc_sc[...] / l_sc[...]
    o = on[:, :tq] - lam * on[:, tq:]
    ms = jnp.mean(o * o, axis=0, keepdims=True)
    o = o * lax.rsqrt(ms + RMS_EPS) * g_ref[...]
    o = o * (1.0 - lam_init)
    o_ref[...] = o.T.astype(BF16)


def _diff_attn(qT, k, vT, lq1, lk1, lq2, lk2, subln_g, lam_init):
    B, S, _ = k.shape
    tq = min(DA_TQ, S)
    tk = min(DA_TK, S // 2)
    unroll = math.gcd(DA_UNROLL, S // tk)
    assert unroll % 2 == 0 and S % tk == 0 and S % tq == 0
    vec =pl.BlockSpec((1, DA_HEAD_DIM), lambda b, h, i: (0, 0))
    kernel = functools.partial(_diff_attn_kernel, tk=tk, unroll=unroll, lam_init=lam_init)
    return pl.pallas_call(
        kernel,
        grid=(B, DA_HEADS, S // tq),
        in_specs=[pl.BlockSpec((None, DA_V_DIM, tq), lambda b, h, i: (b, h, i)),
                  pl.BlockSpec((None, S, DA_V_DIM), lambda b, h, i: (b, 0, h)),
                  pl.BlockSpec((None, DA_V_DIM, S), lambda b, h, i: (b, h, 0)),
                  vec, vec, vec, vec,
                  pl.BlockSpec((DA_V_DIM, 1), lambda b, h, i: (0, 0))],
        out_specs=pl.BlockSpec((None, tq, DA_V_DIM), lambda b, h, i: (b, i, h)),
        out_shape=jax.ShapeDtypeStruct((B, S, DA_WIDTH), BF16),
        scratch_shapes=[pltpu.VMEM((1, 2 * tq), F32), pltpu.VMEM((1, 2 * tq), F32),
                        pltpu.VMEM((DA_V_DIM, 2 * tq), F32), pltpu.VMEM((2, tk, 2 * tq), F32)],
        compiler_params=_cparams(("parallel", "parallel", "arbitrary")),
        name="diff_attn",
    )(qT, k, vT, lq1, lk1, lq2, lk2, subln_g)


def _na_band_start(r0, rows):
    return jnp.clip(r0 - NA_KH // 2, 0, rows - NA_BAND)


def _na_bias(rpb, rows):
    nblk = rows // NA_ROWS
    r0 = jnp.array([0, NA_ROWS * min(1, nblk - 1), NA_ROWS * (nblk - 1)], jnp.int32)
    band = _na_band_start(r0, rows)
    qr = r0[:, None] + jnp.arange(NA_ROWS)[None, :]
    rs = jnp.clip(qr - NA_KH // 2, 0, rows - NA_KH)
    kr = band[:, None] + jnp.arange(NA_BAND)[None, :]
    row_ok = (kr[:, None, :] >= rs[:, :, None]) & (kr[:, None, :] < rs[:, :, None] + NA_KH)
    row_off = jnp.clip(kr[:, None, :] - qr[:, :, None] + (NA_KH - 1), 0, 2 * NA_KH - 2)
    cols = jnp.arange(GRID_W)
    cs = jnp.clip(cols - NA_KW // 2, 0, GRID_W - NA_KW)
    col_ok = (cols[None, :] >= cs[:, None]) & (cols[None, :] < cs[:, None] + NA_KW)
    col_off = jnp.clip(cols[None, :] - cols[:, None] + (NA_KW - 1), 0, 2 * NA_KW - 2)
    tiles = jnp.where(col_ok[None, None], rpb.astype(F32)[:, :, col_off], NEG_BIG)
    sel = jnp.where(row_ok[None, :, :, :, None, None], tiles[:, row_off], NEG_BIG)
    nq, nk = NA_ROWS * GRID_W, NA_BAND * GRID_W
    sel = sel.reshape(NA_HEADS // 2, 2, 3, NA_ROWS, NA_BAND, GRID_W, GRID_W)
    sel = sel.transpose(2, 0, 1, 4, 6, 3, 5)
    return sel.reshape(3, NA_HEADS // 2, 2 * nk, nq)


def _na_kernel(qT_ref, k_ref, vT_ref, bias_ref, o_ref, *, rows):
    i = pl.program_id(2)
    nk = NA_BAND * GRID_W
    off = pl.multiple_of(_na_band_start(i * NA_ROWS, rows) * GRID_W, 2 * GRID_W)
    kb = k_ref[pl.ds(off, nk), :]
    vT = vT_ref[:, pl.ds(off, nk)]
    lane = lax.broadcasted_iota(jnp.int32, kb.shape, 1)
    kbd = jnp.concatenate([jnp.where(lane < NA_HEAD_DIM, kb, jnp.zeros_like(kb)),
                           jnp.where(lane < NA_HEAD_DIM, jnp.zeros_like(kb), kb)], axis=0)
    row = lax.broadcasted_iota(jnp.int32, vT.shape, 0)
    vbdT = jnp.concatenate([jnp.where(row < NA_HEAD_DIM, vT, jnp.zeros_like(vT)),
                            jnp.where(row < NA_HEAD_DIM, jnp.zeros_like(vT), vT)], axis=1)
    s = jnp.dot(kbd, qT_ref[...], preferred_element_type=F32) + bias_ref[...]
    ps, ls = [], []
    for h in range(2):
        sh = s[h * nk:(h + 1) * nk]
        ph = jnp.exp(sh - jnp.max(sh, axis=0, keepdims=True))
        ps.append(ph.astype(BF16))
        ls.append(jnp.sum(ph, axis=0, keepdims=True))
    oT = jnp.dot(vbdT, jnp.concatenate(ps, axis=0), preferred_element_type=F32)
    row_o = lax.broadcasted_iota(jnp.int32, oT.shape, 0)
    oT = oT / jnp.where(row_o < NA_HEAD_DIM, ls[0], ls[1])
    o_ref[...] = oT.T.astype(BF16)


def _na_attn(qnT, kn, vnT, bias):
    B, S, _ = kn.shape
    rows = S // GRID_W
    nblk = rows // NA_ROWS
    nq = NA_ROWS * GRID_W

    def variant(i):
        return jnp.where(i == 0, 0, jnp.where(i == nblk - 1, 2, 1))

    return pl.pallas_call(
        functools.partial(_na_kernel, rows=rows),
        grid=(B, NA_HEADS // 2, nblk),
        in_specs=[pl.BlockSpec((None, LANES, nq), lambda b, h, i: (b, h, i)),
                  pl.BlockSpec((None, S, LANES), lambda b, h, i: (b, 0, h)),
                  pl.BlockSpec((None, LANES, S), lambda b, h, i: (b, h, 0)),
                  pl.BlockSpec((None, None, bias.shape[2], nq), lambda b, h, i: (variant(i), h, 0, 0))],
        out_specs=pl.BlockSpec((None, nq, LANES), lambda b, h, i: (b, i, h)),
        out_shape=jax.ShapeDtypeStruct((B, S, NA_WIDTH), BF16),
        compiler_params=_cparams(("parallel", "parallel", "arbitrary")),
        name="na_attn",
    )(qnT, kn, vnT, bias)


def _layer_norm(h, g, b):
    mu = jnp.mean(h, axis=-1, keepdims=True)
    d = h - mu
    var = jnp.mean(d * d, axis=-1, keepdims=True)
    return d * lax.rsqrt(var + LN_EPS) * g + b


def _merge_kernel(x_ref, a_ref, nb_ref, g_ref, wda_ref, wna_ref, wout_ref, lng_ref, lnb_ref,
                  wr_ref, br_ref, x1_ref, x1b_ref, topi_ref, gate_ref, *, alpha):
    d = x_ref.shape[1]
    ya = jnp.dot(a_ref[...], wda_ref[...], preferred_element_type=F32)
    yb = jnp.dot(nb_ref[...], wna_ref[...], preferred_element_type=F32)
    g = g_ref[...].astype(F32)
    merged = g[:, :d] * ya + g[:, d:] * yb
    mix = jnp.dot(merged.astype(BF16), wout_ref[...], preferred_element_type=F32)
    x1 = _layer_norm(alpha * x_ref[...] + mix, lng_ref[...], lnb_ref[...])
    x1_ref[...] = x1
    x1b_ref[...] = x1.astype(BF16)
    xh = x1.astype(BF16)
    xl = (x1 - xh.astype(F32)).astype(BF16)
    logits = jnp.dot(jnp.concatenate([xh, xl, xh], axis=1), wr_ref[...],
                     preferred_element_type=F32) + br_ref[...]
    lane = lax.broadcasted_iota(jnp.int32, logits.shape, 1).astype(F32)
    vals, idxs = [], []
    for _ in range(TOP_K):
        m = jnp.max(logits, axis=1, keepdims=True)
        idx = jnp.min(jnp.where(logits == m, lane, float(LANES)), axis=1, keepdims=True)
        vals.append(m)
        idxs.append(idx)
        logits = jnp.where(lane == idx, -jnp.inf, logits)
    es = [jnp.exp(v - vals[0]) for v in vals]
    denom = es[0] + es[1] + es[2] + es[3]
    gates = jnp.zeros(logits.shape, F32)
    topi = jnp.zeros(logits.shape, F32)
    for k in range(TOP_K):
        gates = jnp.where(lane == k, es[k] / denom, gates)
        topi = jnp.where(lane == k, idxs[k], topi)
    gate_ref[...] = gates
    topi_ref[...] = topi.astype(jnp.int32)


def _merge(x2, a2, nb2, g2, wda, wna, wout, ln_g, ln_b, wr, br, alpha):
    T, D = x2.shape
    tm = min(PROJ_TM, T)
    tok = lambda width: pl.BlockSpec((tm, width), lambda i: (i, 0))
    const = lambda shape: pl.BlockSpec(shape, lambda i: (0,) * len(shape))
    return pl.pallas_call(
        functools.partial(_merge_kernel, alpha=alpha),
        grid=(T // tm,),
        in_specs=[tok(D), tok(a2.shape[1]), tok(nb2.shape[1]), tok(g2.shape[1]),
                  const(wda.shape), const(wna.shape), const(wout.shape),
                  const((1, D)), const((1, D)), const(wr.shape), const((1, LANES))],
        out_specs=(tok(D), tok(D), tok(LANES), tok(LANES)),
        out_shape=(jax.ShapeDtypeStruct((T, D), F32), jax.ShapeDtypeStruct((T, D), BF16),
                   jax.ShapeDtypeStruct((T, LANES), jnp.int32), jax.ShapeDtypeStruct((T, LANES), F32)),
        compiler_params=_cparams(("parallel",)),
        name="merge",
    )(x2, a2, nb2, g2, wda, wna, wout, ln_g, ln_b, wr, br)


def _glu_group_bias(b1):
    e, n = b1.shape
    return b1.reshape(e, n // (2 * LANES), LANES, 2).transpose(0, 1, 3, 2).reshape(e, 1, n)


def _moe_kernel(be_ref, nused_ref, xs_ref, w1_ref, b1_ref, w2_ref, b2_ref, ys_ref, w1g_sc, w2b_sc):
    i = pl.program_id(0)
    used = i < nused_ref[0]
    fresh = jnp.logical_or(i == 0, be_ref[i] != be_ref[jnp.maximum(i - 1, 0)])
    grp = 2 * LANES

    @pl.when(jnp.logical_and(used, fresh))
    def _():
        r = lax.broadcasted_iota(jnp.int32, (grp, grp), 0)
        c = lax.broadcasted_iota(jnp.int32, (grp, grp), 1)
        src = jnp.where(c < LANES, 2 * c, 2 * (c - LANES) + 1)
        perm = jnp.where(r == src, 1.0, 0.0).astype(BF16)
        for g in range(w1_ref.shape[1] // grp):
            blk = w1_ref[:, g * grp:(g + 1) * grp].astype(BF16)
            w1g_sc[:, g * grp:(g + 1) * grp] = jnp.dot(blk, perm, preferred_element_type=F32).astype(BF16)
        w2b_sc[...] = w2_ref[...].astype(BF16)

    @pl.when(used)
    def _():
        xb = xs_ref[...]
        h = jnp.dot(xb, w1g_sc[...], preferred_element_type=F32) + b1_ref[...]
        acts = []
        for g in range(h.shape[1] // grp):
            x_glu = jnp.minimum(h[:, g * grp:g * grp + LANES], SWIGLU_LIMIT)
            x_lin = jnp.clip(h[:, g * grp + LANES:(g + 1) * grp], -SWIGLU_LIMIT, SWIGLU_LIMIT)
            acts.append(x_glu * (1.0 / (1.0 + jnp.exp(-SWIGLU_ALPHA * x_glu))) * (x_lin + 1.0))
        act = jnp.concatenate(acts, axis=1).astype(BF16)
        y = jnp.dot(act, w2b_sc[...], preferred_element_type=F32) + b2_ref[...]
        ys_ref[...] = y.astype(ys_ref.dtype)

    @pl.when(jnp.logical_not(used))
    def _():
        ys_ref[...] = jnp.zeros(ys_ref.shape, ys_ref.dtype)


def _moe_ffn(xs, block_expert, n_used, w1, b1g, w2, b2):
    P, D = xs.shape
    n_blocks = P // MOE_BLOCK
    de2 = w1.shape[2]
    de = w2.shape[1]
    wspec = lambda r, c: pl.BlockSpec((None, r, c), lambda i, be, nu: (be[i], 0, 0))
    grid_spec = pltpu.PrefetchScalarGridSpec(
        num_scalar_prefetch=2,
        grid=(n_blocks,),
        in_specs=[pl.BlockSpec((MOE_BLOCK, D), lambda i, be, nu: (i, 0)),
                  wspec(D, de2), wspec(1, de2), wspec(de, D), wspec(1, D)],
        out_specs=pl.BlockSpec((MOE_BLOCK, D), lambda i, be, nu: (i, 0)),
        scratch_shapes=[pltpu.VMEM((D, de2), BF16), pltpu.VMEM((de, D), BF16)],
    )
    return pl.pallas_call(
        _moe_kernel,
        grid_spec=grid_spec,
        out_shape=jax.ShapeDtypeStruct((P, D), BF16),
        compiler_params=_cparams(("arbitrary",)),
        name="moe_ffn",
    )(block_expert, n_used, xs, w1, b1g, w2, b2)


def _combine_kernel(x1_ref, y0_ref, y1_ref, y2_ref, y3_ref, gate_ref, lng_ref, lnb_ref, o_ref, *, alpha):
    gates = gate_ref[...]
    ffn = jnp.zeros(x1_ref.shape, F32)
    for k, y_ref in enumerate((y0_ref, y1_ref, y2_ref, y3_ref)):
        ffn = ffn + y_ref[...].astype(F32) * gates[:, k:k + 1]
    o_ref[...] = _layer_norm(alpha * x1_ref[...] + ffn, lng_ref[...], lnb_ref[...])


def _combine(x1, y_slots, gates, ln_g, ln_b, alpha):
    T, D = x1.shape
    tm = min(PROJ_TM, T)
    tok = lambda width: pl.BlockSpec((tm, width), lambda i: (i, 0))
    const = lambda shape: pl.BlockSpec(shape, lambda i: (0,) * len(shape))
    slot = lambda k: pl.BlockSpec((None, tm, D), lambda i: (k, i, 0))
    return pl.pallas_call(
        functools.partial(_combine_kernel, alpha=alpha),
        grid=(T // tm,),
        in_specs=[tok(D)] + [slot(k) for k in range(TOP_K)] + [tok(LANES), const((1, D)), const((1, D))],
        out_specs=tok(D),
        out_shape=jax.ShapeDtypeStruct((T, D), F32),
        compiler_params=_cparams(("parallel",)),
        name="combine",
    )(x1, y_slots, y_slots, y_slots, y_slots, gates, ln_g, ln_b)


def _route(top_i, n_experts):
    T = top_i.shape[0]
    TK = T * TOP_K
    experts = jnp.arange(n_experts, dtype=jnp.int32)
    chosen = top_i[:, :, None] == experts
    picks = jnp.sum(chosen, axis=1, dtype=jnp.int32)
    csum = jnp.cumsum(picks, axis=0)
    counts = csum[-1]
    padded = ((counts + MOE_BLOCK - 1) // MOE_BLOCK) * MOE_BLOCK
    padded_end = jnp.cumsum(padded)
    padded_start = padded_end - padded
    group_start = jnp.cumsum(counts) - counts
    slot_te = padded_start[None, :] + csum - picks
    dest = jnp.sum(jnp.where(chosen, slot_te[:, None, :], 0), axis=2)
    n_blocks = -(-TK // MOE_BLOCK) + n_experts
    blk_first = jnp.arange(n_blocks, dtype=jnp.int32) * MOE_BLOCK
    block_expert = jnp.minimum(jnp.sum(padded_end[None, :] <= blk_first[:, None], axis=1),
                               n_experts - 1).astype(jnp.int32)
    n_used = (padded_end[-1] // MOE_BLOCK).astype(jnp.int32).reshape(1)
    order = jnp.argsort(top_i.reshape(-1), stable=True).astype(jnp.int32)
    rank0 = blk_first - padded_start[block_expert]
    rank = rank0[:, None] + jnp.arange(MOE_BLOCK, dtype=jnp.int32)[None, :]
    valid = rank < counts[block_expert][:, None]
    sorted_pos = jnp.clip(group_start[block_expert][:, None] + rank, 0, TK - 1)
    filler = (blk_first[:, None] + jnp.arange(MOE_BLOCK, dtype=jnp.int32)[None, :]) % T
    src_tok = jnp.where(valid, order[sorted_pos] // TOP_K, filler).reshape(-1)
    return src_tok, block_expert, n_used, dest.T.reshape(-1)


def _rope_tables(seq):
    pos = jnp.arange(seq, dtype=F32)
    inv = ROPE_THETA ** (-jnp.arange(0, DA_HEAD_DIM, 2, dtype=F32) / DA_HEAD_DIM)
    ang = pos[:, None] * inv[None, :]
    ang = jnp.concatenate([ang, ang], axis=-1)
    cos, sin = jnp.cos(ang), jnp.sin(ang)
    half = DA_HEAD_DIM // 2
    sin_signed = jnp.concatenate([-sin[:, :half], sin[:, half:]], axis=-1)
    return jnp.tile(cos, (1, 2)), jnp.tile(sin_signed, (1, 2))


def kernel(x, w_in, b_in, lambda_q1, lambda_k1, lambda_q2, lambda_k2, subln_g, rpb, w_branch_da, w_branch_na, w_out, ln1_g, ln1_b, w_router, b_router, w_mlp1, b_mlp1, w_mlp2, b_mlp2, ln2_g, ln2_b):
    B, S, D = x.shape
    depth = w_in.shape[0]
    n_experts = w_router.shape[2]
    T = B * S
    rows = S // GRID_W
    assert S % GRID_W == 0 and rows % NA_ROWS == 0 and rows >= NA_BAND
    alpha = (2 * depth) ** 0.25
    cos, sin = _rope_tables(S)
    row = lambda v: v.reshape(1, -1)

    for l in range(depth):
        lam_init = 0.8 - 0.6 * math.exp(-0.3 * l)
        qT, k, vT, qn, kn, vn, gates_br = _in_proj(x, w_in[l].astype(BF16), row(b_in[l]), cos, sin)
        a = _diff_attn(qT, k, vT, row(lambda_q1[l]), row(lambda_k1[l]), row(lambda_q2[l]),
                       row(lambda_k2[l]), subln_g[l].reshape(-1, 1), lam_init)
        nb = _na_attn(qn, kn, vn, _na_bias(rpb[l], rows))

        wr = jnp.pad(w_router[l], ((0, 0), (0, LANES - n_experts)))
        wr_h = wr.astype(BF16)
        wr = jnp.concatenate([wr_h, wr_h, (wr - wr_h.astype(F32)).astype(BF16)], axis=0)
        br = jnp.pad(row(b_router[l]), ((0, 0), (0, LANES - n_experts)), constant_values=NEG_BIG)
        x1, x1b, topi, gates = _merge(
            x.reshape(T, D), a.reshape(T, -1), nb.reshape(T, -1), gates_br.reshape(T, -1),
            w_branch_da[l].astype(BF16), w_branch_na[l].astype(BF16), w_out[l].astype(BF16),
            row(ln1_g[l]), row(ln1_b[l]), wr, br, alpha)

        src_tok, block_expert, n_used, dest = _route(topi[:, :TOP_K], n_experts)
        xs = jnp.take(x1b, src_tok, axis=0, mode="clip")
        ys = _moe_ffn(xs, block_expert, n_used, w_mlp1[l], _glu_group_bias(b_mlp1[l]),
                      w_mlp2[l], b_mlp2[l][:, None, :])
        y_slots = jnp.take(ys, dest, axis=0, mode="clip").reshape(TOP_K, T, D)
        x = _combine(x1, y_slots, gates, row(ln2_g[l]), row(ln2_b[l]), alpha).reshape(B, S, D)
    return x
```

```python
import functools
import math

import jax
import jax.numpy as jnp
from jax import lax
from jax.experimental import pallas as pl
from jax.experimental.pallas import tpu as pltpu

F32 = jnp.float32
BF16 = jnp.bfloat16

GRID_W = 64
DA_HEADS = 4
DA_HEAD_DIM = 64
DA_V_DIM = 2 * DA_HEAD_DIM
DA_WIDTH = DA_HEADS * DA_V_DIM
ROPE_THETA = 10000.0
NA_HEADS = 8
NA_HEAD_DIM = 64
NA_WIDTH = NA_HEADS * NA_HEAD_DIM
NA_KH = 8
NA_KW = 16
TOP_K = 4
SWIGLU_ALPHA = 1.702
SWIGLU_LIMIT = 7.0
MOE_BLOCK = 512
LN_EPS = 1e-5
RMS_EPS = 1e-5

LANES = 128
VMEM_LIMIT_BYTES = 56 * 1024 * 1024

LOG2E = math.log2(math.e)
NEG_BIG = -1e30

PROJ_TM = 512
DA_TQ = 512
DA_TK = 512
DA_UNROLL = 8
NA_ROWS = 8
NA_BAND = 16


def _cparams(sem):
    return pltpu.CompilerParams(dimension_semantics=sem, vmem_limit_bytes=VMEM_LIMIT_BYTES)


def _in_proj_kernel(x_ref, w_ref, b_ref, cos_ref, sin_ref,
                    qT_ref, k_ref, vT_ref, qn_ref, kn_ref, vn_ref, g_ref):
    xb = x_ref[...].astype(BF16)

    def seg(lo, hi):
        return jnp.dot(xb, w_ref[:, lo:hi], preferred_element_type=F32) + b_ref[:, lo:hi]

    cos = cos_ref[...]
    sin = sin_ref[...]
    lane = lax.broadcasted_iota(jnp.int32, cos.shape, 1)
    first_half = (lane % DA_HEAD_DIM) < (DA_HEAD_DIM // 2)

    def rope(y):
        outs = []
        for h in range(DA_HEADS):
            yh = y[:, h * LANES:(h + 1) * LANES]
            partner = jnp.where(first_half,
                                pltpu.roll(yh, LANES - DA_HEAD_DIM // 2, 1),
                                pltpu.roll(yh, DA_HEAD_DIM // 2, 1))
            outs.append(yh * cos + partner * sin)
        return jnp.concatenate(outs, axis=1)

    w = DA_WIDTH
    q = rope(seg(0, w)) * (DA_HEAD_DIM ** -0.5 * LOG2E)
    qT_ref[...] = q.T.astype(BF16)
    k_ref[...] = rope(seg(w, 2 * w)).astype(BF16)
    vT_ref[...] = seg(2 * w, 3 * w).T.astype(BF16)
    o = 3 * w
    qn_ref[...] = (seg(o, o + NA_WIDTH) * (NA_HEAD_DIM ** -0.5)).T.astype(BF16)
    kn_ref[...] = seg(o + NA_WIDTH, o + 2 * NA_WIDTH).astype(BF16)
    vn_ref[...] = seg(o + 2 * NA_WIDTH, o + 3 * NA_WIDTH).T.astype(BF16)
    g0 = o + 3 * NA_WIDTH
    gate_pre = seg(g0, w_ref.shape[1])
    g_ref[...] = (1.0 / (1.0 + jnp.exp(-gate_pre))).astype(BF16)


def _in_proj(x, w_in, b_in, cos, sin):
    B, S, D = x.shape
    tm = min(PROJ_TM, S)
    n_cols = w_in.shape[1]
    n_gate = n_cols - 3 * DA_WIDTH - 3 * NA_WIDTH
    tok = lambda width: pl.BlockSpec((None, tm, width), lambda b, i: (b, i, 0))
    tr = pl.BlockSpec((None, DA_WIDTH, tm), lambda b, i: (b, 0, i))
    const = lambda shape: pl.BlockSpec(shape, lambda b, i: (0,) * len(shape))
    out_shape = (
        jax.ShapeDtypeStruct((B, DA_WIDTH, S), BF16),
        jax.ShapeDtypeStruct((B, S, DA_WIDTH), BF16),
        jax.ShapeDtypeStruct((B, DA_WIDTH, S), BF16),
        jax.ShapeDtypeStruct((B, NA_WIDTH, S), BF16),
        jax.ShapeDtypeStruct((B, S, NA_WIDTH), BF16),
        jax.ShapeDtypeStruct((B, NA_WIDTH, S), BF16),
        jax.ShapeDtypeStruct((B, S, n_gate), BF16),
    )
    assert DA_WIDTH == NA_WIDTH
    return pl.pallas_call(
        _in_proj_kernel,
        grid=(B, S // tm),
        in_specs=[tok(D), const((D, n_cols)), const((1, n_cols)),
                  pl.BlockSpec((tm, LANES), lambda b, i: (i, 0)),
                  pl.BlockSpec((tm, LANES), lambda b, i: (i, 0))],
        out_specs=(tr, tok(DA_WIDTH), tr, tr, tok(NA_WIDTH), tr, tok(n_gate)),
        out_shape=out_shape,
        compiler_params=_cparams(("parallel", "parallel")),
        name="in_proj",
    )(x, w_in, b_in, cos, sin)


def _diff_attn_kernel(qT_ref, k_ref, vT_ref, lq1_ref, lk1_ref, lq2_ref, lk2_ref, g_ref,
                      o_ref, m_sc, l_sc, acc_sc, s_sc, *, tk, unroll, lam_init):
    qT = qT_ref[...]
    tq = qT.shape[1]
    row = lax.broadcasted_iota(jnp.int32, qT.shape, 0)
    zero = jnp.zeros_like(qT)
    qbd = jnp.concatenate([jnp.where(row < DA_HEAD_DIM, qT, zero),
                           jnp.where(row < DA_HEAD_DIM, zero, qT)], axis=1)

    m_sc[...] = jnp.full(m_sc.shape, -jnp.inf, F32)
    l_sc[...] = jnp.zeros(l_sc.shape, F32)
    acc_sc[...] = jnp.zeros(acc_sc.shape, F32)

    nk = k_ref.shape[0] // tk

    def scores(j):
        off = pl.multiple_of(j * tk, tk)
        return jnp.dot(k_ref[pl.ds(off, tk), :], qbd, preferred_element_type=F32)

    def accumulate(j, s):
        off = pl.multiple_of(j * tk, tk)
        m_prev = m_sc[...]
        m_new = jnp.maximum(m_prev, jnp.max(s, axis=0, keepdims=True))
        alpha = jnp.exp2(m_prev - m_new)
        p = jnp.exp2(s - m_new)
        l_sc[...] = alpha * l_sc[...] + jnp.sum(p, axis=0, keepdims=True)
        vb = vT_ref[:, pl.ds(off, tk)]
        acc_sc[...] = alpha * acc_sc[...] + jnp.dot(vb, p.astype(BF16), preferred_element_type=F32)
        m_sc[...] = m_new

    s_sc[0] = scores(0)

    def body(jj, carry):
        for u in range(unroll):
            j = unroll * jj + u
            s_sc[(u + 1) % 2] = scores(jnp.minimum(j + 1, nk - 1))
            accumulate(j, s_sc[u % 2])
        return carry

    lax.fori_loop(0, nk // unroll, body, 0)

    lam = (jnp.exp(jnp.sum(lq1_ref[...] * lk1_ref[...], axis=1, keepdims=True))
           - jnp.exp(jnp.sum(lq2_ref[...] * lk2_ref[...], axis=1, keepdims=True)) + lam_init)
    on = acc_sc[...] / l_sc[...]
    o = on[:, :tq] - lam * on[:, tq:]
    ms = jnp.mean(o * o, axis=0, keepdims=True)
    o = o * lax.rsqrt(ms + RMS_EPS) * g_ref[...]
    o = o * (1.0 - lam_init)
    o_ref[...] = o.T.astype(BF16)


def _diff_attn(qT, k, vT, lq1, lk1, lq2, lk2, subln_g, lam_init):
    B, S, _ = k.shape
    tq = min(DA_TQ, S)
    tk = min(DA_TK, S // 2)
    unroll = math.gcd(DA_UNROLL, S // tk)
    assert unroll % 2 == 0 and S % tk == 0 and S % tq == 0
    vec =pl.BlockSpec((1, DA_HEAD_DIM), lambda b, h, i: (0, 0))
    kernel = functools.partial(_diff_attn_kernel, tk=tk, unroll=unroll, lam_init=lam_init)
    return pl.pallas_call(
        kernel,
        grid=(B, DA_HEADS, S // tq),
        in_specs=[pl.BlockSpec((None, DA_V_DIM, tq), lambda b, h, i: (b, h, i)),
                  pl.BlockSpec((None, S, DA_V_DIM), lambda b, h, i: (b, 0, h)),
                  pl.BlockSpec((None, DA_V_DIM, S), lambda b, h, i: (b, h, 0)),
                  vec, vec, vec, vec,
                  pl.BlockSpec((DA_V_DIM, 1), lambda b, h, i: (0, 0))],
        out_specs=pl.BlockSpec((None, tq, DA_V_DIM), lambda b, h, i: (b, i, h)),
        out_shape=jax.ShapeDtypeStruct((B, S, DA_WIDTH), BF16),
        scratch_shapes=[pltpu.VMEM((1, 2 * tq), F32), pltpu.VMEM((1, 2 * tq), F32),
                        pltpu.VMEM((DA_V_DIM, 2 * tq), F32), pltpu.VMEM((2, tk, 2 * tq), F32)],
        compiler_params=_cparams(("parallel", "parallel", "arbitrary")),
        name="diff_attn",
    )(qT, k, vT, lq1, lk1, lq2, lk2, subln_g)


def _na_band_start(r0, rows):
    return jnp.clip(r0 - NA_KH // 2, 0, rows - NA_BAND)


def _na_tiles(rpb):
    cols = jnp.arange(GRID_W)
    cs = jnp.clip(cols - NA_KW // 2, 0, GRID_W - NA_KW)
    col_ok = (cols[:, None] >= cs[None, :]) & (cols[:, None] < cs[None, :] + NA_KW)
    col_off = jnp.clip(cols[:, None] - cols[None, :] + (NA_KW - 1), 0, 2 * NA_KW - 2)
    tiles = jnp.where(col_ok[None, None], rpb.astype(F32)[:, :, col_off], NEG_BIG)
    neg = jnp.full((rpb.shape[0], 1, GRID_W, GRID_W), NEG_BIG, F32)
    return jnp.concatenate([jnp.concatenate([tiles, neg], axis=1),
                            jnp.concatenate([neg, tiles], axis=1)], axis=3)


def _na_geometry(variant, rows):
    nblk = rows // NA_ROWS
    r0 = (0, NA_ROWS * min(1, nblk - 1), NA_ROWS * (nblk - 1))[variant]
    band = min(max(r0 - NA_KH // 2, 0), rows - NA_BAND)
    pairs = []
    for qp in range(NA_ROWS // 2):
        entries = []
        for kr in range(NA_BAND):
            ok = []
            for qr in (r0 + 2 * qp, r0 + 2 * qp + 1):
                rs = min(max(qr - NA_KH // 2, 0), rows - NA_KH)
                ok.append(rs <= band + kr < rs + NA_KH)
            if ok[0] or ok[1]:
                entries.append((kr, band + kr - (r0 + 2 * qp) + NA_KH - 1, ok[0], ok[1]))
        pairs.append(entries)
    return pairs


def _na_kernel(qT_ref, k_ref, vT_ref, tile_ref, o_ref, *, rows):
    i = pl.program_id(2)
    nblk = rows // NA_ROWS
    nk = NA_BAND * GRID_W
    off = pl.multiple_of(_na_band_start(i * NA_ROWS, rows) * GRID_W, 2 * GRID_W)

    def block(variant):
        kb = k_ref[pl.ds(off, nk), :]
        vT = vT_ref[:, pl.ds(off, nk)]
        lane = lax.broadcasted_iota(jnp.int32, kb.shape, 1)
        kbd = jnp.concatenate([jnp.where(lane < NA_HEAD_DIM, kb, jnp.zeros_like(kb)),
                               jnp.where(lane < NA_HEAD_DIM, jnp.zeros_like(kb), kb)], axis=0)
        row = lax.broadcasted_iota(jnp.int32, vT.shape, 0)
        vbdT = jnp.concatenate([jnp.where(row < NA_HEAD_DIM, vT, jnp.zeros_like(vT)),
                                jnp.where(row < NA_HEAD_DIM, jnp.zeros_like(vT), vT)], axis=1)
        s = jnp.dot(kbd, qT_ref[...], preferred_element_type=F32)
        left = lax.broadcasted_iota(jnp.int32, (GRID_W, LANES), 1) < GRID_W
        zero_slab = jnp.zeros((GRID_W, LANES), BF16)
        geometry = _na_geometry(variant, rows)
        p_heads, l_heads = [], []
        for h in range(2):
            p_cols, l_cols = [], []
            for qp, entries in enumerate(geometry):
                logits = {}
                for kr, d, ok_l, ok_r in entries:
                    bias = tile_ref[h, d]
                    if not ok_r:
                        bias = jnp.where(left, bias, NEG_BIG)
                    if not ok_l:
                        bias = jnp.where(left, NEG_BIG, bias)
                    r = h * nk + kr * GRID_W
                    logits[kr] = s[r:r + GRID_W, qp * LANES:(qp + 1) * LANES] + bias
                m = functools.reduce(jnp.maximum, logits.values())
                m = jnp.max(m, axis=0, keepdims=True)
                probs = {kr: jnp.exp(x - m) for kr, x in logits.items()}
                l_cols.append(jnp.sum(functools.reduce(jnp.add, probs.values()), axis=0, keepdims=True))
                p_cols.append(jnp.concatenate(
                    [probs[kr].astype(BF16) if kr in probs else zero_slab for kr in range(NA_BAND)], axis=0))
            p_heads.append(jnp.concatenate(p_cols, axis=1))
            l_heads.append(jnp.concatenate(l_cols, axis=1))
        oT = jnp.dot(vbdT, jnp.concatenate(p_heads, axis=0), preferred_element_type=F32)
        row_o = lax.broadcasted_iota(jnp.int32, oT.shape, 0)
        oT = oT / jnp.where(row_o < NA_HEAD_DIM, l_heads[0], l_heads[1])
        o_ref[...] = oT.T.astype(BF16)

    pl.when(i == 0)(lambda: block(0))
    pl.when(jnp.logical_and(i > 0, i < nblk - 1))(lambda: block(1))
    pl.when(jnp.logical_and(i > 0, i == nblk - 1))(lambda: block(2))


def _na_attn(qnT, kn, vnT, tiles):
    B, S, _ = kn.shape
    rows = S // GRID_W
    nblk = rows // NA_ROWS
    nq = NA_ROWS * GRID_W
    return pl.pallas_call(
        functools.partial(_na_kernel, rows=rows),
        grid=(B, NA_HEADS // 2, nblk),
        in_specs=[pl.BlockSpec((None, LANES, nq), lambda b, h, i: (b, h, i)),
                  pl.BlockSpec((None, S, LANES), lambda b, h, i: (b, 0, h)),
                  pl.BlockSpec((None, LANES, S), lambda b, h, i: (b, h, 0)),
                  pl.BlockSpec((2,) + tiles.shape[1:], lambda b, h, i: (h, 0, 0, 0))],
        out_specs=pl.BlockSpec((None, nq, LANES), lambda b, h, i: (b, i, h)),
        out_shape=jax.ShapeDtypeStruct((B, S, NA_WIDTH), BF16),
        compiler_params=_cparams(("parallel", "parallel", "arbitrary")),
        name="na_attn",
    )(qnT, kn, vnT, tiles)


def _layer_norm(h, g, b):
    mu = jnp.mean(h, axis=-1, keepdims=True)
    d = h - mu
    var = jnp.mean(d * d, axis=-1, keepdims=True)
    return d * lax.rsqrt(var + LN_EPS) * g + b


def _merge_kernel(x_ref, a_ref, nb_ref, g_ref, wda_ref, wna_ref, wout_ref, lng_ref, lnb_ref,
                  wr_ref, br_ref, x1_ref, x1b_ref, topi_ref, gate_ref, *, alpha):
    d = x_ref.shape[1]
    ya = jnp.dot(a_ref[...], wda_ref[...], preferred_element_type=F32)
    yb = jnp.dot(nb_ref[...], wna_ref[...], preferred_element_type=F32)
    g = g_ref[...].astype(F32)
    merged = g[:, :d] * ya + g[:, d:] * yb
    mix = jnp.dot(merged.astype(BF16), wout_ref[...], preferred_element_type=F32)
    x1 = _layer_norm(alpha * x_ref[...] + mix, lng_ref[...], lnb_ref[...])
    x1_ref[...] = x1
    x1b_ref[...] = x1.astype(BF16)
    xh = x1.astype(BF16)
    xl = (x1 - xh.astype(F32)).astype(BF16)
    logits = jnp.dot(jnp.concatenate([xh, xl, xh], axis=1), wr_ref[...],
                     preferred_element_type=F32) + br_ref[...]
    lane = lax.broadcasted_iota(jnp.int32, logits.shape, 1).astype(F32)
    vals, idxs = [], []
    for _ in range(TOP_K):
        m = jnp.max(logits, axis=1, keepdims=True)
        idx = jnp.min(jnp.where(logits == m, lane, float(LANES)), axis=1, keepdims=True)
        vals.append(m)
        idxs.append(idx)
        logits = jnp.where(lane == idx, -jnp.inf, logits)
    es = [jnp.exp(v - vals[0]) for v in vals]
    denom = es[0] + es[1] + es[2] + es[3]
    gates = jnp.zeros(logits.shape, F32)
    topi = jnp.zeros(logits.shape, F32)
    for k in range(TOP_K):
        gates = jnp.where(lane == k, es[k] / denom, gates)
        topi = jnp.where(lane == k, idxs[k], topi)
    gate_ref[...] = gates
    topi_ref[...] = topi.astype(jnp.int32)


def _merge(x2, a2, nb2, g2, wda, wna, wout, ln_g, ln_b, wr, br, alpha):
    T, D = x2.shape
    tm = min(PROJ_TM, T)
    tok = lambda width: pl.BlockSpec((tm, width), lambda i: (i, 0))
    const = lambda shape: pl.BlockSpec(shape, lambda i: (0,) * len(shape))
    return pl.pallas_call(
        functools.partial(_merge_kernel, alpha=alpha),
        grid=(T // tm,),
        in_specs=[tok(D), tok(a2.shape[1]), tok(nb2.shape[1]), tok(g2.shape[1]),
                  const(wda.shape), const(wna.shape), const(wout.shape),
                  const((1, D)), const((1, D)), const(wr.shape), const((1, LANES))],
        out_specs=(tok(D), tok(D), tok(LANES), tok(LANES)),
        out_shape=(jax.ShapeDtypeStruct((T, D), F32), jax.ShapeDtypeStruct((T, D), BF16),
                   jax.ShapeDtypeStruct((T, LANES), jnp.int32), jax.ShapeDtypeStruct((T, LANES), F32)),
        compiler_params=_cparams(("parallel",)),
        name="merge",
    )(x2, a2, nb2, g2, wda, wna, wout, ln_g, ln_b, wr, br)


def _glu_group_bias(b1):
    e, n = b1.shape
    return b1.reshape(e, n // (2 * LANES), LANES, 2).transpose(0, 1, 3, 2).reshape(e, 1, n)


def _moe_kernel(be_ref, nused_ref, xs_ref, w1_ref, b1_ref, w2_ref, b2_ref, ys_ref, w1g_sc, w2b_sc):
    i = pl.program_id(0)
    used = i < nused_ref[0]
    fresh = jnp.logical_or(i == 0, be_ref[i] != be_ref[jnp.maximum(i - 1, 0)])
    grp = 2 * LANES

    @pl.when(jnp.logical_and(used, fresh))
    def _():
        r = lax.broadcasted_iota(jnp.int32, (grp, grp), 0)
        c = lax.broadcasted_iota(jnp.int32, (grp, grp), 1)
        src = jnp.where(c < LANES, 2 * c, 2 * (c - LANES) + 1)
        perm = jnp.where(r == src, 1.0, 0.0).astype(BF16)
        for g in range(w1_ref.shape[1] // grp):
            blk = w1_ref[:, g * grp:(g + 1) * grp].astype(BF16)
            w1g_sc[:, g * grp:(g + 1) * grp] = jnp.dot(blk, perm, preferred_element_type=F32).astype(BF16)
        w2b_sc[...] = w2_ref[...].astype(BF16)

    @pl.when(used)
    def _():
        xb = xs_ref[...]
        h = jnp.dot(xb, w1g_sc[...], preferred_element_type=F32) + b1_ref[...]
        acts = []
        for g in range(h.shape[1] // grp):
            x_glu = jnp.minimum(h[:, g * grp:g * grp + LANES], SWIGLU_LIMIT)
            x_lin = jnp.clip(h[:, g * grp + LANES:(g + 1) * grp], -SWIGLU_LIMIT, SWIGLU_LIMIT)
            acts.append(x_glu * (1.0 / (1.0 + jnp.exp(-SWIGLU_ALPHA * x_glu))) * (x_lin + 1.0))
        act = jnp.concatenate(acts, axis=1).astype(BF16)
        y = jnp.dot(act, w2b_sc[...], preferred_element_type=F32) + b2_ref[...]
        ys_ref[...] = y.astype(ys_ref.dtype)

    @pl.when(jnp.logical_not(used))
    def _():
        ys_ref[...] = jnp.zeros(ys_ref.shape, ys_ref.dtype)


def _moe_ffn(xs, block_expert, n_used, w1, b1g, w2, b2):
    P, D = xs.shape
    n_blocks = P // MOE_BLOCK
    de2 = w1.shape[2]
    de = w2.shape[1]
    wspec = lambda r, c: pl.BlockSpec((None, r, c), lambda i, be, nu: (be[i], 0, 0))
    grid_spec = pltpu.PrefetchScalarGridSpec(
        num_scalar_prefetch=2,
        grid=(n_blocks,),
        in_specs=[pl.BlockSpec((MOE_BLOCK, D), lambda i, be, nu: (i, 0)),
                  wspec(D, de2), wspec(1, de2), wspec(de, D), wspec(1, D)],
        out_specs=pl.BlockSpec((MOE_BLOCK, D), lambda i, be, nu: (i, 0)),
        scratch_shapes=[pltpu.VMEM((D, de2), BF16), pltpu.VMEM((de, D), BF16)],
    )
    return pl.pallas_call(
        _moe_kernel,
        grid_spec=grid_spec,
        out_shape=jax.ShapeDtypeStruct((P, D), BF16),
        compiler_params=_cparams(("arbitrary",)),
        name="moe_ffn",
    )(block_expert, n_used, xs, w1, b1g, w2, b2)


def _combine_kernel(x1_ref, y0_ref, y1_ref, y2_ref, y3_ref, gate_ref, lng_ref, lnb_ref, o_ref, *, alpha):
    gates = gate_ref[...]
    ffn = jnp.zeros(x1_ref.shape, F32)
    for k, y_ref in enumerate((y0_ref, y1_ref, y2_ref, y3_ref)):
        ffn = ffn + y_ref[...].astype(F32) * gates[:, k:k + 1]
    o_ref[...] = _layer_norm(alpha * x1_ref[...] + ffn, lng_ref[...], lnb_ref[...])


def _combine(x1, y_slots, gates, ln_g, ln_b, alpha):
    T, D = x1.shape
    tm = min(PROJ_TM, T)
    tok = lambda width: pl.BlockSpec((tm, width), lambda i: (i, 0))
    const = lambda shape: pl.BlockSpec(shape, lambda i: (0,) * len(shape))
    slot = lambda k: pl.BlockSpec((None, tm, D), lambda i: (k, i, 0))
    return pl.pallas_call(
        functools.partial(_combine_kernel, alpha=alpha),
        grid=(T // tm,),
        in_specs=[tok(D)] + [slot(k) for k in range(TOP_K)] + [tok(LANES), const((1, D)), const((1, D))],
        out_specs=tok(D),
        out_shape=jax.ShapeDtypeStruct((T, D), F32),
        compiler_params=_cparams(("parallel",)),
        name="combine",
    )(x1, y_slots, y_slots, y_slots, y_slots, gates, ln_g, ln_b)


def _route(top_i, n_experts):
    T = top_i.shape[0]
    TK = T * TOP_K
    experts = jnp.arange(n_experts, dtype=jnp.int32)
    chosen = top_i[:, :, None] == experts
    picks = jnp.sum(chosen, axis=1, dtype=jnp.int32)
    csum = jnp.cumsum(picks, axis=0)
    counts = csum[-1]
    padded = ((counts + MOE_BLOCK - 1) // MOE_BLOCK) * MOE_BLOCK
    padded_end = jnp.cumsum(padded)
    padded_start = padded_end - padded
    group_start = jnp.cumsum(counts) - counts
    slot_te = padded_start[None, :] + csum - picks
    dest = jnp.sum(jnp.where(chosen, slot_te[:, None, :], 0), axis=2)
    n_blocks = -(-TK // MOE_BLOCK) + n_experts
    blk_first = jnp.arange(n_blocks, dtype=jnp.int32) * MOE_BLOCK
    block_expert = jnp.minimum(jnp.sum(padded_end[None, :] <= blk_first[:, None], axis=1),
                               n_experts - 1).astype(jnp.int32)
    n_used = (padded_end[-1] // MOE_BLOCK).astype(jnp.int32).reshape(1)
    order = jnp.argsort(top_i.reshape(-1), stable=True).astype(jnp.int32)
    rank0 = blk_first - padded_start[block_expert]
    rank = rank0[:, None] + jnp.arange(MOE_BLOCK, dtype=jnp.int32)[None, :]
    valid = rank < counts[block_expert][:, None]
    sorted_pos = jnp.clip(group_start[block_expert][:, None] + rank, 0, TK - 1)
    filler = (blk_first[:, None] + jnp.arange(MOE_BLOCK, dtype=jnp.int32)[None, :]) % T
    src_tok = jnp.where(valid, order[sorted_pos] // TOP_K, filler).reshape(-1)
    return src_tok, block_expert, n_used, dest.T.reshape(-1)


def _rope_tables(seq):
    pos = jnp.arange(seq, dtype=F32)
    inv = ROPE_THETA ** (-jnp.arange(0, DA_HEAD_DIM, 2, dtype=F32) / DA_HEAD_DIM)
    ang = pos[:, None] * inv[None, :]
    ang = jnp.concatenate([ang, ang], axis=-1)
    cos, sin = jnp.cos(ang), jnp.sin(ang)
    half = DA_HEAD_DIM // 2
    sin_signed = jnp.concatenate([-sin[:, :half], sin[:, half:]], axis=-1)
    return jnp.tile(cos, (1, 2)), jnp.tile(sin_signed, (1, 2))


def kernel(x, w_in, b_in, lambda_q1, lambda_k1, lambda_q2, lambda_k2, subln_g, rpb, w_branch_da, w_branch_na, w_out, ln1_g, ln1_b, w_router, b_router, w_mlp1, b_mlp1, w_mlp2, b_mlp2, ln2_g, ln2_b):
    B, S, D = x.shape
    depth = w_in.shape[0]
    n_experts = w_router.shape[2]
    T = B * S
    rows = S // GRID_W
    assert S % GRID_W == 0 and rows % NA_ROWS == 0 and rows >= NA_BAND
    alpha = (2 * depth) ** 0.25
    cos, sin = _rope_tables(S)
    row = lambda v: v.reshape(1, -1)

    for l in range(depth):
        lam_init = 0.8 - 0.6 * math.exp(-0.3 * l)
        qT, k, vT, qn, kn, vn, gates_br = _in_proj(x, w_in[l].astype(BF16), row(b_in[l]), cos, sin)
        a = _diff_attn(qT, k, vT, row(lambda_q1[l]), row(lambda_k1[l]), row(lambda_q2[l]),
                       row(lambda_k2[l]), subln_g[l].reshape(-1, 1), lam_init)
        nb = _na_attn(qn, kn, vn, _na_tiles(rpb[l]))

        wr = jnp.pad(w_router[l], ((0, 0), (0, LANES - n_experts)))
        wr_h = wr.astype(BF16)
        wr = jnp.concatenate([wr_h, wr_h, (wr - wr_h.astype(F32)).astype(BF16)], axis=0)
        br = jnp.pad(row(b_router[l]), ((0, 0), (0, LANES - n_experts)), constant_values=NEG_BIG)
        x1, x1b, topi, gates = _merge(
            x.reshape(T, D), a.reshape(T, -1), nb.reshape(T, -1), gates_br.reshape(T, -1),
            w_branch_da[l].astype(BF16), w_branch_na[l].astype(BF16), w_out[l].astype(BF16),
            row(ln1_g[l]), row(ln1_b[l]), wr, br, alpha)

        src_tok, block_expert, n_used, dest = _route(topi[:, :TOP_K], n_experts)
        xs = jnp.take(x1b, src_tok, axis=0, mode="clip")
        ys = _moe_ffn(xs, block_expert, n_used, w_mlp1[l], _glu_group_bias(b_mlp1[l]),
                      w_mlp2[l], b_mlp2[l][:, None, :])
        y_slots = jnp.take(ys, dest, axis=0, mode="clip").reshape(TOP_K, T, D)
        x = _combine(x1, y_slots, gates, row(ln2_g[l]), row(ln2_b[l]), alpha).reshape(B, S, D)
    return x
```

```python
import functools
import math

import jax
import jax.numpy as jnp
from jax import lax
from jax.experimental import pallas as pl
from jax.experimental.pallas import tpu as pltpu

F32 = jnp.float32
BF16 = jnp.bfloat16

GRID_W = 64
DA_HEADS = 4
DA_HEAD_DIM = 64
DA_V_DIM = 2 * DA_HEAD_DIM
DA_WIDTH = DA_HEADS * DA_V_DIM
ROPE_THETA = 10000.0
NA_HEADS = 8
NA_HEAD_DIM = 64
NA_WIDTH = NA_HEADS * NA_HEAD_DIM
NA_KH = 8
NA_KW = 16
TOP_K = 4
SWIGLU_ALPHA = 1.702
SWIGLU_LIMIT = 7.0
MOE_BLOCK = 512
LN_EPS = 1e-5
RMS_EPS = 1e-5

LANES = 128
VMEM_LIMIT_BYTES = 56 * 1024 * 1024

LOG2E = math.log2(math.e)
NEG_BIG = -1e30

PROJ_TM = 512
DA_TQ = 512
DA_TK = 512
DA_UNROLL = 8
NA_ROWS = 8
NA_BAND = 16


def _cparams(sem):
    return pltpu.CompilerParams(dimension_semantics=sem, vmem_limit_bytes=VMEM_LIMIT_BYTES)


def _in_proj_kernel(x_ref, w_ref, b_ref, cos_ref, sin_ref,
                    qT_ref, k_ref, vT_ref, qn_ref, kn_ref, vn_ref, g_ref):
    xb = x_ref[...].astype(BF16)

    def seg(lo, hi):
        return jnp.dot(xb, w_ref[:, lo:hi], preferred_element_type=F32) + b_ref[:, lo:hi]

    cos = cos_ref[...]
    sin = sin_ref[...]
    lane = lax.broadcasted_iota(jnp.int32, cos.shape, 1)
    first_half = (lane % DA_HEAD_DIM) < (DA_HEAD_DIM // 2)

    def rope(y):
        outs = []
        for h in range(DA_HEADS):
            yh = y[:, h * LANES:(h + 1) * LANES]
            partner = jnp.where(first_half,
                                pltpu.roll(yh, LANES - DA_HEAD_DIM // 2, 1),
                                pltpu.roll(yh, DA_HEAD_DIM // 2, 1))
            outs.append(yh * cos + partner * sin)
        return jnp.concatenate(outs, axis=1)

    w = DA_WIDTH
    q = rope(seg(0, w)) * (DA_HEAD_DIM ** -0.5 * LOG2E)
    qT_ref[...] = q.T.astype(BF16)
    k_ref[...] = rope(seg(w, 2 * w)).astype(BF16)
    vT_ref[...] = seg(2 * w, 3 * w).T.astype(BF16)
    o = 3 * w
    qn_ref[...] = (seg(o, o + NA_WIDTH) * (NA_HEAD_DIM ** -0.5)).T.astype(BF16)
    kn_ref[...] = seg(o + NA_WIDTH, o + 2 * NA_WIDTH).astype(BF16)
    vn_ref[...] = seg(o + 2 * NA_WIDTH, o + 3 * NA_WIDTH).T.astype(BF16)
    g0 = o + 3 * NA_WIDTH
    gate_pre = seg(g0, w_ref.shape[1])
    g_ref[...] = (1.0 / (1.0 + jnp.exp(-gate_pre))).astype(BF16)


def _in_proj(x, w_in, b_in, cos, sin):
    B, S, D = x.shape
    tm = min(PROJ_TM, S)
    n_cols = w_in.shape[1]
    n_gate = n_cols - 3 * DA_WIDTH - 3 * NA_WIDTH
    tok = lambda width: pl.BlockSpec((None, tm, width), lambda b, i: (b, i, 0))
    tr = pl.BlockSpec((None, DA_WIDTH, tm), lambda b, i: (b, 0, i))
    const = lambda shape: pl.BlockSpec(shape, lambda b, i: (0,) * len(shape))
    out_shape = (
        jax.ShapeDtypeStruct((B, DA_WIDTH, S), BF16),
        jax.ShapeDtypeStruct((B, S, DA_WIDTH), BF16),
        jax.ShapeDtypeStruct((B, DA_WIDTH, S), BF16),
        jax.ShapeDtypeStruct((B, NA_WIDTH, S), BF16),
        jax.ShapeDtypeStruct((B, S, NA_WIDTH), BF16),
        jax.ShapeDtypeStruct((B, NA_WIDTH, S), BF16),
        jax.ShapeDtypeStruct((B, S, n_gate), BF16),
    )
    assert DA_WIDTH == NA_WIDTH
    return pl.pallas_call(
        _in_proj_kernel,
        grid=(B, S // tm),
        in_specs=[tok(D), const((D, n_cols)), const((1, n_cols)),
                  pl.BlockSpec((tm, LANES), lambda b, i: (i, 0)),
                  pl.BlockSpec((tm, LANES), lambda b, i: (i, 0))],
        out_specs=(tr, tok(DA_WIDTH), tr, tr, tok(NA_WIDTH), tr, tok(n_gate)),
        out_shape=out_shape,
        compiler_params=_cparams(("parallel", "parallel")),
        name="in_proj",
    )(x, w_in, b_in, cos, sin)


def _diff_attn_kernel(qT_ref, qTn_ref, k_ref, vT_ref, lq1_ref, lk1_ref, lq2_ref, lk2_ref, g_ref,
                      o_ref, m_sc, l_sc, acc_sc, s_sc, *, tk, unroll, lam_init):
    tq = qT_ref.shape[1]

    def block_diag(q_ref):
        qT = q_ref[...]
        row = lax.broadcasted_iota(jnp.int32, qT.shape, 0)
        zero = jnp.zeros_like(qT)
        return jnp.concatenate([jnp.where(row < DA_HEAD_DIM, qT, zero),
                                jnp.where(row < DA_HEAD_DIM, zero, qT)], axis=1)

    qbd = block_diag(qT_ref)

    m_sc[...] = jnp.full(m_sc.shape, -jnp.inf, F32)
    l_sc[...] = jnp.zeros(l_sc.shape, F32)
    acc_sc[...] = jnp.zeros(acc_sc.shape, F32)

    nk = k_ref.shape[0] // tk

    def scores(j, q=None):
        off = pl.multiple_of(j * tk, tk)
        return jnp.dot(k_ref[pl.ds(off, tk), :], qbd if q is None else q, preferred_element_type=F32)

    def accumulate(j, s):
        off = pl.multiple_of(j * tk, tk)
        m_prev = m_sc[...]
        m_new = jnp.maximum(m_prev, jnp.max(s, axis=0, keepdims=True))
        alpha = jnp.exp2(m_prev - m_new)
        p = jnp.exp2(s - m_new)
        l_sc[...] = alpha * l_sc[...] + jnp.sum(p, axis=0, keepdims=True)
        vb = vT_ref[:, pl.ds(off, tk)]
        acc_sc[...] = alpha * acc_sc[...] + jnp.dot(vb, p.astype(BF16), preferred_element_type=F32)
        m_sc[...] = m_new

    @pl.when(pl.program_id(2) == 0)
    def _():
        s_sc[0] = scores(0)

    def body(jj, carry):
        for u in range(unroll):
            j = unroll * jj + u
            s_sc[(u + 1) % 2] = scores(j + 1)
            accumulate(j, s_sc[u % 2])
        return carry

    lax.fori_loop(0, nk // unroll - 1, body, 0)
    for j in range(nk - unroll, nk):
        s_sc[(j + 1) % 2] = scores(j + 1) if j + 1 < nk else scores(0, block_diag(qTn_ref))
        accumulate(j, s_sc[j % 2])

    lam = (jnp.exp(jnp.sum(lq1_ref[...] * lk1_ref[...], axis=1, keepdims=True))
           - jnp.exp(jnp.sum(lq2_ref[...] * lk2_ref[...], axis=1, keepdims=True)) + lam_init)
    on = acc_sc[...] / l_sc[...]
    o = on[:, :tq] - lam * on[:, tq:]
    ms = jnp.mean(o * o, axis=0, keepdims=True)
    o = o * lax.rsqrt(ms + RMS_EPS) * g_ref[...]
    o = o * (1.0 - lam_init)
    o_ref[...] = o.T.astype(BF16)


def _diff_attn(qT, k, vT, lq1, lk1, lq2, lk2, subln_g, lam_init):
    B, S, _ = k.shape
    tq = min(DA_TQ, S)
    tk = min(DA_TK, S // 2)
    unroll = math.gcd(DA_UNROLL, S // tk)
    assert unroll % 2 == 0 and S % tk == 0 and S % tq == 0
    nq = S // tq
    vec = pl.BlockSpec((1, DA_HEAD_DIM), lambda b, h, i: (0, 0))
    kernel = functools.partial(_diff_attn_kernel, tk=tk, unroll=unroll, lam_init=lam_init)
    return pl.pallas_call(
        kernel,
        grid=(B, DA_HEADS, nq),
        in_specs=[pl.BlockSpec((None, DA_V_DIM, tq), lambda b, h, i: (b, h, i)),
                  pl.BlockSpec((None, DA_V_DIM, tq), lambda b, h, i: (b, h, jnp.minimum(i + 1, nq - 1))),
                  pl.BlockSpec((None, S, DA_V_DIM), lambda b, h, i: (b, 0, h)),
                  pl.BlockSpec((None, DA_V_DIM, S), lambda b, h, i: (b, h, 0)),
                  vec, vec, vec, vec,
                  pl.BlockSpec((DA_V_DIM, 1), lambda b, h, i: (0, 0))],
        out_specs=pl.BlockSpec((None, tq, DA_V_DIM), lambda b, h, i: (b, i, h)),
        out_shape=jax.ShapeDtypeStruct((B, S, DA_WIDTH), BF16),
        scratch_shapes=[pltpu.VMEM((1, 2 * tq), F32), pltpu.VMEM((1, 2 * tq), F32),
                        pltpu.VMEM((DA_V_DIM, 2 * tq), F32), pltpu.VMEM((2, tk, 2 * tq), F32)],
        compiler_params=_cparams(("parallel", "parallel", "arbitrary")),
        name="diff_attn",
    )(qT, qT, k, vT, lq1, lk1, lq2, lk2, subln_g)


def _na_band_start(r0, rows):
    return jnp.clip(r0 - NA_KH // 2, 0, rows - NA_BAND)


def _na_tiles(rpb):
    cols = jnp.arange(GRID_W)
    cs = jnp.clip(cols - NA_KW // 2, 0, GRID_W - NA_KW)
    col_ok = (cols[:, None] >= cs[None, :]) & (cols[:, None] < cs[None, :] + NA_KW)
    col_off = jnp.clip(cols[:, None] - cols[None, :] + (NA_KW - 1), 0, 2 * NA_KW - 2)
    tiles = jnp.where(col_ok[None, None], rpb.astype(F32)[:, :, col_off], NEG_BIG)
    neg = jnp.full((rpb.shape[0], 1, GRID_W, GRID_W), NEG_BIG, F32)
    return jnp.concatenate([jnp.concatenate([tiles, neg], axis=1),
                            jnp.concatenate([neg, tiles], axis=1)], axis=3)


def _na_geometry(variant, rows):
    nblk = rows // NA_ROWS
    r0 = (0, NA_ROWS * min(1, nblk - 1), NA_ROWS * (nblk - 1))[variant]
    band = min(max(r0 - NA_KH // 2, 0), rows - NA_BAND)
    pairs = []
    for qp in range(NA_ROWS // 2):
        entries = []
        for kr in range(NA_BAND):
            ok = []
            for qr in (r0 + 2 * qp, r0 + 2 * qp + 1):
                rs = min(max(qr - NA_KH // 2, 0), rows - NA_KH)
                ok.append(rs <= band + kr < rs + NA_KH)
            if ok[0] or ok[1]:
                entries.append((kr, band + kr - (r0 + 2 * qp) + NA_KH - 1, ok[0], ok[1]))
        pairs.append(entries)
    return pairs


def _na_kernel(qT_ref, k_ref, vT_ref, tile_ref, o_ref, *, rows):
    i = pl.program_id(2)
    nblk = rows // NA_ROWS
    nk = NA_BAND * GRID_W
    off = pl.multiple_of(_na_band_start(i * NA_ROWS, rows) * GRID_W, 2 * GRID_W)

    def block(variant):
        kb = k_ref[pl.ds(off, nk), :]
        vT = vT_ref[:, pl.ds(off, nk)]
        lane = lax.broadcasted_iota(jnp.int32, kb.shape, 1)
        kbd = jnp.concatenate([jnp.where(lane < NA_HEAD_DIM, kb, jnp.zeros_like(kb)),
                               jnp.where(lane < NA_HEAD_DIM, jnp.zeros_like(kb), kb)], axis=0)
        row = lax.broadcasted_iota(jnp.int32, vT.shape, 0)
        vbdT = jnp.concatenate([jnp.where(row < NA_HEAD_DIM, vT, jnp.zeros_like(vT)),
                                jnp.where(row < NA_HEAD_DIM, jnp.zeros_like(vT), vT)], axis=1)
        s = jnp.dot(kbd, qT_ref[...], preferred_element_type=F32)
        left = lax.broadcasted_iota(jnp.int32, (GRID_W, LANES), 1) < GRID_W
        zero_slab = jnp.zeros((GRID_W, LANES), BF16)
        geometry = _na_geometry(variant, rows)
        p_heads, l_heads = [], []
        for h in range(2):
            p_cols, l_cols = [], []
            for qp, entries in enumerate(geometry):
                logits = {}
                for kr, d, ok_l, ok_r in entries:
                    bias = tile_ref[h, d]
                    if not ok_r:
                        bias = jnp.where(left, bias, NEG_BIG)
                    if not ok_l:
                        bias = jnp.where(left, NEG_BIG, bias)
                    r = h * nk + kr * GRID_W
                    logits[kr] = s[r:r + GRID_W, qp * LANES:(qp + 1) * LANES] + bias
                m = functools.reduce(jnp.maximum, logits.values())
                m = jnp.max(m, axis=0, keepdims=True)
                probs = {kr: jnp.exp(x - m) for kr, x in logits.items()}
                l_cols.append(jnp.sum(functools.reduce(jnp.add, probs.values()), axis=0, keepdims=True))
                p_cols.append(jnp.concatenate(
                    [probs[kr].astype(BF16) if kr in probs else zero_slab for kr in range(NA_BAND)], axis=0))
            p_heads.append(jnp.concatenate(p_cols, axis=1))
            l_heads.append(jnp.concatenate(l_cols, axis=1))
        oT = jnp.dot(vbdT, jnp.concatenate(p_heads, axis=0), preferred_element_type=F32)
        row_o = lax.broadcasted_iota(jnp.int32, oT.shape, 0)
        oT = oT / jnp.where(row_o < NA_HEAD_DIM, l_heads[0], l_heads[1])
        o_ref[...] = oT.T.astype(BF16)

    pl.when(i == 0)(lambda: block(0))
    pl.when(jnp.logical_and(i > 0, i < nblk - 1))(lambda: block(1))
    pl.when(jnp.logical_and(i > 0, i == nblk - 1))(lambda: block(2))


def _na_attn(qnT, kn, vnT, tiles):
    B, S, _ = kn.shape
    rows = S // GRID_W
    nblk = rows // NA_ROWS
    nq = NA_ROWS * GRID_W
    return pl.pallas_call(
        functools.partial(_na_kernel, rows=rows),
        grid=(B, NA_HEADS // 2, nblk),
        in_specs=[pl.BlockSpec((None, LANES, nq), lambda b, h, i: (b, h, i)),
                  pl.BlockSpec((None, S, LANES), lambda b, h, i: (b, 0, h)),
                  pl.BlockSpec((None, LANES, S), lambda b, h, i: (b, h, 0)),
                  pl.BlockSpec((2,) + tiles.shape[1:], lambda b, h, i: (h, 0, 0, 0))],
        out_specs=pl.BlockSpec((None, nq, LANES), lambda b, h, i: (b, i, h)),
        out_shape=jax.ShapeDtypeStruct((B, S, NA_WIDTH), BF16),
        compiler_params=_cparams(("parallel", "parallel", "arbitrary")),
        name="na_attn",
    )(qnT, kn, vnT, tiles)


def _layer_norm(h, g, b):
    mu = jnp.mean(h, axis=-1, keepdims=True)
    d = h - mu
    var = jnp.mean(d * d, axis=-1, keepdims=True)
    return d * lax.rsqrt(var + LN_EPS) * g + b


def _merge_kernel(x_ref, a_ref, nb_ref, g_ref, wda_ref, wna_ref, wout_ref, lng_ref, lnb_ref,
                  wr_ref, br_ref, x1_ref, x1b_ref, topi_ref, gate_ref, *, alpha):
    d = x_ref.shape[1]
    ya = jnp.dot(a_ref[...], wda_ref[...], preferred_element_type=F32)
    yb = jnp.dot(nb_ref[...], wna_ref[...], preferred_element_type=F32)
    g = g_ref[...].astype(F32)
    merged = g[:, :d] * ya + g[:, d:] * yb
    mix = jnp.dot(merged.astype(BF16), wout_ref[...], preferred_element_type=F32)
    x1 = _layer_norm(alpha * x_ref[...] + mix, lng_ref[...], lnb_ref[...])
    x1_ref[...] = x1
    x1b_ref[...] = x1.astype(BF16)
    xh = x1.astype(BF16)
    xl = (x1 - xh.astype(F32)).astype(BF16)
    logits = jnp.dot(jnp.concatenate([xh, xl, xh], axis=1), wr_ref[...],
                     preferred_element_type=F32) + br_ref[...]
    lane = lax.broadcasted_iota(jnp.int32, logits.shape, 1).astype(F32)
    vals, idxs = [], []
    for _ in range(TOP_K):
        m = jnp.max(logits, axis=1, keepdims=True)
        idx = jnp.min(jnp.where(logits == m, lane, float(LANES)), axis=1, keepdims=True)
        vals.append(m)
        idxs.append(idx)
        logits = jnp.where(lane == idx, -jnp.inf, logits)
    es = [jnp.exp(v - vals[0]) for v in vals]
    denom = es[0] + es[1] + es[2] + es[3]
    gates = jnp.zeros(logits.shape, F32)
    topi = jnp.zeros(logits.shape, F32)
    for k in range(TOP_K):
        gates = jnp.where(lane == k, es[k] / denom, gates)
        topi = jnp.where(lane == k, idxs[k], topi)
    gate_ref[...] = gates
    topi_ref[...] = topi.astype(jnp.int32)


def _merge(x2, a2, nb2, g2, wda, wna, wout, ln_g, ln_b, wr, br, alpha):
    T, D = x2.shape
    tm = min(PROJ_TM, T)
    tok = lambda width: pl.BlockSpec((tm, width), lambda i: (i, 0))
    const = lambda shape: pl.BlockSpec(shape, lambda i: (0,) * len(shape))
    return pl.pallas_call(
        functools.partial(_merge_kernel, alpha=alpha),
        grid=(T // tm,),
        in_specs=[tok(D), tok(a2.shape[1]), tok(nb2.shape[1]), tok(g2.shape[1]),
                  const(wda.shape), const(wna.shape), const(wout.shape),
                  const((1, D)), const((1, D)), const(wr.shape), const((1, LANES))],
        out_specs=(tok(D), tok(D), tok(LANES), tok(LANES)),
        out_shape=(jax.ShapeDtypeStruct((T, D), F32), jax.ShapeDtypeStruct((T, D), BF16),
                   jax.ShapeDtypeStruct((T, LANES), jnp.int32), jax.ShapeDtypeStruct((T, LANES), F32)),
        compiler_params=_cparams(("parallel",)),
        name="merge",
    )(x2, a2, nb2, g2, wda, wna, wout, ln_g, ln_b, wr, br)


def _glu_group_bias(b1):
    e, n = b1.shape
    return b1.reshape(e, n // (2 * LANES), LANES, 2).transpose(0, 1, 3, 2).reshape(e, 1, n)


def _moe_kernel(be_ref, nused_ref, xs_ref, w1_ref, b1_ref, w2_ref, b2_ref, ys_ref, w1g_sc, w2b_sc):
    i = pl.program_id(0)
    used = i < nused_ref[0]
    fresh = jnp.logical_or(i == 0, be_ref[i] != be_ref[jnp.maximum(i - 1, 0)])
    grp = 2 * LANES

    @pl.when(jnp.logical_and(used, fresh))
    def _():
        r = lax.broadcasted_iota(jnp.int32, (grp, grp), 0)
        c = lax.broadcasted_iota(jnp.int32, (grp, grp), 1)
        src = jnp.where(c < LANES, 2 * c, 2 * (c - LANES) + 1)
        perm = jnp.where(r == src, 1.0, 0.0).astype(BF16)
        for g in range(w1_ref.shape[1] // grp):
            blk = w1_ref[:, g * grp:(g + 1) * grp].astype(BF16)
            w1g_sc[:, g * grp:(g + 1) * grp] = jnp.dot(blk, perm, preferred_element_type=F32).astype(BF16)
        w2b_sc[...] = w2_ref[...].astype(BF16)

    @pl.when(used)
    def _():
        xb = xs_ref[...]
        h = jnp.dot(xb, w1g_sc[...], preferred_element_type=F32) + b1_ref[...]
        acts = []
        for g in range(h.shape[1] // grp):
            x_glu = jnp.minimum(h[:, g * grp:g * grp + LANES], SWIGLU_LIMIT)
            x_lin = jnp.clip(h[:, g * grp + LANES:(g + 1) * grp], -SWIGLU_LIMIT, SWIGLU_LIMIT)
            acts.append(x_glu * (1.0 / (1.0 + jnp.exp(-SWIGLU_ALPHA * x_glu))) * (x_lin + 1.0))
        act = jnp.concatenate(acts, axis=1).astype(BF16)
        y = jnp.dot(act, w2b_sc[...], preferred_element_type=F32) + b2_ref[...]
        ys_ref[...] = y.astype(ys_ref.dtype)

    @pl.when(jnp.logical_not(used))
    def _():
        ys_ref[...] = jnp.zeros(ys_ref.shape, ys_ref.dtype)


def _moe_ffn(xs, block_expert, n_used, w1, b1g, w2, b2):
    P, D = xs.shape
    n_blocks = P // MOE_BLOCK
    de2 = w1.shape[2]
    de = w2.shape[1]
    wspec = lambda r, c: pl.BlockSpec((None, r, c), lambda i, be, nu: (be[i], 0, 0))
    grid_spec = pltpu.PrefetchScalarGridSpec(
        num_scalar_prefetch=2,
        grid=(n_blocks,),
        in_specs=[pl.BlockSpec((MOE_BLOCK, D), lambda i, be, nu: (i, 0)),
                  wspec(D, de2), wspec(1, de2), wspec(de, D), wspec(1, D)],
        out_specs=pl.BlockSpec((MOE_BLOCK, D), lambda i, be, nu: (i, 0)),
        scratch_shapes=[pltpu.VMEM((D, de2), BF16), pltpu.VMEM((de, D), BF16)],
    )
    return pl.pallas_call(
        _moe_kernel,
        grid_spec=grid_spec,
        out_shape=jax.ShapeDtypeStruct((P, D), BF16),
        compiler_params=_cparams(("arbitrary",)),
        name="moe_ffn",
    )(block_expert, n_used, xs, w1, b1g, w2, b2)


def _combine_kernel(x1_ref, y0_ref, y1_ref, y2_ref, y3_ref, gate_ref, lng_ref, lnb_ref, o_ref, *, alpha):
    gates = gate_ref[...]
    ffn = jnp.zeros(x1_ref.shape, F32)
    for k, y_ref in enumerate((y0_ref, y1_ref, y2_ref, y3_ref)):
        ffn = ffn + y_ref[...].astype(F32) * gates[:, k:k + 1]
    o_ref[...] = _layer_norm(alpha * x1_ref[...] + ffn, lng_ref[...], lnb_ref[...])


def _combine(x1, y_slots, gates, ln_g, ln_b, alpha):
    T, D = x1.shape
    tm = min(PROJ_TM, T)
    tok = lambda width: pl.BlockSpec((tm, width), lambda i: (i, 0))
    const = lambda shape: pl.BlockSpec(shape, lambda i: (0,) * len(shape))
    slot = lambda k: pl.BlockSpec((None, tm, D), lambda i: (k, i, 0))
    return pl.pallas_call(
        functools.partial(_combine_kernel, alpha=alpha),
        grid=(T // tm,),
        in_specs=[tok(D)] + [slot(k) for k in range(TOP_K)] + [tok(LANES), const((1, D)), const((1, D))],
        out_specs=tok(D),
        out_shape=jax.ShapeDtypeStruct((T, D), F32),
        compiler_params=_cparams(("parallel",)),
        name="combine",
    )(x1, y_slots, y_slots, y_slots, y_slots, gates, ln_g, ln_b)


def _route(top_i, n_experts):
    T = top_i.shape[0]
    TK = T * TOP_K
    experts = jnp.arange(n_experts, dtype=jnp.int32)
    chosen = top_i[:, :, None] == experts
    picks = jnp.sum(chosen, axis=1, dtype=jnp.int32)
    csum = jnp.cumsum(picks, axis=0)
    counts = csum[-1]
    padded = ((counts + MOE_BLOCK - 1) // MOE_BLOCK) * MOE_BLOCK
    padded_end = jnp.cumsum(padded)
    padded_start = padded_end - padded
    group_start = jnp.cumsum(counts) - counts
    slot_te = padded_start[None, :] + csum - picks
    dest = jnp.sum(jnp.where(chosen, slot_te[:, None, :], 0), axis=2)
    n_blocks = -(-TK // MOE_BLOCK) + n_experts
    blk_first = jnp.arange(n_blocks, dtype=jnp.int32) * MOE_BLOCK
    block_expert = jnp.minimum(jnp.sum(padded_end[None, :] <= blk_first[:, None], axis=1),
                               n_experts - 1).astype(jnp.int32)
    n_used = (padded_end[-1] // MOE_BLOCK).astype(jnp.int32).reshape(1)
    order = jnp.argsort(top_i.reshape(-1), stable=True).astype(jnp.int32)
    rank0 = blk_first - padded_start[block_expert]
    rank = rank0[:, None] + jnp.arange(MOE_BLOCK, dtype=jnp.int32)[None, :]
    valid = rank < counts[block_expert][:, None]
    sorted_pos = jnp.clip(group_start[block_expert][:, None] + rank, 0, TK - 1)
    filler = (blk_first[:, None] + jnp.arange(MOE_BLOCK, dtype=jnp.int32)[None, :]) % T
    src_tok = jnp.where(valid, order[sorted_pos] // TOP_K, filler).reshape(-1)
    return src_tok, block_expert, n_used, dest.T.reshape(-1)


def _rope_tables(seq):
    pos = jnp.arange(seq, dtype=F32)
    inv = ROPE_THETA ** (-jnp.arange(0, DA_HEAD_DIM, 2, dtype=F32) / DA_HEAD_DIM)
    ang = pos[:, None] * inv[None, :]
    ang = jnp.concatenate([ang, ang], axis=-1)
    cos, sin = jnp.cos(ang), jnp.sin(ang)
    half = DA_HEAD_DIM // 2
    sin_signed = jnp.concatenate([-sin[:, :half], sin[:, half:]], axis=-1)
    return jnp.tile(cos, (1, 2)), jnp.tile(sin_signed, (1, 2))


def kernel(x, w_in, b_in, lambda_q1, lambda_k1, lambda_q2, lambda_k2, subln_g, rpb, w_branch_da, w_branch_na, w_out, ln1_g, ln1_b, w_router, b_router, w_mlp1, b_mlp1, w_mlp2, b_mlp2, ln2_g, ln2_b):
    B, S, D = x.shape
    depth = w_in.shape[0]
    n_experts = w_router.shape[2]
    T = B * S
    rows = S // GRID_W
    assert S % GRID_W == 0 and rows % NA_ROWS == 0 and rows >= NA_BAND
    alpha = (2 * depth) ** 0.25
    cos, sin = _rope_tables(S)
    row = lambda v: v.reshape(1, -1)

    for l in range(depth):
        lam_init = 0.8 - 0.6 * math.exp(-0.3 * l)
        qT, k, vT, qn, kn, vn, gates_br = _in_proj(x, w_in[l].astype(BF16), row(b_in[l]), cos, sin)
        a = _diff_attn(qT, k, vT, row(lambda_q1[l]), row(lambda_k1[l]), row(lambda_q2[l]),
                       row(lambda_k2[l]), subln_g[l].reshape(-1, 1), lam_init)
        nb = _na_attn(qn, kn, vn, _na_tiles(rpb[l]))

        wr = jnp.pad(w_router[l], ((0, 0), (0, LANES - n_experts)))
        wr_h = wr.astype(BF16)
        wr = jnp.concatenate([wr_h, wr_h, (wr - wr_h.astype(F32)).astype(BF16)], axis=0)
        br = jnp.pad(row(b_router[l]), ((0, 0), (0, LANES - n_experts)), constant_values=NEG_BIG)
        x1, x1b, topi, gates = _merge(
            x.reshape(T, D), a.reshape(T, -1), nb.reshape(T, -1), gates_br.reshape(T, -1),
            w_branch_da[l].astype(BF16), w_branch_na[l].astype(BF16), w_out[l].astype(BF16),
            row(ln1_g[l]), row(ln1_b[l]), wr, br, alpha)

        src_tok, block_expert, n_used, dest = _route(topi[:, :TOP_K], n_experts)
        xs = jnp.take(x1b, src_tok, axis=0, mode="clip")
        ys = _moe_ffn(xs, block_expert, n_used, w_mlp1[l], _glu_group_bias(b_mlp1[l]),
                      w_mlp2[l], b_mlp2[l][:, None, :])
        y_slots = jnp.take(ys, dest, axis=0, mode="clip").reshape(TOP_K, T, D)
        x = _combine(x1, y_slots, gates, row(ln2_g[l]), row(ln2_b[l]), alpha).reshape(B, S, D)
    return x
```

```python
import functools
import math

import jax
import jax.numpy as jnp
from jax import lax
from jax.experimental import pallas as pl
from jax.experimental.pallas import tpu as pltpu

F32 = jnp.float32
BF16 = jnp.bfloat16

GRID_W = 64
DA_HEADS = 4
DA_HEAD_DIM = 64
DA_V_DIM = 2 * DA_HEAD_DIM
DA_WIDTH = DA_HEADS * DA_V_DIM
ROPE_THETA = 10000.0
NA_HEADS = 8
NA_HEAD_DIM = 64
NA_WIDTH = NA_HEADS * NA_HEAD_DIM
NA_KH = 8
NA_KW = 16
TOP_K = 4
SWIGLU_ALPHA = 1.702
SWIGLU_LIMIT = 7.0
MOE_BLOCK = 512
LN_EPS = 1e-5
RMS_EPS = 1e-5

LANES = 128
VMEM_LIMIT_BYTES = 56 * 1024 * 1024

LOG2E = math.log2(math.e)
NEG_BIG = -1e30

PROJ_TM = 512
DA_TQ = 512
DA_TK = 512
DA_UNROLL = 8
NA_ROWS = 8
NA_BAND = 16


def _cparams(sem):
    return pltpu.CompilerParams(dimension_semantics=sem, vmem_limit_bytes=VMEM_LIMIT_BYTES)


def _in_proj_kernel(x_ref, w_ref, b_ref, cos_ref, sin_ref,
                    qT_ref, k_ref, vT_ref, qn_ref, kn_ref, vn_ref, g_ref):
    xb = x_ref[...].astype(BF16)

    def seg(lo, hi):
        return jnp.dot(xb, w_ref[:, lo:hi], preferred_element_type=F32) + b_ref[:, lo:hi]

    cos = cos_ref[...]
    sin = sin_ref[...]
    lane = lax.broadcasted_iota(jnp.int32, cos.shape, 1)
    first_half = (lane % DA_HEAD_DIM) < (DA_HEAD_DIM // 2)

    def rope(y):
        outs = []
        for h in range(DA_HEADS):
            yh = y[:, h * LANES:(h + 1) * LANES]
            partner = jnp.where(first_half,
                                pltpu.roll(yh, LANES - DA_HEAD_DIM // 2, 1),
                                pltpu.roll(yh, DA_HEAD_DIM // 2, 1))
            outs.append(yh * cos + partner * sin)
        return jnp.concatenate(outs, axis=1)

    w = DA_WIDTH
    q = rope(seg(0, w)) * (DA_HEAD_DIM ** -0.5 * LOG2E)
    qT_ref[...] = q.T.astype(BF16)
    k_ref[...] = rope(seg(w, 2 * w)).astype(BF16)
    vT_ref[...] = seg(2 * w, 3 * w).T.astype(BF16)
    o = 3 * w
    qn_ref[...] = (seg(o, o + NA_WIDTH) * (NA_HEAD_DIM ** -0.5)).T.astype(BF16)
    kn_ref[...] = seg(o + NA_WIDTH, o + 2 * NA_WIDTH).astype(BF16)
    vn_ref[...] = seg(o + 2 * NA_WIDTH, o + 3 * NA_WIDTH).T.astype(BF16)
    g0 = o + 3 * NA_WIDTH
    gate_pre = seg(g0, w_ref.shape[1])
    g_ref[...] = (1.0 / (1.0 + jnp.exp(-gate_pre))).astype(BF16)


def _in_proj(x, w_in, b_in, cos, sin):
    B, S, D = x.shape
    tm = min(PROJ_TM, S)
    n_cols = w_in.shape[1]
    n_gate = n_cols - 3 * DA_WIDTH - 3 * NA_WIDTH
    tok = lambda width: pl.BlockSpec((None, tm, width), lambda b, i: (b, i, 0))
    tr = pl.BlockSpec((None, DA_WIDTH, tm), lambda b, i: (b, 0, i))
    const = lambda shape: pl.BlockSpec(shape, lambda b, i: (0,) * len(shape))
    out_shape = (
        jax.ShapeDtypeStruct((B, DA_WIDTH, S), BF16),
        jax.ShapeDtypeStruct((B, S, DA_WIDTH), BF16),
        jax.ShapeDtypeStruct((B, DA_WIDTH, S), BF16),
        jax.ShapeDtypeStruct((B, NA_WIDTH, S), BF16),
        jax.ShapeDtypeStruct((B, S, NA_WIDTH), BF16),
        jax.ShapeDtypeStruct((B, NA_WIDTH, S), BF16),
        jax.ShapeDtypeStruct((B, S, n_gate), BF16),
    )
    assert DA_WIDTH == NA_WIDTH
    return pl.pallas_call(
        _in_proj_kernel,
        grid=(B, S // tm),
        in_specs=[tok(D), const((D, n_cols)), const((1, n_cols)),
                  pl.BlockSpec((tm, LANES), lambda b, i: (i, 0)),
                  pl.BlockSpec((tm, LANES), lambda b, i: (i, 0))],
        out_specs=(tr, tok(DA_WIDTH), tr, tr, tok(NA_WIDTH), tr, tok(n_gate)),
        out_shape=out_shape,
        compiler_params=_cparams(("parallel", "parallel")),
        name="in_proj",
    )(x, w_in, b_in, cos, sin)


def _diff_attn_kernel(qT_ref, qTn_ref, k_ref, vT_ref, lq1_ref, lk1_ref, lq2_ref, lk2_ref, g_ref,
                      o_ref, m_sc, l_sc, acc_sc, s_sc, *, tk, unroll, lam_init):
    tq = qT_ref.shape[1]

    def block_diag(q_ref):
        qT = q_ref[...]
        row = lax.broadcasted_iota(jnp.int32, qT.shape, 0)
        zero = jnp.zeros_like(qT)
        return jnp.concatenate([jnp.where(row < DA_HEAD_DIM, qT, zero),
                                jnp.where(row < DA_HEAD_DIM, zero, qT)], axis=1)

    qbd = block_diag(qT_ref)

    m_sc[...] = jnp.full(m_sc.shape, -jnp.inf, F32)
    l_sc[...] = jnp.zeros(l_sc.shape, F32)
    acc_sc[...] = jnp.zeros(acc_sc.shape, F32)

    nk = k_ref.shape[0] // tk

    def scores(j, q=None):
        off = pl.multiple_of(j * tk, tk)
        return jnp.dot(k_ref[pl.ds(off, tk), :], qbd if q is None else q, preferred_element_type=F32)

    def accumulate(j, s):
        off = pl.multiple_of(j * tk, tk)
        m_prev = m_sc[...]
        m_new = jnp.maximum(m_prev, jnp.max(s, axis=0, keepdims=True))
        alpha = jnp.exp2(m_prev - m_new)
        p = jnp.exp2(s - m_new)
        l_sc[...] = alpha * l_sc[...] + jnp.sum(p, axis=0, keepdims=True)
        vb = vT_ref[:, pl.ds(off, tk)]
        acc_sc[...] = alpha * acc_sc[...] + jnp.dot(vb, p.astype(BF16), preferred_element_type=F32)
        m_sc[...] = m_new

    @pl.when(pl.program_id(2) == 0)
    def _():
        s_sc[0] = scores(0)

    def body(jj, carry):
        for u in range(unroll):
            j = unroll * jj + u
            s_sc[(u + 1) % 2] = scores(j + 1)
            accumulate(j, s_sc[u % 2])
        return carry

    lax.fori_loop(0, nk // unroll - 1, body, 0)
    for j in range(nk - unroll, nk):
        s_sc[(j + 1) % 2] = scores(j + 1) if j + 1 < nk else scores(0, block_diag(qTn_ref))
        accumulate(j, s_sc[j % 2])

    lam = (jnp.exp(jnp.sum(lq1_ref[...] * lk1_ref[...], axis=1, keepdims=True))
           - jnp.exp(jnp.sum(lq2_ref[...] * lk2_ref[...], axis=1, keepdims=True)) + lam_init)
    on = acc_sc[...] / l_sc[...]
    o = on[:, :tq] - lam * on[:, tq:]
    ms = jnp.mean(o * o, axis=0, keepdims=True)
    o = o * lax.rsqrt(ms + RMS_EPS) * g_ref[...]
    o = o * (1.0 - lam_init)
    o_ref[...] = o.T.astype(BF16)


def _diff_attn(qT, k, vT, lq1, lk1, lq2, lk2, subln_g, lam_init):
    B, S, _ = k.shape
    tq = min(DA_TQ, S)
    tk = min(DA_TK, S // 2)
    unroll = math.gcd(DA_UNROLL, S // tk)
    assert unroll % 2 == 0 and S % tk == 0 and S % tq == 0
    nq = S // tq
    vec = pl.BlockSpec((1, DA_HEAD_DIM), lambda b, h, i: (0, 0))
    kernel = functools.partial(_diff_attn_kernel, tk=tk, unroll=unroll, lam_init=lam_init)
    return pl.pallas_call(
        kernel,
        grid=(B, DA_HEADS, nq),
        in_specs=[pl.BlockSpec((None, DA_V_DIM, tq), lambda b, h, i: (b, h, i)),
                  pl.BlockSpec((None, DA_V_DIM, tq), lambda b, h, i: (b, h, jnp.minimum(i + 1, nq - 1))),
                  pl.BlockSpec((None, S, DA_V_DIM), lambda b, h, i: (b, 0, h)),
                  pl.BlockSpec((None, DA_V_DIM, S), lambda b, h, i: (b, h, 0)),
                  vec, vec, vec, vec,
                  pl.BlockSpec((DA_V_DIM, 1), lambda b, h, i: (0, 0))],
        out_specs=pl.BlockSpec((None, tq, DA_V_DIM), lambda b, h, i: (b, i, h)),
        out_shape=jax.ShapeDtypeStruct((B, S, DA_WIDTH), BF16),
        scratch_shapes=[pltpu.VMEM((1, 2 * tq), F32), pltpu.VMEM((1, 2 * tq), F32),
                        pltpu.VMEM((DA_V_DIM, 2 * tq), F32), pltpu.VMEM((2, tk, 2 * tq), F32)],
        compiler_params=_cparams(("parallel", "parallel", "arbitrary")),
        name="diff_attn",
    )(qT, qT, k, vT, lq1, lk1, lq2, lk2, subln_g)


def _na_band_start(r0, rows):
    return jnp.clip(r0 - NA_KH // 2, 0, rows - NA_BAND)


def _na_tiles(rpb):
    cols = jnp.arange(GRID_W)
    cs = jnp.clip(cols - NA_KW // 2, 0, GRID_W - NA_KW)
    col_ok = (cols[:, None] >= cs[None, :]) & (cols[:, None] < cs[None, :] + NA_KW)
    col_off = jnp.clip(cols[:, None] - cols[None, :] + (NA_KW - 1), 0, 2 * NA_KW - 2)
    tiles = jnp.where(col_ok[None, None], rpb.astype(F32)[:, :, col_off], NEG_BIG)
    neg = jnp.full((rpb.shape[0], 1, GRID_W, GRID_W), NEG_BIG, F32)
    return jnp.concatenate([jnp.concatenate([tiles, neg], axis=1),
                            jnp.concatenate([neg, tiles], axis=1)], axis=3)


def _na_geometry(variant, rows):
    nblk = rows // NA_ROWS
    r0 = (0, NA_ROWS * min(1, nblk - 1), NA_ROWS * (nblk - 1))[variant]
    band = min(max(r0 - NA_KH // 2, 0), rows - NA_BAND)
    pairs = []
    for qp in range(NA_ROWS // 2):
        entries = []
        for kr in range(NA_BAND):
            ok = []
            for qr in (r0 + 2 * qp, r0 + 2 * qp + 1):
                rs = min(max(qr - NA_KH // 2, 0), rows - NA_KH)
                ok.append(rs <= band + kr < rs + NA_KH)
            if ok[0] or ok[1]:
                entries.append((kr, band + kr - (r0 + 2 * qp) + NA_KH - 1, ok[0], ok[1]))
        pairs.append(entries)
    return pairs


def _na_kernel(qT_ref, k_ref, vT_ref, tile_ref, o_ref, *, rows):
    i = pl.program_id(2)
    nblk = rows // NA_ROWS
    nk = NA_BAND * GRID_W
    off = pl.multiple_of(_na_band_start(i * NA_ROWS, rows) * GRID_W, 2 * GRID_W)

    def block(variant):
        kb = k_ref[pl.ds(off, nk), :]
        vT = vT_ref[:, pl.ds(off, nk)]
        lane = lax.broadcasted_iota(jnp.int32, kb.shape, 1)
        kbd = jnp.concatenate([jnp.where(lane < NA_HEAD_DIM, kb, jnp.zeros_like(kb)),
                               jnp.where(lane < NA_HEAD_DIM, jnp.zeros_like(kb), kb)], axis=0)
        row = lax.broadcasted_iota(jnp.int32, vT.shape, 0)
        vbdT = jnp.concatenate([jnp.where(row < NA_HEAD_DIM, vT, jnp.zeros_like(vT)),
                                jnp.where(row < NA_HEAD_DIM, jnp.zeros_like(vT), vT)], axis=1)
        s = jnp.dot(kbd, qT_ref[...], preferred_element_type=F32)
        left = lax.broadcasted_iota(jnp.int32, (GRID_W, LANES), 1) < GRID_W
        zero_slab = jnp.zeros((GRID_W, LANES), BF16)
        geometry = _na_geometry(variant, rows)
        p_heads, l_heads = [], []
        for h in range(2):
            p_cols, l_cols = [], []
            for qp, entries in enumerate(geometry):
                logits = {}
                for kr, d, ok_l, ok_r in entries:
                    bias = tile_ref[h, d]
                    if not ok_r:
                        bias = jnp.where(left, bias, NEG_BIG)
                    if not ok_l:
                        bias = jnp.where(left, NEG_BIG, bias)
                    r = h * nk + kr * GRID_W
                    logits[kr] = s[r:r + GRID_W, qp * LANES:(qp + 1) * LANES] + bias
                m = functools.reduce(jnp.maximum, logits.values())
                m = jnp.max(m, axis=0, keepdims=True)
                probs = {kr: jnp.exp(x - m) for kr, x in logits.items()}
                l_cols.append(jnp.sum(functools.reduce(jnp.add, probs.values()), axis=0, keepdims=True))
                p_cols.append(jnp.concatenate(
                    [probs[kr].astype(BF16) if kr in probs else zero_slab for kr in range(NA_BAND)], axis=0))
            p_heads.append(jnp.concatenate(p_cols, axis=1))
            l_heads.append(jnp.concatenate(l_cols, axis=1))
        oT = jnp.dot(vbdT, jnp.concatenate(p_heads, axis=0), preferred_element_type=F32)
        row_o = lax.broadcasted_iota(jnp.int32, oT.shape, 0)
        oT = oT / jnp.where(row_o < NA_HEAD_DIM, l_heads[0], l_heads[1])
        o_ref[...] = oT.T.astype(BF16)

    pl.when(i == 0)(lambda: block(0))
    pl.when(jnp.logical_and(i > 0, i < nblk - 1))(lambda: block(1))
    pl.when(jnp.logical_and(i > 0, i == nblk - 1))(lambda: block(2))


def _na_attn(qnT, kn, vnT, tiles):
    B, S, _ = kn.shape
    rows = S // GRID_W
    nblk = rows // NA_ROWS
    nq = NA_ROWS * GRID_W
    return pl.pallas_call(
        functools.partial(_na_kernel, rows=rows),
        grid=(B, NA_HEADS // 2, nblk),
        in_specs=[pl.BlockSpec((None, LANES, nq), lambda b, h, i: (b, h, i)),
                  pl.BlockSpec((None, S, LANES), lambda b, h, i: (b, 0, h)),
                  pl.BlockSpec((None, LANES, S), lambda b, h, i: (b, h, 0)),
                  pl.BlockSpec((2,) + tiles.shape[1:], lambda b, h, i: (h, 0, 0, 0))],
        out_specs=pl.BlockSpec((None, nq, LANES), lambda b, h, i: (b, i, h)),
        out_shape=jax.ShapeDtypeStruct((B, S, NA_WIDTH), BF16),
        compiler_params=_cparams(("parallel", "parallel", "arbitrary")),
        name="na_attn",
    )(qnT, kn, vnT, tiles)


def _layer_norm(h, g, b):
    mu = jnp.mean(h, axis=-1, keepdims=True)
    d = h - mu
    var = jnp.mean(d * d, axis=-1, keepdims=True)
    return d * lax.rsqrt(var + LN_EPS) * g + b


def _merge_kernel(x_ref, a_ref, nb_ref, g_ref, wda_ref, wna_ref, wout_ref, lng_ref, lnb_ref,
                  wr_ref, br_ref, x1_ref, x1b_ref, topi_ref, gate_ref, *, alpha):
    d = x_ref.shape[1]
    ya = jnp.dot(a_ref[...], wda_ref[...], preferred_element_type=F32)
    yb = jnp.dot(nb_ref[...], wna_ref[...], preferred_element_type=F32)
    g = g_ref[...].astype(F32)
    merged = g[:, :d] * ya + g[:, d:] * yb
    mix = jnp.dot(merged.astype(BF16), wout_ref[...], preferred_element_type=F32)
    x1 = _layer_norm(alpha * x_ref[...] + mix, lng_ref[...], lnb_ref[...])
    x1_ref[...] = x1
    x1b_ref[...] = x1.astype(BF16)
    tm = x1.shape[0]
    xh = x1.astype(BF16)
    xl = (x1 - xh.astype(F32)).astype(BF16)
    r = jnp.dot(jnp.concatenate([xh, xl], axis=0), wr_ref[...], preferred_element_type=F32)
    logits = (r[:tm, :LANES] + r[:tm, LANES:]) + (r[tm:, :LANES] + r[tm:, LANES:]) + br_ref[...]
    lane = lax.broadcasted_iota(jnp.int32, logits.shape, 1).astype(F32)
    vals, idxs = [], []
    for _ in range(TOP_K):
        m = jnp.max(logits, axis=1, keepdims=True)
        idx = jnp.min(jnp.where(logits == m, lane, float(LANES)), axis=1, keepdims=True)
        vals.append(m)
        idxs.append(idx)
        logits = jnp.where(lane == idx, -jnp.inf, logits)
    es = [jnp.exp(v - vals[0]) for v in vals]
    denom = es[0] + es[1] + es[2] + es[3]
    gates = jnp.zeros(logits.shape, F32)
    topi = jnp.zeros(logits.shape, F32)
    for k in range(TOP_K):
        gates = jnp.where(lane == k, es[k] / denom, gates)
        topi = jnp.where(lane == k, idxs[k], topi)
    gate_ref[...] = gates
    topi_ref[...] = topi.astype(jnp.int32)


def _merge(x2, a2, nb2, g2, wda, wna, wout, ln_g, ln_b, wr, br, alpha):
    T, D = x2.shape
    tm = min(PROJ_TM, T)
    tok = lambda width: pl.BlockSpec((tm, width), lambda i: (i, 0))
    const = lambda shape: pl.BlockSpec(shape, lambda i: (0,) * len(shape))
    return pl.pallas_call(
        functools.partial(_merge_kernel, alpha=alpha),
        grid=(T // tm,),
        in_specs=[tok(D), tok(a2.shape[1]), tok(nb2.shape[1]), tok(g2.shape[1]),
                  const(wda.shape), const(wna.shape), const(wout.shape),
                  const((1, D)), const((1, D)), const(wr.shape), const((1, LANES))],
        out_specs=(tok(D), tok(D), tok(LANES), tok(LANES)),
        out_shape=(jax.ShapeDtypeStruct((T, D), F32), jax.ShapeDtypeStruct((T, D), BF16),
                   jax.ShapeDtypeStruct((T, LANES), jnp.int32), jax.ShapeDtypeStruct((T, LANES), F32)),
        compiler_params=_cparams(("parallel",)),
        name="merge",
    )(x2, a2, nb2, g2, wda, wna, wout, ln_g, ln_b, wr, br)


def _glu_group_bias(b1):
    e, n = b1.shape
    return b1.reshape(e, n // (2 * LANES), LANES, 2).transpose(0, 1, 3, 2).reshape(e, 1, n)


def _moe_kernel(be_ref, nused_ref, xs_ref, w1_ref, b1_ref, w2_ref, b2_ref, *rest, blk0):
    ys_ref, w1g_sc, w2b_sc = rest[-3:]
    i = pl.program_id(0)
    blk = i + blk0
    used = blk < nused_ref[0]
    fresh = jnp.logical_or(i == 0, be_ref[blk] != be_ref[jnp.maximum(blk - 1, 0)])
    grp = 2 * LANES

    @pl.when(jnp.logical_and(used, fresh))
    def _():
        r = lax.broadcasted_iota(jnp.int32, (grp, grp), 0)
        c = lax.broadcasted_iota(jnp.int32, (grp, grp), 1)
        src = jnp.where(c < LANES, 2 * c, 2 * (c - LANES) + 1)
        perm = jnp.where(r == src, 1.0, 0.0).astype(BF16)
        for g in range(w1_ref.shape[1] // grp):
            blk = w1_ref[:, g * grp:(g + 1) * grp].astype(BF16)
            w1g_sc[:, g * grp:(g + 1) * grp] = jnp.dot(blk, perm, preferred_element_type=F32).astype(BF16)
        w2b_sc[...] = w2_ref[...].astype(BF16)

    @pl.when(used)
    def _():
        xb = xs_ref[...]
        h = jnp.dot(xb, w1g_sc[...], preferred_element_type=F32) + b1_ref[...]
        acts = []
        for g in range(h.shape[1] // grp):
            x_glu = jnp.minimum(h[:, g * grp:g * grp + LANES], SWIGLU_LIMIT)
            x_lin = jnp.clip(h[:, g * grp + LANES:(g + 1) * grp], -SWIGLU_LIMIT, SWIGLU_LIMIT)
            acts.append(x_glu * (1.0 / (1.0 + jnp.exp(-SWIGLU_ALPHA * x_glu))) * (x_lin + 1.0))
        act = jnp.concatenate(acts, axis=1).astype(BF16)
        y = jnp.dot(act, w2b_sc[...], preferred_element_type=F32) + b2_ref[...]
        ys_ref[...] = y.astype(ys_ref.dtype)

    @pl.when(jnp.logical_not(used))
    def _():
        ys_ref[...] = jnp.zeros(ys_ref.shape, ys_ref.dtype)


def _moe_ffn(xs, block_expert, n_used, w1, b1g, w2, b2, blk0=0, ys_prev=None):
    D = xs.shape[1]
    P = block_expert.shape[0] * MOE_BLOCK
    de2 = w1.shape[2]
    de = w2.shape[1]
    wspec = lambda r, c: pl.BlockSpec((None, r, c), lambda i, be, nu: (be[i + blk0], 0, 0))
    in_specs = [pl.BlockSpec((MOE_BLOCK, D), lambda i, be, nu: (i, 0)),
                wspec(D, de2), wspec(1, de2), wspec(de, D), wspec(1, D)]
    args = [block_expert, n_used, xs, w1, b1g, w2, b2]
    aliases = {}
    if ys_prev is not None:
        in_specs.append(pl.BlockSpec(memory_space=pl.ANY))
        args.append(ys_prev)
        aliases = {len(args) - 1: 0}
    grid_spec = pltpu.PrefetchScalarGridSpec(
        num_scalar_prefetch=2,
        grid=(xs.shape[0] // MOE_BLOCK,),
        in_specs=in_specs,
        out_specs=pl.BlockSpec((MOE_BLOCK, D), lambda i, be, nu: (i + blk0, 0)),
        scratch_shapes=[pltpu.VMEM((D, de2), BF16), pltpu.VMEM((de, D), BF16)],
    )
    return pl.pallas_call(
        functools.partial(_moe_kernel, blk0=blk0),
        grid_spec=grid_spec,
        out_shape=jax.ShapeDtypeStruct((P, D), BF16),
        input_output_aliases=aliases,
        compiler_params=_cparams(("arbitrary",)),
        name="moe_ffn",
    )(*args)


def _combine_kernel(x1_ref, y0_ref, y1_ref, y2_ref, y3_ref, gate_ref, lng_ref, lnb_ref, o_ref, *, alpha):
    gates = gate_ref[...]
    ffn = jnp.zeros(x1_ref.shape, F32)
    for k, y_ref in enumerate((y0_ref, y1_ref, y2_ref, y3_ref)):
        ffn = ffn + y_ref[...].astype(F32) * gates[:, k:k + 1]
    o_ref[...] = _layer_norm(alpha * x1_ref[...] + ffn, lng_ref[...], lnb_ref[...])


def _combine(x1, y_slots, gates, ln_g, ln_b, alpha):
    T, D = x1.shape
    tm = min(PROJ_TM, T)
    tok = lambda width: pl.BlockSpec((tm, width), lambda i: (i, 0))
    const = lambda shape: pl.BlockSpec(shape, lambda i: (0,) * len(shape))
    slot = lambda k: pl.BlockSpec((None, tm, D), lambda i: (k, i, 0))
    return pl.pallas_call(
        functools.partial(_combine_kernel, alpha=alpha),
        grid=(T // tm,),
        in_specs=[tok(D)] + [slot(k) for k in range(TOP_K)] + [tok(LANES), const((1, D)), const((1, D))],
        out_specs=tok(D),
        out_shape=jax.ShapeDtypeStruct((T, D), F32),
        compiler_params=_cparams(("parallel",)),
        name="combine",
    )(x1, y_slots, y_slots, y_slots, y_slots, gates, ln_g, ln_b)


def _route(top_i, n_experts):
    T = top_i.shape[0]
    TK = T * TOP_K
    experts = jnp.arange(n_experts, dtype=jnp.int32)
    chosen = top_i[:, :, None] == experts
    picks = jnp.sum(chosen, axis=1, dtype=jnp.int32)
    csum = jnp.cumsum(picks, axis=0)
    counts = csum[-1]
    padded = ((counts + MOE_BLOCK - 1) // MOE_BLOCK) * MOE_BLOCK
    padded_end = jnp.cumsum(padded)
    padded_start = padded_end - padded
    group_start = jnp.cumsum(counts) - counts
    slot_te = padded_start[None, :] + csum - picks
    dest = jnp.sum(jnp.where(chosen, slot_te[:, None, :], 0), axis=2)
    n_blocks = -(-TK // MOE_BLOCK) + n_experts
    blk_first = jnp.arange(n_blocks, dtype=jnp.int32) * MOE_BLOCK
    block_expert = jnp.minimum(jnp.sum(padded_end[None, :] <= blk_first[:, None], axis=1),
                               n_experts - 1).astype(jnp.int32)
    n_used = (padded_end[-1] // MOE_BLOCK).astype(jnp.int32).reshape(1)
    order = jnp.argsort(top_i.reshape(-1), stable=True).astype(jnp.int32)
    rank0 = blk_first - padded_start[block_expert]
    rank = rank0[:, None] + jnp.arange(MOE_BLOCK, dtype=jnp.int32)[None, :]
    valid = rank < counts[block_expert][:, None]
    sorted_pos = jnp.clip(group_start[block_expert][:, None] + rank, 0, TK - 1)
    filler = (blk_first[:, None] + jnp.arange(MOE_BLOCK, dtype=jnp.int32)[None, :]) % T
    src_tok = jnp.where(valid, order[sorted_pos] // TOP_K, filler).reshape(-1)
    return src_tok, block_expert, n_used, dest.T.reshape(-1)


def _rope_tables(seq):
    pos = jnp.arange(seq, dtype=F32)
    inv = ROPE_THETA ** (-jnp.arange(0, DA_HEAD_DIM, 2, dtype=F32) / DA_HEAD_DIM)
    ang = pos[:, None] * inv[None, :]
    ang = jnp.concatenate([ang, ang], axis=-1)
    cos, sin = jnp.cos(ang), jnp.sin(ang)
    half = DA_HEAD_DIM // 2
    sin_signed = jnp.concatenate([-sin[:, :half], sin[:, half:]], axis=-1)
    return jnp.tile(cos, (1, 2)), jnp.tile(sin_signed, (1, 2))


def kernel(x, w_in, b_in, lambda_q1, lambda_k1, lambda_q2, lambda_k2, subln_g, rpb, w_branch_da, w_branch_na, w_out, ln1_g, ln1_b, w_router, b_router, w_mlp1, b_mlp1, w_mlp2, b_mlp2, ln2_g, ln2_b):
    B, S, D = x.shape
    depth = w_in.shape[0]
    n_experts = w_router.shape[2]
    T = B * S
    rows = S // GRID_W
    assert S % GRID_W == 0 and rows % NA_ROWS == 0 and rows >= NA_BAND
    alpha = (2 * depth) ** 0.25
    cos, sin = _rope_tables(S)
    row = lambda v: v.reshape(1, -1)

    for l in range(depth):
        lam_init = 0.8 - 0.6 * math.exp(-0.3 * l)
        qT, k, vT, qn, kn, vn, gates_br = _in_proj(x, w_in[l].astype(BF16), row(b_in[l]), cos, sin)
        a = _diff_attn(qT, k, vT, row(lambda_q1[l]), row(lambda_k1[l]), row(lambda_q2[l]),
                       row(lambda_k2[l]), subln_g[l].reshape(-1, 1), lam_init)
        nb = _na_attn(qn, kn, vn, _na_tiles(rpb[l]))

        wr = jnp.pad(w_router[l], ((0, 0), (0, LANES - n_experts)))
        wr_h = wr.astype(BF16)
        wr = jnp.concatenate([wr_h, (wr - wr_h.astype(F32)).astype(BF16)], axis=1)
        br = jnp.pad(row(b_router[l]), ((0, 0), (0, LANES - n_experts)), constant_values=NEG_BIG)
        x1, x1b, topi, gates = _merge(
            x.reshape(T, D), a.reshape(T, -1), nb.reshape(T, -1), gates_br.reshape(T, -1),
            w_branch_da[l].astype(BF16), w_branch_na[l].astype(BF16), w_out[l].astype(BF16),
            row(ln1_g[l]), row(ln1_b[l]), wr, br, alpha)

        src_tok, block_expert, n_used, dest = _route(topi[:, :TOP_K], n_experts)
        ys, blk0 = None, 0
        for nblk in (block_expert.shape[0] // 2, block_expert.shape[0] - block_expert.shape[0] // 2):
            rows_ = src_tok[blk0 * MOE_BLOCK:(blk0 + nblk) * MOE_BLOCK]
            ys = _moe_ffn(jnp.take(x1b, rows_, axis=0, mode="clip"), block_expert, n_used, w_mlp1[l],
                          _glu_group_bias(b_mlp1[l]), w_mlp2[l], b_mlp2[l][:, None, :], blk0, ys)
            blk0 += nblk
        y_slots = jnp.take(ys, dest, axis=0, mode="clip").reshape(TOP_K, T, D)
        x = _combine(x1, y_slots, gates, row(ln2_g[l]), row(ln2_b[l]), alpha).reshape(B, S, D)
    return x
```

```python
import functools
import math

import jax
import jax.numpy as jnp
from jax import lax
from jax.experimental import pallas as pl
from jax.experimental.pallas import tpu as pltpu

F32 = jnp.float32
BF16 = jnp.bfloat16

GRID_W = 64
DA_HEADS = 4
DA_HEAD_DIM = 64
DA_V_DIM = 2 * DA_HEAD_DIM
DA_WIDTH = DA_HEADS * DA_V_DIM
ROPE_THETA = 10000.0
NA_HEADS = 8
NA_HEAD_DIM = 64
NA_WIDTH = NA_HEADS * NA_HEAD_DIM
NA_KH = 8
NA_KW = 16
TOP_K = 4
SWIGLU_ALPHA = 1.702
SWIGLU_LIMIT = 7.0
MOE_BLOCK = 512
LN_EPS = 1e-5
RMS_EPS = 1e-5

LANES = 128
VMEM_LIMIT_BYTES = 56 * 1024 * 1024

LOG2E = math.log2(math.e)
NEG_BIG = -1e30

PROJ_TM = 512
DA_TQ = 512
DA_TK = 512
DA_UNROLL = 8
NA_ROWS = 8
NA_BAND = 16


def _cparams(sem):
    return pltpu.CompilerParams(dimension_semantics=sem, vmem_limit_bytes=VMEM_LIMIT_BYTES)


def _in_proj_kernel(x_ref, w_ref, b_ref, cos_ref, sin_ref,
                    qT_ref, k_ref, vT_ref, qn_ref, kn_ref, vn_ref, g_ref):
    xb = x_ref[...].astype(BF16)

    def seg(lo, hi):
        return jnp.dot(xb, w_ref[:, lo:hi], preferred_element_type=F32) + b_ref[:, lo:hi]

    cos = cos_ref[...]
    sin = sin_ref[...]
    lane = lax.broadcasted_iota(jnp.int32, cos.shape, 1)
    first_half = (lane % DA_HEAD_DIM) < (DA_HEAD_DIM // 2)

    def rope(y):
        outs = []
        for h in range(DA_HEADS):
            yh = y[:, h * LANES:(h + 1) * LANES]
            partner = jnp.where(first_half,
                                pltpu.roll(yh, LANES - DA_HEAD_DIM // 2, 1),
                                pltpu.roll(yh, DA_HEAD_DIM // 2, 1))
            outs.append(yh * cos + partner * sin)
        return jnp.concatenate(outs, axis=1)

    w = DA_WIDTH
    q = rope(seg(0, w)) * (DA_HEAD_DIM ** -0.5 * LOG2E)
    qT_ref[...] = q.T.astype(BF16)
    k_ref[...] = rope(seg(w, 2 * w)).astype(BF16)
    vT_ref[...] = seg(2 * w, 3 * w).T.astype(BF16)
    o = 3 * w
    qn_ref[...] = (seg(o, o + NA_WIDTH) * (NA_HEAD_DIM ** -0.5)).T.astype(BF16)
    kn_ref[...] = seg(o + NA_WIDTH, o + 2 * NA_WIDTH).astype(BF16)
    vn_ref[...] = seg(o + 2 * NA_WIDTH, o + 3 * NA_WIDTH).T.astype(BF16)
    g0 = o + 3 * NA_WIDTH
    gate_pre = seg(g0, w_ref.shape[1])
    g_ref[...] = (1.0 / (1.0 + jnp.exp(-gate_pre))).astype(BF16)


def _in_proj(x, w_in, b_in, cos, sin):
    B, S, D = x.shape
    tm = min(PROJ_TM, S)
    n_cols = w_in.shape[1]
    n_gate = n_cols - 3 * DA_WIDTH - 3 * NA_WIDTH
    tok = lambda width: pl.BlockSpec((None, tm, width), lambda b, i: (b, i, 0))
    tr = pl.BlockSpec((None, DA_WIDTH, tm), lambda b, i: (b, 0, i))
    const = lambda shape: pl.BlockSpec(shape, lambda b, i: (0,) * len(shape))
    out_shape = (
        jax.ShapeDtypeStruct((B, DA_WIDTH, S), BF16),
        jax.ShapeDtypeStruct((B, S, DA_WIDTH), BF16),
        jax.ShapeDtypeStruct((B, DA_WIDTH, S), BF16),
        jax.ShapeDtypeStruct((B, NA_WIDTH, S), BF16),
        jax.ShapeDtypeStruct((B, S, NA_WIDTH), BF16),
        jax.ShapeDtypeStruct((B, NA_WIDTH, S), BF16),
        jax.ShapeDtypeStruct((B, S, n_gate), BF16),
    )
    assert DA_WIDTH == NA_WIDTH
    return pl.pallas_call(
        _in_proj_kernel,
        grid=(B, S // tm),
        in_specs=[tok(D), const((D, n_cols)), const((1, n_cols)),
                  pl.BlockSpec((tm, LANES), lambda b, i: (i, 0)),
                  pl.BlockSpec((tm, LANES), lambda b, i: (i, 0))],
        out_specs=(tr, tok(DA_WIDTH), tr, tr, tok(NA_WIDTH), tr, tok(n_gate)),
        out_shape=out_shape,
        compiler_params=_cparams(("parallel", "parallel")),
        name="in_proj",
    )(x, w_in, b_in, cos, sin)


def _diff_attn_kernel(qT_ref, qTn_ref, k_ref, vT_ref, lq1_ref, lk1_ref, lq2_ref, lk2_ref, g_ref,
                      o_ref, m_sc, l_sc, acc_sc, s_sc, cmax_sc, *, tk, unroll, lam_init):
    tq = qT_ref.shape[1]

    def block_diag(q_ref):
        qT = q_ref[...]
        row = lax.broadcasted_iota(jnp.int32, qT.shape, 0)
        zero = jnp.zeros_like(qT)
        return jnp.concatenate([jnp.where(row < DA_HEAD_DIM, qT, zero),
                                jnp.where(row < DA_HEAD_DIM, zero, qT)], axis=1)

    qbd = block_diag(qT_ref)

    m_sc[...] = jnp.full(m_sc.shape, -jnp.inf, F32)
    l_sc[...] = jnp.zeros(l_sc.shape, F32)
    acc_sc[...] = jnp.zeros(acc_sc.shape, F32)

    nk = k_ref.shape[0] // tk

    def put_scores(slot, j, q=None):
        off = pl.multiple_of(j * tk, tk)
        s = jnp.dot(k_ref[pl.ds(off, tk), :], qbd if q is None else q, preferred_element_type=F32)
        s_sc[slot] = s
        cmax_sc[slot] = jnp.max(s, axis=0, keepdims=True)

    def accumulate(slot, j):
        off = pl.multiple_of(j * tk, tk)
        m_prev = m_sc[...]
        m_new = jnp.maximum(m_prev, cmax_sc[slot])
        alpha = jnp.exp2(m_prev - m_new)
        p = jnp.exp2(s_sc[slot] - m_new)
        l_sc[...] = alpha * l_sc[...] + jnp.sum(p, axis=0, keepdims=True)
        vb = vT_ref[:, pl.ds(off, tk)]
        acc_sc[...] = alpha * acc_sc[...] + jnp.dot(vb, p.astype(BF16), preferred_element_type=F32)
        m_sc[...] = m_new

    @pl.when(pl.program_id(2) == 0)
    def _():
        put_scores(0, 0)

    def body(jj, carry):
        for u in range(unroll):
            j = unroll * jj + u
            put_scores((u + 1) % 2, j + 1)
            accumulate(u % 2, j)
        return carry

    lax.fori_loop(0, nk // unroll - 1, body, 0)
    for j in range(nk - unroll, nk):
        if j + 1 < nk:
            put_scores((j + 1) % 2, j + 1)
        else:
            put_scores(0, 0, block_diag(qTn_ref))
        accumulate(j % 2, j)

    lam = (jnp.exp(jnp.sum(lq1_ref[...] * lk1_ref[...], axis=1, keepdims=True))
           - jnp.exp(jnp.sum(lq2_ref[...] * lk2_ref[...], axis=1, keepdims=True)) + lam_init)
    on = acc_sc[...] / l_sc[...]
    o = on[:, :tq] - lam * on[:, tq:]
    ms = jnp.mean(o * o, axis=0, keepdims=True)
    o = o * lax.rsqrt(ms + RMS_EPS) * g_ref[...]
    o = o * (1.0 - lam_init)
    o_ref[...] = o.T.astype(BF16)


def _diff_attn(qT, k, vT, lq1, lk1, lq2, lk2, subln_g, lam_init):
    B, S, _ = k.shape
    tq = min(DA_TQ, S)
    tk = min(DA_TK, S // 2)
    unroll = math.gcd(DA_UNROLL, S // tk)
    assert unroll % 2 == 0 and S % tk == 0 and S % tq == 0
    nq = S // tq
    vec = pl.BlockSpec((1, DA_HEAD_DIM), lambda b, h, i: (0, 0))
    kernel = functools.partial(_diff_attn_kernel, tk=tk, unroll=unroll, lam_init=lam_init)
    return pl.pallas_call(
        kernel,
        grid=(B, DA_HEADS, nq),
        in_specs=[pl.BlockSpec((None, DA_V_DIM, tq), lambda b, h, i: (b, h, i)),
                  pl.BlockSpec((None, DA_V_DIM, tq), lambda b, h, i: (b, h, jnp.minimum(i + 1, nq - 1))),
                  pl.BlockSpec((None, S, DA_V_DIM), lambda b, h, i: (b, 0, h)),
                  pl.BlockSpec((None, DA_V_DIM, S), lambda b, h, i: (b, h, 0)),
                  vec, vec, vec, vec,
                  pl.BlockSpec((DA_V_DIM, 1), lambda b, h, i: (0, 0))],
        out_specs=pl.BlockSpec((None, tq, DA_V_DIM), lambda b, h, i: (b, i, h)),
        out_shape=jax.ShapeDtypeStruct((B, S, DA_WIDTH), BF16),
        scratch_shapes=[pltpu.VMEM((1, 2 * tq), F32), pltpu.VMEM((1, 2 * tq), F32),
                        pltpu.VMEM((DA_V_DIM, 2 * tq), F32), pltpu.VMEM((2, tk, 2 * tq), F32),
                        pltpu.VMEM((2, 1, 2 * tq), F32)],
        compiler_params=_cparams(("parallel", "parallel", "arbitrary")),
        name="diff_attn",
    )(qT, qT, k, vT, lq1, lk1, lq2, lk2, subln_g)


def _na_band_start(r0, rows):
    return jnp.clip(r0 - NA_KH // 2, 0, rows - NA_BAND)


def _na_tiles(rpb):
    cols = jnp.arange(GRID_W)
    cs = jnp.clip(cols - NA_KW // 2, 0, GRID_W - NA_KW)
    col_ok = (cols[:, None] >= cs[None, :]) & (cols[:, None] < cs[None, :] + NA_KW)
    col_off = jnp.clip(cols[:, None] - cols[None, :] + (NA_KW - 1), 0, 2 * NA_KW - 2)
    tiles = jnp.where(col_ok[None, None], rpb.astype(F32)[:, :, col_off], NEG_BIG)
    neg = jnp.full((rpb.shape[0], 1, GRID_W, GRID_W), NEG_BIG, F32)
    return jnp.concatenate([jnp.concatenate([tiles, neg], axis=1),
                            jnp.concatenate([neg, tiles], axis=1)], axis=3)


def _na_geometry(variant, rows):
    nblk = rows // NA_ROWS
    r0 = (0, NA_ROWS * min(1, nblk - 1), NA_ROWS * (nblk - 1))[variant]
    band = min(max(r0 - NA_KH // 2, 0), rows - NA_BAND)
    pairs = []
    for qp in range(NA_ROWS // 2):
        entries = []
        for kr in range(NA_BAND):
            ok = []
            for qr in (r0 + 2 * qp, r0 + 2 * qp + 1):
                rs = min(max(qr - NA_KH // 2, 0), rows - NA_KH)
                ok.append(rs <= band + kr < rs + NA_KH)
            if ok[0] or ok[1]:
                entries.append((kr, band + kr - (r0 + 2 * qp) + NA_KH - 1, ok[0], ok[1]))
        pairs.append(entries)
    return pairs


def _na_kernel(qT_ref, k_ref, vT_ref, tile_ref, o_ref, *, rows):
    i = pl.program_id(2)
    nblk = rows // NA_ROWS
    nk = NA_BAND * GRID_W
    off = pl.multiple_of(_na_band_start(i * NA_ROWS, rows) * GRID_W, 2 * GRID_W)

    def block(variant):
        kb = k_ref[pl.ds(off, nk), :]
        vT = vT_ref[:, pl.ds(off, nk)]
        lane = lax.broadcasted_iota(jnp.int32, kb.shape, 1)
        kbd = jnp.concatenate([jnp.where(lane < NA_HEAD_DIM, kb, jnp.zeros_like(kb)),
                               jnp.where(lane < NA_HEAD_DIM, jnp.zeros_like(kb), kb)], axis=0)
        row = lax.broadcasted_iota(jnp.int32, vT.shape, 0)
        vbdT = jnp.concatenate([jnp.where(row < NA_HEAD_DIM, vT, jnp.zeros_like(vT)),
                                jnp.where(row < NA_HEAD_DIM, jnp.zeros_like(vT), vT)], axis=1)
        s = jnp.dot(kbd, qT_ref[...], preferred_element_type=F32)
        left = lax.broadcasted_iota(jnp.int32, (GRID_W, LANES), 1) < GRID_W
        zero_slab = jnp.zeros((GRID_W, LANES), BF16)
        geometry = _na_geometry(variant, rows)
        p_heads, l_heads = [], []
        for h in range(2):
            p_cols, l_cols = [], []
            for qp, entries in enumerate(geometry):
                logits = {}
                for kr, d, ok_l, ok_r in entries:
                    bias = tile_ref[h, d]
                    if not ok_r:
                        bias = jnp.where(left, bias, NEG_BIG)
                    if not ok_l:
                        bias = jnp.where(left, NEG_BIG, bias)
                    r = h * nk + kr * GRID_W
                    logits[kr] = s[r:r + GRID_W, qp * LANES:(qp + 1) * LANES] + bias
                m = functools.reduce(jnp.maximum, logits.values())
                m = jnp.max(m, axis=0, keepdims=True)
                probs = {kr: jnp.exp(x - m) for kr, x in logits.items()}
                l_cols.append(jnp.sum(functools.reduce(jnp.add, probs.values()), axis=0, keepdims=True))
                p_cols.append(jnp.concatenate(
                    [probs[kr].astype(BF16) if kr in probs else zero_slab for kr in range(NA_BAND)], axis=0))
            p_heads.append(jnp.concatenate(p_cols, axis=1))
            l_heads.append(jnp.concatenate(l_cols, axis=1))
        oT = jnp.dot(vbdT, jnp.concatenate(p_heads, axis=0), preferred_element_type=F32)
        row_o = lax.broadcasted_iota(jnp.int32, oT.shape, 0)
        oT = oT / jnp.where(row_o < NA_HEAD_DIM, l_heads[0], l_heads[1])
        o_ref[...] = oT.T.astype(BF16)

    pl.when(i == 0)(lambda: block(0))
    pl.when(jnp.logical_and(i > 0, i < nblk - 1))(lambda: block(1))
    pl.when(jnp.logical_and(i > 0, i == nblk - 1))(lambda: block(2))


def _na_attn(qnT, kn, vnT, tiles):
    B, S, _ = kn.shape
    rows = S // GRID_W
    nblk = rows // NA_ROWS
    nq = NA_ROWS * GRID_W
    return pl.pallas_call(
        functools.partial(_na_kernel, rows=rows),
        grid=(B, NA_HEADS // 2, nblk),
        in_specs=[pl.BlockSpec((None, LANES, nq), lambda b, h, i: (b, h, i)),
                  pl.BlockSpec((None, S, LANES), lambda b, h, i: (b, 0, h)),
                  pl.BlockSpec((None, LANES, S), lambda b, h, i: (b, h, 0)),
                  pl.BlockSpec((2,) + tiles.shape[1:], lambda b, h, i: (h, 0, 0, 0))],
        out_specs=pl.BlockSpec((None, nq, LANES), lambda b, h, i: (b, i, h)),
        out_shape=jax.ShapeDtypeStruct((B, S, NA_WIDTH), BF16),
        compiler_params=_cparams(("parallel", "parallel", "arbitrary")),
        name="na_attn",
    )(qnT, kn, vnT, tiles)


def _layer_norm(h, g, b):
    mu = jnp.mean(h, axis=-1, keepdims=True)
    d = h - mu
    var = jnp.mean(d * d, axis=-1, keepdims=True)
    return d * lax.rsqrt(var + LN_EPS) * g + b


def _merge_kernel(x_ref, a_ref, nb_ref, g_ref, wda_ref, wna_ref, wout_ref, lng_ref, lnb_ref,
                  wr_ref, br_ref, x1_ref, x1b_ref, topi_ref, gate_ref, *, alpha):
    d = x_ref.shape[1]
    ya = jnp.dot(a_ref[...], wda_ref[...], preferred_element_type=F32)
    yb = jnp.dot(nb_ref[...], wna_ref[...], preferred_element_type=F32)
    g = g_ref[...].astype(F32)
    merged = g[:, :d] * ya + g[:, d:] * yb
    mix = jnp.dot(merged.astype(BF16), wout_ref[...], preferred_element_type=F32)
    x1 = _layer_norm(alpha * x_ref[...] + mix, lng_ref[...], lnb_ref[...])
    x1_ref[...] = x1
    x1b_ref[...] = x1.astype(BF16)
    tm = x1.shape[0]
    xh = x1.astype(BF16)
    xl = (x1 - xh.astype(F32)).astype(BF16)
    r = jnp.dot(jnp.concatenate([xh, xl], axis=0), wr_ref[...], preferred_element_type=F32)
    logits = (r[:tm, :LANES] + r[:tm, LANES:]) + (r[tm:, :LANES] + r[tm:, LANES:]) + br_ref[...]
    lane = lax.broadcasted_iota(jnp.int32, logits.shape, 1).astype(F32)
    vals, idxs = [], []
    for _ in range(TOP_K):
        m = jnp.max(logits, axis=1, keepdims=True)
        idx = jnp.min(jnp.where(logits == m, lane, float(LANES)), axis=1, keepdims=True)
        vals.append(m)
        idxs.append(idx)
        logits = jnp.where(lane == idx, -jnp.inf, logits)
    es = [jnp.exp(v - vals[0]) for v in vals]
    denom = es[0] + es[1] + es[2] + es[3]
    gates = jnp.zeros(logits.shape, F32)
    topi = jnp.zeros(logits.shape, F32)
    for k in range(TOP_K):
        gates = jnp.where(lane == k, es[k] / denom, gates)
        topi = jnp.where(lane == k, idxs[k], topi)
    gate_ref[...] = gates
    topi_ref[...] = topi.astype(jnp.int32)


def _merge(x2, a2, nb2, g2, wda, wna, wout, ln_g, ln_b, wr, br, alpha):
    T, D = x2.shape
    tm = min(PROJ_TM, T)
    tok = lambda width: pl.BlockSpec((tm, width), lambda i: (i, 0))
    const = lambda shape: pl.BlockSpec(shape, lambda i: (0,) * len(shape))
    return pl.pallas_call(
        functools.partial(_merge_kernel, alpha=alpha),
        grid=(T // tm,),
        in_specs=[tok(D), tok(a2.shape[1]), tok(nb2.shape[1]), tok(g2.shape[1]),
                  const(wda.shape), const(wna.shape), const(wout.shape),
                  const((1, D)), const((1, D)), const(wr.shape), const((1, LANES))],
        out_specs=(tok(D), tok(D), tok(LANES), tok(LANES)),
        out_shape=(jax.ShapeDtypeStruct((T, D), F32), jax.ShapeDtypeStruct((T, D), BF16),
                   jax.ShapeDtypeStruct((T, LANES), jnp.int32), jax.ShapeDtypeStruct((T, LANES), F32)),
        compiler_params=_cparams(("parallel",)),
        name="merge",
    )(x2, a2, nb2, g2, wda, wna, wout, ln_g, ln_b, wr, br)


def _glu_group_bias(b1):
    e, n = b1.shape
    return b1.reshape(e, n // (2 * LANES), LANES, 2).transpose(0, 1, 3, 2).reshape(e, 1, n)


def _moe_kernel(be_ref, nused_ref, xs_ref, w1_ref, b1_ref, w2_ref, b2_ref, *rest, blk0):
    ys_ref, w1g_sc, w2b_sc = rest[-3:]
    i = pl.program_id(0)
    blk = i + blk0
    used = blk < nused_ref[0]
    fresh = jnp.logical_or(i == 0, be_ref[blk] != be_ref[jnp.maximum(blk - 1, 0)])
    grp = 2 * LANES

    @pl.when(jnp.logical_and(used, fresh))
    def _():
        r = lax.broadcasted_iota(jnp.int32, (grp, grp), 0)
        c = lax.broadcasted_iota(jnp.int32, (grp, grp), 1)
        src = jnp.where(c < LANES, 2 * c, 2 * (c - LANES) + 1)
        perm = jnp.where(r == src, 1.0, 0.0).astype(BF16)
        for g in range(w1_ref.shape[1] // grp):
            blk = w1_ref[:, g * grp:(g + 1) * grp].astype(BF16)
            w1g_sc[:, g * grp:(g + 1) * grp] = jnp.dot(blk, perm, preferred_element_type=F32).astype(BF16)
        w2b_sc[...] = w2_ref[...].astype(BF16)

    @pl.when(used)
    def _():
        xb = xs_ref[...]
        h = jnp.dot(xb, w1g_sc[...], preferred_element_type=F32) + b1_ref[...]
        acts = []
        for g in range(h.shape[1] // grp):
            x_glu = jnp.minimum(h[:, g * grp:g * grp + LANES], SWIGLU_LIMIT)
            x_lin = jnp.clip(h[:, g * grp + LANES:(g + 1) * grp], -SWIGLU_LIMIT, SWIGLU_LIMIT)
            acts.append(x_glu * (1.0 / (1.0 + jnp.exp(-SWIGLU_ALPHA * x_glu))) * (x_lin + 1.0))
        act = jnp.concatenate(acts, axis=1).astype(BF16)
        y = jnp.dot(act, w2b_sc[...], preferred_element_type=F32) + b2_ref[...]
        ys_ref[...] = y.astype(ys_ref.dtype)

    @pl.when(jnp.logical_not(used))
    def _():
        ys_ref[...] = jnp.zeros(ys_ref.shape, ys_ref.dtype)


def _moe_ffn(xs, block_expert, n_used, w1, b1g, w2, b2, blk0=0, ys_prev=None):
    D = xs.shape[1]
    P = block_expert.shape[0] * MOE_BLOCK
    de2 = w1.shape[2]
    de = w2.shape[1]
    wspec = lambda r, c: pl.BlockSpec((None, r, c), lambda i, be, nu: (be[i + blk0], 0, 0))
    in_specs = [pl.BlockSpec((MOE_BLOCK, D), lambda i, be, nu: (i, 0)),
                wspec(D, de2), wspec(1, de2), wspec(de, D), wspec(1, D)]
    args = [block_expert, n_used, xs, w1, b1g, w2, b2]
    aliases = {}
    if ys_prev is not None:
        in_specs.append(pl.BlockSpec(memory_space=pl.ANY))
        args.append(ys_prev)
        aliases = {len(args) - 1: 0}
    grid_spec = pltpu.PrefetchScalarGridSpec(
        num_scalar_prefetch=2,
        grid=(xs.shape[0] // MOE_BLOCK,),
        in_specs=in_specs,
        out_specs=pl.BlockSpec((MOE_BLOCK, D), lambda i, be, nu: (i + blk0, 0)),
        scratch_shapes=[pltpu.VMEM((D, de2), BF16), pltpu.VMEM((de, D), BF16)],
    )
    return pl.pallas_call(
        functools.partial(_moe_kernel, blk0=blk0),
        grid_spec=grid_spec,
        out_shape=jax.ShapeDtypeStruct((P, D), BF16),
        input_output_aliases=aliases,
        compiler_params=_cparams(("arbitrary",)),
        name="moe_ffn",
    )(*args)


def _combine_kernel(x1_ref, y0_ref, y1_ref, y2_ref, y3_ref, gate_ref, lng_ref, lnb_ref, *rest, alpha):
    o_ref = rest[-1]
    gates = gate_ref[...]
    ffn = jnp.zeros(x1_ref.shape, F32)
    for k, y_ref in enumerate((y0_ref, y1_ref, y2_ref, y3_ref)):
        ffn = ffn + y_ref[...].astype(F32) * gates[:, k:k + 1]
    o_ref[...] = _layer_norm(alpha * x1_ref[...] + ffn, lng_ref[...], lnb_ref[...])


def _combine(x1, y_slots, gates, ln_g, ln_b, alpha, tok0=0, out_prev=None):
    T, D = x1.shape
    n = y_slots.shape[1]
    tm = min(PROJ_TM, n)
    assert n % tm == 0 and tok0 % tm == 0
    i0 = tok0 // tm
    tok = lambda width: pl.BlockSpec((tm, width), lambda i: (i + i0, 0))
    const = lambda shape: pl.BlockSpec(shape, lambda i: (0,) * len(shape))
    slot = lambda k: pl.BlockSpec((None, tm, D), lambda i: (k, i, 0))
    in_specs = [tok(D)] + [slot(k) for k in range(TOP_K)] + [tok(LANES), const((1, D)), const((1, D))]
    args = [x1, y_slots, y_slots, y_slots, y_slots, gates, ln_g, ln_b]
    aliases = {}
    if out_prev is not None:
        in_specs.append(pl.BlockSpec(memory_space=pl.ANY))
        args.append(out_prev)
        aliases = {len(args) - 1: 0}
    return pl.pallas_call(
        functools.partial(_combine_kernel, alpha=alpha),
        grid=(n // tm,),
        in_specs=in_specs,
        out_specs=tok(D),
        out_shape=jax.ShapeDtypeStruct((T, D), F32),
        input_output_aliases=aliases,
        compiler_params=_cparams(("parallel",)),
        name="combine",
    )(*args)


def _route(top_i, n_experts):
    T = top_i.shape[0]
    TK = T * TOP_K
    experts = jnp.arange(n_experts, dtype=jnp.int32)
    chosen = top_i[:, :, None] == experts
    picks = jnp.sum(chosen, axis=1, dtype=jnp.int32)
    csum = jnp.cumsum(picks, axis=0)
    counts = csum[-1]
    padded = ((counts + MOE_BLOCK - 1) // MOE_BLOCK) * MOE_BLOCK
    padded_end = jnp.cumsum(padded)
    padded_start = padded_end - padded
    group_start = jnp.cumsum(counts) - counts
    slot_te = padded_start[None, :] + csum - picks
    dest = jnp.sum(jnp.where(chosen, slot_te[:, None, :], 0), axis=2)
    n_blocks = -(-TK // MOE_BLOCK) + n_experts
    blk_first = jnp.arange(n_blocks, dtype=jnp.int32) * MOE_BLOCK
    block_expert = jnp.minimum(jnp.sum(padded_end[None, :] <= blk_first[:, None], axis=1),
                               n_experts - 1).astype(jnp.int32)
    n_used = (padded_end[-1] // MOE_BLOCK).astype(jnp.int32).reshape(1)
    order = jnp.argsort(top_i.reshape(-1), stable=True).astype(jnp.int32)
    rank0 = blk_first - padded_start[block_expert]
    rank = rank0[:, None] + jnp.arange(MOE_BLOCK, dtype=jnp.int32)[None, :]
    valid = rank < counts[block_expert][:, None]
    sorted_pos = jnp.clip(group_start[block_expert][:, None] + rank, 0, TK - 1)
    filler = (blk_first[:, None] + jnp.arange(MOE_BLOCK, dtype=jnp.int32)[None, :]) % T
    src_tok = jnp.where(valid, order[sorted_pos] // TOP_K, filler).reshape(-1)
    return src_tok, block_expert, n_used, dest.T.reshape(-1)


def _rope_tables(seq):
    pos = jnp.arange(seq, dtype=F32)
    inv = ROPE_THETA ** (-jnp.arange(0, DA_HEAD_DIM, 2, dtype=F32) / DA_HEAD_DIM)
    ang = pos[:, None] * inv[None, :]
    ang = jnp.concatenate([ang, ang], axis=-1)
    cos, sin = jnp.cos(ang), jnp.sin(ang)
    half = DA_HEAD_DIM // 2
    sin_signed = jnp.concatenate([-sin[:, :half], sin[:, half:]], axis=-1)
    return jnp.tile(cos, (1, 2)), jnp.tile(sin_signed, (1, 2))


def kernel(x, w_in, b_in, lambda_q1, lambda_k1, lambda_q2, lambda_k2, subln_g, rpb, w_branch_da, w_branch_na, w_out, ln1_g, ln1_b, w_router, b_router, w_mlp1, b_mlp1, w_mlp2, b_mlp2, ln2_g, ln2_b):
    B, S, D = x.shape
    depth = w_in.shape[0]
    n_experts = w_router.shape[2]
    T = B * S
    rows = S // GRID_W
    assert S % GRID_W == 0 and rows % NA_ROWS == 0 and rows >= NA_BAND
    alpha = (2 * depth) ** 0.25
    cos, sin = _rope_tables(S)
    row = lambda v: v.reshape(1, -1)

    for l in range(depth):
        lam_init = 0.8 - 0.6 * math.exp(-0.3 * l)
        qT, k, vT, qn, kn, vn, gates_br = _in_proj(x, w_in[l].astype(BF16), row(b_in[l]), cos, sin)
        a = _diff_attn(qT, k, vT, row(lambda_q1[l]), row(lambda_k1[l]), row(lambda_q2[l]),
                       row(lambda_k2[l]), subln_g[l].reshape(-1, 1), lam_init)
        nb = _na_attn(qn, kn, vn, _na_tiles(rpb[l]))

        wr = jnp.pad(w_router[l], ((0, 0), (0, LANES - n_experts)))
        wr_h = wr.astype(BF16)
        wr = jnp.concatenate([wr_h, (wr - wr_h.astype(F32)).astype(BF16)], axis=1)
        br = jnp.pad(row(b_router[l]), ((0, 0), (0, LANES - n_experts)), constant_values=NEG_BIG)
        x1, x1b, topi, gates = _merge(
            x.reshape(T, D), a.reshape(T, -1), nb.reshape(T, -1), gates_br.reshape(T, -1),
            w_branch_da[l].astype(BF16), w_branch_na[l].astype(BF16), w_out[l].astype(BF16),
            row(ln1_g[l]), row(ln1_b[l]), wr, br, alpha)

        src_tok, block_expert, n_used, dest = _route(topi[:, :TOP_K], n_experts)
        ys, blk0 = None, 0
        for nblk in (block_expert.shape[0] // 2, block_expert.shape[0] - block_expert.shape[0] // 2):
            rows_ = src_tok[blk0 * MOE_BLOCK:(blk0 + nblk) * MOE_BLOCK]
            ys = _moe_ffn(jnp.take(x1b, rows_, axis=0, mode="clip"), block_expert, n_used, w_mlp1[l],
                          _glu_group_bias(b_mlp1[l]), w_mlp2[l], b_mlp2[l][:, None, :], blk0, ys)
            blk0 += nblk
        out, half = None, T // 2
        for tok0 in (0, half):
            rows_ = dest.reshape(TOP_K, T)[:, tok0:tok0 + half].reshape(-1)
            y_slots = jnp.take(ys, rows_, axis=0, mode="clip").reshape(TOP_K, half, D)
            out = _combine(x1, y_slots, gates, row(ln2_g[l]), row(ln2_b[l]), alpha, tok0, out)
        x = out.reshape(B, S, D)
    return x
```

```python
import functools
import math

import jax
import jax.numpy as jnp
from jax import lax
from jax.experimental import pallas as pl
from jax.experimental.pallas import tpu as pltpu

F32 = jnp.float32
BF16 = jnp.bfloat16

GRID_W = 64
DA_HEADS = 4
DA_HEAD_DIM = 64
DA_V_DIM = 2 * DA_HEAD_DIM
DA_WIDTH = DA_HEADS * DA_V_DIM
ROPE_THETA = 10000.0
NA_HEADS = 8
NA_HEAD_DIM = 64
NA_WIDTH = NA_HEADS * NA_HEAD_DIM
NA_KH = 8
NA_KW = 16
TOP_K = 4
SWIGLU_ALPHA = 1.702
SWIGLU_LIMIT = 7.0
MOE_BLOCK = 512
LN_EPS = 1e-5
RMS_EPS = 1e-5

LANES = 128
VMEM_LIMIT_BYTES = 56 * 1024 * 1024

LOG2E = math.log2(math.e)
NEG_BIG = -1e30

PROJ_TM = 512
DA_TQ = 512
DA_TK = 512
DA_UNROLL = 8
NA_ROWS = 8
NA_BAND = 16


def _cparams(sem):
    return pltpu.CompilerParams(dimension_semantics=sem, vmem_limit_bytes=VMEM_LIMIT_BYTES)


def _in_proj_kernel(x_ref, w_ref, b_ref, cos_ref, sin_ref,
                    qT_ref, k_ref, vT_ref, qn_ref, kn_ref, vn_ref, g_ref):
    xb = x_ref[...].astype(BF16)

    def seg(lo, hi):
        return jnp.dot(xb, w_ref[:, lo:hi], preferred_element_type=F32) + b_ref[:, lo:hi]

    cos = cos_ref[...]
    sin = sin_ref[...]
    lane = lax.broadcasted_iota(jnp.int32, cos.shape, 1)
    first_half = (lane % DA_HEAD_DIM) < (DA_HEAD_DIM // 2)

    def rope(y):
        outs = []
        for h in range(DA_HEADS):
            yh = y[:, h * LANES:(h + 1) * LANES]
            partner = jnp.where(first_half,
                                pltpu.roll(yh, LANES - DA_HEAD_DIM // 2, 1),
                                pltpu.roll(yh, DA_HEAD_DIM // 2, 1))
            outs.append(yh * cos + partner * sin)
        return jnp.concatenate(outs, axis=1)

    w = DA_WIDTH
    q = rope(seg(0, w)) * (DA_HEAD_DIM ** -0.5 * LOG2E)
    qT_ref[...] = q.T.astype(BF16)
    k_ref[...] = rope(seg(w, 2 * w)).astype(BF16)
    vT_ref[...] = seg(2 * w, 3 * w).T.astype(BF16)
    o = 3 * w
    qn_ref[...] = (seg(o, o + NA_WIDTH) * (NA_HEAD_DIM ** -0.5)).T.astype(BF16)
    kn_ref[...] = seg(o + NA_WIDTH, o + 2 * NA_WIDTH).astype(BF16)
    vn_ref[...] = seg(o + 2 * NA_WIDTH, o + 3 * NA_WIDTH).T.astype(BF16)
    g0 = o + 3 * NA_WIDTH
    gate_pre = seg(g0, w_ref.shape[1])
    g_ref[...] = (1.0 / (1.0 + jnp.exp(-gate_pre))).astype(BF16)


def _in_proj(x, w_in, b_in, cos, sin):
    B, S, D = x.shape
    tm = min(PROJ_TM, S)
    n_cols = w_in.shape[1]
    n_gate = n_cols - 3 * DA_WIDTH - 3 * NA_WIDTH
    tok = lambda width: pl.BlockSpec((None, tm, width), lambda b, i: (b, i, 0))
    tr = pl.BlockSpec((None, DA_WIDTH, tm), lambda b, i: (b, 0, i))
    const = lambda shape: pl.BlockSpec(shape, lambda b, i: (0,) * len(shape))
    out_shape = (
        jax.ShapeDtypeStruct((B, DA_WIDTH, S), BF16),
        jax.ShapeDtypeStruct((B, S, DA_WIDTH), BF16),
        jax.ShapeDtypeStruct((B, DA_WIDTH, S), BF16),
        jax.ShapeDtypeStruct((B, NA_WIDTH, S), BF16),
        jax.ShapeDtypeStruct((B, S, NA_WIDTH), BF16),
        jax.ShapeDtypeStruct((B, NA_WIDTH, S), BF16),
        jax.ShapeDtypeStruct((B, S, n_gate), BF16),
    )
    assert DA_WIDTH == NA_WIDTH
    return pl.pallas_call(
        _in_proj_kernel,
        grid=(B, S // tm),
        in_specs=[tok(D), const((D, n_cols)), const((1, n_cols)),
                  pl.BlockSpec((tm, LANES), lambda b, i: (i, 0)),
                  pl.BlockSpec((tm, LANES), lambda b, i: (i, 0))],
        out_specs=(tr, tok(DA_WIDTH), tr, tr, tok(NA_WIDTH), tr, tok(n_gate)),
        out_shape=out_shape,
        compiler_params=_cparams(("parallel", "parallel")),
        name="in_proj",
    )(x, w_in, b_in, cos, sin)


def _diff_attn_kernel(qT_ref, qTn_ref, k_ref, vT_ref, lq1_ref, lk1_ref, lq2_ref, lk2_ref, g_ref,
                      o_ref, m_sc, l_sc, acc_sc, s_sc, cmax_sc, *, tk, unroll, lam_init):
    tq = qT_ref.shape[1]

    def block_diag(q_ref):
        qT = q_ref[...]
        row = lax.broadcasted_iota(jnp.int32, qT.shape, 0)
        zero = jnp.zeros_like(qT)
        return jnp.concatenate([jnp.where(row < DA_HEAD_DIM, qT, zero),
                                jnp.where(row < DA_HEAD_DIM, zero, qT)], axis=1)

    qbd = block_diag(qT_ref)

    m_sc[...] = jnp.full(m_sc.shape, -jnp.inf, F32)
    l_sc[...] = jnp.zeros(l_sc.shape, F32)
    acc_sc[...] = jnp.zeros(acc_sc.shape, F32)

    nk = k_ref.shape[0] // tk

    def put_scores(slot, j, q=None):
        off = pl.multiple_of(j * tk, tk)
        s = jnp.dot(k_ref[pl.ds(off, tk), :], qbd if q is None else q, preferred_element_type=F32)
        s_sc[slot] = s
        cmax_sc[slot] = jnp.max(s, axis=0, keepdims=True)

    def accumulate(slot, j):
        off = pl.multiple_of(j * tk, tk)
        m_prev = m_sc[...]
        m_new = jnp.maximum(m_prev, cmax_sc[slot])
        alpha = jnp.exp2(m_prev - m_new)
        p = jnp.exp2(s_sc[slot] - m_new)
        l_sc[...] = alpha * l_sc[...] + jnp.sum(p, axis=0, keepdims=True)
        vb = vT_ref[:, pl.ds(off, tk)]
        acc_sc[...] = alpha * acc_sc[...] + jnp.dot(vb, p.astype(BF16), preferred_element_type=F32)
        m_sc[...] = m_new

    @pl.when(pl.program_id(2) == 0)
    def _():
        put_scores(0, 0)

    def body(jj, carry):
        for u in range(unroll):
            j = unroll * jj + u
            put_scores((u + 1) % 2, j + 1)
            accumulate(u % 2, j)
        return carry

    lax.fori_loop(0, nk // unroll - 1, body, 0)
    for j in range(nk - unroll, nk):
        if j + 1 < nk:
            put_scores((j + 1) % 2, j + 1)
        else:
            put_scores(0, 0, block_diag(qTn_ref))
        accumulate(j % 2, j)

    lam = (jnp.exp(jnp.sum(lq1_ref[...] * lk1_ref[...], axis=1, keepdims=True))
           - jnp.exp(jnp.sum(lq2_ref[...] * lk2_ref[...], axis=1, keepdims=True)) + lam_init)
    on = acc_sc[...] / l_sc[...]
    o = on[:, :tq] - lam * on[:, tq:]
    ms = jnp.mean(o * o, axis=0, keepdims=True)
    o = o * lax.rsqrt(ms + RMS_EPS) * g_ref[...]
    o = o * (1.0 - lam_init)
    o_ref[...] = o.T.astype(BF16)


def _diff_attn(qT, k, vT, lq1, lk1, lq2, lk2, subln_g, lam_init):
    B, S, _ = k.shape
    tq = min(DA_TQ, S)
    tk = min(DA_TK, S // 2)
    unroll = math.gcd(DA_UNROLL, S // tk)
    assert unroll % 2 == 0 and S % tk == 0 and S % tq == 0
    nq = S // tq
    vec = pl.BlockSpec((1, DA_HEAD_DIM), lambda b, h, i: (0, 0))
    kernel = functools.partial(_diff_attn_kernel, tk=tk, unroll=unroll, lam_init=lam_init)
    return pl.pallas_call(
        kernel,
        grid=(B, DA_HEADS, nq),
        in_specs=[pl.BlockSpec((None, DA_V_DIM, tq), lambda b, h, i: (b, h, i)),
                  pl.BlockSpec((None, DA_V_DIM, tq), lambda b, h, i: (b, h, jnp.minimum(i + 1, nq - 1))),
                  pl.BlockSpec((None, S, DA_V_DIM), lambda b, h, i: (b, 0, h)),
                  pl.BlockSpec((None, DA_V_DIM, S), lambda b, h, i: (b, h, 0)),
                  vec, vec, vec, vec,
                  pl.BlockSpec((DA_V_DIM, 1), lambda b, h, i: (0, 0))],
        out_specs=pl.BlockSpec((None, tq, DA_V_DIM), lambda b, h, i: (b, i, h)),
        out_shape=jax.ShapeDtypeStruct((B, S, DA_WIDTH), BF16),
        scratch_shapes=[pltpu.VMEM((1, 2 * tq), F32), pltpu.VMEM((1, 2 * tq), F32),
                        pltpu.VMEM((DA_V_DIM, 2 * tq), F32), pltpu.VMEM((2, tk, 2 * tq), F32),
                        pltpu.VMEM((2, 1, 2 * tq), F32)],
        compiler_params=_cparams(("parallel", "parallel", "arbitrary")),
        name="diff_attn",
    )(qT, qT, k, vT, lq1, lk1, lq2, lk2, subln_g)


def _na_band_start(r0, rows):
    return jnp.clip(r0 - NA_KH // 2, 0, rows - NA_BAND)


def _na_tiles(rpb):
    cols = jnp.arange(GRID_W)
    cs = jnp.clip(cols - NA_KW // 2, 0, GRID_W - NA_KW)
    col_ok = (cols[:, None] >= cs[None, :]) & (cols[:, None] < cs[None, :] + NA_KW)
    col_off = jnp.clip(cols[:, None] - cols[None, :] + (NA_KW - 1), 0, 2 * NA_KW - 2)
    tiles = jnp.where(col_ok[None, None], rpb.astype(F32)[:, :, col_off], NEG_BIG)
    neg = jnp.full((rpb.shape[0], 1, GRID_W, GRID_W), NEG_BIG, F32)
    return jnp.concatenate([jnp.concatenate([tiles, neg], axis=1),
                            jnp.concatenate([neg, tiles], axis=1)], axis=3)


def _na_geometry(variant, rows):
    nblk = rows // NA_ROWS
    r0 = (0, NA_ROWS * min(1, nblk - 1), NA_ROWS * (nblk - 1))[variant]
    band = min(max(r0 - NA_KH // 2, 0), rows - NA_BAND)
    pairs = []
    for qp in range(NA_ROWS // 2):
        entries = []
        for kr in range(NA_BAND):
            ok = []
            for qr in (r0 + 2 * qp, r0 + 2 * qp + 1):
                rs = min(max(qr - NA_KH // 2, 0), rows - NA_KH)
                ok.append(rs <= band + kr < rs + NA_KH)
            if ok[0] or ok[1]:
                entries.append((kr, band + kr - (r0 + 2 * qp) + NA_KH - 1, ok[0], ok[1]))
        pairs.append(entries)
    return pairs


def _na_kernel(qT_ref, k_ref, vT_ref, tile_ref, o_ref, *, rows):
    i = pl.program_id(2)
    nblk = rows // NA_ROWS
    nk = NA_BAND * GRID_W
    off = pl.multiple_of(_na_band_start(i * NA_ROWS, rows) * GRID_W, 2 * GRID_W)

    def block(variant):
        kb = k_ref[pl.ds(off, nk), :]
        vT = vT_ref[:, pl.ds(off, nk)]
        lane = lax.broadcasted_iota(jnp.int32, kb.shape, 1)
        kbd = jnp.concatenate([jnp.where(lane < NA_HEAD_DIM, kb, jnp.zeros_like(kb)),
                               jnp.where(lane < NA_HEAD_DIM, jnp.zeros_like(kb), kb)], axis=0)
        row = lax.broadcasted_iota(jnp.int32, vT.shape, 0)
        vbdT = jnp.concatenate([jnp.where(row < NA_HEAD_DIM, vT, jnp.zeros_like(vT)),
                                jnp.where(row < NA_HEAD_DIM, jnp.zeros_like(vT), vT)], axis=1)
        s = jnp.dot(kbd, qT_ref[...], preferred_element_type=F32)
        left = lax.broadcasted_iota(jnp.int32, (GRID_W, LANES), 1) < GRID_W
        zero_slab = jnp.zeros((GRID_W, LANES), BF16)
        geometry = _na_geometry(variant, rows)
        p_heads, l_heads = [], []
        for h in range(2):
            p_cols, l_cols = [], []
            for qp, entries in enumerate(geometry):
                logits = {}
                for kr, d, ok_l, ok_r in entries:
                    bias = tile_ref[h, d]
                    if not ok_r:
                        bias = jnp.where(left, bias, NEG_BIG)
                    if not ok_l:
                        bias = jnp.where(left, NEG_BIG, bias)
                    r = h * nk + kr * GRID_W
                    logits[kr] = s[r:r + GRID_W, qp * LANES:(qp + 1) * LANES] + bias
                m = functools.reduce(jnp.maximum, logits.values())
                m = jnp.max(m, axis=0, keepdims=True)
                probs = {kr: jnp.exp(x - m) for kr, x in logits.items()}
                l_cols.append(jnp.sum(functools.reduce(jnp.add, probs.values()), axis=0, keepdims=True))
                p_cols.append(jnp.concatenate(
                    [probs[kr].astype(BF16) if kr in probs else zero_slab for kr in range(NA_BAND)], axis=0))
            p_heads.append(jnp.concatenate(p_cols, axis=1))
            l_heads.append(jnp.concatenate(l_cols, axis=1))
        oT = jnp.dot(vbdT, jnp.concatenate(p_heads, axis=0), preferred_element_type=F32)
        row_o = lax.broadcasted_iota(jnp.int32, oT.shape, 0)
        oT = oT / jnp.where(row_o < NA_HEAD_DIM, l_heads[0], l_heads[1])
        o_ref[...] = oT.T.astype(BF16)

    pl.when(i == 0)(lambda: block(0))
    pl.when(jnp.logical_and(i > 0, i < nblk - 1))(lambda: block(1))
    pl.when(jnp.logical_and(i > 0, i == nblk - 1))(lambda: block(2))


def _na_attn(qnT, kn, vnT, tiles):
    B, S, _ = kn.shape
    rows = S // GRID_W
    nblk = rows // NA_ROWS
    nq = NA_ROWS * GRID_W
    return pl.pallas_call(
        functools.partial(_na_kernel, rows=rows),
        grid=(B, NA_HEADS // 2, nblk),
        in_specs=[pl.BlockSpec((None, LANES, nq), lambda b, h, i: (b, h, i)),
                  pl.BlockSpec((None, S, LANES), lambda b, h, i: (b, 0, h)),
                  pl.BlockSpec((None, LANES, S), lambda b, h, i: (b, h, 0)),
                  pl.BlockSpec((2,) + tiles.shape[1:], lambda b, h, i: (h, 0, 0, 0))],
        out_specs=pl.BlockSpec((None, nq, LANES), lambda b, h, i: (b, i, h)),
        out_shape=jax.ShapeDtypeStruct((B, S, NA_WIDTH), BF16),
        compiler_params=_cparams(("parallel", "parallel", "arbitrary")),
        name="na_attn",
    )(qnT, kn, vnT, tiles)


def _layer_norm(h, g, b):
    mu = jnp.mean(h, axis=-1, keepdims=True)
    d = h - mu
    var = jnp.mean(d * d, axis=-1, keepdims=True)
    return d * lax.rsqrt(var + LN_EPS) * g + b


def _merge_kernel(x_ref, a_ref, nb_ref, g_ref, wda_ref, wna_ref, wout_ref, lng_ref, lnb_ref,
                  wr_ref, br_ref, x1_ref, x1b_ref, topi_ref, gate_ref, *, alpha):
    d = x_ref.shape[1]
    ya = jnp.dot(a_ref[...], wda_ref[...], preferred_element_type=F32)
    yb = jnp.dot(nb_ref[...], wna_ref[...], preferred_element_type=F32)
    g = g_ref[...].astype(F32)
    merged = g[:, :d] * ya + g[:, d:] * yb
    mix = jnp.dot(merged.astype(BF16), wout_ref[...], preferred_element_type=F32)
    x1 = _layer_norm(alpha * x_ref[...] + mix, lng_ref[...], lnb_ref[...])
    x1_ref[...] = x1
    x1b_ref[...] = x1.astype(BF16)
    tm = x1.shape[0]
    xh = x1.astype(BF16)
    xl = (x1 - xh.astype(F32)).astype(BF16)
    r = jnp.dot(jnp.concatenate([xh, xl], axis=0), wr_ref[...], preferred_element_type=F32)
    logits = (r[:tm, :LANES] + r[:tm, LANES:]) + (r[tm:, :LANES] + r[tm:, LANES:]) + br_ref[...]
    lane = lax.broadcasted_iota(jnp.int32, logits.shape, 1).astype(F32)
    vals, idxs = [], []
    for _ in range(TOP_K):
        m = jnp.max(logits, axis=1, keepdims=True)
        idx = jnp.min(jnp.where(logits == m, lane, float(LANES)), axis=1, keepdims=True)
        vals.append(m)
        idxs.append(idx)
        logits = jnp.where(lane == idx, -jnp.inf, logits)
    es = [jnp.exp(v - vals[0]) for v in vals]
    denom = es[0] + es[1] + es[2] + es[3]
    gates = jnp.zeros(logits.shape, F32)
    topi = jnp.zeros(logits.shape, F32)
    for k in range(TOP_K):
        gates = jnp.where(lane == k, es[k] / denom, gates)
        topi = jnp.where(lane == k, idxs[k], topi)
    gate_ref[...] = gates
    topi_ref[...] = topi.astype(jnp.int32)


def _merge(x2, a2, nb2, g2, wda, wna, wout, ln_g, ln_b, wr, br, alpha):
    T, D = x2.shape
    tm = min(PROJ_TM, T)
    tok = lambda width: pl.BlockSpec((tm, width), lambda i: (i, 0))
    const = lambda shape: pl.BlockSpec(shape, lambda i: (0,) * len(shape))
    return pl.pallas_call(
        functools.partial(_merge_kernel, alpha=alpha),
        grid=(T // tm,),
        in_specs=[tok(D), tok(a2.shape[1]), tok(nb2.shape[1]), tok(g2.shape[1]),
                  const(wda.shape), const(wna.shape), const(wout.shape),
                  const((1, D)), const((1, D)), const(wr.shape), const((1, LANES))],
        out_specs=(tok(D), tok(D), tok(LANES), tok(LANES)),
        out_shape=(jax.ShapeDtypeStruct((T, D), F32), jax.ShapeDtypeStruct((T, D), BF16),
                   jax.ShapeDtypeStruct((T, LANES), jnp.int32), jax.ShapeDtypeStruct((T, LANES), F32)),
        compiler_params=_cparams(("parallel",)),
        name="merge",
    )(x2, a2, nb2, g2, wda, wna, wout, ln_g, ln_b, wr, br)


def _glu_group_bias(b1):
    e, n = b1.shape
    return b1.reshape(e, n // (2 * LANES), LANES, 2).transpose(0, 1, 3, 2).reshape(e, 1, n)


def _moe_kernel(be_ref, nused_ref, xs_ref, w1_ref, b1_ref, w2_ref, b2_ref, *rest, blk0):
    ys_ref, w1g_sc, w2b_sc = rest[-3:]
    i = pl.program_id(0)
    blk = i + blk0
    used = blk < nused_ref[0]
    fresh = jnp.logical_or(i == 0, be_ref[blk] != be_ref[jnp.maximum(blk - 1, 0)])
    grp = 2 * LANES

    @pl.when(jnp.logical_and(used, fresh))
    def _():
        r = lax.broadcasted_iota(jnp.int32, (grp, grp), 0)
        c = lax.broadcasted_iota(jnp.int32, (grp, grp), 1)
        src = jnp.where(c < LANES, 2 * c, 2 * (c - LANES) + 1)
        perm = jnp.where(r == src, 1.0, 0.0).astype(BF16)
        for g in range(w1_ref.shape[1] // grp):
            blk = w1_ref[:, g * grp:(g + 1) * grp].astype(BF16)
            w1g_sc[:, g * grp:(g + 1) * grp] = jnp.dot(blk, perm, preferred_element_type=F32).astype(BF16)
        w2b_sc[...] = w2_ref[...].astype(BF16)

    @pl.when(used)
    def _():
        xb = xs_ref[...]
        h = jnp.dot(xb, w1g_sc[...], preferred_element_type=F32) + b1_ref[...]
        acts = []
        for g in range(h.shape[1] // grp):
            x_glu = jnp.minimum(h[:, g * grp:g * grp + LANES], SWIGLU_LIMIT)
            x_lin = jnp.clip(h[:, g * grp + LANES:(g + 1) * grp], -SWIGLU_LIMIT, SWIGLU_LIMIT)
            acts.append(x_glu * (1.0 / (1.0 + jnp.exp(-SWIGLU_ALPHA * x_glu))) * (x_lin + 1.0))
        act = jnp.concatenate(acts, axis=1).astype(BF16)
        y = jnp.dot(act, w2b_sc[...], preferred_element_type=F32) + b2_ref[...]
        ys_ref[...] = y.astype(ys_ref.dtype)

    @pl.when(jnp.logical_not(used))
    def _():
        ys_ref[...] = jnp.zeros(ys_ref.shape, ys_ref.dtype)


def _moe_ffn(xs, block_expert, n_used, w1, b1g, w2, b2, blk0=0, ys_prev=None):
    D = xs.shape[1]
    P = block_expert.shape[0] * MOE_BLOCK
    de2 = w1.shape[2]
    de = w2.shape[1]
    wspec = lambda r, c: pl.BlockSpec((None, r, c), lambda i, be, nu: (be[i + blk0], 0, 0))
    in_specs = [pl.BlockSpec((MOE_BLOCK, D), lambda i, be, nu: (i, 0)),
                wspec(D, de2), wspec(1, de2), wspec(de, D), wspec(1, D)]
    args = [block_expert, n_used, xs, w1, b1g, w2, b2]
    aliases = {}
    if ys_prev is not None:
        in_specs.append(pl.BlockSpec(memory_space=pl.ANY))
        args.append(ys_prev)
        aliases = {len(args) - 1: 0}
    grid_spec = pltpu.PrefetchScalarGridSpec(
        num_scalar_prefetch=2,
        grid=(xs.shape[0] // MOE_BLOCK,),
        in_specs=in_specs,
        out_specs=pl.BlockSpec((MOE_BLOCK, D), lambda i, be, nu: (i + blk0, 0)),
        scratch_shapes=[pltpu.VMEM((D, de2), BF16), pltpu.VMEM((de, D), BF16)],
    )
    return pl.pallas_call(
        functools.partial(_moe_kernel, blk0=blk0),
        grid_spec=grid_spec,
        out_shape=jax.ShapeDtypeStruct((P, D), BF16),
        input_output_aliases=aliases,
        compiler_params=_cparams(("arbitrary",)),
        name="moe_ffn",
    )(*args)


def _combine_kernel(x1_ref, y0_ref, y1_ref, y2_ref, y3_ref, gate_ref, lng_ref, lnb_ref, o_ref, *, alpha):
    gates = gate_ref[...]
    ffn = jnp.zeros(x1_ref.shape, F32)
    for k, y_ref in enumerate((y0_ref, y1_ref, y2_ref, y3_ref)):
        ffn = ffn + y_ref[...].astype(F32) * gates[:, k:k + 1]
    o_ref[...] = _layer_norm(alpha * x1_ref[...] + ffn, lng_ref[...], lnb_ref[...])


def _combine(x1, y_slots, gates, ln_g, ln_b, alpha):
    T, D = x1.shape
    tm = min(PROJ_TM, T)
    tok = lambda width: pl.BlockSpec((tm, width), lambda i: (i, 0))
    const = lambda shape: pl.BlockSpec(shape, lambda i: (0,) * len(shape))
    slot = lambda k: pl.BlockSpec((None, tm, D), lambda i: (k, i, 0))
    return pl.pallas_call(
        functools.partial(_combine_kernel, alpha=alpha),
        grid=(T // tm,),
        in_specs=[tok(D)] + [slot(k) for k in range(TOP_K)] + [tok(LANES), const((1, D)), const((1, D))],
        out_specs=tok(D),
        out_shape=jax.ShapeDtypeStruct((T, D), F32),
        compiler_params=_cparams(("parallel",)),
        name="combine",
    )(x1, y_slots, y_slots, y_slots, y_slots, gates, ln_g, ln_b)


def _route(top_i, n_experts):
    T = top_i.shape[0]
    TK = T * TOP_K
    experts = jnp.arange(n_experts, dtype=jnp.int32)
    chosen = top_i[:, :, None] == experts
    picks = jnp.sum(chosen, axis=1, dtype=jnp.int32)
    csum = jnp.cumsum(picks, axis=0)
    counts = csum[-1]
    padded = ((counts + MOE_BLOCK - 1) // MOE_BLOCK) * MOE_BLOCK
    padded_end = jnp.cumsum(padded)
    padded_start = padded_end - padded
    group_start = jnp.cumsum(counts) - counts
    slot_te = padded_start[None, :] + csum - picks
    dest = jnp.sum(jnp.where(chosen, slot_te[:, None, :], 0), axis=2)
    n_blocks = -(-TK // MOE_BLOCK) + n_experts
    blk_first = jnp.arange(n_blocks, dtype=jnp.int32) * MOE_BLOCK
    block_expert = jnp.minimum(jnp.sum(padded_end[None, :] <= blk_first[:, None], axis=1),
                               n_experts - 1).astype(jnp.int32)
    n_used = (padded_end[-1] // MOE_BLOCK).astype(jnp.int32).reshape(1)
    order = jnp.argsort(top_i.reshape(-1), stable=True).astype(jnp.int32)
    rank0 = blk_first - padded_start[block_expert]
    rank = rank0[:, None] + jnp.arange(MOE_BLOCK, dtype=jnp.int32)[None, :]
    valid = rank < counts[block_expert][:, None]
    sorted_pos = jnp.clip(group_start[block_expert][:, None] + rank, 0, TK - 1)
    filler = (blk_first[:, None] + jnp.arange(MOE_BLOCK, dtype=jnp.int32)[None, :]) % T
    src_tok = jnp.where(valid, order[sorted_pos] // TOP_K, filler).reshape(-1)
    return src_tok, block_expert, n_used, dest.T.reshape(-1)


def _rope_tables(seq):
    pos = jnp.arange(seq, dtype=F32)
    inv = ROPE_THETA ** (-jnp.arange(0, DA_HEAD_DIM, 2, dtype=F32) / DA_HEAD_DIM)
    ang = pos[:, None] * inv[None, :]
    ang = jnp.concatenate([ang, ang], axis=-1)
    cos, sin = jnp.cos(ang), jnp.sin(ang)
    half = DA_HEAD_DIM // 2
    sin_signed = jnp.concatenate([-sin[:, :half], sin[:, half:]], axis=-1)
    return jnp.tile(cos, (1, 2)), jnp.tile(sin_signed, (1, 2))


def kernel(x, w_in, b_in, lambda_q1, lambda_k1, lambda_q2, lambda_k2, subln_g, rpb, w_branch_da, w_branch_na, w_out, ln1_g, ln1_b, w_router, b_router, w_mlp1, b_mlp1, w_mlp2, b_mlp2, ln2_g, ln2_b):
    B, S, D = x.shape
    depth = w_in.shape[0]
    n_experts = w_router.shape[2]
    T = B * S
    rows = S // GRID_W
    assert S % GRID_W == 0 and rows % NA_ROWS == 0 and rows >= NA_BAND
    alpha = (2 * depth) ** 0.25
    cos, sin = _rope_tables(S)
    row = lambda v: v.reshape(1, -1)

    for l in range(depth):
        lam_init = 0.8 - 0.6 * math.exp(-0.3 * l)
        qT, k, vT, qn, kn, vn, gates_br = _in_proj(x, w_in[l].astype(BF16), row(b_in[l]), cos, sin)
        a = _diff_attn(qT, k, vT, row(lambda_q1[l]), row(lambda_k1[l]), row(lambda_q2[l]),
                       row(lambda_k2[l]), subln_g[l].reshape(-1, 1), lam_init)
        nb = _na_attn(qn, kn, vn, _na_tiles(rpb[l]))

        wr = jnp.pad(w_router[l], ((0, 0), (0, LANES - n_experts)))
        wr_h = wr.astype(BF16)
        wr = jnp.concatenate([wr_h, (wr - wr_h.astype(F32)).astype(BF16)], axis=1)
        br = jnp.pad(row(b_router[l]), ((0, 0), (0, LANES - n_experts)), constant_values=NEG_BIG)
        x1, x1b, topi, gates = _merge(
            x.reshape(T, D), a.reshape(T, -1), nb.reshape(T, -1), gates_br.reshape(T, -1),
            w_branch_da[l].astype(BF16), w_branch_na[l].astype(BF16), w_out[l].astype(BF16),
            row(ln1_g[l]), row(ln1_b[l]), wr, br, alpha)

        src_tok, block_expert, n_used, dest = _route(topi[:, :TOP_K], n_experts)
        ys, blk0 = None, 0
        for nblk in (block_expert.shape[0] // 2, block_expert.shape[0] - block_expert.shape[0] // 2):
            rows_ = src_tok[blk0 * MOE_BLOCK:(blk0 + nblk) * MOE_BLOCK]
            ys = _moe_ffn(jnp.take(x1b, rows_, axis=0, mode="clip"), block_expert, n_used, w_mlp1[l],
                          _glu_group_bias(b_mlp1[l]), w_mlp2[l], b_mlp2[l][:, None, :], blk0, ys)
            blk0 += nblk
        y_slots = jnp.take(ys, dest, axis=0, mode="clip").reshape(TOP_K, T, D)
        x = _combine(x1, y_slots, gates, row(ln2_g[l]), row(ln2_b[l]), alpha).reshape(B, S, D)
    return x
```

```python
import functools
import math

import jax
import jax.numpy as jnp
from jax import lax
from jax.experimental import pallas as pl
from jax.experimental.pallas import tpu as pltpu
from jax.experimental.pallas import tpu_sc as plsc

F32 = jnp.float32
BF16 = jnp.bfloat16

GRID_W = 64
DA_HEADS = 4
DA_HEAD_DIM = 64
DA_V_DIM = 2 * DA_HEAD_DIM
DA_WIDTH = DA_HEADS * DA_V_DIM
ROPE_THETA = 10000.0
NA_HEADS = 8
NA_HEAD_DIM = 64
NA_WIDTH = NA_HEADS * NA_HEAD_DIM
NA_KH = 8
NA_KW = 16
TOP_K = 4
SWIGLU_ALPHA = 1.702
SWIGLU_LIMIT = 7.0
MOE_BLOCK = 512
LN_EPS = 1e-5
RMS_EPS = 1e-5

LANES = 128
VMEM_LIMIT_BYTES = 56 * 1024 * 1024
SC_CORES = 2
SC_SUBCORES = 16
SC_GATHER_ROWS = 64

LOG2E = math.log2(math.e)
NEG_BIG = -1e30

PROJ_TM = 512
DA_TQ = 512
DA_TK = 512
DA_UNROLL = 8
NA_ROWS = 8
NA_BAND = 16


def _cparams(sem):
    return pltpu.CompilerParams(dimension_semantics=sem, vmem_limit_bytes=VMEM_LIMIT_BYTES)


def _pack_bf16_pairs(y):
    w = y.shape[1] // 2
    bits = lax.bitcast_convert_type(y, jnp.uint32)
    rounded = bits + jnp.uint32(0x7FFF) + ((bits >> 16) & jnp.uint32(1))
    return (rounded[:, :w] >> 16) | (rounded[:, w:] & jnp.uint32(0xFFFF0000))


def _unpack_bf16_pairs(words):
    lo = lax.bitcast_convert_type(words << 16, F32)
    hi = lax.bitcast_convert_type(words & jnp.uint32(0xFFFF0000), F32)
    return jnp.concatenate([lo, hi], axis=1)


def _sc_gather_rows(table, idx):
    n, w = idx.shape[0], table.shape[1]
    rows = SC_GATHER_ROWS
    workers = SC_CORES * SC_SUBCORES
    per_worker = n // workers
    n_pairs = per_worker // (2 * rows)
    assert n % (workers * 2 * rows) == 0 and n_pairs >= 1
    mesh = plsc.VectorSubcoreMesh(core_axis_name="c", subcore_axis_name="s",
                                  num_cores=SC_CORES, num_subcores=SC_SUBCORES)

    def body(table_hbm, idx_hbm, out_hbm, idx0, idx1, rows0, rows1, sem0, sem1):
        base = (lax.axis_index("s") * SC_CORES + lax.axis_index("c")) * per_worker
        bufs = ((idx0, rows0, sem0), (idx1, rows1, sem1))

        def gather(b):
            idx_v, rows_v, sem = bufs[b]
            return pltpu.make_async_copy(table_hbm.at[idx_v], rows_v, sem)

        def start(c, b):
            pltpu.sync_copy(idx_hbm.at[pl.ds(base + c * rows, rows)], bufs[b][0])
            gather(b).start()

        def finish(c, b):
            gather(b).wait()
            pltpu.sync_copy(bufs[b][1], out_hbm.at[pl.ds(base + c * rows, rows)])

        start(0, 0)

        @pl.loop(0, n_pairs - 1)
        def _(g):
            c = 2 * g
            start(c + 1, 1)
            finish(c, 0)
            start(c + 2, 0)
            finish(c + 1, 1)

        last = 2 * (n_pairs - 1)
        start(last + 1, 1)
        finish(last, 0)
        finish(last + 1, 1)

    return pl.kernel(
        body, out_type=jax.ShapeDtypeStruct((n, w), table.dtype), mesh=mesh,
        scratch_types=[pltpu.VMEM((rows,), jnp.int32), pltpu.VMEM((rows,), jnp.int32),
                       pltpu.VMEM((rows, w), table.dtype), pltpu.VMEM((rows, w), table.dtype),
                       pltpu.SemaphoreType.DMA, pltpu.SemaphoreType.DMA],
        name="sc_gather_rows",
    )(table, idx)


def _in_proj_kernel(x_ref, w_ref, b_ref, cos_ref, sin_ref,
                    qT_ref, k_ref, vT_ref, qn_ref, kn_ref, vn_ref, g_ref):
    xb = x_ref[...].astype(BF16)

    def seg(lo, hi):
        return jnp.dot(xb, w_ref[:, lo:hi], preferred_element_type=F32) + b_ref[:, lo:hi]

    cos = cos_ref[...]
    sin = sin_ref[...]
    lane = lax.broadcasted_iota(jnp.int32, cos.shape, 1)
    first_half = (lane % DA_HEAD_DIM) < (DA_HEAD_DIM // 2)

    def rope(y):
        outs = []
        for h in range(DA_HEADS):
            yh = y[:, h * LANES:(h + 1) * LANES]
            partner = jnp.where(first_half,
                                pltpu.roll(yh, LANES - DA_HEAD_DIM // 2, 1),
                                pltpu.roll(yh, DA_HEAD_DIM // 2, 1))
            outs.append(yh * cos + partner * sin)
        return jnp.concatenate(outs, axis=1)

    w = DA_WIDTH
    q = rope(seg(0, w)) * (DA_HEAD_DIM ** -0.5 * LOG2E)
    qT_ref[...] = q.T.astype(BF16)
    k_ref[...] = rope(seg(w, 2 * w)).astype(BF16)
    vT_ref[...] = seg(2 * w, 3 * w).T.astype(BF16)
    o = 3 * w
    qn_ref[...] = (seg(o, o + NA_WIDTH) * (NA_HEAD_DIM ** -0.5)).T.astype(BF16)
    kn_ref[...] = seg(o + NA_WIDTH, o + 2 * NA_WIDTH).astype(BF16)
    vn_ref[...] = seg(o + 2 * NA_WIDTH, o + 3 * NA_WIDTH).T.astype(BF16)
    g0 = o + 3 * NA_WIDTH
    gate_pre = seg(g0, w_ref.shape[1])
    g_ref[...] = (1.0 / (1.0 + jnp.exp(-gate_pre))).astype(BF16)


def _in_proj(x, w_in, b_in, cos, sin):
    B, S, D = x.shape
    tm = min(PROJ_TM, S)
    n_cols = w_in.shape[1]
    n_gate = n_cols - 3 * DA_WIDTH - 3 * NA_WIDTH
    tok = lambda width: pl.BlockSpec((None, tm, width), lambda b, i: (b, i, 0))
    tr = pl.BlockSpec((None, DA_WIDTH, tm), lambda b, i: (b, 0, i))
    const = lambda shape: pl.BlockSpec(shape, lambda b, i: (0,) * len(shape))
    out_shape = (
        jax.ShapeDtypeStruct((B, DA_WIDTH, S), BF16),
        jax.ShapeDtypeStruct((B, S, DA_WIDTH), BF16),
        jax.ShapeDtypeStruct((B, DA_WIDTH, S), BF16),
        jax.ShapeDtypeStruct((B, NA_WIDTH, S), BF16),
        jax.ShapeDtypeStruct((B, S, NA_WIDTH), BF16),
        jax.ShapeDtypeStruct((B, NA_WIDTH, S), BF16),
        jax.ShapeDtypeStruct((B, S, n_gate), BF16),
    )
    assert DA_WIDTH == NA_WIDTH
    return pl.pallas_call(
        _in_proj_kernel,
        grid=(B, S // tm),
        in_specs=[tok(D), const((D, n_cols)), const((1, n_cols)),
                  pl.BlockSpec((tm, LANES), lambda b, i: (i, 0)),
                  pl.BlockSpec((tm, LANES), lambda b, i: (i, 0))],
        out_specs=(tr, tok(DA_WIDTH), tr, tr, tok(NA_WIDTH), tr, tok(n_gate)),
        out_shape=out_shape,
        compiler_params=_cparams(("parallel", "parallel")),
        name="in_proj",
    )(x, w_in, b_in, cos, sin)


def _diff_attn_kernel(qT_ref, qTn_ref, k_ref, vT_ref, lq1_ref, lk1_ref, lq2_ref, lk2_ref, g_ref,
                      o_ref, m_sc, l_sc, acc_sc, s_sc, cmax_sc, *, tk, unroll, lam_init):
    tq = qT_ref.shape[1]

    def block_diag(q_ref):
        qT = q_ref[...]
        row = lax.broadcasted_iota(jnp.int32, qT.shape, 0)
        zero = jnp.zeros_like(qT)
        return jnp.concatenate([jnp.where(row < DA_HEAD_DIM, qT, zero),
                                jnp.where(row < DA_HEAD_DIM, zero, qT)], axis=1)

    qbd = block_diag(qT_ref)

    m_sc[...] = jnp.full(m_sc.shape, -jnp.inf, F32)
    l_sc[...] = jnp.zeros(l_sc.shape, F32)
    acc_sc[...] = jnp.zeros(acc_sc.shape, F32)

    nk = k_ref.shape[0] // tk

    def put_scores(slot, j, q=None):
        off = pl.multiple_of(j * tk, tk)
        s = jnp.dot(k_ref[pl.ds(off, tk), :], qbd if q is None else q, preferred_element_type=F32)
        s_sc[slot] = s
        cmax_sc[slot] = jnp.max(s, axis=0, keepdims=True)

    def accumulate(slot, j):
        off = pl.multiple_of(j * tk, tk)
        m_prev = m_sc[...]
        m_new = jnp.maximum(m_prev, cmax_sc[slot])
        alpha = jnp.exp2(m_prev - m_new)
        p = jnp.exp2(s_sc[slot] - m_new)
        l_sc[...] = alpha * l_sc[...] + jnp.sum(p, axis=0, keepdims=True)
        vb = vT_ref[:, pl.ds(off, tk)]
        acc_sc[...] = alpha * acc_sc[...] + jnp.dot(vb, p.astype(BF16), preferred_element_type=F32)
        m_sc[...] = m_new

    @pl.when(pl.program_id(2) == 0)
    def _():
        put_scores(0, 0)

    def body(jj, carry):
        for u in range(unroll):
            j = unroll * jj + u
            put_scores((u + 1) % 2, j + 1)
            accumulate(u % 2, j)
        return carry

    lax.fori_loop(0, nk // unroll - 1, body, 0)
    for j in range(nk - unroll, nk):
        if j + 1 < nk:
            put_scores((j + 1) % 2, j + 1)
        else:
            put_scores(0, 0, block_diag(qTn_ref))
        accumulate(j % 2, j)

    lam = (jnp.exp(jnp.sum(lq1_ref[...] * lk1_ref[...], axis=1, keepdims=True))
           - jnp.exp(jnp.sum(lq2_ref[...] * lk2_ref[...], axis=1, keepdims=True)) + lam_init)
    on = acc_sc[...] / l_sc[...]
    o = on[:, :tq] - lam * on[:, tq:]
    ms = jnp.mean(o * o, axis=0, keepdims=True)
    o = o * lax.rsqrt(ms + RMS_EPS) * g_ref[...]
    o = o * (1.0 - lam_init)
    o_ref[...] = o.T.astype(BF16)


def _diff_attn(qT, k, vT, lq1, lk1, lq2, lk2, subln_g, lam_init):
    B, S, _ = k.shape
    tq = min(DA_TQ, S)
    tk = min(DA_TK, S // 2)
    unroll = math.gcd(DA_UNROLL, S // tk)
    assert unroll % 2 == 0 and S % tk == 0 and S % tq == 0
    nq = S // tq
    vec = pl.BlockSpec((1, DA_HEAD_DIM), lambda b, h, i: (0, 0))
    kernel = functools.partial(_diff_attn_kernel, tk=tk, unroll=unroll, lam_init=lam_init)
    return pl.pallas_call(
        kernel,
        grid=(B, DA_HEADS, nq),
        in_specs=[pl.BlockSpec((None, DA_V_DIM, tq), lambda b, h, i: (b, h, i)),
                  pl.BlockSpec((None, DA_V_DIM, tq), lambda b, h, i: (b, h, jnp.minimum(i + 1, nq - 1))),
                  pl.BlockSpec((None, S, DA_V_DIM), lambda b, h, i: (b, 0, h)),
                  pl.BlockSpec((None, DA_V_DIM, S), lambda b, h, i: (b, h, 0)),
                  vec, vec, vec, vec,
                  pl.BlockSpec((DA_V_DIM, 1), lambda b, h, i: (0, 0))],
        out_specs=pl.BlockSpec((None, tq, DA_V_DIM), lambda b, h, i: (b, i, h)),
        out_shape=jax.ShapeDtypeStruct((B, S, DA_WIDTH), BF16),
        scratch_shapes=[pltpu.VMEM((1, 2 * tq), F32), pltpu.VMEM((1, 2 * tq), F32),
                        pltpu.VMEM((DA_V_DIM, 2 * tq), F32), pltpu.VMEM((2, tk, 2 * tq), F32),
                        pltpu.VMEM((2, 1, 2 * tq), F32)],
        compiler_params=_cparams(("parallel", "parallel", "arbitrary")),
        name="diff_attn",
    )(qT, qT, k, vT, lq1, lk1, lq2, lk2, subln_g)


def _na_band_start(r0, rows):
    return jnp.clip(r0 - NA_KH // 2, 0, rows - NA_BAND)


def _na_tiles(rpb):
    cols = jnp.arange(GRID_W)
    cs = jnp.clip(cols - NA_KW // 2, 0, GRID_W - NA_KW)
    col_ok = (cols[:, None] >= cs[None, :]) & (cols[:, None] < cs[None, :] + NA_KW)
    col_off = jnp.clip(cols[:, None] - cols[None, :] + (NA_KW - 1), 0, 2 * NA_KW - 2)
    tiles = jnp.where(col_ok[None, None], rpb.astype(F32)[:, :, col_off], NEG_BIG)
    neg = jnp.full((rpb.shape[0], 1, GRID_W, GRID_W), NEG_BIG, F32)
    return jnp.concatenate([jnp.concatenate([tiles, neg], axis=1),
                            jnp.concatenate([neg, tiles], axis=1)], axis=3)


def _na_geometry(variant, rows):
    nblk = rows // NA_ROWS
    r0 = (0, NA_ROWS * min(1, nblk - 1), NA_ROWS * (nblk - 1))[variant]
    band = min(max(r0 - NA_KH // 2, 0), rows - NA_BAND)
    pairs = []
    for qp in range(NA_ROWS // 2):
        entries = []
        for kr in range(NA_BAND):
            ok = []
            for qr in (r0 + 2 * qp, r0 + 2 * qp + 1):
                rs = min(max(qr - NA_KH // 2, 0), rows - NA_KH)
                ok.append(rs <= band + kr < rs + NA_KH)
            if ok[0] or ok[1]:
                entries.append((kr, band + kr - (r0 + 2 * qp) + NA_KH - 1, ok[0], ok[1]))
        pairs.append(entries)
    return pairs


def _na_kernel(qT_ref, k_ref, vT_ref, tile_ref, o_ref, *, rows):
    i = pl.program_id(2)
    nblk = rows // NA_ROWS
    nk = NA_BAND * GRID_W
    off = pl.multiple_of(_na_band_start(i * NA_ROWS, rows) * GRID_W, 2 * GRID_W)

    def block(variant):
        kb = k_ref[pl.ds(off, nk), :]
        vT = vT_ref[:, pl.ds(off, nk)]
        lane = lax.broadcasted_iota(jnp.int32, kb.shape, 1)
        kbd = jnp.concatenate([jnp.where(lane < NA_HEAD_DIM, kb, jnp.zeros_like(kb)),
                               jnp.where(lane < NA_HEAD_DIM, jnp.zeros_like(kb), kb)], axis=0)
        row = lax.broadcasted_iota(jnp.int32, vT.shape, 0)
        vbdT = jnp.concatenate([jnp.where(row < NA_HEAD_DIM, vT, jnp.zeros_like(vT)),
                                jnp.where(row < NA_HEAD_DIM, jnp.zeros_like(vT), vT)], axis=1)
        s = jnp.dot(kbd, qT_ref[...], preferred_element_type=F32)
        left = lax.broadcasted_iota(jnp.int32, (GRID_W, LANES), 1) < GRID_W
        zero_slab = jnp.zeros((GRID_W, LANES), BF16)
        geometry = _na_geometry(variant, rows)
        p_heads, l_heads = [], []
        for h in range(2):
            p_cols, l_cols = [], []
            for qp, entries in enumerate(geometry):
                logits = {}
                for kr, d, ok_l, ok_r in entries:
                    bias = tile_ref[h, d]
                    if not ok_r:
                        bias = jnp.where(left, bias, NEG_BIG)
                    if not ok_l:
                        bias = jnp.where(left, NEG_BIG, bias)
                    r = h * nk + kr * GRID_W
                    logits[kr] = s[r:r + GRID_W, qp * LANES:(qp + 1) * LANES] + bias
                m = functools.reduce(jnp.maximum, logits.values())
                m = jnp.max(m, axis=0, keepdims=True)
                probs = {kr: jnp.exp(x - m) for kr, x in logits.items()}
                l_cols.append(jnp.sum(functools.reduce(jnp.add, probs.values()), axis=0, keepdims=True))
                p_cols.append(jnp.concatenate(
                    [probs[kr].astype(BF16) if kr in probs else zero_slab for kr in range(NA_BAND)], axis=0))
            p_heads.append(jnp.concatenate(p_cols, axis=1))
            l_heads.append(jnp.concatenate(l_cols, axis=1))
        oT = jnp.dot(vbdT, jnp.concatenate(p_heads, axis=0), preferred_element_type=F32)
        row_o = lax.broadcasted_iota(jnp.int32, oT.shape, 0)
        oT = oT / jnp.where(row_o < NA_HEAD_DIM, l_heads[0], l_heads[1])
        o_ref[...] = oT.T.astype(BF16)

    pl.when(i == 0)(lambda: block(0))
    pl.when(jnp.logical_and(i > 0, i < nblk - 1))(lambda: block(1))
    pl.when(jnp.logical_and(i > 0, i == nblk - 1))(lambda: block(2))


def _na_attn(qnT, kn, vnT, tiles):
    B, S, _ = kn.shape
    rows = S // GRID_W
    nblk = rows // NA_ROWS
    nq = NA_ROWS * GRID_W
    return pl.pallas_call(
        functools.partial(_na_kernel, rows=rows),
        grid=(B, NA_HEADS // 2, nblk),
        in_specs=[pl.BlockSpec((None, LANES, nq), lambda b, h, i: (b, h, i)),
                  pl.BlockSpec((None, S, LANES), lambda b, h, i: (b, 0, h)),
                  pl.BlockSpec((None, LANES, S), lambda b, h, i: (b, h, 0)),
                  pl.BlockSpec((2,) + tiles.shape[1:], lambda b, h, i: (h, 0, 0, 0))],
        out_specs=pl.BlockSpec((None, nq, LANES), lambda b, h, i: (b, i, h)),
        out_shape=jax.ShapeDtypeStruct((B, S, NA_WIDTH), BF16),
        compiler_params=_cparams(("parallel", "parallel", "arbitrary")),
        name="na_attn",
    )(qnT, kn, vnT, tiles)


def _layer_norm(h, g, b):
    mu = jnp.mean(h, axis=-1, keepdims=True)
    d = h - mu
    var = jnp.mean(d * d, axis=-1, keepdims=True)
    return d * lax.rsqrt(var + LN_EPS) * g + b


def _merge_kernel(x_ref, a_ref, nb_ref, g_ref, wda_ref, wna_ref, wout_ref, lng_ref, lnb_ref,
                  wr_ref, br_ref, x1_ref, x1b_ref, topi_ref, gate_ref, *, alpha):
    d = x_ref.shape[1]
    ya = jnp.dot(a_ref[...], wda_ref[...], preferred_element_type=F32)
    yb = jnp.dot(nb_ref[...], wna_ref[...], preferred_element_type=F32)
    g = g_ref[...].astype(F32)
    merged = g[:, :d] * ya + g[:, d:] * yb
    mix = jnp.dot(merged.astype(BF16), wout_ref[...], preferred_element_type=F32)
    x1 = _layer_norm(alpha * x_ref[...] + mix, lng_ref[...], lnb_ref[...])
    x1_ref[...] = x1
    x1b_ref[...] = _pack_bf16_pairs(x1)
    tm = x1.shape[0]
    xh = x1.astype(BF16)
    xl = (x1 - xh.astype(F32)).astype(BF16)
    r = jnp.dot(jnp.concatenate([xh, xl], axis=0), wr_ref[...], preferred_element_type=F32)
    logits = (r[:tm, :LANES] + r[:tm, LANES:]) + (r[tm:, :LANES] + r[tm:, LANES:]) + br_ref[...]
    lane = lax.broadcasted_iota(jnp.int32, logits.shape, 1).astype(F32)
    vals, idxs = [], []
    for _ in range(TOP_K):
        m = jnp.max(logits, axis=1, keepdims=True)
        idx = jnp.min(jnp.where(logits == m, lane, float(LANES)), axis=1, keepdims=True)
        vals.append(m)
        idxs.append(idx)
        logits = jnp.where(lane == idx, -jnp.inf, logits)
    es = [jnp.exp(v - vals[0]) for v in vals]
    denom = es[0] + es[1] + es[2] + es[3]
    gates = jnp.zeros(logits.shape, F32)
    topi = jnp.zeros(logits.shape, F32)
    for k in range(TOP_K):
        gates = jnp.where(lane == k, es[k] / denom, gates)
        topi = jnp.where(lane == k, idxs[k], topi)
    gate_ref[...] = gates
    topi_ref[...] = topi.astype(jnp.int32)


def _merge(x2, a2, nb2, g2, wda, wna, wout, ln_g, ln_b, wr, br, alpha):
    T, D = x2.shape
    tm = min(PROJ_TM, T)
    tok = lambda width: pl.BlockSpec((tm, width), lambda i: (i, 0))
    const = lambda shape: pl.BlockSpec(shape, lambda i: (0,) * len(shape))
    return pl.pallas_call(
        functools.partial(_merge_kernel, alpha=alpha),
        grid=(T // tm,),
        in_specs=[tok(D), tok(a2.shape[1]), tok(nb2.shape[1]), tok(g2.shape[1]),
                  const(wda.shape), const(wna.shape), const(wout.shape),
                  const((1, D)), const((1, D)), const(wr.shape), const((1, LANES))],
        out_specs=(tok(D), tok(D // 2), tok(LANES), tok(LANES)),
        out_shape=(jax.ShapeDtypeStruct((T, D), F32), jax.ShapeDtypeStruct((T, D // 2), jnp.uint32),
                   jax.ShapeDtypeStruct((T, LANES), jnp.int32), jax.ShapeDtypeStruct((T, LANES), F32)),
        compiler_params=_cparams(("parallel",)),
        name="merge",
    )(x2, a2, nb2, g2, wda, wna, wout, ln_g, ln_b, wr, br)


def _glu_group_bias(b1):
    e, n = b1.shape
    return b1.reshape(e, n // (2 * LANES), LANES, 2).transpose(0, 1, 3, 2).reshape(e, 1, n)


def _moe_kernel(be_ref, nused_ref, xs_ref, w1_ref, b1_ref, w2_ref, b2_ref, *rest, blk0):
    ys_ref, w1g_sc, w2b_sc = rest[-3:]
    i = pl.program_id(0)
    blk = i + blk0
    used = blk < nused_ref[0]
    fresh = jnp.logical_or(i == 0, be_ref[blk] != be_ref[jnp.maximum(blk - 1, 0)])
    grp = 2 * LANES

    @pl.when(jnp.logical_and(used, fresh))
    def _():
        r = lax.broadcasted_iota(jnp.int32, (grp, grp), 0)
        c = lax.broadcasted_iota(jnp.int32, (grp, grp), 1)
        src = jnp.where(c < LANES, 2 * c, 2 * (c - LANES) + 1)
        perm = jnp.where(r == src, 1.0, 0.0).astype(BF16)
        for g in range(w1_ref.shape[1] // grp):
            blk = w1_ref[:, g * grp:(g + 1) * grp].astype(BF16)
            w1g_sc[:, g * grp:(g + 1) * grp] = jnp.dot(blk, perm, preferred_element_type=F32).astype(BF16)
        w2b_sc[...] = w2_ref[...].astype(BF16)

    @pl.when(used)
    def _():
        xb = _unpack_bf16_pairs(xs_ref[...]).astype(BF16)
        h = jnp.dot(xb, w1g_sc[...], preferred_element_type=F32) + b1_ref[...]
        acts = []
        for g in range(h.shape[1] // grp):
            x_glu = jnp.minimum(h[:, g * grp:g * grp + LANES], SWIGLU_LIMIT)
            x_lin = jnp.clip(h[:, g * grp + LANES:(g + 1) * grp], -SWIGLU_LIMIT, SWIGLU_LIMIT)
            acts.append(x_glu * (1.0 / (1.0 + jnp.exp(-SWIGLU_ALPHA * x_glu))) * (x_lin + 1.0))
        act = jnp.concatenate(acts, axis=1).astype(BF16)
        y = jnp.dot(act, w2b_sc[...], preferred_element_type=F32) + b2_ref[...]
        ys_ref[...] = _pack_bf16_pairs(y)

    @pl.when(jnp.logical_not(used))
    def _():
        ys_ref[...] = jnp.zeros(ys_ref.shape, ys_ref.dtype)


def _moe_ffn(xs, block_expert, n_used, w1, b1g, w2, b2, blk0=0, ys_prev=None):
    D = 2 * xs.shape[1]
    P = block_expert.shape[0] * MOE_BLOCK
    de2 = w1.shape[2]
    de = w2.shape[1]
    wspec = lambda r, c: pl.BlockSpec((None, r, c), lambda i, be, nu: (be[i + blk0], 0, 0))
    in_specs = [pl.BlockSpec((MOE_BLOCK, D // 2), lambda i, be, nu: (i, 0)),
                wspec(D, de2), wspec(1, de2), wspec(de, D), wspec(1, D)]
    args = [block_expert, n_used, xs, w1, b1g, w2, b2]
    aliases = {}
    if ys_prev is not None:
        in_specs.append(pl.BlockSpec(memory_space=pl.ANY))
        args.append(ys_prev)
        aliases = {len(args) - 1: 0}
    grid_spec = pltpu.PrefetchScalarGridSpec(
        num_scalar_prefetch=2,
        grid=(xs.shape[0] // MOE_BLOCK,),
        in_specs=in_specs,
        out_specs=pl.BlockSpec((MOE_BLOCK, D // 2), lambda i, be, nu: (i + blk0, 0)),
        scratch_shapes=[pltpu.VMEM((D, de2), BF16), pltpu.VMEM((de, D), BF16)],
    )
    return pl.pallas_call(
        functools.partial(_moe_kernel, blk0=blk0),
        grid_spec=grid_spec,
        out_shape=jax.ShapeDtypeStruct((P, D // 2), jnp.uint32),
        input_output_aliases=aliases,
        compiler_params=_cparams(("arbitrary",)),
        name="moe_ffn",
    )(*args)


def _combine_kernel(x1_ref, y0_ref, y1_ref, y2_ref, y3_ref, gate_ref, lng_ref, lnb_ref, o_ref, *, alpha):
    gates = gate_ref[...]
    ffn = jnp.zeros(x1_ref.shape, F32)
    for k, y_ref in enumerate((y0_ref, y1_ref, y2_ref, y3_ref)):
        ffn = ffn + _unpack_bf16_pairs(y_ref[...]) * gates[:, k:k + 1]
    o_ref[...] = _layer_norm(alpha * x1_ref[...] + ffn, lng_ref[...], lnb_ref[...])


def _combine(x1, y_slots, gates, ln_g, ln_b, alpha):
    T, D = x1.shape
    tm = min(PROJ_TM, T)
    tok = lambda width: pl.BlockSpec((tm, width), lambda i: (i, 0))
    const = lambda shape: pl.BlockSpec(shape, lambda i: (0,) * len(shape))
    slot = lambda k: pl.BlockSpec((None, tm, D // 2), lambda i: (k, i, 0))
    return pl.pallas_call(
        functools.partial(_combine_kernel, alpha=alpha),
        grid=(T // tm,),
        in_specs=[tok(D)] + [slot(k) for k in range(TOP_K)] + [tok(LANES), const((1, D)), const((1, D))],
        out_specs=tok(D),
        out_shape=jax.ShapeDtypeStruct((T, D), F32),
        compiler_params=_cparams(("parallel",)),
        name="combine",
    )(x1, y_slots, y_slots, y_slots, y_slots, gates, ln_g, ln_b)


def _route(top_i, n_experts):
    T = top_i.shape[0]
    TK = T * TOP_K
    experts = jnp.arange(n_experts, dtype=jnp.int32)
    chosen = top_i[:, :, None] == experts
    picks = jnp.sum(chosen, axis=1, dtype=jnp.int32)
    csum = jnp.cumsum(picks, axis=0)
    counts = csum[-1]
    padded = ((counts + MOE_BLOCK - 1) // MOE_BLOCK) * MOE_BLOCK
    padded_end = jnp.cumsum(padded)
    padded_start = padded_end - padded
    group_start = jnp.cumsum(counts) - counts
    slot_te = padded_start[None, :] + csum - picks
    dest = jnp.sum(jnp.where(chosen, slot_te[:, None, :], 0), axis=2)
    n_blocks = -(-TK // MOE_BLOCK) + n_experts
    blk_first = jnp.arange(n_blocks, dtype=jnp.int32) * MOE_BLOCK
    block_expert = jnp.minimum(jnp.sum(padded_end[None, :] <= blk_first[:, None], axis=1),
                               n_experts - 1).astype(jnp.int32)
    n_used = (padded_end[-1] // MOE_BLOCK).astype(jnp.int32).reshape(1)
    order = jnp.argsort(top_i.reshape(-1), stable=True).astype(jnp.int32)
    rank0 = blk_first - padded_start[block_expert]
    rank = rank0[:, None] + jnp.arange(MOE_BLOCK, dtype=jnp.int32)[None, :]
    valid = rank < counts[block_expert][:, None]
    sorted_pos = jnp.clip(group_start[block_expert][:, None] + rank, 0, TK - 1)
    filler = (blk_first[:, None] + jnp.arange(MOE_BLOCK, dtype=jnp.int32)[None, :]) % T
    src_tok = jnp.where(valid, order[sorted_pos] // TOP_K, filler).reshape(-1)
    return src_tok, block_expert, n_used, dest.T.reshape(-1)


def _rope_tables(seq):
    pos = jnp.arange(seq, dtype=F32)
    inv = ROPE_THETA ** (-jnp.arange(0, DA_HEAD_DIM, 2, dtype=F32) / DA_HEAD_DIM)
    ang = pos[:, None] * inv[None, :]
    ang = jnp.concatenate([ang, ang], axis=-1)
    cos, sin = jnp.cos(ang), jnp.sin(ang)
    half = DA_HEAD_DIM // 2
    sin_signed = jnp.concatenate([-sin[:, :half], sin[:, half:]], axis=-1)
    return jnp.tile(cos, (1, 2)), jnp.tile(sin_signed, (1, 2))


def kernel(x, w_in, b_in, lambda_q1, lambda_k1, lambda_q2, lambda_k2, subln_g, rpb, w_branch_da, w_branch_na, w_out, ln1_g, ln1_b, w_router, b_router, w_mlp1, b_mlp1, w_mlp2, b_mlp2, ln2_g, ln2_b):
    B, S, D = x.shape
    depth = w_in.shape[0]
    n_experts = w_router.shape[2]
    T = B * S
    rows = S // GRID_W
    assert S % GRID_W == 0 and rows % NA_ROWS == 0 and rows >= NA_BAND
    alpha = (2 * depth) ** 0.25
    cos, sin = _rope_tables(S)
    row = lambda v: v.reshape(1, -1)

    for l in range(depth):
        lam_init = 0.8 - 0.6 * math.exp(-0.3 * l)
        qT, k, vT, qn, kn, vn, gates_br = _in_proj(x, w_in[l].astype(BF16), row(b_in[l]), cos, sin)
        a = _diff_attn(qT, k, vT, row(lambda_q1[l]), row(lambda_k1[l]), row(lambda_q2[l]),
                       row(lambda_k2[l]), subln_g[l].reshape(-1, 1), lam_init)
        nb = _na_attn(qn, kn, vn, _na_tiles(rpb[l]))

        wr = jnp.pad(w_router[l], ((0, 0), (0, LANES - n_experts)))
        wr_h = wr.astype(BF16)
        wr = jnp.concatenate([wr_h, (wr - wr_h.astype(F32)).astype(BF16)], axis=1)
        br = jnp.pad(row(b_router[l]), ((0, 0), (0, LANES - n_experts)), constant_values=NEG_BIG)
        x1, x1b, topi, gates = _merge(
            x.reshape(T, D), a.reshape(T, -1), nb.reshape(T, -1), gates_br.reshape(T, -1),
            w_branch_da[l].astype(BF16), w_branch_na[l].astype(BF16), w_out[l].astype(BF16),
            row(ln1_g[l]), row(ln1_b[l]), wr, br, alpha)

        src_tok, block_expert, n_used, dest = _route(topi[:, :TOP_K], n_experts)
        ys, blk0 = None, 0
        for nblk in (block_expert.shape[0] // 2, block_expert.shape[0] - block_expert.shape[0] // 2):
            rows_ = src_tok[blk0 * MOE_BLOCK:(blk0 + nblk) * MOE_BLOCK]
            ys = _moe_ffn(_sc_gather_rows(x1b, rows_), block_expert, n_used, w_mlp1[l],
                          _glu_group_bias(b_mlp1[l]), w_mlp2[l], b_mlp2[l][:, None, :], blk0, ys)
            blk0 += nblk
        y_slots = _sc_gather_rows(ys, dest).reshape(TOP_K, T, D // 2)
        x = _combine(x1, y_slots, gates, row(ln2_g[l]), row(ln2_b[l]), alpha).reshape(B, S, D)
    return x
```

```python
import functools
import math

import jax
import jax.numpy as jnp
from jax import lax
from jax.experimental import pallas as pl
from jax.experimental.pallas import tpu as pltpu
from jax.experimental.pallas import tpu_sc as plsc

F32 = jnp.float32
BF16 = jnp.bfloat16

GRID_W = 64
DA_HEADS = 4
DA_HEAD_DIM = 64
DA_V_DIM = 2 * DA_HEAD_DIM
DA_WIDTH = DA_HEADS * DA_V_DIM
ROPE_THETA = 10000.0
NA_HEADS = 8
NA_HEAD_DIM = 64
NA_WIDTH = NA_HEADS * NA_HEAD_DIM
NA_KH = 8
NA_KW = 16
TOP_K = 4
SWIGLU_ALPHA = 1.702
SWIGLU_LIMIT = 7.0
MOE_BLOCK = 512
LN_EPS = 1e-5
RMS_EPS = 1e-5

LANES = 128
VMEM_LIMIT_BYTES = 56 * 1024 * 1024
SC_CORES = 2
SC_SUBCORES = 16
SC_GATHER_ROWS = 64

LOG2E = math.log2(math.e)
NEG_BIG = -1e30

PROJ_TM = 512
DA_TQ = 512
DA_TK = 512
DA_UNROLL = 8
NA_ROWS = 8
NA_BAND = 16


def _cparams(sem):
    return pltpu.CompilerParams(dimension_semantics=sem, vmem_limit_bytes=VMEM_LIMIT_BYTES)


def _pack_bf16_pairs(y):
    w = y.shape[1] // 2
    bits = lax.bitcast_convert_type(y, jnp.uint32)
    rounded = bits + jnp.uint32(0x7FFF) + ((bits >> 16) & jnp.uint32(1))
    return (rounded[:, :w] >> 16) | (rounded[:, w:] & jnp.uint32(0xFFFF0000))


def _unpack_bf16_pairs(words):
    lo = lax.bitcast_convert_type(words << 16, F32)
    hi = lax.bitcast_convert_type(words & jnp.uint32(0xFFFF0000), F32)
    return jnp.concatenate([lo, hi], axis=1)


def _sc_gather_rows(table, idx):
    n, w = idx.shape[0], table.shape[1]
    rows = SC_GATHER_ROWS
    workers = SC_CORES * SC_SUBCORES
    per_worker = n // workers
    n_pairs = per_worker // (2 * rows)
    assert n % (workers * 2 * rows) == 0 and n_pairs >= 1
    mesh = plsc.VectorSubcoreMesh(core_axis_name="c", subcore_axis_name="s",
                                  num_cores=SC_CORES, num_subcores=SC_SUBCORES)

    def body(table_hbm, idx_hbm, out_hbm, idx0, idx1, rows0, rows1, sem0, sem1):
        base = (lax.axis_index("s") * SC_CORES + lax.axis_index("c")) * per_worker
        bufs = ((idx0, rows0, sem0), (idx1, rows1, sem1))

        def gather(b):
            idx_v, rows_v, sem = bufs[b]
            return pltpu.make_async_copy(table_hbm.at[idx_v], rows_v, sem)

        def start(c, b):
            pltpu.sync_copy(idx_hbm.at[pl.ds(base + c * rows, rows)], bufs[b][0])
            gather(b).start()

        def finish(c, b):
            gather(b).wait()
            pltpu.sync_copy(bufs[b][1], out_hbm.at[pl.ds(base + c * rows, rows)])

        start(0, 0)

        @pl.loop(0, n_pairs - 1)
        def _(g):
            c = 2 * g
            start(c + 1, 1)
            finish(c, 0)
            start(c + 2, 0)
            finish(c + 1, 1)

        last = 2 * (n_pairs - 1)
        start(last + 1, 1)
        finish(last, 0)
        finish(last + 1, 1)

    return pl.kernel(
        body, out_type=jax.ShapeDtypeStruct((n, w), table.dtype), mesh=mesh,
        scratch_types=[pltpu.VMEM((rows,), jnp.int32), pltpu.VMEM((rows,), jnp.int32),
                       pltpu.VMEM((rows, w), table.dtype), pltpu.VMEM((rows, w), table.dtype),
                       pltpu.SemaphoreType.DMA, pltpu.SemaphoreType.DMA],
        name="sc_gather_rows",
    )(table, idx)


def _sc_scatter_rows(table, dest, n_out):
    t, w = table.shape
    copies = dest.shape[0] // t
    rows = SC_GATHER_ROWS
    workers = SC_CORES * SC_SUBCORES
    per_worker = t // workers
    n_pairs = per_worker // (2 * rows)
    assert t % (workers * 2 * rows) == 0 and n_pairs >= 1
    mesh = plsc.VectorSubcoreMesh(core_axis_name="c", subcore_axis_name="s",
                                  num_cores=SC_CORES, num_subcores=SC_SUBCORES)

    def body(table_hbm, dest_hbm, out_hbm, *scratch):
        base = (lax.axis_index("s") * SC_CORES + lax.axis_index("c")) * per_worker
        per_buf = copies + 2
        bufs = (scratch[:per_buf], scratch[per_buf:])

        def scatters(b):
            rows_v, sem = bufs[b][0], bufs[b][1]
            return [pltpu.make_async_copy(rows_v, out_hbm.at[idx_v], sem) for idx_v in bufs[b][2:]]

        def start(c, b):
            t0 = base + c * rows
            pltpu.sync_copy(table_hbm.at[pl.ds(t0, rows)], bufs[b][0])
            for k, idx_v in enumerate(bufs[b][2:]):
                pltpu.sync_copy(dest_hbm.at[pl.ds(k * t + t0, rows)], idx_v)
            for copy in scatters(b):
                copy.start()

        def finish(b):
            for copy in scatters(b):
                copy.wait()

        start(0, 0)

        @pl.loop(0, n_pairs - 1)
        def _(g):
            start(2 * g + 1, 1)
            finish(0)
            start(2 * g + 2, 0)
            finish(1)

        start(2 * n_pairs - 1, 1)
        finish(0)
        finish(1)

    one_buf = ([pltpu.VMEM((rows, w), table.dtype), pltpu.SemaphoreType.DMA]
               + [pltpu.VMEM((rows,), jnp.int32) for _ in range(copies)])
    return pl.kernel(
        body, out_type=jax.ShapeDtypeStruct((n_out, w), table.dtype), mesh=mesh,
        scratch_types=one_buf + one_buf, name="sc_scatter_rows",
    )(table, dest)


def _in_proj_kernel(x_ref, w_ref, b_ref, cos_ref, sin_ref,
                    qT_ref, k_ref, vT_ref, qn_ref, kn_ref, vn_ref, g_ref):
    xb = x_ref[...].astype(BF16)

    def seg(lo, hi):
        return jnp.dot(xb, w_ref[:, lo:hi], preferred_element_type=F32) + b_ref[:, lo:hi]

    cos = cos_ref[...]
    sin = sin_ref[...]
    lane = lax.broadcasted_iota(jnp.int32, cos.shape, 1)
    first_half = (lane % DA_HEAD_DIM) < (DA_HEAD_DIM // 2)

    def rope(y):
        outs = []
        for h in range(DA_HEADS):
            yh = y[:, h * LANES:(h + 1) * LANES]
            partner = jnp.where(first_half,
                                pltpu.roll(yh, LANES - DA_HEAD_DIM // 2, 1),
                                pltpu.roll(yh, DA_HEAD_DIM // 2, 1))
            outs.append(yh * cos + partner * sin)
        return jnp.concatenate(outs, axis=1)

    w = DA_WIDTH
    q = rope(seg(0, w)) * (DA_HEAD_DIM ** -0.5 * LOG2E)
    qT_ref[...] = q.T.astype(BF16)
    k_ref[...] = rope(seg(w, 2 * w)).astype(BF16)
    vT_ref[...] = seg(2 * w, 3 * w).T.astype(BF16)
    o = 3 * w
    qn_ref[...] = (seg(o, o + NA_WIDTH) * (NA_HEAD_DIM ** -0.5)).T.astype(BF16)
    kn_ref[...] = seg(o + NA_WIDTH, o + 2 * NA_WIDTH).astype(BF16)
    vn_ref[...] = seg(o + 2 * NA_WIDTH, o + 3 * NA_WIDTH).T.astype(BF16)
    g0 = o + 3 * NA_WIDTH
    gate_pre = seg(g0, w_ref.shape[1])
    g_ref[...] = (1.0 / (1.0 + jnp.exp(-gate_pre))).astype(BF16)


def _in_proj(x, w_in, b_in, cos, sin):
    B, S, D = x.shape
    tm = min(PROJ_TM, S)
    n_cols = w_in.shape[1]
    n_gate = n_cols - 3 * DA_WIDTH - 3 * NA_WIDTH
    tok = lambda width: pl.BlockSpec((None, tm, width), lambda b, i: (b, i, 0))
    tr = pl.BlockSpec((None, DA_WIDTH, tm), lambda b, i: (b, 0, i))
    const = lambda shape: pl.BlockSpec(shape, lambda b, i: (0,) * len(shape))
    out_shape = (
        jax.ShapeDtypeStruct((B, DA_WIDTH, S), BF16),
        jax.ShapeDtypeStruct((B, S, DA_WIDTH), BF16),
        jax.ShapeDtypeStruct((B, DA_WIDTH, S), BF16),
        jax.ShapeDtypeStruct((B, NA_WIDTH, S), BF16),
        jax.ShapeDtypeStruct((B, S, NA_WIDTH), BF16),
        jax.ShapeDtypeStruct((B, NA_WIDTH, S), BF16),
        jax.ShapeDtypeStruct((B, S, n_gate), BF16),
    )
    assert DA_WIDTH == NA_WIDTH
    return pl.pallas_call(
        _in_proj_kernel,
        grid=(B, S // tm),
        in_specs=[tok(D), const((D, n_cols)), const((1, n_cols)),
                  pl.BlockSpec((tm, LANES), lambda b, i: (i, 0)),
                  pl.BlockSpec((tm, LANES), lambda b, i: (i, 0))],
        out_specs=(tr, tok(DA_WIDTH), tr, tr, tok(NA_WIDTH), tr, tok(n_gate)),
        out_shape=out_shape,
        compiler_params=_cparams(("parallel", "parallel")),
        name="in_proj",
    )(x, w_in, b_in, cos, sin)


def _diff_attn_kernel(qT_ref, qTn_ref, k_ref, vT_ref, lq1_ref, lk1_ref, lq2_ref, lk2_ref, g_ref,
                      o_ref, m_sc, l_sc, acc_sc, s_sc, cmax_sc, *, tk, unroll, lam_init):
    tq = qT_ref.shape[1]

    def block_diag(q_ref):
        qT = q_ref[...]
        row = lax.broadcasted_iota(jnp.int32, qT.shape, 0)
        zero = jnp.zeros_like(qT)
        return jnp.concatenate([jnp.where(row < DA_HEAD_DIM, qT, zero),
                                jnp.where(row < DA_HEAD_DIM, zero, qT)], axis=1)

    qbd = block_diag(qT_ref)

    m_sc[...] = jnp.full(m_sc.shape, -jnp.inf, F32)
    l_sc[...] = jnp.zeros(l_sc.shape, F32)
    acc_sc[...] = jnp.zeros(acc_sc.shape, F32)

    nk = k_ref.shape[0] // tk

    def put_scores(slot, j, q=None):
        off = pl.multiple_of(j * tk, tk)
        s = jnp.dot(k_ref[pl.ds(off, tk), :], qbd if q is None else q, preferred_element_type=F32)
        s_sc[slot] = s
        cmax_sc[slot] = jnp.max(s, axis=0, keepdims=True)

    def accumulate(slot, j):
        off = pl.multiple_of(j * tk, tk)
        m_prev = m_sc[...]
        m_new = jnp.maximum(m_prev, cmax_sc[slot])
        alpha = jnp.exp2(m_prev - m_new)
        p = jnp.exp2(s_sc[slot] - m_new)
        l_sc[...] = alpha * l_sc[...] + jnp.sum(p, axis=0, keepdims=True)
        vb = vT_ref[:, pl.ds(off, tk)]
        acc_sc[...] = alpha * acc_sc[...] + jnp.dot(vb, p.astype(BF16), preferred_element_type=F32)
        m_sc[...] = m_new

    @pl.when(pl.program_id(2) == 0)
    def _():
        put_scores(0, 0)

    def body(jj, carry):
        for u in range(unroll):
            j = unroll * jj + u
            put_scores((u + 1) % 2, j + 1)
            accumulate(u % 2, j)
        return carry

    lax.fori_loop(0, nk // unroll - 1, body, 0)
    for j in range(nk - unroll, nk):
        if j + 1 < nk:
            put_scores((j + 1) % 2, j + 1)
        else:
            put_scores(0, 0, block_diag(qTn_ref))
        accumulate(j % 2, j)

    lam = (jnp.exp(jnp.sum(lq1_ref[...] * lk1_ref[...], axis=1, keepdims=True))
           - jnp.exp(jnp.sum(lq2_ref[...] * lk2_ref[...], axis=1, keepdims=True)) + lam_init)
    on = acc_sc[...] / l_sc[...]
    o = on[:, :tq] - lam * on[:, tq:]
    ms = jnp.mean(o * o, axis=0, keepdims=True)
    o = o * lax.rsqrt(ms + RMS_EPS) * g_ref[...]
    o = o * (1.0 - lam_init)
    o_ref[...] = o.T.astype(BF16)


def _diff_attn(qT, k, vT, lq1, lk1, lq2, lk2, subln_g, lam_init):
    B, S, _ = k.shape
    tq = min(DA_TQ, S)
    tk = min(DA_TK, S // 2)
    unroll = math.gcd(DA_UNROLL, S // tk)
    assert unroll % 2 == 0 and S % tk == 0 and S % tq == 0
    nq = S // tq
    vec = pl.BlockSpec((1, DA_HEAD_DIM), lambda b, h, i: (0, 0))
    kernel = functools.partial(_diff_attn_kernel, tk=tk, unroll=unroll, lam_init=lam_init)
    return pl.pallas_call(
        kernel,
        grid=(B, DA_HEADS, nq),
        in_specs=[pl.BlockSpec((None, DA_V_DIM, tq), lambda b, h, i: (b, h, i)),
                  pl.BlockSpec((None, DA_V_DIM, tq), lambda b, h, i: (b, h, jnp.minimum(i + 1, nq - 1))),
                  pl.BlockSpec((None, S, DA_V_DIM), lambda b, h, i: (b, 0, h)),
                  pl.BlockSpec((None, DA_V_DIM, S), lambda b, h, i: (b, h, 0)),
                  vec, vec, vec, vec,
                  pl.BlockSpec((DA_V_DIM, 1), lambda b, h, i: (0, 0))],
        out_specs=pl.BlockSpec((None, tq, DA_V_DIM), lambda b, h, i: (b, i, h)),
        out_shape=jax.ShapeDtypeStruct((B, S, DA_WIDTH), BF16),
        scratch_shapes=[pltpu.VMEM((1, 2 * tq), F32), pltpu.VMEM((1, 2 * tq), F32),
                        pltpu.VMEM((DA_V_DIM, 2 * tq), F32), pltpu.VMEM((2, tk, 2 * tq), F32),
                        pltpu.VMEM((2, 1, 2 * tq), F32)],
        compiler_params=_cparams(("parallel", "parallel", "arbitrary")),
        name="diff_attn",
    )(qT, qT, k, vT, lq1, lk1, lq2, lk2, subln_g)


def _na_band_start(r0, rows):
    return jnp.clip(r0 - NA_KH // 2, 0, rows - NA_BAND)


def _na_tiles(rpb):
    cols = jnp.arange(GRID_W)
    cs = jnp.clip(cols - NA_KW // 2, 0, GRID_W - NA_KW)
    col_ok = (cols[:, None] >= cs[None, :]) & (cols[:, None] < cs[None, :] + NA_KW)
    col_off = jnp.clip(cols[:, None] - cols[None, :] + (NA_KW - 1), 0, 2 * NA_KW - 2)
    tiles = jnp.where(col_ok[None, None], rpb.astype(F32)[:, :, col_off], NEG_BIG)
    neg = jnp.full((rpb.shape[0], 1, GRID_W, GRID_W), NEG_BIG, F32)
    return jnp.concatenate([jnp.concatenate([tiles, neg], axis=1),
                            jnp.concatenate([neg, tiles], axis=1)], axis=3)


def _na_geometry(variant, rows):
    nblk = rows // NA_ROWS
    r0 = (0, NA_ROWS * min(1, nblk - 1), NA_ROWS * (nblk - 1))[variant]
    band = min(max(r0 - NA_KH // 2, 0), rows - NA_BAND)
    pairs = []
    for qp in range(NA_ROWS // 2):
        entries = []
        for kr in range(NA_BAND):
            ok = []
            for qr in (r0 + 2 * qp, r0 + 2 * qp + 1):
                rs = min(max(qr - NA_KH // 2, 0), rows - NA_KH)
                ok.append(rs <= band + kr < rs + NA_KH)
            if ok[0] or ok[1]:
                entries.append((kr, band + kr - (r0 + 2 * qp) + NA_KH - 1, ok[0], ok[1]))
        pairs.append(entries)
    return pairs


def _na_kernel(qT_ref, k_ref, vT_ref, tile_ref, o_ref, *, rows):
    i = pl.program_id(2)
    nblk = rows // NA_ROWS
    nk = NA_BAND * GRID_W
    off = pl.multiple_of(_na_band_start(i * NA_ROWS, rows) * GRID_W, 2 * GRID_W)

    def block(variant):
        kb = k_ref[pl.ds(off, nk), :]
        vT = vT_ref[:, pl.ds(off, nk)]
        lane = lax.broadcasted_iota(jnp.int32, kb.shape, 1)
        kbd = jnp.concatenate([jnp.where(lane < NA_HEAD_DIM, kb, jnp.zeros_like(kb)),
                               jnp.where(lane < NA_HEAD_DIM, jnp.zeros_like(kb), kb)], axis=0)
        row = lax.broadcasted_iota(jnp.int32, vT.shape, 0)
        vbdT = jnp.concatenate([jnp.where(row < NA_HEAD_DIM, vT, jnp.zeros_like(vT)),
                                jnp.where(row < NA_HEAD_DIM, jnp.zeros_like(vT), vT)], axis=1)
        s = jnp.dot(kbd, qT_ref[...], preferred_element_type=F32)
        left = lax.broadcasted_iota(jnp.int32, (GRID_W, LANES), 1) < GRID_W
        zero_slab = jnp.zeros((GRID_W, LANES), BF16)
        geometry = _na_geometry(variant, rows)
        p_heads, l_heads = [], []
        for h in range(2):
            p_cols, l_cols = [], []
            for qp, entries in enumerate(geometry):
                logits = {}
                for kr, d, ok_l, ok_r in entries:
                    bias = tile_ref[h, d]
                    if not ok_r:
                        bias = jnp.where(left, bias, NEG_BIG)
                    if not ok_l:
                        bias = jnp.where(left, NEG_BIG, bias)
                    r = h * nk + kr * GRID_W
                    logits[kr] = s[r:r + GRID_W, qp * LANES:(qp + 1) * LANES] + bias
                m = functools.reduce(jnp.maximum, logits.values())
                m = jnp.max(m, axis=0, keepdims=True)
                probs = {kr: jnp.exp(x - m) for kr, x in logits.items()}
                l_cols.append(jnp.sum(functools.reduce(jnp.add, probs.values()), axis=0, keepdims=True))
                p_cols.append(jnp.concatenate(
                    [probs[kr].astype(BF16) if kr in probs else zero_slab for kr in range(NA_BAND)], axis=0))
            p_heads.append(jnp.concatenate(p_cols, axis=1))
            l_heads.append(jnp.concatenate(l_cols, axis=1))
        oT = jnp.dot(vbdT, jnp.concatenate(p_heads, axis=0), preferred_element_type=F32)
        row_o = lax.broadcasted_iota(jnp.int32, oT.shape, 0)
        oT = oT / jnp.where(row_o < NA_HEAD_DIM, l_heads[0], l_heads[1])
        o_ref[...] = oT.T.astype(BF16)

    pl.when(i == 0)(lambda: block(0))
    pl.when(jnp.logical_and(i > 0, i < nblk - 1))(lambda: block(1))
    pl.when(jnp.logical_and(i > 0, i == nblk - 1))(lambda: block(2))


def _na_attn(qnT, kn, vnT, tiles):
    B, S, _ = kn.shape
    rows = S // GRID_W
    nblk = rows // NA_ROWS
    nq = NA_ROWS * GRID_W
    return pl.pallas_call(
        functools.partial(_na_kernel, rows=rows),
        grid=(B, NA_HEADS // 2, nblk),
        in_specs=[pl.BlockSpec((None, LANES, nq), lambda b, h, i: (b, h, i)),
                  pl.BlockSpec((None, S, LANES), lambda b, h, i: (b, 0, h)),
                  pl.BlockSpec((None, LANES, S), lambda b, h, i: (b, h, 0)),
                  pl.BlockSpec((2,) + tiles.shape[1:], lambda b, h, i: (h, 0, 0, 0))],
        out_specs=pl.BlockSpec((None, nq, LANES), lambda b, h, i: (b, i, h)),
        out_shape=jax.ShapeDtypeStruct((B, S, NA_WIDTH), BF16),
        compiler_params=_cparams(("parallel", "parallel", "arbitrary")),
        name="na_attn",
    )(qnT, kn, vnT, tiles)


def _layer_norm(h, g, b):
    mu = jnp.mean(h, axis=-1, keepdims=True)
    d = h - mu
    var = jnp.mean(d * d, axis=-1, keepdims=True)
    return d * lax.rsqrt(var + LN_EPS) * g + b


def _merge_kernel(x_ref, a_ref, nb_ref, g_ref, wda_ref, wna_ref, wout_ref, lng_ref, lnb_ref,
                  wr_ref, br_ref, x1_ref, x1b_ref, topi_ref, gate_ref, *, alpha):
    d = x_ref.shape[1]
    ya = jnp.dot(a_ref[...], wda_ref[...], preferred_element_type=F32)
    yb = jnp.dot(nb_ref[...], wna_ref[...], preferred_element_type=F32)
    g = g_ref[...].astype(F32)
    merged = g[:, :d] * ya + g[:, d:] * yb
    mix = jnp.dot(merged.astype(BF16), wout_ref[...], preferred_element_type=F32)
    x1 = _layer_norm(alpha * x_ref[...] + mix, lng_ref[...], lnb_ref[...])
    x1_ref[...] = x1
    x1b_ref[...] = _pack_bf16_pairs(x1)
    tm = x1.shape[0]
    xh = x1.astype(BF16)
    xl = (x1 - xh.astype(F32)).astype(BF16)
    r = jnp.dot(jnp.concatenate([xh, xl], axis=0), wr_ref[...], preferred_element_type=F32)
    logits = (r[:tm, :LANES] + r[:tm, LANES:]) + (r[tm:, :LANES] + r[tm:, LANES:]) + br_ref[...]
    lane = lax.broadcasted_iota(jnp.int32, logits.shape, 1).astype(F32)
    vals, idxs = [], []
    for _ in range(TOP_K):
        m = jnp.max(logits, axis=1, keepdims=True)
        idx = jnp.min(jnp.where(logits == m, lane, float(LANES)), axis=1, keepdims=True)
        vals.append(m)
        idxs.append(idx)
        logits = jnp.where(lane == idx, -jnp.inf, logits)
    es = [jnp.exp(v - vals[0]) for v in vals]
    denom = es[0] + es[1] + es[2] + es[3]
    gates = jnp.zeros(logits.shape, F32)
    topi = jnp.zeros(logits.shape, F32)
    for k in range(TOP_K):
        gates = jnp.where(lane == k, es[k] / denom, gates)
        topi = jnp.where(lane == k, idxs[k], topi)
    gate_ref[...] = gates
    topi_ref[...] = topi.astype(jnp.int32)


def _merge(x2, a2, nb2, g2, wda, wna, wout, ln_g, ln_b, wr, br, alpha):
    T, D = x2.shape
    tm = min(PROJ_TM, T)
    tok = lambda width: pl.BlockSpec((tm, width), lambda i: (i, 0))
    const = lambda shape: pl.BlockSpec(shape, lambda i: (0,) * len(shape))
    return pl.pallas_call(
        functools.partial(_merge_kernel, alpha=alpha),
        grid=(T // tm,),
        in_specs=[tok(D), tok(a2.shape[1]), tok(nb2.shape[1]), tok(g2.shape[1]),
                  const(wda.shape), const(wna.shape), const(wout.shape),
                  const((1, D)), const((1, D)), const(wr.shape), const((1, LANES))],
        out_specs=(tok(D), tok(D // 2), tok(LANES), tok(LANES)),
        out_shape=(jax.ShapeDtypeStruct((T, D), F32), jax.ShapeDtypeStruct((T, D // 2), jnp.uint32),
                   jax.ShapeDtypeStruct((T, LANES), jnp.int32), jax.ShapeDtypeStruct((T, LANES), F32)),
        compiler_params=_cparams(("parallel",)),
        name="merge",
    )(x2, a2, nb2, g2, wda, wna, wout, ln_g, ln_b, wr, br)


def _glu_group_bias(b1):
    e, n = b1.shape
    return b1.reshape(e, n // (2 * LANES), LANES, 2).transpose(0, 1, 3, 2).reshape(e, 1, n)


def _moe_kernel(be_ref, nused_ref, xs_ref, w1_ref, b1_ref, w2_ref, b2_ref, *rest, blk0):
    ys_ref, w1g_sc, w2b_sc = rest[-3:]
    i = pl.program_id(0)
    blk = i + blk0
    used = blk < nused_ref[0]
    fresh = jnp.logical_or(i == 0, be_ref[blk] != be_ref[jnp.maximum(blk - 1, 0)])
    grp = 2 * LANES

    @pl.when(jnp.logical_and(used, fresh))
    def _():
        r = lax.broadcasted_iota(jnp.int32, (grp, grp), 0)
        c = lax.broadcasted_iota(jnp.int32, (grp, grp), 1)
        src = jnp.where(c < LANES, 2 * c, 2 * (c - LANES) + 1)
        perm = jnp.where(r == src, 1.0, 0.0).astype(BF16)
        for g in range(w1_ref.shape[1] // grp):
            blk = w1_ref[:, g * grp:(g + 1) * grp].astype(BF16)
            w1g_sc[:, g * grp:(g + 1) * grp] = jnp.dot(blk, perm, preferred_element_type=F32).astype(BF16)
        w2b_sc[...] = w2_ref[...].astype(BF16)

    @pl.when(used)
    def _():
        xb = _unpack_bf16_pairs(xs_ref[...]).astype(BF16)
        h = jnp.dot(xb, w1g_sc[...], preferred_element_type=F32) + b1_ref[...]
        acts = []
        for g in range(h.shape[1] // grp):
            x_glu = jnp.minimum(h[:, g * grp:g * grp + LANES], SWIGLU_LIMIT)
            x_lin = jnp.clip(h[:, g * grp + LANES:(g + 1) * grp], -SWIGLU_LIMIT, SWIGLU_LIMIT)
            acts.append(x_glu * (1.0 / (1.0 + jnp.exp(-SWIGLU_ALPHA * x_glu))) * (x_lin + 1.0))
        act = jnp.concatenate(acts, axis=1).astype(BF16)
        y = jnp.dot(act, w2b_sc[...], preferred_element_type=F32) + b2_ref[...]
        ys_ref[...] = _pack_bf16_pairs(y)

    @pl.when(jnp.logical_not(used))
    def _():
        ys_ref[...] = jnp.zeros(ys_ref.shape, ys_ref.dtype)


def _moe_ffn(xs, block_expert, n_used, w1, b1g, w2, b2, blk0=0, ys_prev=None):
    D = 2 * xs.shape[1]
    P = block_expert.shape[0] * MOE_BLOCK
    de2 = w1.shape[2]
    de = w2.shape[1]
    wspec = lambda r, c: pl.BlockSpec((None, r, c), lambda i, be, nu: (be[i + blk0], 0, 0))
    in_specs = [pl.BlockSpec((MOE_BLOCK, D // 2), lambda i, be, nu: (i, 0)),
                wspec(D, de2), wspec(1, de2), wspec(de, D), wspec(1, D)]
    args = [block_expert, n_used, xs, w1, b1g, w2, b2]
    aliases = {}
    if ys_prev is not None:
        in_specs.append(pl.BlockSpec(memory_space=pl.ANY))
        args.append(ys_prev)
        aliases = {len(args) - 1: 0}
    grid_spec = pltpu.PrefetchScalarGridSpec(
        num_scalar_prefetch=2,
        grid=(xs.shape[0] // MOE_BLOCK,),
        in_specs=in_specs,
        out_specs=pl.BlockSpec((MOE_BLOCK, D // 2), lambda i, be, nu: (i + blk0, 0)),
        scratch_shapes=[pltpu.VMEM((D, de2), BF16), pltpu.VMEM((de, D), BF16)],
    )
    return pl.pallas_call(
        functools.partial(_moe_kernel, blk0=blk0),
        grid_spec=grid_spec,
        out_shape=jax.ShapeDtypeStruct((P, D // 2), jnp.uint32),
        input_output_aliases=aliases,
        compiler_params=_cparams(("arbitrary",)),
        name="moe_ffn",
    )(*args)


def _combine_kernel(x1_ref, y0_ref, y1_ref, y2_ref, y3_ref, gate_ref, lng_ref, lnb_ref, o_ref, *, alpha):
    gates = gate_ref[...]
    ffn = jnp.zeros(x1_ref.shape, F32)
    for k, y_ref in enumerate((y0_ref, y1_ref, y2_ref, y3_ref)):
        ffn = ffn + _unpack_bf16_pairs(y_ref[...]) * gates[:, k:k + 1]
    o_ref[...] = _layer_norm(alpha * x1_ref[...] + ffn, lng_ref[...], lnb_ref[...])


def _combine(x1, y_slots, gates, ln_g, ln_b, alpha):
    T, D = x1.shape
    tm = min(PROJ_TM, T)
    tok = lambda width: pl.BlockSpec((tm, width), lambda i: (i, 0))
    const = lambda shape: pl.BlockSpec(shape, lambda i: (0,) * len(shape))
    slot = lambda k: pl.BlockSpec((None, tm, D // 2), lambda i: (k, i, 0))
    return pl.pallas_call(
        functools.partial(_combine_kernel, alpha=alpha),
        grid=(T // tm,),
        in_specs=[tok(D)] + [slot(k) for k in range(TOP_K)] + [tok(LANES), const((1, D)), const((1, D))],
        out_specs=tok(D),
        out_shape=jax.ShapeDtypeStruct((T, D), F32),
        compiler_params=_cparams(("parallel",)),
        name="combine",
    )(x1, y_slots, y_slots, y_slots, y_slots, gates, ln_g, ln_b)


def _route(top_i, n_experts):
    T = top_i.shape[0]
    TK = T * TOP_K
    experts = jnp.arange(n_experts, dtype=jnp.int32)
    chosen = top_i[:, :, None] == experts
    picks = jnp.sum(chosen, axis=1, dtype=jnp.int32)
    csum = jnp.cumsum(picks, axis=0)
    counts = csum[-1]
    padded = ((counts + MOE_BLOCK - 1) // MOE_BLOCK) * MOE_BLOCK
    padded_end = jnp.cumsum(padded)
    padded_start = padded_end - padded
    group_start = jnp.cumsum(counts) - counts
    slot_te = padded_start[None, :] + csum - picks
    dest = jnp.sum(jnp.where(chosen, slot_te[:, None, :], 0), axis=2)
    n_blocks = -(-TK // MOE_BLOCK) + n_experts
    blk_first = jnp.arange(n_blocks, dtype=jnp.int32) * MOE_BLOCK
    block_expert = jnp.minimum(jnp.sum(padded_end[None, :] <= blk_first[:, None], axis=1),
                               n_experts - 1).astype(jnp.int32)
    n_used = (padded_end[-1] // MOE_BLOCK).astype(jnp.int32).reshape(1)
    order = jnp.argsort(top_i.reshape(-1), stable=True).astype(jnp.int32)
    rank0 = blk_first - padded_start[block_expert]
    rank = rank0[:, None] + jnp.arange(MOE_BLOCK, dtype=jnp.int32)[None, :]
    valid = rank < counts[block_expert][:, None]
    sorted_pos = jnp.clip(group_start[block_expert][:, None] + rank, 0, TK - 1)
    filler = (blk_first[:, None] + jnp.arange(MOE_BLOCK, dtype=jnp.int32)[None, :]) % T
    src_tok = jnp.where(valid, order[sorted_pos] // TOP_K, filler).reshape(-1)
    return src_tok, block_expert, n_used, dest.T.reshape(-1)


def _rope_tables(seq):
    pos = jnp.arange(seq, dtype=F32)
    inv = ROPE_THETA ** (-jnp.arange(0, DA_HEAD_DIM, 2, dtype=F32) / DA_HEAD_DIM)
    ang = pos[:, None] * inv[None, :]
    ang = jnp.concatenate([ang, ang], axis=-1)
    cos, sin = jnp.cos(ang), jnp.sin(ang)
    half = DA_HEAD_DIM // 2
    sin_signed = jnp.concatenate([-sin[:, :half], sin[:, half:]], axis=-1)
    return jnp.tile(cos, (1, 2)), jnp.tile(sin_signed, (1, 2))


def kernel(x, w_in, b_in, lambda_q1, lambda_k1, lambda_q2, lambda_k2, subln_g, rpb, w_branch_da, w_branch_na, w_out, ln1_g, ln1_b, w_router, b_router, w_mlp1, b_mlp1, w_mlp2, b_mlp2, ln2_g, ln2_b):
    B, S, D = x.shape
    depth = w_in.shape[0]
    n_experts = w_router.shape[2]
    T = B * S
    rows = S // GRID_W
    assert S % GRID_W == 0 and rows % NA_ROWS == 0 and rows >= NA_BAND
    alpha = (2 * depth) ** 0.25
    cos, sin = _rope_tables(S)
    row = lambda v: v.reshape(1, -1)

    for l in range(depth):
        lam_init = 0.8 - 0.6 * math.exp(-0.3 * l)
        qT, k, vT, qn, kn, vn, gates_br = _in_proj(x, w_in[l].astype(BF16), row(b_in[l]), cos, sin)
        a = _diff_attn(qT, k, vT, row(lambda_q1[l]), row(lambda_k1[l]), row(lambda_q2[l]),
                       row(lambda_k2[l]), subln_g[l].reshape(-1, 1), lam_init)
        nb = _na_attn(qn, kn, vn, _na_tiles(rpb[l]))

        wr = jnp.pad(w_router[l], ((0, 0), (0, LANES - n_experts)))
        wr_h = wr.astype(BF16)
        wr = jnp.concatenate([wr_h, (wr - wr_h.astype(F32)).astype(BF16)], axis=1)
        br = jnp.pad(row(b_router[l]), ((0, 0), (0, LANES - n_experts)), constant_values=NEG_BIG)
        x1, x1b, topi, gates = _merge(
            x.reshape(T, D), a.reshape(T, -1), nb.reshape(T, -1), gates_br.reshape(T, -1),
            w_branch_da[l].astype(BF16), w_branch_na[l].astype(BF16), w_out[l].astype(BF16),
            row(ln1_g[l]), row(ln1_b[l]), wr, br, alpha)

        src_tok, block_expert, n_used, dest = _route(topi[:, :TOP_K], n_experts)
        xs = _sc_scatter_rows(x1b, dest, block_expert.shape[0] * MOE_BLOCK)
        ys = _moe_ffn(xs, block_expert, n_used, w_mlp1[l], _glu_group_bias(b_mlp1[l]),
                      w_mlp2[l], b_mlp2[l][:, None, :])
        y_slots = _sc_gather_rows(ys, dest).reshape(TOP_K, T, D // 2)
        x = _combine(x1, y_slots, gates, row(ln2_g[l]), row(ln2_b[l]), alpha).reshape(B, S, D)
    return x
```

```python
import functools
import math

import jax
import jax.numpy as jnp
from jax import lax
from jax.experimental import pallas as pl
from jax.experimental.pallas import tpu as pltpu
from jax.experimental.pallas import tpu_sc as plsc

F32 = jnp.float32
BF16 = jnp.bfloat16

GRID_W = 64
DA_HEADS = 4
DA_HEAD_DIM = 64
DA_V_DIM = 2 * DA_HEAD_DIM
DA_WIDTH = DA_HEADS * DA_V_DIM
ROPE_THETA = 10000.0
NA_HEADS = 8
NA_HEAD_DIM = 64
NA_WIDTH = NA_HEADS * NA_HEAD_DIM
NA_KH = 8
NA_KW = 16
TOP_K = 4
SWIGLU_ALPHA = 1.702
SWIGLU_LIMIT = 7.0
MOE_BLOCK = 512
LN_EPS = 1e-5
RMS_EPS = 1e-5

LANES = 128
VMEM_LIMIT_BYTES = 56 * 1024 * 1024
SC_CORES = 2
SC_SUBCORES = 16
SC_GATHER_ROWS = 64

LOG2E = math.log2(math.e)
NEG_BIG = -1e30

PROJ_TM = 512
DA_TQ = 512
DA_TK = 512
DA_UNROLL = 8
NA_ROWS = 8
NA_BAND = 16


def _cparams(sem):
    return pltpu.CompilerParams(dimension_semantics=sem, vmem_limit_bytes=VMEM_LIMIT_BYTES)


def _pack_bf16_pairs(y):
    w = y.shape[1] // 2
    bits = lax.bitcast_convert_type(y, jnp.uint32)
    rounded = bits + jnp.uint32(0x7FFF) + ((bits >> 16) & jnp.uint32(1))
    return (rounded[:, :w] >> 16) | (rounded[:, w:] & jnp.uint32(0xFFFF0000))


def _unpack_bf16_pairs(words):
    lo = lax.bitcast_convert_type(words << 16, F32)
    hi = lax.bitcast_convert_type(words & jnp.uint32(0xFFFF0000), F32)
    return jnp.concatenate([lo, hi], axis=1)


def _sc_gather_rows(table, idx):
    n, w = idx.shape[0], table.shape[1]
    rows = SC_GATHER_ROWS
    workers = SC_CORES * SC_SUBCORES
    per_worker = n // workers
    n_pairs = per_worker // (2 * rows)
    assert n % (workers * 2 * rows) == 0 and n_pairs >= 1
    mesh = plsc.VectorSubcoreMesh(core_axis_name="c", subcore_axis_name="s",
                                  num_cores=SC_CORES, num_subcores=SC_SUBCORES)

    def body(table_hbm, idx_hbm, out_hbm, idx0, idx1, rows0, rows1, sem0, sem1):
        base = (lax.axis_index("s") * SC_CORES + lax.axis_index("c")) * per_worker
        bufs = ((idx0, rows0, sem0), (idx1, rows1, sem1))

        def gather(b):
            idx_v, rows_v, sem = bufs[b]
            return pltpu.make_async_copy(table_hbm.at[idx_v], rows_v, sem)

        def start(c, b):
            pltpu.sync_copy(idx_hbm.at[pl.ds(base + c * rows, rows)], bufs[b][0])
            gather(b).start()

        def finish(c, b):
            gather(b).wait()
            pltpu.sync_copy(bufs[b][1], out_hbm.at[pl.ds(base + c * rows, rows)])

        start(0, 0)

        @pl.loop(0, n_pairs - 1)
        def _(g):
            c = 2 * g
            start(c + 1, 1)
            finish(c, 0)
            start(c + 2, 0)
            finish(c + 1, 1)

        last = 2 * (n_pairs - 1)
        start(last + 1, 1)
        finish(last, 0)
        finish(last + 1, 1)

    return pl.kernel(
        body, out_type=jax.ShapeDtypeStruct((n, w), table.dtype), mesh=mesh,
        scratch_types=[pltpu.VMEM((rows,), jnp.int32), pltpu.VMEM((rows,), jnp.int32),
                       pltpu.VMEM((rows, w), table.dtype), pltpu.VMEM((rows, w), table.dtype),
                       pltpu.SemaphoreType.DMA, pltpu.SemaphoreType.DMA],
        name="sc_gather_rows",
    )(table, idx)


def _sc_scatter_rows(table, dest, n_out):
    t, w = table.shape
    copies = dest.shape[0] // t
    rows = SC_GATHER_ROWS
    workers = SC_CORES * SC_SUBCORES
    per_worker = t // workers
    n_pairs = per_worker // (2 * rows)
    assert t % (workers * 2 * rows) == 0 and n_pairs >= 1
    mesh = plsc.VectorSubcoreMesh(core_axis_name="c", subcore_axis_name="s",
                                  num_cores=SC_CORES, num_subcores=SC_SUBCORES)

    def body(table_hbm, dest_hbm, out_hbm, *scratch):
        base = (lax.axis_index("s") * SC_CORES + lax.axis_index("c")) * per_worker
        per_buf = copies + 2
        bufs = (scratch[:per_buf], scratch[per_buf:])

        def scatters(b):
            rows_v, sem = bufs[b][0], bufs[b][1]
            return [pltpu.make_async_copy(rows_v, out_hbm.at[idx_v], sem) for idx_v in bufs[b][2:]]

        def start(c, b):
            t0 = base + c * rows
            pltpu.sync_copy(table_hbm.at[pl.ds(t0, rows)], bufs[b][0])
            for k, idx_v in enumerate(bufs[b][2:]):
                pltpu.sync_copy(dest_hbm.at[pl.ds(k * t + t0, rows)], idx_v)
            for copy in scatters(b):
                copy.start()

        def finish(b):
            for copy in scatters(b):
                copy.wait()

        start(0, 0)

        @pl.loop(0, n_pairs - 1)
        def _(g):
            start(2 * g + 1, 1)
            finish(0)
            start(2 * g + 2, 0)
            finish(1)

        start(2 * n_pairs - 1, 1)
        finish(0)
        finish(1)

    one_buf = ([pltpu.VMEM((rows, w), table.dtype), pltpu.SemaphoreType.DMA]
               + [pltpu.VMEM((rows,), jnp.int32) for _ in range(copies)])
    return pl.kernel(
        body, out_type=jax.ShapeDtypeStruct((n_out, w), table.dtype), mesh=mesh,
        scratch_types=one_buf + one_buf, name="sc_scatter_rows",
    )(table, dest)


def _in_proj_kernel(x_ref, w_ref, b_ref, cos_ref, sin_ref,
                    qT_ref, k_ref, vT_ref, qn_ref, kn_ref, vn_ref, g_ref):
    xb = x_ref[...].astype(BF16)

    def seg(lo, hi):
        return jnp.dot(xb, w_ref[:, lo:hi], preferred_element_type=F32) + b_ref[:, lo:hi]

    cos = cos_ref[...]
    sin = sin_ref[...]
    lane = lax.broadcasted_iota(jnp.int32, cos.shape, 1)
    first_half = (lane % DA_HEAD_DIM) < (DA_HEAD_DIM // 2)

    def rope(y):
        outs = []
        for h in range(DA_HEADS):
            yh = y[:, h * LANES:(h + 1) * LANES]
            partner = jnp.where(first_half,
                                pltpu.roll(yh, LANES - DA_HEAD_DIM // 2, 1),
                                pltpu.roll(yh, DA_HEAD_DIM // 2, 1))
            outs.append(yh * cos + partner * sin)
        return jnp.concatenate(outs, axis=1)

    w = DA_WIDTH
    q = rope(seg(0, w)) * (DA_HEAD_DIM ** -0.5 * LOG2E)
    qT_ref[...] = q.T.astype(BF16)
    k_ref[...] = rope(seg(w, 2 * w)).astype(BF16)
    vT_ref[...] = seg(2 * w, 3 * w).T.astype(BF16)
    o = 3 * w
    qn_ref[...] = (seg(o, o + NA_WIDTH) * (NA_HEAD_DIM ** -0.5)).T.astype(BF16)
    kn_ref[...] = seg(o + NA_WIDTH, o + 2 * NA_WIDTH).astype(BF16)
    vn_ref[...] = seg(o + 2 * NA_WIDTH, o + 3 * NA_WIDTH).T.astype(BF16)
    g0 = o + 3 * NA_WIDTH
    gate_pre = seg(g0, w_ref.shape[1])
    g_ref[...] = (1.0 / (1.0 + jnp.exp(-gate_pre))).astype(BF16)


def _in_proj(x, w_in, b_in, cos, sin):
    B, S, D = x.shape
    tm = min(PROJ_TM, S)
    n_cols = w_in.shape[1]
    n_gate = n_cols - 3 * DA_WIDTH - 3 * NA_WIDTH
    tok = lambda width: pl.BlockSpec((None, tm, width), lambda b, i: (b, i, 0))
    tr = pl.BlockSpec((None, DA_WIDTH, tm), lambda b, i: (b, 0, i))
    const = lambda shape: pl.BlockSpec(shape, lambda b, i: (0,) * len(shape))
    out_shape = (
        jax.ShapeDtypeStruct((B, DA_WIDTH, S), BF16),
        jax.ShapeDtypeStruct((B, S, DA_WIDTH), BF16),
        jax.ShapeDtypeStruct((B, DA_WIDTH, S), BF16),
        jax.ShapeDtypeStruct((B, NA_WIDTH, S), BF16),
        jax.ShapeDtypeStruct((B, S, NA_WIDTH), BF16),
        jax.ShapeDtypeStruct((B, NA_WIDTH, S), BF16),
        jax.ShapeDtypeStruct((B, S, n_gate), BF16),
    )
    assert DA_WIDTH == NA_WIDTH
    return pl.pallas_call(
        _in_proj_kernel,
        grid=(B, S // tm),
        in_specs=[tok(D), const((D, n_cols)), const((1, n_cols)),
                  pl.BlockSpec((tm, LANES), lambda b, i: (i, 0)),
                  pl.BlockSpec((tm, LANES), lambda b, i: (i, 0))],
        out_specs=(tr, tok(DA_WIDTH), tr, tr, tok(NA_WIDTH), tr, tok(n_gate)),
        out_shape=out_shape,
        compiler_params=_cparams(("parallel", "parallel")),
        name="in_proj",
    )(x, w_in, b_in, cos, sin)


def _diff_attn_kernel(qT_ref, qTn_ref, k_ref, vT_ref, lq1_ref, lk1_ref, lq2_ref, lk2_ref, g_ref,
                      o_ref, m_sc, l_sc, acc_sc, s_sc, cmax_sc, *, tk, unroll, lam_init):
    tq = qT_ref.shape[1]

    def block_diag(q_ref):
        qT = q_ref[...]
        row = lax.broadcasted_iota(jnp.int32, qT.shape, 0)
        zero = jnp.zeros_like(qT)
        return jnp.concatenate([jnp.where(row < DA_HEAD_DIM, qT, zero),
                                jnp.where(row < DA_HEAD_DIM, zero, qT)], axis=1)

    qbd = block_diag(qT_ref)

    m_sc[...] = jnp.full(m_sc.shape, -jnp.inf, F32)
    l_sc[...] = jnp.zeros(l_sc.shape, F32)
    acc_sc[...] = jnp.zeros(acc_sc.shape, F32)

    nk = k_ref.shape[0] // tk

    def put_scores(slot, j, q=None):
        off = pl.multiple_of(j * tk, tk)
        s = jnp.dot(k_ref[pl.ds(off, tk), :], qbd if q is None else q, preferred_element_type=F32)
        s_sc[slot] = s
        cmax_sc[slot] = jnp.max(s, axis=0, keepdims=True)

    def accumulate(slot, j):
        off = pl.multiple_of(j * tk, tk)
        m_prev = m_sc[...]
        m_new = jnp.maximum(m_prev, cmax_sc[slot])
        alpha = jnp.exp2(m_prev - m_new)
        p = jnp.exp2(s_sc[slot] - m_new)
        l_sc[...] = alpha * l_sc[...] + jnp.sum(p, axis=0, keepdims=True)
        vb = vT_ref[:, pl.ds(off, tk)]
        acc_sc[...] = alpha * acc_sc[...] + jnp.dot(vb, p.astype(BF16), preferred_element_type=F32)
        m_sc[...] = m_new

    @pl.when(pl.program_id(2) == 0)
    def _():
        put_scores(0, 0)

    def body(jj, carry):
        for u in range(unroll):
            j = unroll * jj + u
            put_scores((u + 1) % 2, j + 1)
            accumulate(u % 2, j)
        return carry

    lax.fori_loop(0, nk // unroll - 1, body, 0)
    for j in range(nk - unroll, nk):
        if j + 1 < nk:
            put_scores((j + 1) % 2, j + 1)
        else:
            put_scores(0, 0, block_diag(qTn_ref))
        accumulate(j % 2, j)

    lam = (jnp.exp(jnp.sum(lq1_ref[...] * lk1_ref[...], axis=1, keepdims=True))
           - jnp.exp(jnp.sum(lq2_ref[...] * lk2_ref[...], axis=1, keepdims=True)) + lam_init)
    on = acc_sc[...] / l_sc[...]
    o = on[:, :tq] - lam * on[:, tq:]
    ms = jnp.mean(o * o, axis=0, keepdims=True)
    o = o * lax.rsqrt(ms + RMS_EPS) * g_ref[...]
    o = o * (1.0 - lam_init)
    o_ref[...] = o.T.astype(BF16)


def _diff_attn(qT, k, vT, lq1, lk1, lq2, lk2, subln_g, lam_init):
    B, S, _ = k.shape
    tq = min(DA_TQ, S)
    tk = min(DA_TK, S // 2)
    unroll = math.gcd(DA_UNROLL, S // tk)
    assert unroll % 2 == 0 and S % tk == 0 and S % tq == 0
    nq = S // tq
    vec = pl.BlockSpec((1, DA_HEAD_DIM), lambda b, h, i: (0, 0))
    kernel = functools.partial(_diff_attn_kernel, tk=tk, unroll=unroll, lam_init=lam_init)
    return pl.pallas_call(
        kernel,
        grid=(B, DA_HEADS, nq),
        in_specs=[pl.BlockSpec((None, DA_V_DIM, tq), lambda b, h, i: (b, h, i)),
                  pl.BlockSpec((None, DA_V_DIM, tq), lambda b, h, i: (b, h, jnp.minimum(i + 1, nq - 1))),
                  pl.BlockSpec((None, S, DA_V_DIM), lambda b, h, i: (b, 0, h)),
                  pl.BlockSpec((None, DA_V_DIM, S), lambda b, h, i: (b, h, 0)),
                  vec, vec, vec, vec,
                  pl.BlockSpec((DA_V_DIM, 1), lambda b, h, i: (0, 0))],
        out_specs=pl.BlockSpec((None, tq, DA_V_DIM), lambda b, h, i: (b, i, h)),
        out_shape=jax.ShapeDtypeStruct((B, S, DA_WIDTH), BF16),
        scratch_shapes=[pltpu.VMEM((1, 2 * tq), F32), pltpu.VMEM((1, 2 * tq), F32),
                        pltpu.VMEM((DA_V_DIM, 2 * tq), F32), pltpu.VMEM((2, tk, 2 * tq), F32),
                        pltpu.VMEM((2, 1, 2 * tq), F32)],
        compiler_params=_cparams(("parallel", "parallel", "arbitrary")),
        name="diff_attn",
    )(qT, qT, k, vT, lq1, lk1, lq2, lk2, subln_g)


def _na_band_start(r0, rows):
    return jnp.clip(r0 - NA_KH // 2, 0, rows - NA_BAND)


def _na_tiles(rpb):
    cols = jnp.arange(GRID_W)
    cs = jnp.clip(cols - NA_KW // 2, 0, GRID_W - NA_KW)
    col_ok = (cols[:, None] >= cs[None, :]) & (cols[:, None] < cs[None, :] + NA_KW)
    col_off = jnp.clip(cols[:, None] - cols[None, :] + (NA_KW - 1), 0, 2 * NA_KW - 2)
    tiles = jnp.where(col_ok[None, None], rpb.astype(F32)[:, :, col_off], NEG_BIG)
    neg = jnp.full((rpb.shape[0], 1, GRID_W, GRID_W), NEG_BIG, F32)
    return jnp.concatenate([jnp.concatenate([tiles, neg], axis=1),
                            jnp.concatenate([neg, tiles], axis=1)], axis=3)


def _na_geometry(variant, rows):
    nblk = rows // NA_ROWS
    r0 = (0, NA_ROWS * min(1, nblk - 1), NA_ROWS * (nblk - 1))[variant]
    band = min(max(r0 - NA_KH // 2, 0), rows - NA_BAND)
    pairs = []
    for qp in range(NA_ROWS // 2):
        entries = []
        for kr in range(NA_BAND):
            ok = []
            for qr in (r0 + 2 * qp, r0 + 2 * qp + 1):
                rs = min(max(qr - NA_KH // 2, 0), rows - NA_KH)
                ok.append(rs <= band + kr < rs + NA_KH)
            if ok[0] or ok[1]:
                entries.append((kr, band + kr - (r0 + 2 * qp) + NA_KH - 1, ok[0], ok[1]))
        pairs.append(entries)
    return pairs


def _na_kernel(qT_ref, k_ref, vT_ref, tile_ref, o_ref, *, rows):
    i = pl.program_id(2)
    nblk = rows // NA_ROWS
    nk = NA_BAND * GRID_W
    off = pl.multiple_of(_na_band_start(i * NA_ROWS, rows) * GRID_W, 2 * GRID_W)

    def block(variant):
        kb = k_ref[pl.ds(off, nk), :]
        vT = vT_ref[:, pl.ds(off, nk)]
        lane = lax.broadcasted_iota(jnp.int32, kb.shape, 1)
        kbd = jnp.concatenate([jnp.where(lane < NA_HEAD_DIM, kb, jnp.zeros_like(kb)),
                               jnp.where(lane < NA_HEAD_DIM, jnp.zeros_like(kb), kb)], axis=0)
        row = lax.broadcasted_iota(jnp.int32, vT.shape, 0)
        vbdT = jnp.concatenate([jnp.where(row < NA_HEAD_DIM, vT, jnp.zeros_like(vT)),
                                jnp.where(row < NA_HEAD_DIM, jnp.zeros_like(vT), vT)], axis=1)
        s = jnp.dot(kbd, qT_ref[...], preferred_element_type=F32)
        left = lax.broadcasted_iota(jnp.int32, (GRID_W, LANES), 1) < GRID_W
        zero_slab = jnp.zeros((GRID_W, LANES), BF16)
        geometry = _na_geometry(variant, rows)
        p_heads, l_heads = [], []
        for h in range(2):
            p_cols, l_cols = [], []
            for qp, entries in enumerate(geometry):
                logits = {}
                for kr, d, ok_l, ok_r in entries:
                    bias = tile_ref[h, d]
                    if not ok_r:
                        bias = jnp.where(left, bias, NEG_BIG)
                    if not ok_l:
                        bias = jnp.where(left, NEG_BIG, bias)
                    r = h * nk + kr * GRID_W
                    logits[kr] = s[r:r + GRID_W, qp * LANES:(qp + 1) * LANES] + bias
                m = functools.reduce(jnp.maximum, logits.values())
                m = jnp.max(m, axis=0, keepdims=True)
                probs = {kr: jnp.exp(x - m) for kr, x in logits.items()}
                l_cols.append(jnp.sum(functools.reduce(jnp.add, probs.values()), axis=0, keepdims=True))
                p_cols.append(jnp.concatenate(
                    [probs[kr].astype(BF16) if kr in probs else zero_slab for kr in range(NA_BAND)], axis=0))
            p_heads.append(jnp.concatenate(p_cols, axis=1))
            l_heads.append(jnp.concatenate(l_cols, axis=1))
        oT = jnp.dot(vbdT, jnp.concatenate(p_heads, axis=0), preferred_element_type=F32)
        row_o = lax.broadcasted_iota(jnp.int32, oT.shape, 0)
        oT = oT / jnp.where(row_o < NA_HEAD_DIM, l_heads[0], l_heads[1])
        o_ref[...] = oT.T.astype(BF16)

    pl.when(i == 0)(lambda: block(0))
    pl.when(jnp.logical_and(i > 0, i < nblk - 1))(lambda: block(1))
    pl.when(jnp.logical_and(i > 0, i == nblk - 1))(lambda: block(2))


def _na_attn(qnT, kn, vnT, tiles):
    B, S, _ = kn.shape
    rows = S // GRID_W
    nblk = rows // NA_ROWS
    nq = NA_ROWS * GRID_W
    return pl.pallas_call(
        functools.partial(_na_kernel, rows=rows),
        grid=(B, NA_HEADS // 2, nblk),
        in_specs=[pl.BlockSpec((None, LANES, nq), lambda b, h, i: (b, h, i)),
                  pl.BlockSpec((None, S, LANES), lambda b, h, i: (b, 0, h)),
                  pl.BlockSpec((None, LANES, S), lambda b, h, i: (b, h, 0)),
                  pl.BlockSpec((2,) + tiles.shape[1:], lambda b, h, i: (h, 0, 0, 0))],
        out_specs=pl.BlockSpec((None, nq, LANES), lambda b, h, i: (b, i, h)),
        out_shape=jax.ShapeDtypeStruct((B, S, NA_WIDTH), BF16),
        compiler_params=_cparams(("parallel", "parallel", "arbitrary")),
        name="na_attn",
    )(qnT, kn, vnT, tiles)


def _layer_norm(h, g, b):
    mu = jnp.mean(h, axis=-1, keepdims=True)
    d = h - mu
    var = jnp.mean(d * d, axis=-1, keepdims=True)
    return d * lax.rsqrt(var + LN_EPS) * g + b


def _merge_kernel(x_ref, a_ref, nb_ref, g_ref, wda_ref, wna_ref, wout_ref, lng_ref, lnb_ref,
                  wr_ref, br_ref, x1_ref, x1p_ref, topi_ref, gate_ref, *, alpha):
    d = x_ref.shape[1]
    ya = jnp.dot(a_ref[...], wda_ref[...], preferred_element_type=F32)
    yb = jnp.dot(nb_ref[...], wna_ref[...], preferred_element_type=F32)
    g = g_ref[...].astype(F32)
    merged = g[:, :d] * ya + g[:, d:] * yb
    mix = jnp.dot(merged.astype(BF16), wout_ref[...], preferred_element_type=F32)
    x1 = _layer_norm(alpha * x_ref[...] + mix, lng_ref[...], lnb_ref[...])
    x1_ref[...] = x1
    x1p_ref[...] = _pack_bf16_pairs(x1)
    tm = x1.shape[0]
    xh = x1.astype(BF16)
    xl = (x1 - xh.astype(F32)).astype(BF16)
    r = jnp.dot(jnp.concatenate([xh, xl], axis=0), wr_ref[...], preferred_element_type=F32)
    logits = (r[:tm, :LANES] + r[:tm, LANES:]) + (r[tm:, :LANES] + r[tm:, LANES:]) + br_ref[...]
    lane = lax.broadcasted_iota(jnp.int32, logits.shape, 1).astype(F32)
    vals, idxs = [], []
    for _ in range(TOP_K):
        m = jnp.max(logits, axis=1, keepdims=True)
        idx = jnp.min(jnp.where(logits == m, lane, float(LANES)), axis=1, keepdims=True)
        vals.append(m)
        idxs.append(idx)
        logits = jnp.where(lane == idx, -jnp.inf, logits)
    es = [jnp.exp(v - vals[0]) for v in vals]
    denom = es[0] + es[1] + es[2] + es[3]
    gates = jnp.zeros(logits.shape, F32)
    topi = jnp.zeros(logits.shape, F32)
    for k in range(TOP_K):
        gates = jnp.where(lane == k, es[k] / denom, gates)
        topi = jnp.where(lane == k, idxs[k], topi)
    gate_ref[...] = gates
    topi_ref[...] = topi.astype(jnp.int32)


def _merge(x2, a2, nb2, g2, wda, wna, wout, ln_g, ln_b, wr, br, alpha):
    T, D = x2.shape
    tm = min(PROJ_TM, T)
    tok = lambda width: pl.BlockSpec((tm, width), lambda i: (i, 0))
    const = lambda shape: pl.BlockSpec(shape, lambda i: (0,) * len(shape))
    return pl.pallas_call(
        functools.partial(_merge_kernel, alpha=alpha),
        grid=(T // tm,),
        in_specs=[tok(D), tok(a2.shape[1]), tok(nb2.shape[1]), tok(g2.shape[1]),
                  const(wda.shape), const(wna.shape), const(wout.shape),
                  const((1, D)), const((1, D)), const(wr.shape), const((1, LANES))],
        out_specs=(tok(D), tok(D // 2), tok(LANES), tok(LANES)),
        out_shape=(jax.ShapeDtypeStruct((T, D), F32), jax.ShapeDtypeStruct((T, D // 2), jnp.uint32),
                   jax.ShapeDtypeStruct((T, LANES), jnp.int32), jax.ShapeDtypeStruct((T, LANES), F32)),
        compiler_params=_cparams(("parallel",)),
        name="merge",
    )(x2, a2, nb2, g2, wda, wna, wout, ln_g, ln_b, wr, br)


def _glu_group_bias(b1):
    e, n = b1.shape
    return b1.reshape(e, n // (2 * LANES), LANES, 2).transpose(0, 1, 3, 2).reshape(e, 1, n)


def _moe_kernel(be_ref, nused_ref, xs_ref, w1_ref, b1_ref, w2_ref, b2_ref, ys_ref, w1g_sc, w2b_sc):
    i = pl.program_id(0)
    used = i < nused_ref[0]
    fresh = jnp.logical_or(i == 0, be_ref[i] != be_ref[jnp.maximum(i - 1, 0)])
    grp = 2 * LANES

    @pl.when(jnp.logical_and(used, fresh))
    def _():
        r = lax.broadcasted_iota(jnp.int32, (grp, grp), 0)
        c = lax.broadcasted_iota(jnp.int32, (grp, grp), 1)
        src = jnp.where(c < LANES, 2 * c, 2 * (c - LANES) + 1)
        perm = jnp.where(r == src, 1.0, 0.0).astype(BF16)
        for g in range(w1_ref.shape[1] // grp):
            cols = w1_ref[:, g * grp:(g + 1) * grp].astype(BF16)
            w1g_sc[:, g * grp:(g + 1) * grp] = jnp.dot(cols, perm, preferred_element_type=F32).astype(BF16)
        w2b_sc[...] = w2_ref[...].astype(BF16)

    @pl.when(used)
    def _():
        xb = _unpack_bf16_pairs(xs_ref[...]).astype(BF16)
        h = jnp.dot(xb, w1g_sc[...], preferred_element_type=F32) + b1_ref[...]
        acts = []
        for g in range(h.shape[1] // grp):
            x_glu = jnp.minimum(h[:, g * grp:g * grp + LANES], SWIGLU_LIMIT)
            x_lin = jnp.clip(h[:, g * grp + LANES:(g + 1) * grp], -SWIGLU_LIMIT, SWIGLU_LIMIT)
            acts.append(x_glu * (1.0 / (1.0 + jnp.exp(-SWIGLU_ALPHA * x_glu))) * (x_lin + 1.0))
        act = jnp.concatenate(acts, axis=1).astype(BF16)
        y = jnp.dot(act, w2b_sc[...], preferred_element_type=F32) + b2_ref[...]
        ys_ref[...] = _pack_bf16_pairs(y)

    @pl.when(jnp.logical_not(used))
    def _():
        ys_ref[...] = jnp.zeros(ys_ref.shape, ys_ref.dtype)


def _moe_ffn(xs, block_expert, n_used, w1, b1g, w2, b2):
    P, half = xs.shape
    D = 2 * half
    de2 = w1.shape[2]
    de = w2.shape[1]
    wspec = lambda r, c: pl.BlockSpec((None, r, c), lambda i, be, nu: (be[i], 0, 0))
    grid_spec = pltpu.PrefetchScalarGridSpec(
        num_scalar_prefetch=2,
        grid=(P // MOE_BLOCK,),
        in_specs=[pl.BlockSpec((MOE_BLOCK, half), lambda i, be, nu: (i, 0)),
                  wspec(D, de2), wspec(1, de2), wspec(de, D), wspec(1, D)],
        out_specs=pl.BlockSpec((MOE_BLOCK, half), lambda i, be, nu: (i, 0)),
        scratch_shapes=[pltpu.VMEM((D, de2), BF16), pltpu.VMEM((de, D), BF16)],
    )
    return pl.pallas_call(
        _moe_kernel,
        grid_spec=grid_spec,
        out_shape=jax.ShapeDtypeStruct((P, half), jnp.uint32),
        compiler_params=_cparams(("arbitrary",)),
        name="moe_ffn",
    )(block_expert, n_used, xs, w1, b1g, w2, b2)


def _combine_kernel(x1_ref, y0_ref, y1_ref, y2_ref, y3_ref, gate_ref, lng_ref, lnb_ref, o_ref, *, alpha):
    gates = gate_ref[...]
    ffn = jnp.zeros(x1_ref.shape, F32)
    for k, y_ref in enumerate((y0_ref, y1_ref, y2_ref, y3_ref)):
        ffn = ffn + _unpack_bf16_pairs(y_ref[...]) * gates[:, k:k + 1]
    o_ref[...] = _layer_norm(alpha * x1_ref[...] + ffn, lng_ref[...], lnb_ref[...])


def _combine(x1, y_slots, gates, ln_g, ln_b, alpha):
    T, D = x1.shape
    tm = min(PROJ_TM, T)
    tok = lambda width: pl.BlockSpec((tm, width), lambda i: (i, 0))
    const = lambda shape: pl.BlockSpec(shape, lambda i: (0,) * len(shape))
    slot = lambda k: pl.BlockSpec((None, tm, D // 2), lambda i: (k, i, 0))
    return pl.pallas_call(
        functools.partial(_combine_kernel, alpha=alpha),
        grid=(T // tm,),
        in_specs=[tok(D)] + [slot(k) for k in range(TOP_K)] + [tok(LANES), const((1, D)), const((1, D))],
        out_specs=tok(D),
        out_shape=jax.ShapeDtypeStruct((T, D), F32),
        compiler_params=_cparams(("parallel",)),
        name="combine",
    )(x1, y_slots, y_slots, y_slots, y_slots, gates, ln_g, ln_b)


def _route(top_i, n_experts):
    T = top_i.shape[0]
    TK = T * TOP_K
    experts = jnp.arange(n_experts, dtype=jnp.int32)
    chosen = top_i[:, :, None] == experts
    picks = jnp.sum(chosen, axis=1, dtype=jnp.int32)
    csum = jnp.cumsum(picks, axis=0)
    counts = csum[-1]
    padded = ((counts + MOE_BLOCK - 1) // MOE_BLOCK) * MOE_BLOCK
    padded_end = jnp.cumsum(padded)
    padded_start = padded_end - padded
    slot_te = padded_start[None, :] + csum - picks
    dest = jnp.sum(jnp.where(chosen, slot_te[:, None, :], 0), axis=2)
    n_blocks = -(-TK // MOE_BLOCK) + n_experts
    blk_first = jnp.arange(n_blocks, dtype=jnp.int32) * MOE_BLOCK
    block_expert = jnp.minimum(jnp.sum(padded_end[None, :] <= blk_first[:, None], axis=1),
                               n_experts - 1).astype(jnp.int32)
    n_used = (padded_end[-1] // MOE_BLOCK).astype(jnp.int32).reshape(1)
    return block_expert, n_used, dest.T.reshape(-1)


def _rope_tables(seq):
    pos = jnp.arange(seq, dtype=F32)
    inv = ROPE_THETA ** (-jnp.arange(0, DA_HEAD_DIM, 2, dtype=F32) / DA_HEAD_DIM)
    ang = pos[:, None] * inv[None, :]
    ang = jnp.concatenate([ang, ang], axis=-1)
    cos, sin = jnp.cos(ang), jnp.sin(ang)
    half = DA_HEAD_DIM // 2
    sin_signed = jnp.concatenate([-sin[:, :half], sin[:, half:]], axis=-1)
    return jnp.tile(cos, (1, 2)), jnp.tile(sin_signed, (1, 2))


def kernel(x, w_in, b_in, lambda_q1, lambda_k1, lambda_q2, lambda_k2, subln_g, rpb, w_branch_da, w_branch_na, w_out, ln1_g, ln1_b, w_router, b_router, w_mlp1, b_mlp1, w_mlp2, b_mlp2, ln2_g, ln2_b):
    B, S, D = x.shape
    depth = w_in.shape[0]
    n_experts = w_router.shape[2]
    T = B * S
    rows = S // GRID_W
    assert S % GRID_W == 0 and rows % NA_ROWS == 0 and rows >= NA_BAND
    alpha = (2 * depth) ** 0.25
    cos, sin = _rope_tables(S)
    row = lambda v: v.reshape(1, -1)

    for l in range(depth):
        lam_init = 0.8 - 0.6 * math.exp(-0.3 * l)
        qT, k, vT, qn, kn, vn, gates_br = _in_proj(x, w_in[l].astype(BF16), row(b_in[l]), cos, sin)
        a = _diff_attn(qT, k, vT, row(lambda_q1[l]), row(lambda_k1[l]), row(lambda_q2[l]),
                       row(lambda_k2[l]), subln_g[l].reshape(-1, 1), lam_init)
        nb = _na_attn(qn, kn, vn, _na_tiles(rpb[l]))

        wr = jnp.pad(w_router[l], ((0, 0), (0, LANES - n_experts)))
        wr_h = wr.astype(BF16)
        wr = jnp.concatenate([wr_h, (wr - wr_h.astype(F32)).astype(BF16)], axis=1)
        br = jnp.pad(row(b_router[l]), ((0, 0), (0, LANES - n_experts)), constant_values=NEG_BIG)
        x1, x1p, topi, gates = _merge(
            x.reshape(T, D), a.reshape(T, -1), nb.reshape(T, -1), gates_br.reshape(T, -1),
            w_branch_da[l].astype(BF16), w_branch_na[l].astype(BF16), w_out[l].astype(BF16),
            row(ln1_g[l]), row(ln1_b[l]), wr, br, alpha)

        block_expert, n_used, dest = _route(topi[:, :TOP_K], n_experts)
        xs = _sc_scatter_rows(x1p, dest, block_expert.shape[0] * MOE_BLOCK)
        ys = _moe_ffn(xs, block_expert, n_used, w_mlp1[l], _glu_group_bias(b_mlp1[l]),
                      w_mlp2[l], b_mlp2[l][:, None, :])
        y_slots = _sc_gather_rows(ys, dest).reshape(TOP_K, T, D // 2)
        x = _combine(x1, y_slots, gates, row(ln2_g[l]), row(ln2_b[l]), alpha).reshape(B, S, D)
    return x
```

```python
import functools
import math

import jax
import jax.numpy as jnp
from jax import lax
from jax.experimental import pallas as pl
from jax.experimental.pallas import tpu as pltpu
from jax.experimental.pallas import tpu_sc as plsc

F32 = jnp.float32
BF16 = jnp.bfloat16

GRID_W = 64
DA_HEADS = 4
DA_HEAD_DIM = 64
DA_V_DIM = 2 * DA_HEAD_DIM
DA_WIDTH = DA_HEADS * DA_V_DIM
ROPE_THETA = 10000.0
NA_HEADS = 8
NA_HEAD_DIM = 64
NA_WIDTH = NA_HEADS * NA_HEAD_DIM
NA_KH = 8
NA_KW = 16
TOP_K = 4
SWIGLU_ALPHA = 1.702
SWIGLU_LIMIT = 7.0
MOE_BLOCK = 512
LN_EPS = 1e-5
RMS_EPS = 1e-5

LANES = 128
VMEM_LIMIT_BYTES = 56 * 1024 * 1024
SC_CORES = 2
SC_SUBCORES = 16
SC_GATHER_ROWS = 64

LOG2E = math.log2(math.e)
NEG_BIG = -1e30

PROJ_TM = 512
DA_TQ = 512
DA_TK = 512
DA_UNROLL = 8
NA_ROWS = 8
NA_BAND = 16


def _cparams(sem):
    return pltpu.CompilerParams(dimension_semantics=sem, vmem_limit_bytes=VMEM_LIMIT_BYTES)


def _pack_bf16_pairs(y):
    w = y.shape[1] // 2
    bits = lax.bitcast_convert_type(y, jnp.uint32)
    rounded = bits + jnp.uint32(0x7FFF) + ((bits >> 16) & jnp.uint32(1))
    return (rounded[:, :w] >> 16) | (rounded[:, w:] & jnp.uint32(0xFFFF0000))


def _unpack_bf16_pairs(words):
    lo = lax.bitcast_convert_type(words << 16, F32)
    hi = lax.bitcast_convert_type(words & jnp.uint32(0xFFFF0000), F32)
    return jnp.concatenate([lo, hi], axis=1)


def _sc_gather_rows(table, idx):
    n, w = idx.shape[0], table.shape[1]
    rows = SC_GATHER_ROWS
    workers = SC_CORES * SC_SUBCORES
    per_worker = n // workers
    n_pairs = per_worker // (2 * rows)
    assert n % (workers * 2 * rows) == 0 and n_pairs >= 1
    mesh = plsc.VectorSubcoreMesh(core_axis_name="c", subcore_axis_name="s",
                                  num_cores=SC_CORES, num_subcores=SC_SUBCORES)

    def body(table_hbm, idx_hbm, out_hbm, idx0, idx1, rows0, rows1, sem0, sem1):
        base = (lax.axis_index("s") * SC_CORES + lax.axis_index("c")) * per_worker
        bufs = ((idx0, rows0, sem0), (idx1, rows1, sem1))

        def gather(b):
            idx_v, rows_v, sem = bufs[b]
            return pltpu.make_async_copy(table_hbm.at[idx_v], rows_v, sem)

        def start(c, b):
            pltpu.sync_copy(idx_hbm.at[pl.ds(base + c * rows, rows)], bufs[b][0])
            gather(b).start()

        def finish(c, b):
            gather(b).wait()
            pltpu.sync_copy(bufs[b][1], out_hbm.at[pl.ds(base + c * rows, rows)])

        start(0, 0)

        @pl.loop(0, n_pairs - 1)
        def _(g):
            c = 2 * g
            start(c + 1, 1)
            finish(c, 0)
            start(c + 2, 0)
            finish(c + 1, 1)

        last = 2 * (n_pairs - 1)
        start(last + 1, 1)
        finish(last, 0)
        finish(last + 1, 1)

    return pl.kernel(
        body, out_type=jax.ShapeDtypeStruct((n, w), table.dtype), mesh=mesh,
        scratch_types=[pltpu.VMEM((rows,), jnp.int32), pltpu.VMEM((rows,), jnp.int32),
                       pltpu.VMEM((rows, w), table.dtype), pltpu.VMEM((rows, w), table.dtype),
                       pltpu.SemaphoreType.DMA, pltpu.SemaphoreType.DMA],
        name="sc_gather_rows",
    )(table, idx)


def _sc_scatter_rows(table, dest, rest):
    t, w = table.shape
    copies = dest.shape[0] // t
    n_out = dest.shape[0] + rest.shape[0]
    rows = SC_GATHER_ROWS
    workers = SC_CORES * SC_SUBCORES
    per_worker = t // workers
    n_pairs = per_worker // (2 * rows)
    rest_per_worker = rest.shape[0] // workers
    assert t % (workers * 2 * rows) == 0 and n_pairs >= 1 and rest.shape[0] % (workers * rows) == 0
    mesh = plsc.VectorSubcoreMesh(core_axis_name="c", subcore_axis_name="s",
                                  num_cores=SC_CORES, num_subcores=SC_SUBCORES)

    def body(table_hbm, dest_hbm, rest_hbm, zeros_hbm, out_hbm, *scratch):
        worker = lax.axis_index("s") * SC_CORES + lax.axis_index("c")
        base = worker * per_worker
        per_buf = copies + 2
        bufs = (scratch[:per_buf], scratch[per_buf:])

        def scatters(b):
            rows_v, sem = bufs[b][0], bufs[b][1]
            return [pltpu.make_async_copy(rows_v, out_hbm.at[idx_v], sem) for idx_v in bufs[b][2:]]

        def start(c, b):
            t0 = base + c * rows
            pltpu.sync_copy(table_hbm.at[pl.ds(t0, rows)], bufs[b][0])
            for k, idx_v in enumerate(bufs[b][2:]):
                pltpu.sync_copy(dest_hbm.at[pl.ds(k * t + t0, rows)], idx_v)
            for copy in scatters(b):
                copy.start()

        def finish(b):
            for copy in scatters(b):
                copy.wait()

        start(0, 0)

        @pl.loop(0, n_pairs - 1)
        def _(g):
            start(2 * g + 1, 1)
            finish(0)
            start(2 * g + 2, 0)
            finish(1)

        start(2 * n_pairs - 1, 1)
        finish(0)
        finish(1)

        rows_v, sem, idx_v = bufs[0][:3]
        pltpu.sync_copy(zeros_hbm, rows_v)

        @pl.loop(0, rest_per_worker // rows)
        def _(c):
            pltpu.sync_copy(rest_hbm.at[pl.ds(worker * rest_per_worker + c * rows, rows)], idx_v)
            fill = pltpu.make_async_copy(rows_v, out_hbm.at[idx_v], sem)
            fill.start()
            fill.wait()

    one_buf = ([pltpu.VMEM((rows, w), table.dtype), pltpu.SemaphoreType.DMA]
               + [pltpu.VMEM((rows,), jnp.int32) for _ in range(copies)])
    return pl.kernel(
        body, out_type=jax.ShapeDtypeStruct((n_out, w), table.dtype), mesh=mesh,
        scratch_types=one_buf + one_buf, name="sc_scatter_rows",
    )(table, dest, rest, jnp.zeros((rows, w), table.dtype))


def _in_proj_kernel(x_ref, w_ref, b_ref, cos_ref, sin_ref,
                    qT_ref, k_ref, vT_ref, qn_ref, kn_ref, vn_ref, g_ref):
    xb = x_ref[...].astype(BF16)

    def seg(lo, hi):
        return jnp.dot(xb, w_ref[:, lo:hi], preferred_element_type=F32) + b_ref[:, lo:hi]

    cos = cos_ref[...]
    sin = sin_ref[...]
    lane = lax.broadcasted_iota(jnp.int32, cos.shape, 1)
    first_half = (lane % DA_HEAD_DIM) < (DA_HEAD_DIM // 2)

    def rope(y):
        outs = []
        for h in range(DA_HEADS):
            yh = y[:, h * LANES:(h + 1) * LANES]
            partner = jnp.where(first_half,
                                pltpu.roll(yh, LANES - DA_HEAD_DIM // 2, 1),
                                pltpu.roll(yh, DA_HEAD_DIM // 2, 1))
            outs.append(yh * cos + partner * sin)
        return jnp.concatenate(outs, axis=1)

    w = DA_WIDTH
    q = rope(seg(0, w)) * (DA_HEAD_DIM ** -0.5 * LOG2E)
    qT_ref[...] = q.T.astype(BF16)
    k_ref[...] = rope(seg(w, 2 * w)).astype(BF16)
    vT_ref[...] = seg(2 * w, 3 * w).T.astype(BF16)
    o = 3 * w
    qn_ref[...] = (seg(o, o + NA_WIDTH) * (NA_HEAD_DIM ** -0.5)).T.astype(BF16)
    kn_ref[...] = seg(o + NA_WIDTH, o + 2 * NA_WIDTH).astype(BF16)
    vn_ref[...] = seg(o + 2 * NA_WIDTH, o + 3 * NA_WIDTH).T.astype(BF16)
    g0 = o + 3 * NA_WIDTH
    gate_pre = seg(g0, w_ref.shape[1])
    g_ref[...] = (1.0 / (1.0 + jnp.exp(-gate_pre))).astype(BF16)


def _in_proj(x, w_in, b_in, cos, sin):
    B, S, D = x.shape
    tm = min(PROJ_TM, S)
    n_cols = w_in.shape[1]
    n_gate = n_cols - 3 * DA_WIDTH - 3 * NA_WIDTH
    tok = lambda width: pl.BlockSpec((None, tm, width), lambda b, i: (b, i, 0))
    tr = pl.BlockSpec((None, DA_WIDTH, tm), lambda b, i: (b, 0, i))
    const = lambda shape: pl.BlockSpec(shape, lambda b, i: (0,) * len(shape))
    out_shape = (
        jax.ShapeDtypeStruct((B, DA_WIDTH, S), BF16),
        jax.ShapeDtypeStruct((B, S, DA_WIDTH), BF16),
        jax.ShapeDtypeStruct((B, DA_WIDTH, S), BF16),
        jax.ShapeDtypeStruct((B, NA_WIDTH, S), BF16),
        jax.ShapeDtypeStruct((B, S, NA_WIDTH), BF16),
        jax.ShapeDtypeStruct((B, NA_WIDTH, S), BF16),
        jax.ShapeDtypeStruct((B, S, n_gate), BF16),
    )
    assert DA_WIDTH == NA_WIDTH
    return pl.pallas_call(
        _in_proj_kernel,
        grid=(B, S // tm),
        in_specs=[tok(D), const((D, n_cols)), const((1, n_cols)),
                  pl.BlockSpec((tm, LANES), lambda b, i: (i, 0)),
                  pl.BlockSpec((tm, LANES), lambda b, i: (i, 0))],
        out_specs=(tr, tok(DA_WIDTH), tr, tr, tok(NA_WIDTH), tr, tok(n_gate)),
        out_shape=out_shape,
        compiler_params=_cparams(("parallel", "parallel")),
        name="in_proj",
    )(x, w_in, b_in, cos, sin)


def _diff_attn_kernel(qT_ref, qTn_ref, k_ref, vT_ref, lq1_ref, lk1_ref, lq2_ref, lk2_ref, g_ref,
                      o_ref, m_sc, l_sc, acc_sc, s_sc, cmax_sc, *, tk, unroll, lam_init):
    tq = qT_ref.shape[1]

    def block_diag(q_ref):
        qT = q_ref[...]
        row = lax.broadcasted_iota(jnp.int32, qT.shape, 0)
        zero = jnp.zeros_like(qT)
        return jnp.concatenate([jnp.where(row < DA_HEAD_DIM, qT, zero),
                                jnp.where(row < DA_HEAD_DIM, zero, qT)], axis=1)

    qbd = block_diag(qT_ref)

    m_sc[...] = jnp.full(m_sc.shape, -jnp.inf, F32)
    l_sc[...] = jnp.zeros(l_sc.shape, F32)
    acc_sc[...] = jnp.zeros(acc_sc.shape, F32)

    nk = k_ref.shape[0] // tk

    def put_scores(slot, j, q=None):
        off = pl.multiple_of(j * tk, tk)
        s = jnp.dot(k_ref[pl.ds(off, tk), :], qbd if q is None else q, preferred_element_type=F32)
        s_sc[slot] = s
        cmax_sc[slot] = jnp.max(s, axis=0, keepdims=True)

    def accumulate(slot, j):
        off = pl.multiple_of(j * tk, tk)
        m_prev = m_sc[...]
        m_new = jnp.maximum(m_prev, cmax_sc[slot])
        alpha = jnp.exp2(m_prev - m_new)
        p = jnp.exp2(s_sc[slot] - m_new)
        l_sc[...] = alpha * l_sc[...] + jnp.sum(p, axis=0, keepdims=True)
        vb = vT_ref[:, pl.ds(off, tk)]
        acc_sc[...] = alpha * acc_sc[...] + jnp.dot(vb, p.astype(BF16), preferred_element_type=F32)
        m_sc[...] = m_new

    @pl.when(pl.program_id(2) == 0)
    def _():
        put_scores(0, 0)

    def body(jj, carry):
        for u in range(unroll):
            j = unroll * jj + u
            put_scores((u + 1) % 2, j + 1)
            accumulate(u % 2, j)
        return carry

    lax.fori_loop(0, nk // unroll - 1, body, 0)
    for j in range(nk - unroll, nk):
        if j + 1 < nk:
            put_scores((j + 1) % 2, j + 1)
        else:
            put_scores(0, 0, block_diag(qTn_ref))
        accumulate(j % 2, j)

    lam = (jnp.exp(jnp.sum(lq1_ref[...] * lk1_ref[...], axis=1, keepdims=True))
           - jnp.exp(jnp.sum(lq2_ref[...] * lk2_ref[...], axis=1, keepdims=True)) + lam_init)
    on = acc_sc[...] / l_sc[...]
    o = on[:, :tq] - lam * on[:, tq:]
    ms = jnp.mean(o * o, axis=0, keepdims=True)
    o = o * lax.rsqrt(ms + RMS_EPS) * g_ref[...]
    o = o * (1.0 - lam_init)
    o_ref[...] = o.T.astype(BF16)


def _diff_attn(qT, k, vT, lq1, lk1, lq2, lk2, subln_g, lam_init):
    B, S, _ = k.shape
    tq = min(DA_TQ, S)
    tk = min(DA_TK, S // 2)
    unroll = math.gcd(DA_UNROLL, S // tk)
    assert unroll % 2 == 0 and S % tk == 0 and S % tq == 0
    nq = S // tq
    vec = pl.BlockSpec((1, DA_HEAD_DIM), lambda b, h, i: (0, 0))
    kernel = functools.partial(_diff_attn_kernel, tk=tk, unroll=unroll, lam_init=lam_init)
    return pl.pallas_call(
        kernel,
        grid=(B, DA_HEADS, nq),
        in_specs=[pl.BlockSpec((None, DA_V_DIM, tq), lambda b, h, i: (b, h, i)),
                  pl.BlockSpec((None, DA_V_DIM, tq), lambda b, h, i: (b, h, jnp.minimum(i + 1, nq - 1))),
                  pl.BlockSpec((None, S, DA_V_DIM), lambda b, h, i: (b, 0, h)),
                  pl.BlockSpec((None, DA_V_DIM, S), lambda b, h, i: (b, h, 0)),
                  vec, vec, vec, vec,
                  pl.BlockSpec((DA_V_DIM, 1), lambda b, h, i: (0, 0))],
        out_specs=pl.BlockSpec((None, tq, DA_V_DIM), lambda b, h, i: (b, i, h)),
        out_shape=jax.ShapeDtypeStruct((B, S, DA_WIDTH), BF16),
        scratch_shapes=[pltpu.VMEM((1, 2 * tq), F32), pltpu.VMEM((1, 2 * tq), F32),
                        pltpu.VMEM((DA_V_DIM, 2 * tq), F32), pltpu.VMEM((2, tk, 2 * tq), F32),
                        pltpu.VMEM((2, 1, 2 * tq), F32)],
        compiler_params=_cparams(("parallel", "parallel", "arbitrary")),
        name="diff_attn",
    )(qT, qT, k, vT, lq1, lk1, lq2, lk2, subln_g)


def _na_band_start(r0, rows):
    return jnp.clip(r0 - NA_KH // 2, 0, rows - NA_BAND)


def _na_tiles(rpb):
    cols = jnp.arange(GRID_W)
    cs = jnp.clip(cols - NA_KW // 2, 0, GRID_W - NA_KW)
    col_ok = (cols[:, None] >= cs[None, :]) & (cols[:, None] < cs[None, :] + NA_KW)
    col_off = jnp.clip(cols[:, None] - cols[None, :] + (NA_KW - 1), 0, 2 * NA_KW - 2)
    tiles = jnp.where(col_ok[None, None], rpb.astype(F32)[:, :, col_off], NEG_BIG)
    neg = jnp.full((rpb.shape[0], 1, GRID_W, GRID_W), NEG_BIG, F32)
    return jnp.concatenate([jnp.concatenate([tiles, neg], axis=1),
                            jnp.concatenate([neg, tiles], axis=1)], axis=3)


def _na_geometry(variant, rows):
    nblk = rows // NA_ROWS
    r0 = (0, NA_ROWS * min(1, nblk - 1), NA_ROWS * (nblk - 1))[variant]
    band = min(max(r0 - NA_KH // 2, 0), rows - NA_BAND)
    pairs = []
    for qp in range(NA_ROWS // 2):
        entries = []
        for kr in range(NA_BAND):
            ok = []
            for qr in (r0 + 2 * qp, r0 + 2 * qp + 1):
                rs = min(max(qr - NA_KH // 2, 0), rows - NA_KH)
                ok.append(rs <= band + kr < rs + NA_KH)
            if ok[0] or ok[1]:
                entries.append((kr, band + kr - (r0 + 2 * qp) + NA_KH - 1, ok[0], ok[1]))
        pairs.append(entries)
    return pairs


def _na_kernel(qT_ref, k_ref, vT_ref, tile_ref, o_ref, *, rows):
    i = pl.program_id(2)
    nblk = rows // NA_ROWS
    nk = NA_BAND * GRID_W
    off = pl.multiple_of(_na_band_start(i * NA_ROWS, rows) * GRID_W, 2 * GRID_W)

    def block(variant):
        kb = k_ref[pl.ds(off, nk), :]
        vT = vT_ref[:, pl.ds(off, nk)]
        lane = lax.broadcasted_iota(jnp.int32, kb.shape, 1)
        kbd = jnp.concatenate([jnp.where(lane < NA_HEAD_DIM, kb, jnp.zeros_like(kb)),
                               jnp.where(lane < NA_HEAD_DIM, jnp.zeros_like(kb), kb)], axis=0)
        row = lax.broadcasted_iota(jnp.int32, vT.shape, 0)
        vbdT = jnp.concatenate([jnp.where(row < NA_HEAD_DIM, vT, jnp.zeros_like(vT)),
                                jnp.where(row < NA_HEAD_DIM, jnp.zeros_like(vT), vT)], axis=1)
        s = jnp.dot(kbd, qT_ref[...], preferred_element_type=F32)
        left = lax.broadcasted_iota(jnp.int32, (GRID_W, LANES), 1) < GRID_W
        zero_slab = jnp.zeros((GRID_W, LANES), BF16)
        geometry = _na_geometry(variant, rows)
        p_heads, l_heads = [], []
        for h in range(2):
            p_cols, l_cols = [], []
            for qp, entries in enumerate(geometry):
                logits = {}
                for kr, d, ok_l, ok_r in entries:
                    bias = tile_ref[h, d]
                    if not ok_r:
                        bias = jnp.where(left, bias, NEG_BIG)
                    if not ok_l:
                        bias = jnp.where(left, NEG_BIG, bias)
                    r = h * nk + kr * GRID_W
                    logits[kr] = s[r:r + GRID_W, qp * LANES:(qp + 1) * LANES] + bias
                m = functools.reduce(jnp.maximum, logits.values())
                m = jnp.max(m, axis=0, keepdims=True)
                probs = {kr: jnp.exp(x - m) for kr, x in logits.items()}
                l_cols.append(jnp.sum(functools.reduce(jnp.add, probs.values()), axis=0, keepdims=True))
                p_cols.append(jnp.concatenate(
                    [probs[kr].astype(BF16) if kr in probs else zero_slab for kr in range(NA_BAND)], axis=0))
            p_heads.append(jnp.concatenate(p_cols, axis=1))
            l_heads.append(jnp.concatenate(l_cols, axis=1))
        oT = jnp.dot(vbdT, jnp.concatenate(p_heads, axis=0), preferred_element_type=F32)
        row_o = lax.broadcasted_iota(jnp.int32, oT.shape, 0)
        oT = oT / jnp.where(row_o < NA_HEAD_DIM, l_heads[0], l_heads[1])
        o_ref[...] = oT.T.astype(BF16)

    pl.when(i == 0)(lambda: block(0))
    pl.when(jnp.logical_and(i > 0, i < nblk - 1))(lambda: block(1))
    pl.when(jnp.logical_and(i > 0, i == nblk - 1))(lambda: block(2))


def _na_attn(qnT, kn, vnT, tiles):
    B, S, _ = kn.shape
    rows = S // GRID_W
    nblk = rows // NA_ROWS
    nq = NA_ROWS * GRID_W
    return pl.pallas_call(
        functools.partial(_na_kernel, rows=rows),
        grid=(B, NA_HEADS // 2, nblk),
        in_specs=[pl.BlockSpec((None, LANES, nq), lambda b, h, i: (b, h, i)),
                  pl.BlockSpec((None, S, LANES), lambda b, h, i: (b, 0, h)),
                  pl.BlockSpec((None, LANES, S), lambda b, h, i: (b, h, 0)),
                  pl.BlockSpec((2,) + tiles.shape[1:], lambda b, h, i: (h, 0, 0, 0))],
        out_specs=pl.BlockSpec((None, nq, LANES), lambda b, h, i: (b, i, h)),
        out_shape=jax.ShapeDtypeStruct((B, S, NA_WIDTH), BF16),
        compiler_params=_cparams(("parallel", "parallel", "arbitrary")),
        name="na_attn",
    )(qnT, kn, vnT, tiles)


def _layer_norm(h, g, b):
    mu = jnp.mean(h, axis=-1, keepdims=True)
    d = h - mu
    var = jnp.mean(d * d, axis=-1, keepdims=True)
    return d * lax.rsqrt(var + LN_EPS) * g + b


def _merge_kernel(x_ref, a_ref, nb_ref, g_ref, wda_ref, wna_ref, wout_ref, lng_ref, lnb_ref,
                  wr_ref, br_ref, x1_ref, x1p_ref, topi_ref, gate_ref, *, alpha):
    d = x_ref.shape[1]
    ya = jnp.dot(a_ref[...], wda_ref[...], preferred_element_type=F32)
    yb = jnp.dot(nb_ref[...], wna_ref[...], preferred_element_type=F32)
    g = g_ref[...].astype(F32)
    merged = g[:, :d] * ya + g[:, d:] * yb
    mix = jnp.dot(merged.astype(BF16), wout_ref[...], preferred_element_type=F32)
    x1 = _layer_norm(alpha * x_ref[...] + mix, lng_ref[...], lnb_ref[...])
    x1_ref[...] = x1
    x1p_ref[...] = _pack_bf16_pairs(x1)
    tm = x1.shape[0]
    xh = x1.astype(BF16)
    xl = (x1 - xh.astype(F32)).astype(BF16)
    r = jnp.dot(jnp.concatenate([xh, xl], axis=0), wr_ref[...], preferred_element_type=F32)
    logits = (r[:tm, :LANES] + r[:tm, LANES:]) + (r[tm:, :LANES] + r[tm:, LANES:]) + br_ref[...]
    lane = lax.broadcasted_iota(jnp.int32, logits.shape, 1).astype(F32)
    vals, idxs = [], []
    for _ in range(TOP_K):
        m = jnp.max(logits, axis=1, keepdims=True)
        idx = jnp.min(jnp.where(logits == m, lane, float(LANES)), axis=1, keepdims=True)
        vals.append(m)
        idxs.append(idx)
        logits = jnp.where(lane == idx, -jnp.inf, logits)
    es = [jnp.exp(v - vals[0]) for v in vals]
    denom = es[0] + es[1] + es[2] + es[3]
    gates = jnp.zeros(logits.shape, F32)
    topi = jnp.zeros(logits.shape, F32)
    for k in range(TOP_K):
        gates = jnp.where(lane == k, es[k] / denom, gates)
        topi = jnp.where(lane == k, idxs[k], topi)
    gate_ref[...] = gates
    topi_ref[...] = topi.astype(jnp.int32)


def _merge(x2, a2, nb2, g2, wda, wna, wout, ln_g, ln_b, wr, br, alpha):
    T, D = x2.shape
    tm = min(PROJ_TM, T)
    tok = lambda width: pl.BlockSpec((tm, width), lambda i: (i, 0))
    const = lambda shape: pl.BlockSpec(shape, lambda i: (0,) * len(shape))
    return pl.pallas_call(
        functools.partial(_merge_kernel, alpha=alpha),
        grid=(T // tm,),
        in_specs=[tok(D), tok(a2.shape[1]), tok(nb2.shape[1]), tok(g2.shape[1]),
                  const(wda.shape), const(wna.shape), const(wout.shape),
                  const((1, D)), const((1, D)), const(wr.shape), const((1, LANES))],
        out_specs=(tok(D), tok(D // 2), tok(LANES), tok(LANES)),
        out_shape=(jax.ShapeDtypeStruct((T, D), F32), jax.ShapeDtypeStruct((T, D // 2), jnp.uint32),
                   jax.ShapeDtypeStruct((T, LANES), jnp.int32), jax.ShapeDtypeStruct((T, LANES), F32)),
        compiler_params=_cparams(("parallel",)),
        name="merge",
    )(x2, a2, nb2, g2, wda, wna, wout, ln_g, ln_b, wr, br)


def _glu_group_bias(b1):
    e, n = b1.shape
    return b1.reshape(e, n // (2 * LANES), LANES, 2).transpose(0, 1, 3, 2).reshape(e, 1, n)


def _moe_kernel(be_ref, nused_ref, xs_ref, w1_ref, b1_ref, w2_ref, b2_ref, ys_ref, w1g_sc, w2b_sc):
    i = pl.program_id(0)
    used = i < nused_ref[0]
    fresh = jnp.logical_or(i == 0, be_ref[i] != be_ref[jnp.maximum(i - 1, 0)])
    grp = 2 * LANES

    @pl.when(jnp.logical_and(used, fresh))
    def _():
        r = lax.broadcasted_iota(jnp.int32, (grp, grp), 0)
        c = lax.broadcasted_iota(jnp.int32, (grp, grp), 1)
        src = jnp.where(c < LANES, 2 * c, 2 * (c - LANES) + 1)
        perm = jnp.where(r == src, 1.0, 0.0).astype(BF16)
        for g in range(w1_ref.shape[1] // grp):
            cols = w1_ref[:, g * grp:(g + 1) * grp].astype(BF16)
            w1g_sc[:, g * grp:(g + 1) * grp] = jnp.dot(cols, perm, preferred_element_type=F32).astype(BF16)
        w2b_sc[...] = w2_ref[...].astype(BF16)

    @pl.when(used)
    def _():
        xb = _unpack_bf16_pairs(xs_ref[...]).astype(BF16)
        h = jnp.dot(xb, w1g_sc[...], preferred_element_type=F32) + b1_ref[...]
        acts = []
        for g in range(h.shape[1] // grp):
            x_glu = jnp.minimum(h[:, g * grp:g * grp + LANES], SWIGLU_LIMIT)
            x_lin = jnp.clip(h[:, g * grp + LANES:(g + 1) * grp], -SWIGLU_LIMIT, SWIGLU_LIMIT)
            acts.append(x_glu * (1.0 / (1.0 + jnp.exp(-SWIGLU_ALPHA * x_glu))) * (x_lin + 1.0))
        act = jnp.concatenate(acts, axis=1).astype(BF16)
        y = jnp.dot(act, w2b_sc[...], preferred_element_type=F32) + b2_ref[...]
        ys_ref[...] = _pack_bf16_pairs(y)

    @pl.when(jnp.logical_not(used))
    def _():
        ys_ref[...] = jnp.zeros(ys_ref.shape, ys_ref.dtype)


def _moe_ffn(xs, block_expert, n_used, w1, b1g, w2, b2):
    P, half = xs.shape
    D = 2 * half
    de2 = w1.shape[2]
    de = w2.shape[1]
    wspec = lambda r, c: pl.BlockSpec((None, r, c), lambda i, be, nu: (be[i], 0, 0))
    grid_spec = pltpu.PrefetchScalarGridSpec(
        num_scalar_prefetch=2,
        grid=(P // MOE_BLOCK,),
        in_specs=[pl.BlockSpec((MOE_BLOCK, half), lambda i, be, nu: (i, 0)),
                  wspec(D, de2), wspec(1, de2), wspec(de, D), wspec(1, D)],
        out_specs=pl.BlockSpec((MOE_BLOCK, half), lambda i, be, nu: (i, 0)),
        scratch_shapes=[pltpu.VMEM((D, de2), BF16), pltpu.VMEM((de, D), BF16)],
    )
    return pl.pallas_call(
        _moe_kernel,
        grid_spec=grid_spec,
        out_shape=jax.ShapeDtypeStruct((P, half), jnp.uint32),
        compiler_params=_cparams(("arbitrary",)),
        name="moe_ffn",
    )(block_expert, n_used, xs, w1, b1g, w2, b2)


def _combine_kernel(x1_ref, y0_ref, y1_ref, y2_ref, y3_ref, gate_ref, lng_ref, lnb_ref, o_ref, *, alpha):
    gates = gate_ref[...]
    ffn = jnp.zeros(x1_ref.shape, F32)
    for k, y_ref in enumerate((y0_ref, y1_ref, y2_ref, y3_ref)):
        ffn = ffn + _unpack_bf16_pairs(y_ref[...]) * gates[:, k:k + 1]
    o_ref[...] = _layer_norm(alpha * x1_ref[...] + ffn, lng_ref[...], lnb_ref[...])


def _combine(x1, y_slots, gates, ln_g, ln_b, alpha):
    T, D = x1.shape
    tm = min(PROJ_TM, T)
    tok = lambda width: pl.BlockSpec((tm, width), lambda i: (i, 0))
    const = lambda shape: pl.BlockSpec(shape, lambda i: (0,) * len(shape))
    slot = lambda k: pl.BlockSpec((None, tm, D // 2), lambda i: (k, i, 0))
    return pl.pallas_call(
        functools.partial(_combine_kernel, alpha=alpha),
        grid=(T // tm,),
        in_specs=[tok(D)] + [slot(k) for k in range(TOP_K)] + [tok(LANES), const((1, D)), const((1, D))],
        out_specs=tok(D),
        out_shape=jax.ShapeDtypeStruct((T, D), F32),
        compiler_params=_cparams(("parallel",)),
        name="combine",
    )(x1, y_slots, y_slots, y_slots, y_slots, gates, ln_g, ln_b)


def _route(top_i, n_experts):
    T = top_i.shape[0]
    TK = T * TOP_K
    experts = jnp.arange(n_experts, dtype=jnp.int32)
    chosen = top_i[:, :, None] == experts
    picks = jnp.sum(chosen, axis=1, dtype=jnp.int32)
    csum = jnp.cumsum(picks, axis=0)
    counts = csum[-1]
    padded = ((counts + MOE_BLOCK - 1) // MOE_BLOCK) * MOE_BLOCK
    padded_end = jnp.cumsum(padded)
    padded_start = padded_end - padded
    slot_te = padded_start[None, :] + csum - picks
    dest = jnp.sum(jnp.where(chosen, slot_te[:, None, :], 0), axis=2)
    n_blocks = -(-TK // MOE_BLOCK) + n_experts
    blk_first = jnp.arange(n_blocks, dtype=jnp.int32) * MOE_BLOCK
    block_expert = jnp.minimum(jnp.sum(padded_end[None, :] <= blk_first[:, None], axis=1),
                               n_experts - 1).astype(jnp.int32)
    n_used = (padded_end[-1] // MOE_BLOCK).astype(jnp.int32).reshape(1)
    pad = padded - counts
    pad_end = jnp.cumsum(pad)
    j = jnp.arange(n_blocks * MOE_BLOCK - TK, dtype=jnp.int32)
    owner = j[:, None] >= pad_end[None, :]
    group = jnp.sum(owner, axis=1)
    mine = group[:, None] == experts[None, :]
    in_group = jnp.sum(jnp.where(mine, (padded_start + counts - (pad_end - pad))[None, :], 0), axis=1) + j
    pad_slots = jnp.where(group < n_experts, in_group, padded_end[-1] + j - pad_end[-1])
    return block_expert, n_used, dest.T.reshape(-1), pad_slots.astype(jnp.int32)


def _rope_tables(seq):
    pos = jnp.arange(seq, dtype=F32)
    inv = ROPE_THETA ** (-jnp.arange(0, DA_HEAD_DIM, 2, dtype=F32) / DA_HEAD_DIM)
    ang = pos[:, None] * inv[None, :]
    ang = jnp.concatenate([ang, ang], axis=-1)
    cos, sin = jnp.cos(ang), jnp.sin(ang)
    half = DA_HEAD_DIM // 2
    sin_signed = jnp.concatenate([-sin[:, :half], sin[:, half:]], axis=-1)
    return jnp.tile(cos, (1, 2)), jnp.tile(sin_signed, (1, 2))


def kernel(x, w_in, b_in, lambda_q1, lambda_k1, lambda_q2, lambda_k2, subln_g, rpb, w_branch_da, w_branch_na, w_out, ln1_g, ln1_b, w_router, b_router, w_mlp1, b_mlp1, w_mlp2, b_mlp2, ln2_g, ln2_b):
    B, S, D = x.shape
    depth = w_in.shape[0]
    n_experts = w_router.shape[2]
    T = B * S
    rows = S // GRID_W
    assert S % GRID_W == 0 and rows % NA_ROWS == 0 and rows >= NA_BAND
    alpha = (2 * depth) ** 0.25
    cos, sin = _rope_tables(S)
    row = lambda v: v.reshape(1, -1)

    for l in range(depth):
        lam_init = 0.8 - 0.6 * math.exp(-0.3 * l)
        qT, k, vT, qn, kn, vn, gates_br = _in_proj(x, w_in[l].astype(BF16), row(b_in[l]), cos, sin)
        a = _diff_attn(qT, k, vT, row(lambda_q1[l]), row(lambda_k1[l]), row(lambda_q2[l]),
                       row(lambda_k2[l]), subln_g[l].reshape(-1, 1), lam_init)
        nb = _na_attn(qn, kn, vn, _na_tiles(rpb[l]))

        wr = jnp.pad(w_router[l], ((0, 0), (0, LANES - n_experts)))
        wr_h = wr.astype(BF16)
        wr = jnp.concatenate([wr_h, (wr - wr_h.astype(F32)).astype(BF16)], axis=1)
        br = jnp.pad(row(b_router[l]), ((0, 0), (0, LANES - n_experts)), constant_values=NEG_BIG)
        x1, x1p, topi, gates = _merge(
            x.reshape(T, D), a.reshape(T, -1), nb.reshape(T, -1), gates_br.reshape(T, -1),
            w_branch_da[l].astype(BF16), w_branch_na[l].astype(BF16), w_out[l].astype(BF16),
            row(ln1_g[l]), row(ln1_b[l]), wr, br, alpha)

        block_expert, n_used, dest, pad_slots = _route(topi[:, :TOP_K], n_experts)
        xs = _sc_scatter_rows(x1p, dest, pad_slots)
        ys = _moe_ffn(xs, block_expert, n_used, w_mlp1[l], _glu_group_bias(b_mlp1[l]),
                      w_mlp2[l], b_mlp2[l][:, None, :])
        y_slots = _sc_gather_rows(ys, dest).reshape(TOP_K, T, D // 2)
        x = _combine(x1, y_slots, gates, row(ln2_g[l]), row(ln2_b[l]), alpha).reshape(B, S, D)
    return x
```

```python
import functools
import math

import jax
import jax.numpy as jnp
from jax import lax
from jax.experimental import pallas as pl
from jax.experimental.pallas import tpu as pltpu
from jax.experimental.pallas import tpu_sc as plsc

F32 = jnp.float32
BF16 = jnp.bfloat16

GRID_W = 64
DA_HEADS = 4
DA_HEAD_DIM = 64
DA_V_DIM = 2 * DA_HEAD_DIM
DA_WIDTH = DA_HEADS * DA_V_DIM
ROPE_THETA = 10000.0
NA_HEADS = 8
NA_HEAD_DIM = 64
NA_WIDTH = NA_HEADS * NA_HEAD_DIM
NA_KH = 8
NA_KW = 16
TOP_K = 4
SWIGLU_ALPHA = 1.702
SWIGLU_LIMIT = 7.0
MOE_BLOCK = 512
LN_EPS = 1e-5
RMS_EPS = 1e-5

LANES = 128
VMEM_LIMIT_BYTES = 56 * 1024 * 1024
SC_CORES = 2
SC_SUBCORES = 16
SC_GATHER_ROWS = 64

LOG2E = math.log2(math.e)
NEG_BIG = -1e30

PROJ_TM = 512
DA_TQ = 512
DA_TK = 512
DA_UNROLL = 8
NA_ROWS = 8
NA_BAND = 16
NA_PAIRS = 2


def _cparams(sem):
    return pltpu.CompilerParams(dimension_semantics=sem, vmem_limit_bytes=VMEM_LIMIT_BYTES)


def _pack_bf16_pairs(y):
    w = y.shape[1] // 2
    bits = lax.bitcast_convert_type(y, jnp.uint32)
    rounded = bits + jnp.uint32(0x7FFF) + ((bits >> 16) & jnp.uint32(1))
    return (rounded[:, :w] >> 16) | (rounded[:, w:] & jnp.uint32(0xFFFF0000))


def _unpack_bf16_pairs(words):
    lo = lax.bitcast_convert_type(words << 16, F32)
    hi = lax.bitcast_convert_type(words & jnp.uint32(0xFFFF0000), F32)
    return jnp.concatenate([lo, hi], axis=1)


def _sc_gather_rows(table, idx):
    n, w = idx.shape[0], table.shape[1]
    rows = SC_GATHER_ROWS
    workers = SC_CORES * SC_SUBCORES
    per_worker = n // workers
    n_pairs = per_worker // (2 * rows)
    assert n % (workers * 2 * rows) == 0 and n_pairs >= 1
    mesh = plsc.VectorSubcoreMesh(core_axis_name="c", subcore_axis_name="s",
                                  num_cores=SC_CORES, num_subcores=SC_SUBCORES)

    def body(table_hbm, idx_hbm, out_hbm, idx0, idx1, rows0, rows1, sem0, sem1):
        base = (lax.axis_index("s") * SC_CORES + lax.axis_index("c")) * per_worker
        bufs = ((idx0, rows0, sem0), (idx1, rows1, sem1))

        def gather(b):
            idx_v, rows_v, sem = bufs[b]
            return pltpu.make_async_copy(table_hbm.at[idx_v], rows_v, sem)

        def start(c, b):
            pltpu.sync_copy(idx_hbm.at[pl.ds(base + c * rows, rows)], bufs[b][0])
            gather(b).start()

        def finish(c, b):
            gather(b).wait()
            pltpu.sync_copy(bufs[b][1], out_hbm.at[pl.ds(base + c * rows, rows)])

        start(0, 0)

        @pl.loop(0, n_pairs - 1)
        def _(g):
            c = 2 * g
            start(c + 1, 1)
            finish(c, 0)
            start(c + 2, 0)
            finish(c + 1, 1)

        last = 2 * (n_pairs - 1)
        start(last + 1, 1)
        finish(last, 0)
        finish(last + 1, 1)

    return pl.kernel(
        body, out_type=jax.ShapeDtypeStruct((n, w), table.dtype), mesh=mesh,
        scratch_types=[pltpu.VMEM((rows,), jnp.int32), pltpu.VMEM((rows,), jnp.int32),
                       pltpu.VMEM((rows, w), table.dtype), pltpu.VMEM((rows, w), table.dtype),
                       pltpu.SemaphoreType.DMA, pltpu.SemaphoreType.DMA],
        name="sc_gather_rows",
    )(table, idx)


def _sc_scatter_rows(table, dest, rest):
    t, w = table.shape
    copies = dest.shape[0] // t
    n_out = dest.shape[0] + rest.shape[0]
    rows = SC_GATHER_ROWS
    workers = SC_CORES * SC_SUBCORES
    per_worker = t // workers
    n_pairs = per_worker // (2 * rows)
    rest_per_worker = rest.shape[0] // workers
    assert t % (workers * 2 * rows) == 0 and n_pairs >= 1 and rest.shape[0] % (workers * rows) == 0
    mesh = plsc.VectorSubcoreMesh(core_axis_name="c", subcore_axis_name="s",
                                  num_cores=SC_CORES, num_subcores=SC_SUBCORES)

    n_fill = rest_per_worker // rows

    def body(table_hbm, dest_hbm, rest_hbm, zeros_hbm, out_hbm, *scratch):
        worker = lax.axis_index("s") * SC_CORES + lax.axis_index("c")
        base = worker * per_worker
        per_buf = copies + 2
        bufs = (scratch[:per_buf], scratch[per_buf:2 * per_buf])
        zero_v, zero_sem = scratch[2 * per_buf:2 * per_buf + 2]
        fill_idx = scratch[2 * per_buf + 2:]

        def fills():
            return [pltpu.make_async_copy(zero_v, out_hbm.at[idx_v], zero_sem) for idx_v in fill_idx]

        pltpu.sync_copy(zeros_hbm, zero_v)
        for c, idx_v in enumerate(fill_idx):
            pltpu.sync_copy(rest_hbm.at[pl.ds(worker * rest_per_worker + c * rows, rows)], idx_v)
        for fill in fills():
            fill.start()

        def scatters(b):
            rows_v, sem = bufs[b][0], bufs[b][1]
            return [pltpu.make_async_copy(rows_v, out_hbm.at[idx_v], sem) for idx_v in bufs[b][2:]]

        def start(c, b):
            t0 = base + c * rows
            pltpu.sync_copy(table_hbm.at[pl.ds(t0, rows)], bufs[b][0])
            for k, idx_v in enumerate(bufs[b][2:]):
                pltpu.sync_copy(dest_hbm.at[pl.ds(k * t + t0, rows)], idx_v)
            for copy in scatters(b):
                copy.start()

        def finish(b):
            for copy in scatters(b):
                copy.wait()

        start(0, 0)

        @pl.loop(0, n_pairs - 1)
        def _(g):
            start(2 * g + 1, 1)
            finish(0)
            start(2 * g + 2, 0)
            finish(1)

        start(2 * n_pairs - 1, 1)
        finish(0)
        finish(1)

        for fill in fills():
            fill.wait()

    one_buf = ([pltpu.VMEM((rows, w), table.dtype), pltpu.SemaphoreType.DMA]
               + [pltpu.VMEM((rows,), jnp.int32) for _ in range(copies)])
    zero_buf = ([pltpu.VMEM((rows, w), table.dtype), pltpu.SemaphoreType.DMA]
                + [pltpu.VMEM((rows,), jnp.int32) for _ in range(n_fill)])
    return pl.kernel(
        body, out_type=jax.ShapeDtypeStruct((n_out, w), table.dtype), mesh=mesh,
        scratch_types=one_buf + one_buf + zero_buf, name="sc_scatter_rows",
    )(table, dest, rest, jnp.zeros((rows, w), table.dtype))


def _in_proj_kernel(x_ref, w_ref, b_ref, cos_ref, sin_ref,
                    qT_ref, k_ref, vT_ref, qn_ref, kn_ref, vn_ref, g_ref):
    xb = x_ref[...].astype(BF16)

    def seg(lo, hi):
        return jnp.dot(xb, w_ref[:, lo:hi], preferred_element_type=F32) + b_ref[:, lo:hi]

    cos = cos_ref[...]
    sin = sin_ref[...]
    lane = lax.broadcasted_iota(jnp.int32, cos.shape, 1)
    first_half = (lane % DA_HEAD_DIM) < (DA_HEAD_DIM // 2)

    def rope(y):
        outs = []
        for h in range(DA_HEADS):
            yh = y[:, h * LANES:(h + 1) * LANES]
            partner = jnp.where(first_half,
                                pltpu.roll(yh, LANES - DA_HEAD_DIM // 2, 1),
                                pltpu.roll(yh, DA_HEAD_DIM // 2, 1))
            outs.append(yh * cos + partner * sin)
        return jnp.concatenate(outs, axis=1)

    w = DA_WIDTH
    q = rope(seg(0, w)) * (DA_HEAD_DIM ** -0.5 * LOG2E)
    qT_ref[...] = q.T.astype(BF16)
    k_ref[...] = rope(seg(w, 2 * w)).astype(BF16)
    vT_ref[...] = seg(2 * w, 3 * w).T.astype(BF16)
    o = 3 * w
    qn_ref[...] = (seg(o, o + NA_WIDTH) * (NA_HEAD_DIM ** -0.5)).T.astype(BF16)
    kn_ref[...] = seg(o + NA_WIDTH, o + 2 * NA_WIDTH).astype(BF16)
    vn_ref[...] = seg(o + 2 * NA_WIDTH, o + 3 * NA_WIDTH).T.astype(BF16)
    g0 = o + 3 * NA_WIDTH
    gate_pre = seg(g0, w_ref.shape[1])
    g_ref[...] = (1.0 / (1.0 + jnp.exp(-gate_pre))).astype(BF16)


def _in_proj(x, w_in, b_in, cos, sin):
    B, S, D = x.shape
    tm = min(PROJ_TM, S)
    n_cols = w_in.shape[1]
    n_gate = n_cols - 3 * DA_WIDTH - 3 * NA_WIDTH
    tok = lambda width: pl.BlockSpec((None, tm, width), lambda b, i: (b, i, 0))
    tr = pl.BlockSpec((None, DA_WIDTH, tm), lambda b, i: (b, 0, i))
    const = lambda shape: pl.BlockSpec(shape, lambda b, i: (0,) * len(shape))
    out_shape = (
        jax.ShapeDtypeStruct((B, DA_WIDTH, S), BF16),
        jax.ShapeDtypeStruct((B, S, DA_WIDTH), BF16),
        jax.ShapeDtypeStruct((B, DA_WIDTH, S), BF16),
        jax.ShapeDtypeStruct((B, NA_WIDTH, S), BF16),
        jax.ShapeDtypeStruct((B, S, NA_WIDTH), BF16),
        jax.ShapeDtypeStruct((B, NA_WIDTH, S), BF16),
        jax.ShapeDtypeStruct((B, S, n_gate), BF16),
    )
    assert DA_WIDTH == NA_WIDTH
    return pl.pallas_call(
        _in_proj_kernel,
        grid=(B, S // tm),
        in_specs=[tok(D), const((D, n_cols)), const((1, n_cols)),
                  pl.BlockSpec((tm, LANES), lambda b, i: (i, 0)),
                  pl.BlockSpec((tm, LANES), lambda b, i: (i, 0))],
        out_specs=(tr, tok(DA_WIDTH), tr, tr, tok(NA_WIDTH), tr, tok(n_gate)),
        out_shape=out_shape,
        compiler_params=_cparams(("parallel", "parallel")),
        name="in_proj",
    )(x, w_in, b_in, cos, sin)


def _diff_attn_kernel(qT_ref, qTn_ref, k_ref, vT_ref, lq1_ref, lk1_ref, lq2_ref, lk2_ref, g_ref,
                      o_ref, m_sc, l_sc, acc_sc, s_sc, cmax_sc, *, tk, unroll, lam_init):
    tq = qT_ref.shape[1]

    def block_diag(q_ref):
        qT = q_ref[...]
        row = lax.broadcasted_iota(jnp.int32, qT.shape, 0)
        zero = jnp.zeros_like(qT)
        return jnp.concatenate([jnp.where(row < DA_HEAD_DIM, qT, zero),
                                jnp.where(row < DA_HEAD_DIM, zero, qT)], axis=1)

    qbd = block_diag(qT_ref)

    m_sc[...] = jnp.full(m_sc.shape, -jnp.inf, F32)
    l_sc[...] = jnp.zeros(l_sc.shape, F32)
    acc_sc[...] = jnp.zeros(acc_sc.shape, F32)

    nk = k_ref.shape[0] // tk

    def put_scores(slot, j, q=None):
        off = pl.multiple_of(j * tk, tk)
        s = jnp.dot(k_ref[pl.ds(off, tk), :], qbd if q is None else q, preferred_element_type=F32)
        s_sc[slot] = s
        cmax_sc[slot] = jnp.max(s, axis=0, keepdims=True)

    def accumulate(slot, j):
        off = pl.multiple_of(j * tk, tk)
        m_prev = m_sc[...]
        m_new = jnp.maximum(m_prev, cmax_sc[slot])
        alpha = jnp.exp2(m_prev - m_new)
        p = jnp.exp2(s_sc[slot] - m_new)
        l_sc[...] = alpha * l_sc[...] + jnp.sum(p, axis=0, keepdims=True)
        vb = vT_ref[:, pl.ds(off, tk)]
        acc_sc[...] = alpha * acc_sc[...] + jnp.dot(vb, p.astype(BF16), preferred_element_type=F32)
        m_sc[...] = m_new

    @pl.when(pl.program_id(2) == 0)
    def _():
        put_scores(0, 0)

    def body(jj, carry):
        for u in range(unroll):
            j = unroll * jj + u
            put_scores((u + 1) % 2, j + 1)
            accumulate(u % 2, j)
        return carry

    lax.fori_loop(0, nk // unroll - 1, body, 0)
    for j in range(nk - unroll, nk):
        if j + 1 < nk:
            put_scores((j + 1) % 2, j + 1)
        else:
            put_scores(0, 0, block_diag(qTn_ref))
        accumulate(j % 2, j)

    lam = (jnp.exp(jnp.sum(lq1_ref[...] * lk1_ref[...], axis=1, keepdims=True))
           - jnp.exp(jnp.sum(lq2_ref[...] * lk2_ref[...], axis=1, keepdims=True)) + lam_init)
    on = acc_sc[...] / l_sc[...]
    o = on[:, :tq] - lam * on[:, tq:]
    ms = jnp.mean(o * o, axis=0, keepdims=True)
    o = o * lax.rsqrt(ms + RMS_EPS) * g_ref[...]
    o = o * (1.0 - lam_init)
    o_ref[...] = o.T.astype(BF16)


def _diff_attn(qT, k, vT, lq1, lk1, lq2, lk2, subln_g, lam_init):
    B, S, _ = k.shape
    tq = min(DA_TQ, S)
    tk = min(DA_TK, S // 2)
    unroll = math.gcd(DA_UNROLL, S // tk)
    assert unroll % 2 == 0 and S % tk == 0 and S % tq == 0
    nq = S // tq
    vec = pl.BlockSpec((1, DA_HEAD_DIM), lambda b, h, i: (0, 0))
    kernel = functools.partial(_diff_attn_kernel, tk=tk, unroll=unroll, lam_init=lam_init)
    return pl.pallas_call(
        kernel,
        grid=(B, DA_HEADS, nq),
        in_specs=[pl.BlockSpec((None, DA_V_DIM, tq), lambda b, h, i: (b, h, i)),
                  pl.BlockSpec((None, DA_V_DIM, tq), lambda b, h, i: (b, h, jnp.minimum(i + 1, nq - 1))),
                  pl.BlockSpec((None, S, DA_V_DIM), lambda b, h, i: (b, 0, h)),
                  pl.BlockSpec((None, DA_V_DIM, S), lambda b, h, i: (b, h, 0)),
                  vec, vec, vec, vec,
                  pl.BlockSpec((DA_V_DIM, 1), lambda b, h, i: (0, 0))],
        out_specs=pl.BlockSpec((None, tq, DA_V_DIM), lambda b, h, i: (b, i, h)),
        out_shape=jax.ShapeDtypeStruct((B, S, DA_WIDTH), BF16),
        scratch_shapes=[pltpu.VMEM((1, 2 * tq), F32), pltpu.VMEM((1, 2 * tq), F32),
                        pltpu.VMEM((DA_V_DIM, 2 * tq), F32), pltpu.VMEM((2, tk, 2 * tq), F32),
                        pltpu.VMEM((2, 1, 2 * tq), F32)],
        compiler_params=_cparams(("parallel", "parallel", "arbitrary")),
        name="diff_attn",
    )(qT, qT, k, vT, lq1, lk1, lq2, lk2, subln_g)


def _na_band_start(r0, rows):
    return jnp.clip(r0 - NA_KH // 2, 0, rows - NA_BAND)


def _na_tiles(rpb):
    cols = jnp.arange(GRID_W)
    cs = jnp.clip(cols - NA_KW // 2, 0, GRID_W - NA_KW)
    col_ok = (cols[:, None] >= cs[None, :]) & (cols[:, None] < cs[None, :] + NA_KW)
    col_off = jnp.clip(cols[:, None] - cols[None, :] + (NA_KW - 1), 0, 2 * NA_KW - 2)
    tiles = jnp.where(col_ok[None, None], rpb.astype(F32)[:, :, col_off], NEG_BIG)
    neg = jnp.full((rpb.shape[0], 1, GRID_W, GRID_W), NEG_BIG, F32)
    return jnp.concatenate([jnp.concatenate([tiles, neg], axis=1),
                            jnp.concatenate([neg, tiles], axis=1)], axis=3)


def _na_geometry(variant, rows):
    nblk = rows // NA_ROWS
    r0 = (0, NA_ROWS * min(1, nblk - 1), NA_ROWS * (nblk - 1))[variant]
    band = min(max(r0 - NA_KH // 2, 0), rows - NA_BAND)
    pairs = []
    for qp in range(NA_ROWS // 2):
        entries = []
        for kr in range(NA_BAND):
            ok = []
            for qr in (r0 + 2 * qp, r0 + 2 * qp + 1):
                rs = min(max(qr - NA_KH // 2, 0), rows - NA_KH)
                ok.append(rs <= band + kr < rs + NA_KH)
            if ok[0] or ok[1]:
                entries.append((kr, band + kr - (r0 + 2 * qp) + NA_KH - 1, ok[0], ok[1]))
        pairs.append(entries)
    return pairs


def _na_kernel(qT_ref, k_ref, vT_ref, tile_ref, o_ref, *, rows):
    i = pl.program_id(2)
    nblk = rows // NA_ROWS
    nk = NA_BAND * GRID_W
    off = pl.multiple_of(_na_band_start(i * NA_ROWS, rows) * GRID_W, 2 * GRID_W)

    def block(variant):
        for pair in range(NA_PAIRS):
            block_pair(variant, pair)

    def block_pair(variant, pair):
        c0 = pair * LANES
        kb = k_ref[pl.ds(off, nk), c0:c0 + LANES]
        vT = vT_ref[c0:c0 + LANES, pl.ds(off, nk)]
        lane = lax.broadcasted_iota(jnp.int32, kb.shape, 1)
        kbd = jnp.concatenate([jnp.where(lane < NA_HEAD_DIM, kb, jnp.zeros_like(kb)),
                               jnp.where(lane < NA_HEAD_DIM, jnp.zeros_like(kb), kb)], axis=0)
        row = lax.broadcasted_iota(jnp.int32, vT.shape, 0)
        vbdT = jnp.concatenate([jnp.where(row < NA_HEAD_DIM, vT, jnp.zeros_like(vT)),
                                jnp.where(row < NA_HEAD_DIM, jnp.zeros_like(vT), vT)], axis=1)
        s = jnp.dot(kbd, qT_ref[c0:c0 + LANES, :], preferred_element_type=F32)
        left = lax.broadcasted_iota(jnp.int32, (GRID_W, LANES), 1) < GRID_W
        zero_slab = jnp.zeros((GRID_W, LANES), BF16)
        geometry = _na_geometry(variant, rows)
        p_heads, l_heads = [], []
        for h in range(2):
            p_cols, l_cols = [], []
            for qp, entries in enumerate(geometry):
                logits = {}
                for kr, d, ok_l, ok_r in entries:
                    bias = tile_ref[2 * pair + h, d]
                    if not ok_r:
                        bias = jnp.where(left, bias, NEG_BIG)
                    if not ok_l:
                        bias = jnp.where(left, NEG_BIG, bias)
                    r = h * nk + kr * GRID_W
                    logits[kr] = s[r:r + GRID_W, qp * LANES:(qp + 1) * LANES] + bias
                m = functools.reduce(jnp.maximum, logits.values())
                m = jnp.max(m, axis=0, keepdims=True)
                probs = {kr: jnp.exp(x - m) for kr, x in logits.items()}
                l_cols.append(jnp.sum(functools.reduce(jnp.add, probs.values()), axis=0, keepdims=True))
                p_cols.append(jnp.concatenate(
                    [probs[kr].astype(BF16) if kr in probs else zero_slab for kr in range(NA_BAND)], axis=0))
            p_heads.append(jnp.concatenate(p_cols, axis=1))
            l_heads.append(jnp.concatenate(l_cols, axis=1))
        oT = jnp.dot(vbdT, jnp.concatenate(p_heads, axis=0), preferred_element_type=F32)
        row_o = lax.broadcasted_iota(jnp.int32, oT.shape, 0)
        oT = oT / jnp.where(row_o < NA_HEAD_DIM, l_heads[0], l_heads[1])
        o_ref[:, c0:c0 + LANES] = oT.T.astype(BF16)

    pl.when(i == 0)(lambda: block(0))
    pl.when(jnp.logical_and(i > 0, i < nblk - 1))(lambda: block(1))
    pl.when(jnp.logical_and(i > 0, i == nblk - 1))(lambda: block(2))


def _na_attn(qnT, kn, vnT, tiles):
    B, S, _ = kn.shape
    rows = S // GRID_W
    nblk = rows // NA_ROWS
    nq = NA_ROWS * GRID_W
    width = NA_PAIRS * LANES
    return pl.pallas_call(
        functools.partial(_na_kernel, rows=rows),
        grid=(B, NA_HEADS // (2 * NA_PAIRS), nblk),
        in_specs=[pl.BlockSpec((None, width, nq), lambda b, h, i: (b, h, i)),
                  pl.BlockSpec((None, S, width), lambda b, h, i: (b, 0, h)),
                  pl.BlockSpec((None, width, S), lambda b, h, i: (b, h, 0)),
                  pl.BlockSpec((2 * NA_PAIRS,) + tiles.shape[1:], lambda b, h, i: (h, 0, 0, 0))],
        out_specs=pl.BlockSpec((None, nq, width), lambda b, h, i: (b, i, h)),
        out_shape=jax.ShapeDtypeStruct((B, S, NA_WIDTH), BF16),
        compiler_params=_cparams(("parallel", "parallel", "arbitrary")),
        name="na_attn",
    )(qnT, kn, vnT, tiles)


def _layer_norm(h, g, b):
    mu = jnp.mean(h, axis=-1, keepdims=True)
    d = h - mu
    var = jnp.mean(d * d, axis=-1, keepdims=True)
    return d * lax.rsqrt(var + LN_EPS) * g + b


def _merge_kernel(x_ref, a_ref, nb_ref, g_ref, wda_ref, wna_ref, wout_ref, lng_ref, lnb_ref,
                  wr_ref, br_ref, x1_ref, x1p_ref, topi_ref, gate_ref, *, alpha):
    d = x_ref.shape[1]
    ya = jnp.dot(a_ref[...], wda_ref[...], preferred_element_type=F32)
    yb = jnp.dot(nb_ref[...], wna_ref[...], preferred_element_type=F32)
    g = g_ref[...].astype(F32)
    merged = g[:, :d] * ya + g[:, d:] * yb
    mix = jnp.dot(merged.astype(BF16), wout_ref[...], preferred_element_type=F32)
    x1 = _layer_norm(alpha * x_ref[...] + mix, lng_ref[...], lnb_ref[...])
    x1_ref[...] = x1
    x1p_ref[...] = _pack_bf16_pairs(x1)
    tm = x1.shape[0]
    xh = x1.astype(BF16)
    xl = (x1 - xh.astype(F32)).astype(BF16)
    r = jnp.dot(jnp.concatenate([xh, xl], axis=0), wr_ref[...], preferred_element_type=F32)
    logits = (r[:tm, :LANES] + r[:tm, LANES:]) + (r[tm:, :LANES] + r[tm:, LANES:]) + br_ref[...]
    lane = lax.broadcasted_iota(jnp.int32, logits.shape, 1).astype(F32)
    vals, idxs = [], []
    for _ in range(TOP_K):
        m = jnp.max(logits, axis=1, keepdims=True)
        idx = jnp.min(jnp.where(logits == m, lane, float(LANES)), axis=1, keepdims=True)
        vals.append(m)
        idxs.append(idx)
        logits = jnp.where(lane == idx, -jnp.inf, logits)
    es = [jnp.exp(v - vals[0]) for v in vals]
    denom = es[0] + es[1] + es[2] + es[3]
    gates = jnp.zeros(logits.shape, F32)
    topi = jnp.zeros(logits.shape, F32)
    for k in range(TOP_K):
        gates = jnp.where(lane == k, es[k] / denom, gates)
        topi = jnp.where(lane == k, idxs[k], topi)
    gate_ref[...] = gates
    topi_ref[...] = topi.astype(jnp.int32)


def _merge(x2, a2, nb2, g2, wda, wna, wout, ln_g, ln_b, wr, br, alpha):
    T, D = x2.shape
    tm = min(PROJ_TM, T)
    tok = lambda width: pl.BlockSpec((tm, width), lambda i: (i, 0))
    const = lambda shape: pl.BlockSpec(shape, lambda i: (0,) * len(shape))
    return pl.pallas_call(
        functools.partial(_merge_kernel, alpha=alpha),
        grid=(T // tm,),
        in_specs=[tok(D), tok(a2.shape[1]), tok(nb2.shape[1]), tok(g2.shape[1]),
                  const(wda.shape), const(wna.shape), const(wout.shape),
                  const((1, D)), const((1, D)), const(wr.shape), const((1, LANES))],
        out_specs=(tok(D), tok(D // 2), tok(LANES), tok(LANES)),
        out_shape=(jax.ShapeDtypeStruct((T, D), F32), jax.ShapeDtypeStruct((T, D // 2), jnp.uint32),
                   jax.ShapeDtypeStruct((T, LANES), jnp.int32), jax.ShapeDtypeStruct((T, LANES), F32)),
        compiler_params=_cparams(("parallel",)),
        name="merge",
    )(x2, a2, nb2, g2, wda, wna, wout, ln_g, ln_b, wr, br)


def _glu_group_bias(b1):
    e, n = b1.shape
    return b1.reshape(e, n // (2 * LANES), LANES, 2).transpose(0, 1, 3, 2).reshape(e, 1, n)


def _moe_kernel(be_ref, nused_ref, xs_ref, w1_ref, b1_ref, w2_ref, b2_ref, ys_ref, w1g_sc, w2b_sc):
    i = pl.program_id(0)
    used = i < nused_ref[0]
    fresh = jnp.logical_or(i == 0, be_ref[i] != be_ref[jnp.maximum(i - 1, 0)])
    grp = 2 * LANES

    @pl.when(jnp.logical_and(used, fresh))
    def _():
        r = lax.broadcasted_iota(jnp.int32, (grp, grp), 0)
        c = lax.broadcasted_iota(jnp.int32, (grp, grp), 1)
        src = jnp.where(c < LANES, 2 * c, 2 * (c - LANES) + 1)
        perm = jnp.where(r == src, 1.0, 0.0).astype(BF16)
        for g in range(w1_ref.shape[1] // grp):
            cols = w1_ref[:, g * grp:(g + 1) * grp].astype(BF16)
            w1g_sc[:, g * grp:(g + 1) * grp] = jnp.dot(cols, perm, preferred_element_type=F32).astype(BF16)
        w2b_sc[...] = w2_ref[...].astype(BF16)

    @pl.when(used)
    def _():
        xb = _unpack_bf16_pairs(xs_ref[...]).astype(BF16)
        h = jnp.dot(xb, w1g_sc[...], preferred_element_type=F32) + b1_ref[...]
        acts = []
        for g in range(h.shape[1] // grp):
            x_glu = jnp.minimum(h[:, g * grp:g * grp + LANES], SWIGLU_LIMIT)
            x_lin = jnp.clip(h[:, g * grp + LANES:(g + 1) * grp], -SWIGLU_LIMIT, SWIGLU_LIMIT)
            acts.append(x_glu * (1.0 / (1.0 + jnp.exp(-SWIGLU_ALPHA * x_glu))) * (x_lin + 1.0))
        act = jnp.concatenate(acts, axis=1).astype(BF16)
        y = jnp.dot(act, w2b_sc[...], preferred_element_type=F32) + b2_ref[...]
        ys_ref[...] = _pack_bf16_pairs(y)

    @pl.when(jnp.logical_not(used))
    def _():
        ys_ref[...] = jnp.zeros(ys_ref.shape, ys_ref.dtype)


def _moe_ffn(xs, block_expert, n_used, w1, b1g, w2, b2):
    P, half = xs.shape
    D = 2 * half
    de2 = w1.shape[2]
    de = w2.shape[1]
    wspec = lambda r, c: pl.BlockSpec((None, r, c), lambda i, be, nu: (be[i], 0, 0))
    grid_spec = pltpu.PrefetchScalarGridSpec(
        num_scalar_prefetch=2,
        grid=(P // MOE_BLOCK,),
        in_specs=[pl.BlockSpec((MOE_BLOCK, half), lambda i, be, nu: (i, 0)),
                  wspec(D, de2), wspec(1, de2), wspec(de, D), wspec(1, D)],
        out_specs=pl.BlockSpec((MOE_BLOCK, half), lambda i, be, nu: (i, 0)),
        scratch_shapes=[pltpu.VMEM((D, de2), BF16), pltpu.VMEM((de, D), BF16)],
    )
    return pl.pallas_call(
        _moe_kernel,
        grid_spec=grid_spec,
        out_shape=jax.ShapeDtypeStruct((P, half), jnp.uint32),
        compiler_params=_cparams(("arbitrary",)),
        name="moe_ffn",
    )(block_expert, n_used, xs, w1, b1g, w2, b2)


def _combine_kernel(x1_ref, y0_ref, y1_ref, y2_ref, y3_ref, gate_ref, lng_ref, lnb_ref, o_ref, *, alpha):
    gates = gate_ref[...]
    ffn = jnp.zeros(x1_ref.shape, F32)
    for k, y_ref in enumerate((y0_ref, y1_ref, y2_ref, y3_ref)):
        ffn = ffn + _unpack_bf16_pairs(y_ref[...]) * gates[:, k:k + 1]
    o_ref[...] = _layer_norm(alpha * x1_ref[...] + ffn, lng_ref[...], lnb_ref[...])


def _combine(x1, y_slots, gates, ln_g, ln_b, alpha):
    T, D = x1.shape
    tm = min(PROJ_TM, T)
    tok = lambda width: pl.BlockSpec((tm, width), lambda i: (i, 0))
    const = lambda shape: pl.BlockSpec(shape, lambda i: (0,) * len(shape))
    slot = lambda k: pl.BlockSpec((None, tm, D // 2), lambda i: (k, i, 0))
    return pl.pallas_call(
        functools.partial(_combine_kernel, alpha=alpha),
        grid=(T // tm,),
        in_specs=[tok(D)] + [slot(k) for k in range(TOP_K)] + [tok(LANES), const((1, D)), const((1, D))],
        out_specs=tok(D),
        out_shape=jax.ShapeDtypeStruct((T, D), F32),
        compiler_params=_cparams(("parallel",)),
        name="combine",
    )(x1, y_slots, y_slots, y_slots, y_slots, gates, ln_g, ln_b)


def _route(top_i, n_experts):
    T = top_i.shape[0]
    TK = T * TOP_K
    experts = jnp.arange(n_experts, dtype=jnp.int32)
    chosen = top_i[:, :, None] == experts
    picks = jnp.sum(chosen, axis=1, dtype=jnp.int32)
    csum = jnp.cumsum(picks, axis=0)
    counts = csum[-1]
    padded = ((counts + MOE_BLOCK - 1) // MOE_BLOCK) * MOE_BLOCK
    padded_end = jnp.cumsum(padded)
    padded_start = padded_end - padded
    slot_te = padded_start[None, :] + csum - picks
    dest = jnp.sum(jnp.where(chosen, slot_te[:, None, :], 0), axis=2)
    n_blocks = -(-TK // MOE_BLOCK) + n_experts
    blk_first = jnp.arange(n_blocks, dtype=jnp.int32) * MOE_BLOCK
    block_expert = jnp.minimum(jnp.sum(padded_end[None, :] <= blk_first[:, None], axis=1),
                               n_experts - 1).astype(jnp.int32)
    n_used = (padded_end[-1] // MOE_BLOCK).astype(jnp.int32).reshape(1)
    pad = padded - counts
    pad_end = jnp.cumsum(pad)
    j = jnp.arange(n_blocks * MOE_BLOCK - TK, dtype=jnp.int32)
    owner = j[:, None] >= pad_end[None, :]
    group = jnp.sum(owner, axis=1)
    mine = group[:, None] == experts[None, :]
    in_group = jnp.sum(jnp.where(mine, (padded_start + counts - (pad_end - pad))[None, :], 0), axis=1) + j
    pad_slots = jnp.where(group < n_experts, in_group, padded_end[-1] + j - pad_end[-1])
    return block_expert, n_used, dest.T.reshape(-1), pad_slots.astype(jnp.int32)


def _rope_tables(seq):
    pos = jnp.arange(seq, dtype=F32)
    inv = ROPE_THETA ** (-jnp.arange(0, DA_HEAD_DIM, 2, dtype=F32) / DA_HEAD_DIM)
    ang = pos[:, None] * inv[None, :]
    ang = jnp.concatenate([ang, ang], axis=-1)
    cos, sin = jnp.cos(ang), jnp.sin(ang)
    half = DA_HEAD_DIM // 2
    sin_signed = jnp.concatenate([-sin[:, :half], sin[:, half:]], axis=-1)
    return jnp.tile(cos, (1, 2)), jnp.tile(sin_signed, (1, 2))


def kernel(x, w_in, b_in, lambda_q1, lambda_k1, lambda_q2, lambda_k2, subln_g, rpb, w_branch_da, w_branch_na, w_out, ln1_g, ln1_b, w_router, b_router, w_mlp1, b_mlp1, w_mlp2, b_mlp2, ln2_g, ln2_b):
    B, S, D = x.shape
    depth = w_in.shape[0]
    n_experts = w_router.shape[2]
    T = B * S
    rows = S // GRID_W
    assert S % GRID_W == 0 and rows % NA_ROWS == 0 and rows >= NA_BAND
    alpha = (2 * depth) ** 0.25
    cos, sin = _rope_tables(S)
    row = lambda v: v.reshape(1, -1)

    for l in range(depth):
        lam_init = 0.8 - 0.6 * math.exp(-0.3 * l)
        qT, k, vT, qn, kn, vn, gates_br = _in_proj(x, w_in[l].astype(BF16), row(b_in[l]), cos, sin)
        a = _diff_attn(qT, k, vT, row(lambda_q1[l]), row(lambda_k1[l]), row(lambda_q2[l]),
                       row(lambda_k2[l]), subln_g[l].reshape(-1, 1), lam_init)
        nb = _na_attn(qn, kn, vn, _na_tiles(rpb[l]))

        wr = jnp.pad(w_router[l], ((0, 0), (0, LANES - n_experts)))
        wr_h = wr.astype(BF16)
        wr = jnp.concatenate([wr_h, (wr - wr_h.astype(F32)).astype(BF16)], axis=1)
        br = jnp.pad(row(b_router[l]), ((0, 0), (0, LANES - n_experts)), constant_values=NEG_BIG)
        x1, x1p, topi, gates = _merge(
            x.reshape(T, D), a.reshape(T, -1), nb.reshape(T, -1), gates_br.reshape(T, -1),
            w_branch_da[l].astype(BF16), w_branch_na[l].astype(BF16), w_out[l].astype(BF16),
            row(ln1_g[l]), row(ln1_b[l]), wr, br, alpha)

        block_expert, n_used, dest, pad_slots = _route(topi[:, :TOP_K], n_experts)
        xs = _sc_scatter_rows(x1p, dest, pad_slots)
        ys = _moe_ffn(xs, block_expert, n_used, w_mlp1[l], _glu_group_bias(b_mlp1[l]),
                      w_mlp2[l], b_mlp2[l][:, None, :])
        y_slots = _sc_gather_rows(ys, dest).reshape(TOP_K, T, D // 2)
        x = _combine(x1, y_slots, gates, row(ln2_g[l]), row(ln2_b[l]), alpha).reshape(B, S, D)
    return x
```

```python
import functools
import math

import jax
import jax.numpy as jnp
from jax import lax
from jax.experimental import pallas as pl
from jax.experimental.pallas import tpu as pltpu
from jax.experimental.pallas import tpu_sc as plsc

F32 = jnp.float32
BF16 = jnp.bfloat16

GRID_W = 64
DA_HEADS = 4
DA_HEAD_DIM = 64
DA_V_DIM = 2 * DA_HEAD_DIM
DA_WIDTH = DA_HEADS * DA_V_DIM
ROPE_THETA = 10000.0
NA_HEADS = 8
NA_HEAD_DIM = 64
NA_WIDTH = NA_HEADS * NA_HEAD_DIM
NA_KH = 8
NA_KW = 16
TOP_K = 4
SWIGLU_ALPHA = 1.702
SWIGLU_LIMIT = 7.0
MOE_BLOCK = 512
LN_EPS = 1e-5
RMS_EPS = 1e-5

LANES = 128
VMEM_LIMIT_BYTES = 56 * 1024 * 1024
SC_CORES = 2
SC_SUBCORES = 16
SC_GATHER_ROWS = 64

LOG2E = math.log2(math.e)
NEG_BIG = -1e30

PROJ_TM = 512
DA_TQ = 512
DA_TK = 512
DA_UNROLL = 8
NA_ROWS = 8
NA_BAND = 16
NA_PAIRS = 4


def _cparams(sem):
    return pltpu.CompilerParams(dimension_semantics=sem, vmem_limit_bytes=VMEM_LIMIT_BYTES)


def _pack_bf16_pairs(y):
    w = y.shape[1] // 2
    bits = lax.bitcast_convert_type(y, jnp.uint32)
    rounded = bits + jnp.uint32(0x7FFF) + ((bits >> 16) & jnp.uint32(1))
    return (rounded[:, :w] >> 16) | (rounded[:, w:] & jnp.uint32(0xFFFF0000))


def _unpack_bf16_pairs(words):
    lo = lax.bitcast_convert_type(words << 16, F32)
    hi = lax.bitcast_convert_type(words & jnp.uint32(0xFFFF0000), F32)
    return jnp.concatenate([lo, hi], axis=1)


def _sc_gather_rows(table, idx):
    n, w = idx.shape[0], table.shape[1]
    rows = SC_GATHER_ROWS
    workers = SC_CORES * SC_SUBCORES
    per_worker = n // workers
    n_pairs = per_worker // (2 * rows)
    assert n % (workers * 2 * rows) == 0 and n_pairs >= 1
    mesh = plsc.VectorSubcoreMesh(core_axis_name="c", subcore_axis_name="s",
                                  num_cores=SC_CORES, num_subcores=SC_SUBCORES)

    def body(table_hbm, idx_hbm, out_hbm, idx0, idx1, rows0, rows1, sem0, sem1):
        base = (lax.axis_index("s") * SC_CORES + lax.axis_index("c")) * per_worker
        bufs = ((idx0, rows0, sem0), (idx1, rows1, sem1))

        def gather(b):
            idx_v, rows_v, sem = bufs[b]
            return pltpu.make_async_copy(table_hbm.at[idx_v], rows_v, sem)

        def start(c, b):
            pltpu.sync_copy(idx_hbm.at[pl.ds(base + c * rows, rows)], bufs[b][0])
            gather(b).start()

        def finish(c, b):
            gather(b).wait()
            pltpu.sync_copy(bufs[b][1], out_hbm.at[pl.ds(base + c * rows, rows)])

        start(0, 0)

        @pl.loop(0, n_pairs - 1)
        def _(g):
            c = 2 * g
            start(c + 1, 1)
            finish(c, 0)
            start(c + 2, 0)
            finish(c + 1, 1)

        last = 2 * (n_pairs - 1)
        start(last + 1, 1)
        finish(last, 0)
        finish(last + 1, 1)

    return pl.kernel(
        body, out_type=jax.ShapeDtypeStruct((n, w), table.dtype), mesh=mesh,
        scratch_types=[pltpu.VMEM((rows,), jnp.int32), pltpu.VMEM((rows,), jnp.int32),
                       pltpu.VMEM((rows, w), table.dtype), pltpu.VMEM((rows, w), table.dtype),
                       pltpu.SemaphoreType.DMA, pltpu.SemaphoreType.DMA],
        name="sc_gather_rows",
    )(table, idx)


def _sc_scatter_rows(table, dest, rest):
    t, w = table.shape
    copies = dest.shape[0] // t
    n_out = dest.shape[0] + rest.shape[0]
    rows = SC_GATHER_ROWS
    workers = SC_CORES * SC_SUBCORES
    per_worker = t // workers
    n_pairs = per_worker // (2 * rows)
    rest_per_worker = rest.shape[0] // workers
    assert t % (workers * 2 * rows) == 0 and n_pairs >= 1 and rest.shape[0] % (workers * rows) == 0
    mesh = plsc.VectorSubcoreMesh(core_axis_name="c", subcore_axis_name="s",
                                  num_cores=SC_CORES, num_subcores=SC_SUBCORES)

    n_fill = rest_per_worker // rows

    def body(table_hbm, dest_hbm, rest_hbm, zeros_hbm, out_hbm, *scratch):
        worker = lax.axis_index("s") * SC_CORES + lax.axis_index("c")
        base = worker * per_worker
        per_buf = copies + 2
        bufs = (scratch[:per_buf], scratch[per_buf:2 * per_buf])
        zero_v, zero_sem = scratch[2 * per_buf:2 * per_buf + 2]
        fill_idx = scratch[2 * per_buf + 2:]

        def fills():
            return [pltpu.make_async_copy(zero_v, out_hbm.at[idx_v], zero_sem) for idx_v in fill_idx]

        pltpu.sync_copy(zeros_hbm, zero_v)
        for c, idx_v in enumerate(fill_idx):
            pltpu.sync_copy(rest_hbm.at[pl.ds(worker * rest_per_worker + c * rows, rows)], idx_v)
        for fill in fills():
            fill.start()

        def scatters(b):
            rows_v, sem = bufs[b][0], bufs[b][1]
            return [pltpu.make_async_copy(rows_v, out_hbm.at[idx_v], sem) for idx_v in bufs[b][2:]]

        def start(c, b):
            t0 = base + c * rows
            pltpu.sync_copy(table_hbm.at[pl.ds(t0, rows)], bufs[b][0])
            for k, idx_v in enumerate(bufs[b][2:]):
                pltpu.sync_copy(dest_hbm.at[pl.ds(k * t + t0, rows)], idx_v)
            for copy in scatters(b):
                copy.start()

        def finish(b):
            for copy in scatters(b):
                copy.wait()

        start(0, 0)

        @pl.loop(0, n_pairs - 1)
        def _(g):
            start(2 * g + 1, 1)
            finish(0)
            start(2 * g + 2, 0)
            finish(1)

        start(2 * n_pairs - 1, 1)
        finish(0)
        finish(1)

        for fill in fills():
            fill.wait()

    one_buf = ([pltpu.VMEM((rows, w), table.dtype), pltpu.SemaphoreType.DMA]
               + [pltpu.VMEM((rows,), jnp.int32) for _ in range(copies)])
    zero_buf = ([pltpu.VMEM((rows, w), table.dtype), pltpu.SemaphoreType.DMA]
                + [pltpu.VMEM((rows,), jnp.int32) for _ in range(n_fill)])
    return pl.kernel(
        body, out_type=jax.ShapeDtypeStruct((n_out, w), table.dtype), mesh=mesh,
        scratch_types=one_buf + one_buf + zero_buf, name="sc_scatter_rows",
    )(table, dest, rest, jnp.zeros((rows, w), table.dtype))


def _in_proj_kernel(x_ref, w_ref, b_ref, cos_ref, sin_ref,
                    qT_ref, k_ref, vT_ref, qn_ref, kn_ref, vn_ref, g_ref):
    xb = x_ref[...].astype(BF16)

    def seg(lo, hi):
        return jnp.dot(xb, w_ref[:, lo:hi], preferred_element_type=F32) + b_ref[:, lo:hi]

    cos = cos_ref[...]
    sin = sin_ref[...]
    lane = lax.broadcasted_iota(jnp.int32, cos.shape, 1)
    first_half = (lane % DA_HEAD_DIM) < (DA_HEAD_DIM // 2)

    def rope(y):
        outs = []
        for h in range(DA_HEADS):
            yh = y[:, h * LANES:(h + 1) * LANES]
            partner = jnp.where(first_half,
                                pltpu.roll(yh, LANES - DA_HEAD_DIM // 2, 1),
                                pltpu.roll(yh, DA_HEAD_DIM // 2, 1))
            outs.append(yh * cos + partner * sin)
        return jnp.concatenate(outs, axis=1)

    w = DA_WIDTH
    q = rope(seg(0, w)) * (DA_HEAD_DIM ** -0.5 * LOG2E)
    qT_ref[...] = q.T.astype(BF16)
    k_ref[...] = rope(seg(w, 2 * w)).astype(BF16)
    vT_ref[...] = seg(2 * w, 3 * w).T.astype(BF16)
    o = 3 * w
    qn_ref[...] = (seg(o, o + NA_WIDTH) * (NA_HEAD_DIM ** -0.5)).T.astype(BF16)
    kn_ref[...] = seg(o + NA_WIDTH, o + 2 * NA_WIDTH).astype(BF16)
    vn_ref[...] = seg(o + 2 * NA_WIDTH, o + 3 * NA_WIDTH).T.astype(BF16)
    g0 = o + 3 * NA_WIDTH
    gate_pre = seg(g0, w_ref.shape[1])
    g_ref[...] = (1.0 / (1.0 + jnp.exp(-gate_pre))).astype(BF16)


def _in_proj(x, w_in, b_in, cos, sin):
    B, S, D = x.shape
    tm = min(PROJ_TM, S)
    n_cols = w_in.shape[1]
    n_gate = n_cols - 3 * DA_WIDTH - 3 * NA_WIDTH
    tok = lambda width: pl.BlockSpec((None, tm, width), lambda b, i: (b, i, 0))
    tr = pl.BlockSpec((None, DA_WIDTH, tm), lambda b, i: (b, 0, i))
    const = lambda shape: pl.BlockSpec(shape, lambda b, i: (0,) * len(shape))
    out_shape = (
        jax.ShapeDtypeStruct((B, DA_WIDTH, S), BF16),
        jax.ShapeDtypeStruct((B, S, DA_WIDTH), BF16),
        jax.ShapeDtypeStruct((B, DA_WIDTH, S), BF16),
        jax.ShapeDtypeStruct((B, NA_WIDTH, S), BF16),
        jax.ShapeDtypeStruct((B, S, NA_WIDTH), BF16),
        jax.ShapeDtypeStruct((B, NA_WIDTH, S), BF16),
        jax.ShapeDtypeStruct((B, S, n_gate), BF16),
    )
    assert DA_WIDTH == NA_WIDTH
    return pl.pallas_call(
        _in_proj_kernel,
        grid=(B, S // tm),
        in_specs=[tok(D), const((D, n_cols)), const((1, n_cols)),
                  pl.BlockSpec((tm, LANES), lambda b, i: (i, 0)),
                  pl.BlockSpec((tm, LANES), lambda b, i: (i, 0))],
        out_specs=(tr, tok(DA_WIDTH), tr, tr, tok(NA_WIDTH), tr, tok(n_gate)),
        out_shape=out_shape,
        compiler_params=_cparams(("parallel", "parallel")),
        name="in_proj",
    )(x, w_in, b_in, cos, sin)


def _diff_attn_kernel(qT_ref, qTn_ref, k_ref, vT_ref, lq1_ref, lk1_ref, lq2_ref, lk2_ref, g_ref,
                      o_ref, m_sc, l_sc, acc_sc, s_sc, cmax_sc, *, tk, unroll, lam_init):
    tq = qT_ref.shape[1]

    def block_diag(q_ref):
        qT = q_ref[...]
        row = lax.broadcasted_iota(jnp.int32, qT.shape, 0)
        zero = jnp.zeros_like(qT)
        return jnp.concatenate([jnp.where(row < DA_HEAD_DIM, qT, zero),
                                jnp.where(row < DA_HEAD_DIM, zero, qT)], axis=1)

    qbd = block_diag(qT_ref)

    m_sc[...] = jnp.full(m_sc.shape, -jnp.inf, F32)
    l_sc[...] = jnp.zeros(l_sc.shape, F32)
    acc_sc[...] = jnp.zeros(acc_sc.shape, F32)

    nk = k_ref.shape[0] // tk

    def put_scores(slot, j, q=None):
        off = pl.multiple_of(j * tk, tk)
        s = jnp.dot(k_ref[pl.ds(off, tk), :], qbd if q is None else q, preferred_element_type=F32)
        s_sc[slot] = s
        cmax_sc[slot] = jnp.max(s, axis=0, keepdims=True)

    def accumulate(slot, j):
        off = pl.multiple_of(j * tk, tk)
        m_prev = m_sc[...]
        m_new = jnp.maximum(m_prev, cmax_sc[slot])
        alpha = jnp.exp2(m_prev - m_new)
        p = jnp.exp2(s_sc[slot] - m_new)
        l_sc[...] = alpha * l_sc[...] + jnp.sum(p, axis=0, keepdims=True)
        vb = vT_ref[:, pl.ds(off, tk)]
        acc_sc[...] = alpha * acc_sc[...] + jnp.dot(vb, p.astype(BF16), preferred_element_type=F32)
        m_sc[...] = m_new

    @pl.when(pl.program_id(2) == 0)
    def _():
        put_scores(0, 0)

    def body(jj, carry):
        for u in range(unroll):
            j = unroll * jj + u
            put_scores((u + 1) % 2, j + 1)
            accumulate(u % 2, j)
        return carry

    lax.fori_loop(0, nk // unroll - 1, body, 0)
    for j in range(nk - unroll, nk):
        if j + 1 < nk:
            put_scores((j + 1) % 2, j + 1)
        else:
            put_scores(0, 0, block_diag(qTn_ref))
        accumulate(j % 2, j)

    lam = (jnp.exp(jnp.sum(lq1_ref[...] * lk1_ref[...], axis=1, keepdims=True))
           - jnp.exp(jnp.sum(lq2_ref[...] * lk2_ref[...], axis=1, keepdims=True)) + lam_init)
    on = acc_sc[...] / l_sc[...]
    o = on[:, :tq] - lam * on[:, tq:]
    ms = jnp.mean(o * o, axis=0, keepdims=True)
    o = o * lax.rsqrt(ms + RMS_EPS) * g_ref[...]
    o = o * (1.0 - lam_init)
    o_ref[...] = o.T.astype(BF16)


def _diff_attn(qT, k, vT, lq1, lk1, lq2, lk2, subln_g, lam_init):
    B, S, _ = k.shape
    tq = min(DA_TQ, S)
    tk = min(DA_TK, S // 2)
    unroll = math.gcd(DA_UNROLL, S // tk)
    assert unroll % 2 == 0 and S % tk == 0 and S % tq == 0
    nq = S // tq
    vec = pl.BlockSpec((1, DA_HEAD_DIM), lambda b, h, i: (0, 0))
    kernel = functools.partial(_diff_attn_kernel, tk=tk, unroll=unroll, lam_init=lam_init)
    return pl.pallas_call(
        kernel,
        grid=(B, DA_HEADS, nq),
        in_specs=[pl.BlockSpec((None, DA_V_DIM, tq), lambda b, h, i: (b, h, i)),
                  pl.BlockSpec((None, DA_V_DIM, tq), lambda b, h, i: (b, h, jnp.minimum(i + 1, nq - 1))),
                  pl.BlockSpec((None, S, DA_V_DIM), lambda b, h, i: (b, 0, h)),
                  pl.BlockSpec((None, DA_V_DIM, S), lambda b, h, i: (b, h, 0)),
                  vec, vec, vec, vec,
                  pl.BlockSpec((DA_V_DIM, 1), lambda b, h, i: (0, 0))],
        out_specs=pl.BlockSpec((None, tq, DA_V_DIM), lambda b, h, i: (b, i, h)),
        out_shape=jax.ShapeDtypeStruct((B, S, DA_WIDTH), BF16),
        scratch_shapes=[pltpu.VMEM((1, 2 * tq), F32), pltpu.VMEM((1, 2 * tq), F32),
                        pltpu.VMEM((DA_V_DIM, 2 * tq), F32), pltpu.VMEM((2, tk, 2 * tq), F32),
                        pltpu.VMEM((2, 1, 2 * tq), F32)],
        compiler_params=_cparams(("parallel", "parallel", "arbitrary")),
        name="diff_attn",
    )(qT, qT, k, vT, lq1, lk1, lq2, lk2, subln_g)


def _na_band_start(r0, rows):
    return jnp.clip(r0 - NA_KH // 2, 0, rows - NA_BAND)


def _na_tiles(rpb):
    cols = jnp.arange(GRID_W)
    cs = jnp.clip(cols - NA_KW // 2, 0, GRID_W - NA_KW)
    col_ok = (cols[:, None] >= cs[None, :]) & (cols[:, None] < cs[None, :] + NA_KW)
    col_off = jnp.clip(cols[:, None] - cols[None, :] + (NA_KW - 1), 0, 2 * NA_KW - 2)
    tiles = jnp.where(col_ok[None, None], rpb.astype(F32)[:, :, col_off], NEG_BIG)
    neg = jnp.full((rpb.shape[0], 1, GRID_W, GRID_W), NEG_BIG, F32)
    return jnp.concatenate([jnp.concatenate([tiles, neg], axis=1),
                            jnp.concatenate([neg, tiles], axis=1)], axis=3)


def _na_geometry(variant, rows):
    nblk = rows // NA_ROWS
    r0 = (0, NA_ROWS * min(1, nblk - 1), NA_ROWS * (nblk - 1))[variant]
    band = min(max(r0 - NA_KH // 2, 0), rows - NA_BAND)
    pairs = []
    for qp in range(NA_ROWS // 2):
        entries = []
        for kr in range(NA_BAND):
            ok = []
            for qr in (r0 + 2 * qp, r0 + 2 * qp + 1):
                rs = min(max(qr - NA_KH // 2, 0), rows - NA_KH)
                ok.append(rs <= band + kr < rs + NA_KH)
            if ok[0] or ok[1]:
                entries.append((kr, band + kr - (r0 + 2 * qp) + NA_KH - 1, ok[0], ok[1]))
        pairs.append(entries)
    return pairs


def _na_kernel(qT_ref, k_ref, vT_ref, tile_ref, o_ref, *, rows):
    i = pl.program_id(2)
    nblk = rows // NA_ROWS
    nk = NA_BAND * GRID_W
    off = pl.multiple_of(_na_band_start(i * NA_ROWS, rows) * GRID_W, 2 * GRID_W)

    def block(variant):
        for pair in range(NA_PAIRS):
            block_pair(variant, pair)

    def block_pair(variant, pair):
        c0 = pair * LANES
        kb = k_ref[pl.ds(off, nk), c0:c0 + LANES]
        vT = vT_ref[c0:c0 + LANES, pl.ds(off, nk)]
        lane = lax.broadcasted_iota(jnp.int32, kb.shape, 1)
        kbd = jnp.concatenate([jnp.where(lane < NA_HEAD_DIM, kb, jnp.zeros_like(kb)),
                               jnp.where(lane < NA_HEAD_DIM, jnp.zeros_like(kb), kb)], axis=0)
        row = lax.broadcasted_iota(jnp.int32, vT.shape, 0)
        vbdT = jnp.concatenate([jnp.where(row < NA_HEAD_DIM, vT, jnp.zeros_like(vT)),
                                jnp.where(row < NA_HEAD_DIM, jnp.zeros_like(vT), vT)], axis=1)
        s = jnp.dot(kbd, qT_ref[c0:c0 + LANES, :], preferred_element_type=F32)
        left = lax.broadcasted_iota(jnp.int32, (GRID_W, LANES), 1) < GRID_W
        zero_slab = jnp.zeros((GRID_W, LANES), BF16)
        geometry = _na_geometry(variant, rows)
        p_heads, l_heads = [], []
        for h in range(2):
            p_cols, l_cols = [], []
            for qp, entries in enumerate(geometry):
                logits = {}
                for kr, d, ok_l, ok_r in entries:
                    bias = tile_ref[2 * pair + h, d]
                    if not ok_r:
                        bias = jnp.where(left, bias, NEG_BIG)
                    if not ok_l:
                        bias = jnp.where(left, NEG_BIG, bias)
                    r = h * nk + kr * GRID_W
                    logits[kr] = s[r:r + GRID_W, qp * LANES:(qp + 1) * LANES] + bias
                m = functools.reduce(jnp.maximum, logits.values())
                m = jnp.max(m, axis=0, keepdims=True)
                probs = {kr: jnp.exp(x - m) for kr, x in logits.items()}
                l_cols.append(jnp.sum(functools.reduce(jnp.add, probs.values()), axis=0, keepdims=True))
                p_cols.append(jnp.concatenate(
                    [probs[kr].astype(BF16) if kr in probs else zero_slab for kr in range(NA_BAND)], axis=0))
            p_heads.append(jnp.concatenate(p_cols, axis=1))
            l_heads.append(jnp.concatenate(l_cols, axis=1))
        oT = jnp.dot(vbdT, jnp.concatenate(p_heads, axis=0), preferred_element_type=F32)
        row_o = lax.broadcasted_iota(jnp.int32, oT.shape, 0)
        oT = oT / jnp.where(row_o < NA_HEAD_DIM, l_heads[0], l_heads[1])
        o_ref[:, c0:c0 + LANES] = oT.T.astype(BF16)

    pl.when(i == 0)(lambda: block(0))
    pl.when(jnp.logical_and(i > 0, i < nblk - 1))(lambda: block(1))
    pl.when(jnp.logical_and(i > 0, i == nblk - 1))(lambda: block(2))


def _na_attn(qnT, kn, vnT, tiles):
    B, S, _ = kn.shape
    rows = S // GRID_W
    nblk = rows // NA_ROWS
    nq = NA_ROWS * GRID_W
    width = NA_PAIRS * LANES
    return pl.pallas_call(
        functools.partial(_na_kernel, rows=rows),
        grid=(B, NA_HEADS // (2 * NA_PAIRS), nblk),
        in_specs=[pl.BlockSpec((None, width, nq), lambda b, h, i: (b, h, i)),
                  pl.BlockSpec((None, S, width), lambda b, h, i: (b, 0, h)),
                  pl.BlockSpec((None, width, S), lambda b, h, i: (b, h, 0)),
                  pl.BlockSpec((2 * NA_PAIRS,) + tiles.shape[1:], lambda b, h, i: (h, 0, 0, 0))],
        out_specs=pl.BlockSpec((None, nq, width), lambda b, h, i: (b, i, h)),
        out_shape=jax.ShapeDtypeStruct((B, S, NA_WIDTH), BF16),
        compiler_params=_cparams(("parallel", "parallel", "arbitrary")),
        name="na_attn",
    )(qnT, kn, vnT, tiles)


def _layer_norm(h, g, b):
    mu = jnp.mean(h, axis=-1, keepdims=True)
    d = h - mu
    var = jnp.mean(d * d, axis=-1, keepdims=True)
    return d * lax.rsqrt(var + LN_EPS) * g + b


def _merge_kernel(x_ref, a_ref, nb_ref, g_ref, wda_ref, wna_ref, wout_ref, lng_ref, lnb_ref,
                  wr_ref, br_ref, x1_ref, x1p_ref, topi_ref, gate_ref, *, alpha):
    d = x_ref.shape[1]
    ya = jnp.dot(a_ref[...], wda_ref[...], preferred_element_type=F32)
    yb = jnp.dot(nb_ref[...], wna_ref[...], preferred_element_type=F32)
    g = g_ref[...].astype(F32)
    merged = g[:, :d] * ya + g[:, d:] * yb
    mix = jnp.dot(merged.astype(BF16), wout_ref[...], preferred_element_type=F32)
    x1 = _layer_norm(alpha * x_ref[...] + mix, lng_ref[...], lnb_ref[...])
    x1_ref[...] = x1
    x1p_ref[...] = _pack_bf16_pairs(x1)
    tm = x1.shape[0]
    xh = x1.astype(BF16)
    xl = (x1 - xh.astype(F32)).astype(BF16)
    r = jnp.dot(jnp.concatenate([xh, xl], axis=0), wr_ref[...], preferred_element_type=F32)
    logits = (r[:tm, :LANES] + r[:tm, LANES:]) + (r[tm:, :LANES] + r[tm:, LANES:]) + br_ref[...]
    lane = lax.broadcasted_iota(jnp.int32, logits.shape, 1).astype(F32)
    vals, idxs = [], []
    for _ in range(TOP_K):
        m = jnp.max(logits, axis=1, keepdims=True)
        idx = jnp.min(jnp.where(logits == m, lane, float(LANES)), axis=1, keepdims=True)
        vals.append(m)
        idxs.append(idx)
        logits = jnp.where(lane == idx, -jnp.inf, logits)
    es = [jnp.exp(v - vals[0]) for v in vals]
    denom = es[0] + es[1] + es[2] + es[3]
    gates = jnp.zeros(logits.shape, F32)
    topi = jnp.zeros(logits.shape, F32)
    for k in range(TOP_K):
        gates = jnp.where(lane == k, es[k] / denom, gates)
        topi = jnp.where(lane == k, idxs[k], topi)
    gate_ref[...] = gates
    topi_ref[...] = topi.astype(jnp.int32)


def _merge(x2, a2, nb2, g2, wda, wna, wout, ln_g, ln_b, wr, br, alpha):
    T, D = x2.shape
    tm = min(PROJ_TM, T)
    tok = lambda width: pl.BlockSpec((tm, width), lambda i: (i, 0))
    const = lambda shape: pl.BlockSpec(shape, lambda i: (0,) * len(shape))
    return pl.pallas_call(
        functools.partial(_merge_kernel, alpha=alpha),
        grid=(T // tm,),
        in_specs=[tok(D), tok(a2.shape[1]), tok(nb2.shape[1]), tok(g2.shape[1]),
                  const(wda.shape), const(wna.shape), const(wout.shape),
                  const((1, D)), const((1, D)), const(wr.shape), const((1, LANES))],
        out_specs=(tok(D), tok(D // 2), tok(LANES), tok(LANES)),
        out_shape=(jax.ShapeDtypeStruct((T, D), F32), jax.ShapeDtypeStruct((T, D // 2), jnp.uint32),
                   jax.ShapeDtypeStruct((T, LANES), jnp.int32), jax.ShapeDtypeStruct((T, LANES), F32)),
        compiler_params=_cparams(("parallel",)),
        name="merge",
    )(x2, a2, nb2, g2, wda, wna, wout, ln_g, ln_b, wr, br)


def _glu_group_bias(b1):
    e, n = b1.shape
    return b1.reshape(e, n // (2 * LANES), LANES, 2).transpose(0, 1, 3, 2).reshape(e, 1, n)


def _moe_kernel(be_ref, nused_ref, xs_ref, w1_ref, b1_ref, w2_ref, b2_ref, ys_ref, w1g_sc, w2b_sc):
    i = pl.program_id(0)
    used = i < nused_ref[0]
    fresh = jnp.logical_or(i == 0, be_ref[i] != be_ref[jnp.maximum(i - 1, 0)])
    grp = 2 * LANES

    @pl.when(jnp.logical_and(used, fresh))
    def _():
        r = lax.broadcasted_iota(jnp.int32, (grp, grp), 0)
        c = lax.broadcasted_iota(jnp.int32, (grp, grp), 1)
        src = jnp.where(c < LANES, 2 * c, 2 * (c - LANES) + 1)
        perm = jnp.where(r == src, 1.0, 0.0).astype(BF16)
        for g in range(w1_ref.shape[1] // grp):
            cols = w1_ref[:, g * grp:(g + 1) * grp].astype(BF16)
            w1g_sc[:, g * grp:(g + 1) * grp] = jnp.dot(cols, perm, preferred_element_type=F32).astype(BF16)
        w2b_sc[...] = w2_ref[...].astype(BF16)

    @pl.when(used)
    def _():
        xb = _unpack_bf16_pairs(xs_ref[...]).astype(BF16)
        h = jnp.dot(xb, w1g_sc[...], preferred_element_type=F32) + b1_ref[...]
        acts = []
        for g in range(h.shape[1] // grp):
            x_glu = jnp.minimum(h[:, g * grp:g * grp + LANES], SWIGLU_LIMIT)
            x_lin = jnp.clip(h[:, g * grp + LANES:(g + 1) * grp], -SWIGLU_LIMIT, SWIGLU_LIMIT)
            acts.append(x_glu * (1.0 / (1.0 + jnp.exp(-SWIGLU_ALPHA * x_glu))) * (x_lin + 1.0))
        act = jnp.concatenate(acts, axis=1).astype(BF16)
        y = jnp.dot(act, w2b_sc[...], preferred_element_type=F32) + b2_ref[...]
        ys_ref[...] = _pack_bf16_pairs(y)

    @pl.when(jnp.logical_not(used))
    def _():
        ys_ref[...] = jnp.zeros(ys_ref.shape, ys_ref.dtype)


def _moe_ffn(xs, block_expert, n_used, w1, b1g, w2, b2):
    P, half = xs.shape
    D = 2 * half
    de2 = w1.shape[2]
    de = w2.shape[1]
    wspec = lambda r, c: pl.BlockSpec((None, r, c), lambda i, be, nu: (be[i], 0, 0))
    grid_spec = pltpu.PrefetchScalarGridSpec(
        num_scalar_prefetch=2,
        grid=(P // MOE_BLOCK,),
        in_specs=[pl.BlockSpec((MOE_BLOCK, half), lambda i, be, nu: (i, 0)),
                  wspec(D, de2), wspec(1, de2), wspec(de, D), wspec(1, D)],
        out_specs=pl.BlockSpec((MOE_BLOCK, half), lambda i, be, nu: (i, 0)),
        scratch_shapes=[pltpu.VMEM((D, de2), BF16), pltpu.VMEM((de, D), BF16)],
    )
    return pl.pallas_call(
        _moe_kernel,
        grid_spec=grid_spec,
        out_shape=jax.ShapeDtypeStruct((P, half), jnp.uint32),
        compiler_params=_cparams(("arbitrary",)),
        name="moe_ffn",
    )(block_expert, n_used, xs, w1, b1g, w2, b2)


def _combine_kernel(x1_ref, y0_ref, y1_ref, y2_ref, y3_ref, gate_ref, lng_ref, lnb_ref, o_ref, *, alpha):
    gates = gate_ref[...]
    ffn = jnp.zeros(x1_ref.shape, F32)
    for k, y_ref in enumerate((y0_ref, y1_ref, y2_ref, y3_ref)):
        ffn = ffn + _unpack_bf16_pairs(y_ref[...]) * gates[:, k:k + 1]
    o_ref[...] = _layer_norm(alpha * x1_ref[...] + ffn, lng_ref[...], lnb_ref[...])


def _combine(x1, y_slots, gates, ln_g, ln_b, alpha):
    T, D = x1.shape
    tm = min(PROJ_TM, T)
    tok = lambda width: pl.BlockSpec((tm, width), lambda i: (i, 0))
    const = lambda shape: pl.BlockSpec(shape, lambda i: (0,) * len(shape))
    slot = lambda k: pl.BlockSpec((None, tm, D // 2), lambda i: (k, i, 0))
    return pl.pallas_call(
        functools.partial(_combine_kernel, alpha=alpha),
        grid=(T // tm,),
        in_specs=[tok(D)] + [slot(k) for k in range(TOP_K)] + [tok(LANES), const((1, D)), const((1, D))],
        out_specs=tok(D),
        out_shape=jax.ShapeDtypeStruct((T, D), F32),
        compiler_params=_cparams(("parallel",)),
        name="combine",
    )(x1, y_slots, y_slots, y_slots, y_slots, gates, ln_g, ln_b)


def _route(top_i, n_experts):
    T = top_i.shape[0]
    TK = T * TOP_K
    experts = jnp.arange(n_experts, dtype=jnp.int32)
    chosen = top_i[:, :, None] == experts
    picks = jnp.sum(chosen, axis=1, dtype=jnp.int32)
    csum = jnp.cumsum(picks, axis=0)
    counts = csum[-1]
    padded = ((counts + MOE_BLOCK - 1) // MOE_BLOCK) * MOE_BLOCK
    padded_end = jnp.cumsum(padded)
    padded_start = padded_end - padded
    slot_te = padded_start[None, :] + csum - picks
    dest = jnp.sum(jnp.where(chosen, slot_te[:, None, :], 0), axis=2)
    n_blocks = -(-TK // MOE_BLOCK) + n_experts
    blk_first = jnp.arange(n_blocks, dtype=jnp.int32) * MOE_BLOCK
    block_expert = jnp.minimum(jnp.sum(padded_end[None, :] <= blk_first[:, None], axis=1),
                               n_experts - 1).astype(jnp.int32)
    n_used = (padded_end[-1] // MOE_BLOCK).astype(jnp.int32).reshape(1)
    pad = padded - counts
    pad_end = jnp.cumsum(pad)
    j = jnp.arange(n_blocks * MOE_BLOCK - TK, dtype=jnp.int32)
    owner = j[:, None] >= pad_end[None, :]
    group = jnp.sum(owner, axis=1)
    mine = group[:, None] == experts[None, :]
    in_group = jnp.sum(jnp.where(mine, (padded_start + counts - (pad_end - pad))[None, :], 0), axis=1) + j
    pad_slots = jnp.where(group < n_experts, in_group, padded_end[-1] + j - pad_end[-1])
    return block_expert, n_used, dest.T.reshape(-1), pad_slots.astype(jnp.int32)


def _rope_tables(seq):
    pos = jnp.arange(seq, dtype=F32)
    inv = ROPE_THETA ** (-jnp.arange(0, DA_HEAD_DIM, 2, dtype=F32) / DA_HEAD_DIM)
    ang = pos[:, None] * inv[None, :]
    ang = jnp.concatenate([ang, ang], axis=-1)
    cos, sin = jnp.cos(ang), jnp.sin(ang)
    half = DA_HEAD_DIM // 2
    sin_signed = jnp.concatenate([-sin[:, :half], sin[:, half:]], axis=-1)
    return jnp.tile(cos, (1, 2)), jnp.tile(sin_signed, (1, 2))


def kernel(x, w_in, b_in, lambda_q1, lambda_k1, lambda_q2, lambda_k2, subln_g, rpb, w_branch_da, w_branch_na, w_out, ln1_g, ln1_b, w_router, b_router, w_mlp1, b_mlp1, w_mlp2, b_mlp2, ln2_g, ln2_b):
    B, S, D = x.shape
    depth = w_in.shape[0]
    n_experts = w_router.shape[2]
    T = B * S
    rows = S // GRID_W
    assert S % GRID_W == 0 and rows % NA_ROWS == 0 and rows >= NA_BAND
    alpha = (2 * depth) ** 0.25
    cos, sin = _rope_tables(S)
    row = lambda v: v.reshape(1, -1)

    for l in range(depth):
        lam_init = 0.8 - 0.6 * math.exp(-0.3 * l)
        qT, k, vT, qn, kn, vn, gates_br = _in_proj(x, w_in[l].astype(BF16), row(b_in[l]), cos, sin)
        a = _diff_attn(qT, k, vT, row(lambda_q1[l]), row(lambda_k1[l]), row(lambda_q2[l]),
                       row(lambda_k2[l]), subln_g[l].reshape(-1, 1), lam_init)
        nb = _na_attn(qn, kn, vn, _na_tiles(rpb[l]))

        wr = jnp.pad(w_router[l], ((0, 0), (0, LANES - n_experts)))
        wr_h = wr.astype(BF16)
        wr = jnp.concatenate([wr_h, (wr - wr_h.astype(F32)).astype(BF16)], axis=1)
        br = jnp.pad(row(b_router[l]), ((0, 0), (0, LANES - n_experts)), constant_values=NEG_BIG)
        x1, x1p, topi, gates = _merge(
            x.reshape(T, D), a.reshape(T, -1), nb.reshape(T, -1), gates_br.reshape(T, -1),
            w_branch_da[l].astype(BF16), w_branch_na[l].astype(BF16), w_out[l].astype(BF16),
            row(ln1_g[l]), row(ln1_b[l]), wr, br, alpha)

        block_expert, n_used, dest, pad_slots = _route(topi[:, :TOP_K], n_experts)
        xs = _sc_scatter_rows(x1p, dest, pad_slots)
        ys = _moe_ffn(xs, block_expert, n_used, w_mlp1[l], _glu_group_bias(b_mlp1[l]),
                      w_mlp2[l], b_mlp2[l][:, None, :])
        y_slots = _sc_gather_rows(ys, dest).reshape(TOP_K, T, D // 2)
        x = _combine(x1, y_slots, gates, row(ln2_g[l]), row(ln2_b[l]), alpha).reshape(B, S, D)
    return x
```

```python
import functools
import math

import jax
import jax.numpy as jnp
from jax import lax
from jax.experimental import pallas as pl
from jax.experimental.pallas import tpu as pltpu
from jax.experimental.pallas import tpu_sc as plsc

F32 = jnp.float32
BF16 = jnp.bfloat16

GRID_W = 64
DA_HEADS = 4
DA_HEAD_DIM = 64
DA_V_DIM = 2 * DA_HEAD_DIM
DA_WIDTH = DA_HEADS * DA_V_DIM
ROPE_THETA = 10000.0
NA_HEADS = 8
NA_HEAD_DIM = 64
NA_WIDTH = NA_HEADS * NA_HEAD_DIM
NA_KH = 8
NA_KW = 16
TOP_K = 4
SWIGLU_ALPHA = 1.702
SWIGLU_LIMIT = 7.0
MOE_BLOCK = 512
LN_EPS = 1e-5
RMS_EPS = 1e-5

LANES = 128
VMEM_LIMIT_BYTES = 56 * 1024 * 1024
SC_CORES = 2
SC_SUBCORES = 16
SC_GATHER_ROWS = 64

LOG2E = math.log2(math.e)
NEG_BIG = -1e30

PROJ_TM = 512
DA_TQ = 512
DA_TK = 512
DA_UNROLL = 8
DA_HEADS_PER_STEP = 2
NA_ROWS = 8
NA_BAND = 16
NA_PAIRS = 4


def _cparams(sem):
    return pltpu.CompilerParams(dimension_semantics=sem, vmem_limit_bytes=VMEM_LIMIT_BYTES)


def _pack_bf16_pairs(y):
    w = y.shape[1] // 2
    bits = lax.bitcast_convert_type(y, jnp.uint32)
    rounded = bits + jnp.uint32(0x7FFF) + ((bits >> 16) & jnp.uint32(1))
    return (rounded[:, :w] >> 16) | (rounded[:, w:] & jnp.uint32(0xFFFF0000))


def _unpack_bf16_pairs(words):
    lo = lax.bitcast_convert_type(words << 16, F32)
    hi = lax.bitcast_convert_type(words & jnp.uint32(0xFFFF0000), F32)
    return jnp.concatenate([lo, hi], axis=1)


def _sc_gather_rows(table, idx):
    n, w = idx.shape[0], table.shape[1]
    rows = SC_GATHER_ROWS
    workers = SC_CORES * SC_SUBCORES
    per_worker = n // workers
    n_pairs = per_worker // (2 * rows)
    assert n % (workers * 2 * rows) == 0 and n_pairs >= 1
    mesh = plsc.VectorSubcoreMesh(core_axis_name="c", subcore_axis_name="s",
                                  num_cores=SC_CORES, num_subcores=SC_SUBCORES)

    def body(table_hbm, idx_hbm, out_hbm, idx0, idx1, rows0, rows1, sem0, sem1):
        base = (lax.axis_index("s") * SC_CORES + lax.axis_index("c")) * per_worker
        bufs = ((idx0, rows0, sem0), (idx1, rows1, sem1))

        def gather(b):
            idx_v, rows_v, sem = bufs[b]
            return pltpu.make_async_copy(table_hbm.at[idx_v], rows_v, sem)

        def start(c, b):
            pltpu.sync_copy(idx_hbm.at[pl.ds(base + c * rows, rows)], bufs[b][0])
            gather(b).start()

        def finish(c, b):
            gather(b).wait()
            pltpu.sync_copy(bufs[b][1], out_hbm.at[pl.ds(base + c * rows, rows)])

        start(0, 0)

        @pl.loop(0, n_pairs - 1)
        def _(g):
            c = 2 * g
            start(c + 1, 1)
            finish(c, 0)
            start(c + 2, 0)
            finish(c + 1, 1)

        last = 2 * (n_pairs - 1)
        start(last + 1, 1)
        finish(last, 0)
        finish(last + 1, 1)

    return pl.kernel(
        body, out_type=jax.ShapeDtypeStruct((n, w), table.dtype), mesh=mesh,
        scratch_types=[pltpu.VMEM((rows,), jnp.int32), pltpu.VMEM((rows,), jnp.int32),
                       pltpu.VMEM((rows, w), table.dtype), pltpu.VMEM((rows, w), table.dtype),
                       pltpu.SemaphoreType.DMA, pltpu.SemaphoreType.DMA],
        name="sc_gather_rows",
    )(table, idx)


def _sc_scatter_rows(table, dest, rest):
    t, w = table.shape
    copies = dest.shape[0] // t
    n_out = dest.shape[0] + rest.shape[0]
    rows = SC_GATHER_ROWS
    workers = SC_CORES * SC_SUBCORES
    per_worker = t // workers
    n_pairs = per_worker // (2 * rows)
    rest_per_worker = rest.shape[0] // workers
    assert t % (workers * 2 * rows) == 0 and n_pairs >= 1 and rest.shape[0] % (workers * rows) == 0
    mesh = plsc.VectorSubcoreMesh(core_axis_name="c", subcore_axis_name="s",
                                  num_cores=SC_CORES, num_subcores=SC_SUBCORES)

    n_fill = rest_per_worker // rows

    def body(table_hbm, dest_hbm, rest_hbm, zeros_hbm, out_hbm, *scratch):
        worker = lax.axis_index("s") * SC_CORES + lax.axis_index("c")
        base = worker * per_worker
        per_buf = copies + 2
        bufs = (scratch[:per_buf], scratch[per_buf:2 * per_buf])
        zero_v, zero_sem = scratch[2 * per_buf:2 * per_buf + 2]
        fill_idx = scratch[2 * per_buf + 2:]

        def fills():
            return [pltpu.make_async_copy(zero_v, out_hbm.at[idx_v], zero_sem) for idx_v in fill_idx]

        pltpu.sync_copy(zeros_hbm, zero_v)
        for c, idx_v in enumerate(fill_idx):
            pltpu.sync_copy(rest_hbm.at[pl.ds(worker * rest_per_worker + c * rows, rows)], idx_v)
        for fill in fills():
            fill.start()

        def scatters(b):
            rows_v, sem = bufs[b][0], bufs[b][1]
            return [pltpu.make_async_copy(rows_v, out_hbm.at[idx_v], sem) for idx_v in bufs[b][2:]]

        def start(c, b):
            t0 = base + c * rows
            pltpu.sync_copy(table_hbm.at[pl.ds(t0, rows)], bufs[b][0])
            for k, idx_v in enumerate(bufs[b][2:]):
                pltpu.sync_copy(dest_hbm.at[pl.ds(k * t + t0, rows)], idx_v)
            for copy in scatters(b):
                copy.start()

        def finish(b):
            for copy in scatters(b):
                copy.wait()

        start(0, 0)

        @pl.loop(0, n_pairs - 1)
        def _(g):
            start(2 * g + 1, 1)
            finish(0)
            start(2 * g + 2, 0)
            finish(1)

        start(2 * n_pairs - 1, 1)
        finish(0)
        finish(1)

        for fill in fills():
            fill.wait()

    one_buf = ([pltpu.VMEM((rows, w), table.dtype), pltpu.SemaphoreType.DMA]
               + [pltpu.VMEM((rows,), jnp.int32) for _ in range(copies)])
    zero_buf = ([pltpu.VMEM((rows, w), table.dtype), pltpu.SemaphoreType.DMA]
                + [pltpu.VMEM((rows,), jnp.int32) for _ in range(n_fill)])
    return pl.kernel(
        body, out_type=jax.ShapeDtypeStruct((n_out, w), table.dtype), mesh=mesh,
        scratch_types=one_buf + one_buf + zero_buf, name="sc_scatter_rows",
    )(table, dest, rest, jnp.zeros((rows, w), table.dtype))


def _in_proj_kernel(x_ref, w_ref, b_ref, cos_ref, sin_ref,
                    qT_ref, k_ref, vT_ref, qn_ref, kn_ref, vn_ref, g_ref):
    xb = x_ref[...].astype(BF16)

    def seg(lo, hi):
        return jnp.dot(xb, w_ref[:, lo:hi], preferred_element_type=F32) + b_ref[:, lo:hi]

    cos = cos_ref[...]
    sin = sin_ref[...]
    lane = lax.broadcasted_iota(jnp.int32, cos.shape, 1)
    first_half = (lane % DA_HEAD_DIM) < (DA_HEAD_DIM // 2)

    def rope(y):
        outs = []
        for h in range(DA_HEADS):
            yh = y[:, h * LANES:(h + 1) * LANES]
            partner = jnp.where(first_half,
                                pltpu.roll(yh, LANES - DA_HEAD_DIM // 2, 1),
                                pltpu.roll(yh, DA_HEAD_DIM // 2, 1))
            outs.append(yh * cos + partner * sin)
        return jnp.concatenate(outs, axis=1)

    w = DA_WIDTH
    q = rope(seg(0, w)) * (DA_HEAD_DIM ** -0.5 * LOG2E)
    qT_ref[...] = q.T.astype(BF16)
    k_ref[...] = rope(seg(w, 2 * w)).astype(BF16)
    vT_ref[...] = seg(2 * w, 3 * w).T.astype(BF16)
    o = 3 * w
    qn_ref[...] = (seg(o, o + NA_WIDTH) * (NA_HEAD_DIM ** -0.5)).T.astype(BF16)
    kn_ref[...] = seg(o + NA_WIDTH, o + 2 * NA_WIDTH).astype(BF16)
    vn_ref[...] = seg(o + 2 * NA_WIDTH, o + 3 * NA_WIDTH).T.astype(BF16)
    g0 = o + 3 * NA_WIDTH
    gate_pre = seg(g0, w_ref.shape[1])
    g_ref[...] = (1.0 / (1.0 + jnp.exp(-gate_pre))).astype(BF16)


def _in_proj(x, w_in, b_in, cos, sin):
    B, S, D = x.shape
    tm = min(PROJ_TM, S)
    n_cols = w_in.shape[1]
    n_gate = n_cols - 3 * DA_WIDTH - 3 * NA_WIDTH
    tok = lambda width: pl.BlockSpec((None, tm, width), lambda b, i: (b, i, 0))
    tr = pl.BlockSpec((None, DA_WIDTH, tm), lambda b, i: (b, 0, i))
    const = lambda shape: pl.BlockSpec(shape, lambda b, i: (0,) * len(shape))
    out_shape = (
        jax.ShapeDtypeStruct((B, DA_WIDTH, S), BF16),
        jax.ShapeDtypeStruct((B, S, DA_WIDTH), BF16),
        jax.ShapeDtypeStruct((B, DA_WIDTH, S), BF16),
        jax.ShapeDtypeStruct((B, NA_WIDTH, S), BF16),
        jax.ShapeDtypeStruct((B, S, NA_WIDTH), BF16),
        jax.ShapeDtypeStruct((B, NA_WIDTH, S), BF16),
        jax.ShapeDtypeStruct((B, S, n_gate), BF16),
    )
    assert DA_WIDTH == NA_WIDTH
    return pl.pallas_call(
        _in_proj_kernel,
        grid=(B, S // tm),
        in_specs=[tok(D), const((D, n_cols)), const((1, n_cols)),
                  pl.BlockSpec((tm, LANES), lambda b, i: (i, 0)),
                  pl.BlockSpec((tm, LANES), lambda b, i: (i, 0))],
        out_specs=(tr, tok(DA_WIDTH), tr, tr, tok(NA_WIDTH), tr, tok(n_gate)),
        out_shape=out_shape,
        compiler_params=_cparams(("parallel", "parallel")),
        name="in_proj",
    )(x, w_in, b_in, cos, sin)


def _diff_attn_kernel(qT_ref, qTn_ref, k_ref, vT_ref, lq1_ref, lk1_ref, lq2_ref, lk2_ref, g_ref,
                      o_ref, m_sc, l_sc, acc_sc, s_sc, cmax_sc, *, tk, unroll, lam_init):
    tq = qT_ref.shape[1]
    nk = k_ref.shape[0] // tk
    heads = qT_ref.shape[0] // DA_V_DIM

    def block_diag(q_ref, h):
        qT = q_ref[h * DA_V_DIM:(h + 1) * DA_V_DIM, :]
        row = lax.broadcasted_iota(jnp.int32, qT.shape, 0)
        zero = jnp.zeros_like(qT)
        return jnp.concatenate([jnp.where(row < DA_HEAD_DIM, qT, zero),
                                jnp.where(row < DA_HEAD_DIM, zero, qT)], axis=1)

    def put_scores(slot, j, q, h):
        off = pl.multiple_of(j * tk, tk)
        s = jnp.dot(k_ref[pl.ds(off, tk), h * DA_V_DIM:(h + 1) * DA_V_DIM], q, preferred_element_type=F32)
        s_sc[slot] = s
        cmax_sc[slot] = jnp.max(s, axis=0, keepdims=True)

    def accumulate(slot, j, h):
        off = pl.multiple_of(j * tk, tk)
        m_prev = m_sc[...]
        m_new = jnp.maximum(m_prev, cmax_sc[slot])
        alpha = jnp.exp2(m_prev - m_new)
        p = jnp.exp2(s_sc[slot] - m_new)
        l_sc[...] = alpha * l_sc[...] + jnp.sum(p, axis=0, keepdims=True)
        vb = vT_ref[h * DA_V_DIM:(h + 1) * DA_V_DIM, pl.ds(off, tk)]
        acc_sc[...] = alpha * acc_sc[...] + jnp.dot(vb, p.astype(BF16), preferred_element_type=F32)
        m_sc[...] = m_new

    lam = (jnp.exp(jnp.sum(lq1_ref[...] * lk1_ref[...], axis=1, keepdims=True))
           - jnp.exp(jnp.sum(lq2_ref[...] * lk2_ref[...], axis=1, keepdims=True)) + lam_init)

    @pl.when(pl.program_id(2) == 0)
    def _():
        put_scores(0, 0, block_diag(qT_ref, 0), 0)

    for h in range(heads):
        qbd = block_diag(qT_ref, h)
        m_sc[...] = jnp.full(m_sc.shape, -jnp.inf, F32)
        l_sc[...] = jnp.zeros(l_sc.shape, F32)
        acc_sc[...] = jnp.zeros(acc_sc.shape, F32)

        def body(jj, carry, h=h, qbd=qbd):
            for u in range(unroll):
                j = unroll * jj + u
                put_scores((u + 1) % 2, j + 1, qbd, h)
                accumulate(u % 2, j, h)
            return carry

        lax.fori_loop(0, nk // unroll - 1, body, 0)
        for j in range(nk - unroll, nk):
            if j + 1 < nk:
                put_scores((j + 1) % 2, j + 1, qbd, h)
            elif h + 1 < heads:
                put_scores(0, 0, block_diag(qT_ref, h + 1), h + 1)
            else:
                put_scores(0, 0, block_diag(qTn_ref, 0), 0)
            accumulate(j % 2, j, h)

        on = acc_sc[...] / l_sc[...]
        o = on[:, :tq] - lam * on[:, tq:]
        ms = jnp.mean(o * o, axis=0, keepdims=True)
        o = o * lax.rsqrt(ms + RMS_EPS) * g_ref[...]
        o = o * (1.0 - lam_init)
        o_ref[:, h * DA_V_DIM:(h + 1) * DA_V_DIM] = o.T.astype(BF16)


def _diff_attn(qT, k, vT, lq1, lk1, lq2, lk2, subln_g, lam_init):
    B, S, _ = k.shape
    tq = min(DA_TQ, S)
    tk = min(DA_TK, S // 2)
    unroll = math.gcd(DA_UNROLL, S // tk)
    assert unroll % 2 == 0 and S % tk == 0 and S % tq == 0
    nq = S // tq
    width = DA_HEADS_PER_STEP * DA_V_DIM
    vec = pl.BlockSpec((1, DA_HEAD_DIM), lambda b, h, i: (0, 0))
    kernel = functools.partial(_diff_attn_kernel, tk=tk, unroll=unroll, lam_init=lam_init)
    return pl.pallas_call(
        kernel,
        grid=(B, DA_HEADS // DA_HEADS_PER_STEP, nq),
        in_specs=[pl.BlockSpec((None, width, tq), lambda b, h, i: (b, h, i)),
                  pl.BlockSpec((None, width, tq), lambda b, h, i: (b, h, jnp.minimum(i + 1, nq - 1))),
                  pl.BlockSpec((None, S, width), lambda b, h, i: (b, 0, h)),
                  pl.BlockSpec((None, width, S), lambda b, h, i: (b, h, 0)),
                  vec, vec, vec, vec,
                  pl.BlockSpec((DA_V_DIM, 1), lambda b, h, i: (0, 0))],
        out_specs=pl.BlockSpec((None, tq, width), lambda b, h, i: (b, i, h)),
        out_shape=jax.ShapeDtypeStruct((B, S, DA_WIDTH), BF16),
        scratch_shapes=[pltpu.VMEM((1, 2 * tq), F32), pltpu.VMEM((1, 2 * tq), F32),
                        pltpu.VMEM((DA_V_DIM, 2 * tq), F32), pltpu.VMEM((2, tk, 2 * tq), F32),
                        pltpu.VMEM((2, 1, 2 * tq), F32)],
        compiler_params=_cparams(("parallel", "parallel", "arbitrary")),
        name="diff_attn",
    )(qT, qT, k, vT, lq1, lk1, lq2, lk2, subln_g)


def _na_band_start(r0, rows):
    return jnp.clip(r0 - NA_KH // 2, 0, rows - NA_BAND)


def _na_tiles(rpb):
    cols = jnp.arange(GRID_W)
    cs = jnp.clip(cols - NA_KW // 2, 0, GRID_W - NA_KW)
    col_ok = (cols[:, None] >= cs[None, :]) & (cols[:, None] < cs[None, :] + NA_KW)
    col_off = jnp.clip(cols[:, None] - cols[None, :] + (NA_KW - 1), 0, 2 * NA_KW - 2)
    tiles = jnp.where(col_ok[None, None], rpb.astype(F32)[:, :, col_off], NEG_BIG)
    neg = jnp.full((rpb.shape[0], 1, GRID_W, GRID_W), NEG_BIG, F32)
    return jnp.concatenate([jnp.concatenate([tiles, neg], axis=1),
                            jnp.concatenate([neg, tiles], axis=1)], axis=3)


def _na_geometry(variant, rows):
    nblk = rows // NA_ROWS
    r0 = (0, NA_ROWS * min(1, nblk - 1), NA_ROWS * (nblk - 1))[variant]
    band = min(max(r0 - NA_KH // 2, 0), rows - NA_BAND)
    pairs = []
    for qp in range(NA_ROWS // 2):
        entries = []
        for kr in range(NA_BAND):
            ok = []
            for qr in (r0 + 2 * qp, r0 + 2 * qp + 1):
                rs = min(max(qr - NA_KH // 2, 0), rows - NA_KH)
                ok.append(rs <= band + kr < rs + NA_KH)
            if ok[0] or ok[1]:
                entries.append((kr, band + kr - (r0 + 2 * qp) + NA_KH - 1, ok[0], ok[1]))
        pairs.append(entries)
    return pairs


def _na_kernel(qT_ref, k_ref, vT_ref, tile_ref, o_ref, *, rows):
    i = pl.program_id(2)
    nblk = rows // NA_ROWS
    nk = NA_BAND * GRID_W
    off = pl.multiple_of(_na_band_start(i * NA_ROWS, rows) * GRID_W, 2 * GRID_W)

    def block(variant):
        for pair in range(NA_PAIRS):
            block_pair(variant, pair)

    def block_pair(variant, pair):
        c0 = pair * LANES
        kb = k_ref[pl.ds(off, nk), c0:c0 + LANES]
        vT = vT_ref[c0:c0 + LANES, pl.ds(off, nk)]
        lane = lax.broadcasted_iota(jnp.int32, kb.shape, 1)
        kbd = jnp.concatenate([jnp.where(lane < NA_HEAD_DIM, kb, jnp.zeros_like(kb)),
                               jnp.where(lane < NA_HEAD_DIM, jnp.zeros_like(kb), kb)], axis=0)
        row = lax.broadcasted_iota(jnp.int32, vT.shape, 0)
        vbdT = jnp.concatenate([jnp.where(row < NA_HEAD_DIM, vT, jnp.zeros_like(vT)),
                                jnp.where(row < NA_HEAD_DIM, jnp.zeros_like(vT), vT)], axis=1)
        s = jnp.dot(kbd, qT_ref[c0:c0 + LANES, :], preferred_element_type=F32)
        left = lax.broadcasted_iota(jnp.int32, (GRID_W, LANES), 1) < GRID_W
        zero_slab = jnp.zeros((GRID_W, LANES), BF16)
        geometry = _na_geometry(variant, rows)
        p_heads, l_heads = [], []
        for h in range(2):
            p_cols, l_cols = [], []
            for qp, entries in enumerate(geometry):
                logits = {}
                for kr, d, ok_l, ok_r in entries:
                    bias = tile_ref[2 * pair + h, d]
                    if not ok_r:
                        bias = jnp.where(left, bias, NEG_BIG)
                    if not ok_l:
                        bias = jnp.where(left, NEG_BIG, bias)
                    r = h * nk + kr * GRID_W
                    logits[kr] = s[r:r + GRID_W, qp * LANES:(qp + 1) * LANES] + bias
                m = functools.reduce(jnp.maximum, logits.values())
                m = jnp.max(m, axis=0, keepdims=True)
                probs = {kr: jnp.exp(x - m) for kr, x in logits.items()}
                l_cols.append(jnp.sum(functools.reduce(jnp.add, probs.values()), axis=0, keepdims=True))
                p_cols.append(jnp.concatenate(
                    [probs[kr].astype(BF16) if kr in probs else zero_slab for kr in range(NA_BAND)], axis=0))
            p_heads.append(jnp.concatenate(p_cols, axis=1))
            l_heads.append(jnp.concatenate(l_cols, axis=1))
        oT = jnp.dot(vbdT, jnp.concatenate(p_heads, axis=0), preferred_element_type=F32)
        row_o = lax.broadcasted_iota(jnp.int32, oT.shape, 0)
        oT = oT / jnp.where(row_o < NA_HEAD_DIM, l_heads[0], l_heads[1])
        o_ref[:, c0:c0 + LANES] = oT.T.astype(BF16)

    pl.when(i == 0)(lambda: block(0))
    pl.when(jnp.logical_and(i > 0, i < nblk - 1))(lambda: block(1))
    pl.when(jnp.logical_and(i > 0, i == nblk - 1))(lambda: block(2))


def _na_attn(qnT, kn, vnT, tiles):
    B, S, _ = kn.shape
    rows = S // GRID_W
    nblk = rows // NA_ROWS
    nq = NA_ROWS * GRID_W
    width = NA_PAIRS * LANES
    return pl.pallas_call(
        functools.partial(_na_kernel, rows=rows),
        grid=(B, NA_HEADS // (2 * NA_PAIRS), nblk),
        in_specs=[pl.BlockSpec((None, width, nq), lambda b, h, i: (b, h, i)),
                  pl.BlockSpec((None, S, width), lambda b, h, i: (b, 0, h)),
                  pl.BlockSpec((None, width, S), lambda b, h, i: (b, h, 0)),
                  pl.BlockSpec((2 * NA_PAIRS,) + tiles.shape[1:], lambda b, h, i: (h, 0, 0, 0))],
        out_specs=pl.BlockSpec((None, nq, width), lambda b, h, i: (b, i, h)),
        out_shape=jax.ShapeDtypeStruct((B, S, NA_WIDTH), BF16),
        compiler_params=_cparams(("parallel", "parallel", "arbitrary")),
        name="na_attn",
    )(qnT, kn, vnT, tiles)


def _layer_norm(h, g, b):
    mu = jnp.mean(h, axis=-1, keepdims=True)
    d = h - mu
    var = jnp.mean(d * d, axis=-1, keepdims=True)
    return d * lax.rsqrt(var + LN_EPS) * g + b


def _merge_kernel(x_ref, a_ref, nb_ref, g_ref, wda_ref, wna_ref, wout_ref, lng_ref, lnb_ref,
                  wr_ref, br_ref, x1_ref, x1p_ref, topi_ref, gate_ref, *, alpha):
    d = x_ref.shape[1]
    ya = jnp.dot(a_ref[...], wda_ref[...], preferred_element_type=F32)
    yb = jnp.dot(nb_ref[...], wna_ref[...], preferred_element_type=F32)
    g = g_ref[...].astype(F32)
    merged = g[:, :d] * ya + g[:, d:] * yb
    mix = jnp.dot(merged.astype(BF16), wout_ref[...], preferred_element_type=F32)
    x1 = _layer_norm(alpha * x_ref[...] + mix, lng_ref[...], lnb_ref[...])
    x1_ref[...] = x1
    x1p_ref[...] = _pack_bf16_pairs(x1)
    tm = x1.shape[0]
    xh = x1.astype(BF16)
    xl = (x1 - xh.astype(F32)).astype(BF16)
    r = jnp.dot(jnp.concatenate([xh, xl], axis=0), wr_ref[...], preferred_element_type=F32)
    logits = (r[:tm, :LANES] + r[:tm, LANES:]) + (r[tm:, :LANES] + r[tm:, LANES:]) + br_ref[...]
    lane = lax.broadcasted_iota(jnp.int32, logits.shape, 1).astype(F32)
    vals, idxs = [], []
    for _ in range(TOP_K):
        m = jnp.max(logits, axis=1, keepdims=True)
        idx = jnp.min(jnp.where(logits == m, lane, float(LANES)), axis=1, keepdims=True)
        vals.append(m)
        idxs.append(idx)
        logits = jnp.where(lane == idx, -jnp.inf, logits)
    es = [jnp.exp(v - vals[0]) for v in vals]
    denom = es[0] + es[1] + es[2] + es[3]
    gates = jnp.zeros(logits.shape, F32)
    topi = jnp.zeros(logits.shape, F32)
    for k in range(TOP_K):
        gates = jnp.where(lane == k, es[k] / denom, gates)
        topi = jnp.where(lane == k, idxs[k], topi)
    gate_ref[...] = gates
    topi_ref[...] = topi.astype(jnp.int32)


def _merge(x2, a2, nb2, g2, wda, wna, wout, ln_g, ln_b, wr, br, alpha):
    T, D = x2.shape
    tm = min(PROJ_TM, T)
    tok = lambda width: pl.BlockSpec((tm, width), lambda i: (i, 0))
    const = lambda shape: pl.BlockSpec(shape, lambda i: (0,) * len(shape))
    return pl.pallas_call(
        functools.partial(_merge_kernel, alpha=alpha),
        grid=(T // tm,),
        in_specs=[tok(D), tok(a2.shape[1]), tok(nb2.shape[1]), tok(g2.shape[1]),
                  const(wda.shape), const(wna.shape), const(wout.shape),
                  const((1, D)), const((1, D)), const(wr.shape), const((1, LANES))],
        out_specs=(tok(D), tok(D // 2), tok(LANES), tok(LANES)),
        out_shape=(jax.ShapeDtypeStruct((T, D), F32), jax.ShapeDtypeStruct((T, D // 2), jnp.uint32),
                   jax.ShapeDtypeStruct((T, LANES), jnp.int32), jax.ShapeDtypeStruct((T, LANES), F32)),
        compiler_params=_cparams(("parallel",)),
        name="merge",
    )(x2, a2, nb2, g2, wda, wna, wout, ln_g, ln_b, wr, br)


def _glu_group_bias(b1):
    e, n = b1.shape
    return b1.reshape(e, n // (2 * LANES), LANES, 2).transpose(0, 1, 3, 2).reshape(e, 1, n)


def _moe_kernel(be_ref, nused_ref, xs_ref, w1_ref, b1_ref, w2_ref, b2_ref, ys_ref, w1g_sc, w2b_sc):
    i = pl.program_id(0)
    used = i < nused_ref[0]
    fresh = jnp.logical_or(i == 0, be_ref[i] != be_ref[jnp.maximum(i - 1, 0)])
    grp = 2 * LANES

    @pl.when(jnp.logical_and(used, fresh))
    def _():
        r = lax.broadcasted_iota(jnp.int32, (grp, grp), 0)
        c = lax.broadcasted_iota(jnp.int32, (grp, grp), 1)
        src = jnp.where(c < LANES, 2 * c, 2 * (c - LANES) + 1)
        perm = jnp.where(r == src, 1.0, 0.0).astype(BF16)
        for g in range(w1_ref.shape[1] // grp):
            cols = w1_ref[:, g * grp:(g + 1) * grp].astype(BF16)
            w1g_sc[:, g * grp:(g + 1) * grp] = jnp.dot(cols, perm, preferred_element_type=F32).astype(BF16)
        w2b_sc[...] = w2_ref[...].astype(BF16)

    @pl.when(used)
    def _():
        xb = _unpack_bf16_pairs(xs_ref[...]).astype(BF16)
        h = jnp.dot(xb, w1g_sc[...], preferred_element_type=F32) + b1_ref[...]
        acts = []
        for g in range(h.shape[1] // grp):
            x_glu = jnp.minimum(h[:, g * grp:g * grp + LANES], SWIGLU_LIMIT)
            x_lin = jnp.clip(h[:, g * grp + LANES:(g + 1) * grp], -SWIGLU_LIMIT, SWIGLU_LIMIT)
            acts.append(x_glu * (1.0 / (1.0 + jnp.exp(-SWIGLU_ALPHA * x_glu))) * (x_lin + 1.0))
        act = jnp.concatenate(acts, axis=1).astype(BF16)
        y = jnp.dot(act, w2b_sc[...], preferred_element_type=F32) + b2_ref[...]
        ys_ref[...] = _pack_bf16_pairs(y)

    @pl.when(jnp.logical_not(used))
    def _():
        ys_ref[...] = jnp.zeros(ys_ref.shape, ys_ref.dtype)


def _moe_ffn(xs, block_expert, n_used, w1, b1g, w2, b2):
    P, half = xs.shape
    D = 2 * half
    de2 = w1.shape[2]
    de = w2.shape[1]
    wspec = lambda r, c: pl.BlockSpec((None, r, c), lambda i, be, nu: (be[i], 0, 0))
    grid_spec = pltpu.PrefetchScalarGridSpec(
        num_scalar_prefetch=2,
        grid=(P // MOE_BLOCK,),
        in_specs=[pl.BlockSpec((MOE_BLOCK, half), lambda i, be, nu: (i, 0)),
                  wspec(D, de2), wspec(1, de2), wspec(de, D), wspec(1, D)],
        out_specs=pl.BlockSpec((MOE_BLOCK, half), lambda i, be, nu: (i, 0)),
        scratch_shapes=[pltpu.VMEM((D, de2), BF16), pltpu.VMEM((de, D), BF16)],
    )
    return pl.pallas_call(
        _moe_kernel,
        grid_spec=grid_spec,
        out_shape=jax.ShapeDtypeStruct((P, half), jnp.uint32),
        compiler_params=_cparams(("arbitrary",)),
        name="moe_ffn",
    )(block_expert, n_used, xs, w1, b1g, w2, b2)


def _combine_kernel(x1_ref, y0_ref, y1_ref, y2_ref, y3_ref, gate_ref, lng_ref, lnb_ref, o_ref, *, alpha):
    gates = gate_ref[...]
    ffn = jnp.zeros(x1_ref.shape, F32)
    for k, y_ref in enumerate((y0_ref, y1_ref, y2_ref, y3_ref)):
        ffn = ffn + _unpack_bf16_pairs(y_ref[...]) * gates[:, k:k + 1]
    o_ref[...] = _layer_norm(alpha * x1_ref[...] + ffn, lng_ref[...], lnb_ref[...])


def _combine(x1, y_slots, gates, ln_g, ln_b, alpha):
    T, D = x1.shape
    tm = min(PROJ_TM, T)
    tok = lambda width: pl.BlockSpec((tm, width), lambda i: (i, 0))
    const = lambda shape: pl.BlockSpec(shape, lambda i: (0,) * len(shape))
    slot = lambda k: pl.BlockSpec((None, tm, D // 2), lambda i: (k, i, 0))
    return pl.pallas_call(
        functools.partial(_combine_kernel, alpha=alpha),
        grid=(T // tm,),
        in_specs=[tok(D)] + [slot(k) for k in range(TOP_K)] + [tok(LANES), const((1, D)), const((1, D))],
        out_specs=tok(D),
        out_shape=jax.ShapeDtypeStruct((T, D), F32),
        compiler_params=_cparams(("parallel",)),
        name="combine",
    )(x1, y_slots, y_slots, y_slots, y_slots, gates, ln_g, ln_b)


def _route(top_i, n_experts):
    T = top_i.shape[0]
    TK = T * TOP_K
    experts = jnp.arange(n_experts, dtype=jnp.int32)
    chosen = top_i[:, :, None] == experts
    picks = jnp.sum(chosen, axis=1, dtype=jnp.int32)
    csum = jnp.cumsum(picks, axis=0)
    counts = csum[-1]
    padded = ((counts + MOE_BLOCK - 1) // MOE_BLOCK) * MOE_BLOCK
    padded_end = jnp.cumsum(padded)
    padded_start = padded_end - padded
    slot_te = padded_start[None, :] + csum - picks
    dest = jnp.sum(jnp.where(chosen, slot_te[:, None, :], 0), axis=2)
    n_blocks = -(-TK // MOE_BLOCK) + n_experts
    blk_first = jnp.arange(n_blocks, dtype=jnp.int32) * MOE_BLOCK
    block_expert = jnp.minimum(jnp.sum(padded_end[None, :] <= blk_first[:, None], axis=1),
                               n_experts - 1).astype(jnp.int32)
    n_used = (padded_end[-1] // MOE_BLOCK).astype(jnp.int32).reshape(1)
    pad = padded - counts
    pad_end = jnp.cumsum(pad)
    j = jnp.arange(n_blocks * MOE_BLOCK - TK, dtype=jnp.int32)
    owner = j[:, None] >= pad_end[None, :]
    group = jnp.sum(owner, axis=1)
    mine = group[:, None] == experts[None, :]
    in_group = jnp.sum(jnp.where(mine, (padded_start + counts - (pad_end - pad))[None, :], 0), axis=1) + j
    pad_slots = jnp.where(group < n_experts, in_group, padded_end[-1] + j - pad_end[-1])
    return block_expert, n_used, dest.T.reshape(-1), pad_slots.astype(jnp.int32)


def _rope_tables(seq):
    pos = jnp.arange(seq, dtype=F32)
    inv = ROPE_THETA ** (-jnp.arange(0, DA_HEAD_DIM, 2, dtype=F32) / DA_HEAD_DIM)
    ang = pos[:, None] * inv[None, :]
    ang = jnp.concatenate([ang, ang], axis=-1)
    cos, sin = jnp.cos(ang), jnp.sin(ang)
    half = DA_HEAD_DIM // 2
    sin_signed = jnp.concatenate([-sin[:, :half], sin[:, half:]], axis=-1)
    return jnp.tile(cos, (1, 2)), jnp.tile(sin_signed, (1, 2))


def kernel(x, w_in, b_in, lambda_q1, lambda_k1, lambda_q2, lambda_k2, subln_g, rpb, w_branch_da, w_branch_na, w_out, ln1_g, ln1_b, w_router, b_router, w_mlp1, b_mlp1, w_mlp2, b_mlp2, ln2_g, ln2_b):
    B, S, D = x.shape
    depth = w_in.shape[0]
    n_experts = w_router.shape[2]
    T = B * S
    rows = S // GRID_W
    assert S % GRID_W == 0 and rows % NA_ROWS == 0 and rows >= NA_BAND
    alpha = (2 * depth) ** 0.25
    cos, sin = _rope_tables(S)
    row = lambda v: v.reshape(1, -1)

    for l in range(depth):
        lam_init = 0.8 - 0.6 * math.exp(-0.3 * l)
        qT, k, vT, qn, kn, vn, gates_br = _in_proj(x, w_in[l].astype(BF16), row(b_in[l]), cos, sin)
        a = _diff_attn(qT, k, vT, row(lambda_q1[l]), row(lambda_k1[l]), row(lambda_q2[l]),
                       row(lambda_k2[l]), subln_g[l].reshape(-1, 1), lam_init)
        nb = _na_attn(qn, kn, vn, _na_tiles(rpb[l]))

        wr = jnp.pad(w_router[l], ((0, 0), (0, LANES - n_experts)))
        wr_h = wr.astype(BF16)
        wr = jnp.concatenate([wr_h, (wr - wr_h.astype(F32)).astype(BF16)], axis=1)
        br = jnp.pad(row(b_router[l]), ((0, 0), (0, LANES - n_experts)), constant_values=NEG_BIG)
        x1, x1p, topi, gates = _merge(
            x.reshape(T, D), a.reshape(T, -1), nb.reshape(T, -1), gates_br.reshape(T, -1),
            w_branch_da[l].astype(BF16), w_branch_na[l].astype(BF16), w_out[l].astype(BF16),
            row(ln1_g[l]), row(ln1_b[l]), wr, br, alpha)

        block_expert, n_used, dest, pad_slots = _route(topi[:, :TOP_K], n_experts)
        xs = _sc_scatter_rows(x1p, dest, pad_slots)
        ys = _moe_ffn(xs, block_expert, n_used, w_mlp1[l], _glu_group_bias(b_mlp1[l]),
                      w_mlp2[l], b_mlp2[l][:, None, :])
        y_slots = _sc_gather_rows(ys, dest).reshape(TOP_K, T, D // 2)
        x = _combine(x1, y_slots, gates, row(ln2_g[l]), row(ln2_b[l]), alpha).reshape(B, S, D)
    return x
```

```python
import functools
import math

import jax
import jax.numpy as jnp
from jax import lax
from jax.experimental import pallas as pl
from jax.experimental.pallas import tpu as pltpu
from jax.experimental.pallas import tpu_sc as plsc

F32 = jnp.float32
BF16 = jnp.bfloat16

GRID_W = 64
DA_HEADS = 4
DA_HEAD_DIM = 64
DA_V_DIM = 2 * DA_HEAD_DIM
DA_WIDTH = DA_HEADS * DA_V_DIM
ROPE_THETA = 10000.0
NA_HEADS = 8
NA_HEAD_DIM = 64
NA_WIDTH = NA_HEADS * NA_HEAD_DIM
NA_KH = 8
NA_KW = 16
TOP_K = 4
SWIGLU_ALPHA = 1.702
SWIGLU_LIMIT = 7.0
MOE_BLOCK = 512
LN_EPS = 1e-5
RMS_EPS = 1e-5

LANES = 128
VMEM_LIMIT_BYTES = 56 * 1024 * 1024
SC_CORES = 2
SC_SUBCORES = 16
SC_GATHER_ROWS = 64

LOG2E = math.log2(math.e)
NEG_BIG = -1e30

PROJ_TM = 512
DA_TQ = 512
DA_TK = 512
DA_UNROLL = 8
NA_ROWS = 8
NA_BAND = 16
NA_PAIRS = 4


def _cparams(sem):
    return pltpu.CompilerParams(dimension_semantics=sem, vmem_limit_bytes=VMEM_LIMIT_BYTES)


def _pack_bf16_pairs(y):
    w = y.shape[1] // 2
    bits = lax.bitcast_convert_type(y, jnp.uint32)
    rounded = bits + jnp.uint32(0x7FFF) + ((bits >> 16) & jnp.uint32(1))
    return (rounded[:, :w] >> 16) | (rounded[:, w:] & jnp.uint32(0xFFFF0000))


def _unpack_bf16_pairs(words):
    lo = lax.bitcast_convert_type(words << 16, F32)
    hi = lax.bitcast_convert_type(words & jnp.uint32(0xFFFF0000), F32)
    return jnp.concatenate([lo, hi], axis=1)


def _sc_gather_rows(table, idx):
    n, w = idx.shape[0], table.shape[1]
    rows = SC_GATHER_ROWS
    workers = SC_CORES * SC_SUBCORES
    per_worker = n // workers
    n_pairs = per_worker // (2 * rows)
    assert n % (workers * 2 * rows) == 0 and n_pairs >= 1
    mesh = plsc.VectorSubcoreMesh(core_axis_name="c", subcore_axis_name="s",
                                  num_cores=SC_CORES, num_subcores=SC_SUBCORES)

    def body(table_hbm, idx_hbm, out_hbm, idx0, idx1, rows0, rows1, sem0, sem1):
        base = (lax.axis_index("s") * SC_CORES + lax.axis_index("c")) * per_worker
        bufs = ((idx0, rows0, sem0), (idx1, rows1, sem1))

        def gather(b):
            idx_v, rows_v, sem = bufs[b]
            return pltpu.make_async_copy(table_hbm.at[idx_v], rows_v, sem)

        def start(c, b):
            pltpu.sync_copy(idx_hbm.at[pl.ds(base + c * rows, rows)], bufs[b][0])
            gather(b).start()

        def finish(c, b):
            gather(b).wait()
            pltpu.sync_copy(bufs[b][1], out_hbm.at[pl.ds(base + c * rows, rows)])

        start(0, 0)

        @pl.loop(0, n_pairs - 1)
        def _(g):
            c = 2 * g
            start(c + 1, 1)
            finish(c, 0)
            start(c + 2, 0)
            finish(c + 1, 1)

        last = 2 * (n_pairs - 1)
        start(last + 1, 1)
        finish(last, 0)
        finish(last + 1, 1)

    return pl.kernel(
        body, out_type=jax.ShapeDtypeStruct((n, w), table.dtype), mesh=mesh,
        scratch_types=[pltpu.VMEM((rows,), jnp.int32), pltpu.VMEM((rows,), jnp.int32),
                       pltpu.VMEM((rows, w), table.dtype), pltpu.VMEM((rows, w), table.dtype),
                       pltpu.SemaphoreType.DMA, pltpu.SemaphoreType.DMA],
        name="sc_gather_rows",
    )(table, idx)


def _sc_scatter_rows(table, dest, rest):
    t, w = table.shape
    copies = dest.shape[0] // t
    n_out = dest.shape[0] + rest.shape[0]
    rows = SC_GATHER_ROWS
    workers = SC_CORES * SC_SUBCORES
    per_worker = t // workers
    n_pairs = per_worker // (2 * rows)
    rest_per_worker = rest.shape[0] // workers
    assert t % (workers * 2 * rows) == 0 and n_pairs >= 1 and rest.shape[0] % (workers * rows) == 0
    mesh = plsc.VectorSubcoreMesh(core_axis_name="c", subcore_axis_name="s",
                                  num_cores=SC_CORES, num_subcores=SC_SUBCORES)

    n_fill = rest_per_worker // rows

    def body(table_hbm, dest_hbm, rest_hbm, zeros_hbm, out_hbm, *scratch):
        worker = lax.axis_index("s") * SC_CORES + lax.axis_index("c")
        base = worker * per_worker
        per_buf = copies + 2
        bufs = (scratch[:per_buf], scratch[per_buf:2 * per_buf])
        zero_v, zero_sem = scratch[2 * per_buf:2 * per_buf + 2]
        fill_idx = scratch[2 * per_buf + 2:]

        def fills():
            return [pltpu.make_async_copy(zero_v, out_hbm.at[idx_v], zero_sem) for idx_v in fill_idx]

        pltpu.sync_copy(zeros_hbm, zero_v)
        for c, idx_v in enumerate(fill_idx):
            pltpu.sync_copy(rest_hbm.at[pl.ds(worker * rest_per_worker + c * rows, rows)], idx_v)
        for fill in fills():
            fill.start()

        def scatters(b):
            rows_v, sem = bufs[b][0], bufs[b][1]
            return [pltpu.make_async_copy(rows_v, out_hbm.at[idx_v], sem) for idx_v in bufs[b][2:]]

        def start(c, b):
            t0 = base + c * rows
            pltpu.sync_copy(table_hbm.at[pl.ds(t0, rows)], bufs[b][0])
            for k, idx_v in enumerate(bufs[b][2:]):
                pltpu.sync_copy(dest_hbm.at[pl.ds(k * t + t0, rows)], idx_v)
            for copy in scatters(b):
                copy.start()

        def finish(b):
            for copy in scatters(b):
                copy.wait()

        start(0, 0)

        @pl.loop(0, n_pairs - 1)
        def _(g):
            start(2 * g + 1, 1)
            finish(0)
            start(2 * g + 2, 0)
            finish(1)

        start(2 * n_pairs - 1, 1)
        finish(0)
        finish(1)

        for fill in fills():
            fill.wait()

    one_buf = ([pltpu.VMEM((rows, w), table.dtype), pltpu.SemaphoreType.DMA]
               + [pltpu.VMEM((rows,), jnp.int32) for _ in range(copies)])
    zero_buf = ([pltpu.VMEM((rows, w), table.dtype), pltpu.SemaphoreType.DMA]
                + [pltpu.VMEM((rows,), jnp.int32) for _ in range(n_fill)])
    return pl.kernel(
        body, out_type=jax.ShapeDtypeStruct((n_out, w), table.dtype), mesh=mesh,
        scratch_types=one_buf + one_buf + zero_buf, name="sc_scatter_rows",
    )(table, dest, rest, jnp.zeros((rows, w), table.dtype))


def _in_proj_kernel(x_ref, w_ref, b_ref, cos_ref, sin_ref,
                    qT_ref, k_ref, vT_ref, qn_ref, kn_ref, vn_ref, g_ref):
    xb = x_ref[...].astype(BF16)

    def seg(lo, hi):
        return jnp.dot(xb, w_ref[:, lo:hi], preferred_element_type=F32) + b_ref[:, lo:hi]

    cos = cos_ref[...]
    sin = sin_ref[...]
    lane = lax.broadcasted_iota(jnp.int32, cos.shape, 1)
    first_half = (lane % DA_HEAD_DIM) < (DA_HEAD_DIM // 2)

    def rope(y):
        outs = []
        for h in range(DA_HEADS):
            yh = y[:, h * LANES:(h + 1) * LANES]
            partner = jnp.where(first_half,
                                pltpu.roll(yh, LANES - DA_HEAD_DIM // 2, 1),
                                pltpu.roll(yh, DA_HEAD_DIM // 2, 1))
            outs.append(yh * cos + partner * sin)
        return jnp.concatenate(outs, axis=1)

    w = DA_WIDTH
    q = rope(seg(0, w)) * (DA_HEAD_DIM ** -0.5 * LOG2E)
    qT_ref[...] = q.T.astype(BF16)
    k_ref[...] = rope(seg(w, 2 * w)).astype(BF16)
    vT_ref[...] = seg(2 * w, 3 * w).T.astype(BF16)
    o = 3 * w
    qn_ref[...] = (seg(o, o + NA_WIDTH) * (NA_HEAD_DIM ** -0.5)).T.astype(BF16)
    kn_ref[...] = seg(o + NA_WIDTH, o + 2 * NA_WIDTH).astype(BF16)
    vn_ref[...] = seg(o + 2 * NA_WIDTH, o + 3 * NA_WIDTH).T.astype(BF16)
    g0 = o + 3 * NA_WIDTH
    gate_pre = seg(g0, w_ref.shape[1])
    g_ref[...] = (1.0 / (1.0 + jnp.exp(-gate_pre))).astype(BF16)


def _in_proj(x, w_in, b_in, cos, sin):
    B, S, D = x.shape
    tm = min(PROJ_TM, S)
    n_cols = w_in.shape[1]
    n_gate = n_cols - 3 * DA_WIDTH - 3 * NA_WIDTH
    tok = lambda width: pl.BlockSpec((None, tm, width), lambda b, i: (b, i, 0))
    tr = pl.BlockSpec((None, DA_WIDTH, tm), lambda b, i: (b, 0, i))
    const = lambda shape: pl.BlockSpec(shape, lambda b, i: (0,) * len(shape))
    out_shape = (
        jax.ShapeDtypeStruct((B, DA_WIDTH, S), BF16),
        jax.ShapeDtypeStruct((B, S, DA_WIDTH), BF16),
        jax.ShapeDtypeStruct((B, DA_WIDTH, S), BF16),
        jax.ShapeDtypeStruct((B, NA_WIDTH, S), BF16),
        jax.ShapeDtypeStruct((B, S, NA_WIDTH), BF16),
        jax.ShapeDtypeStruct((B, NA_WIDTH, S), BF16),
        jax.ShapeDtypeStruct((B, S, n_gate), BF16),
    )
    assert DA_WIDTH == NA_WIDTH
    return pl.pallas_call(
        _in_proj_kernel,
        grid=(B, S // tm),
        in_specs=[tok(D), const((D, n_cols)), const((1, n_cols)),
                  pl.BlockSpec((tm, LANES), lambda b, i: (i, 0)),
                  pl.BlockSpec((tm, LANES), lambda b, i: (i, 0))],
        out_specs=(tr, tok(DA_WIDTH), tr, tr, tok(NA_WIDTH), tr, tok(n_gate)),
        out_shape=out_shape,
        compiler_params=_cparams(("parallel", "parallel")),
        name="in_proj",
    )(x, w_in, b_in, cos, sin)


def _diff_attn_kernel(qT_ref, qTn_ref, k_ref, vT_ref, lq1_ref, lk1_ref, lq2_ref, lk2_ref, g_ref,
                      o_ref, m_sc, l_sc, acc_sc, s_sc, cmax_sc, *, tk, unroll, lam_init):
    tq = qT_ref.shape[1]

    def block_diag(q_ref):
        qT = q_ref[...]
        row = lax.broadcasted_iota(jnp.int32, qT.shape, 0)
        zero = jnp.zeros_like(qT)
        return jnp.concatenate([jnp.where(row < DA_HEAD_DIM, qT, zero),
                                jnp.where(row < DA_HEAD_DIM, zero, qT)], axis=1)

    qbd = block_diag(qT_ref)

    m_sc[...] = jnp.full(m_sc.shape, -jnp.inf, F32)
    l_sc[...] = jnp.zeros(l_sc.shape, F32)
    acc_sc[...] = jnp.zeros(acc_sc.shape, F32)

    nk = k_ref.shape[0] // tk

    def put_scores(slot, j, q=None):
        off = pl.multiple_of(j * tk, tk)
        s = jnp.dot(k_ref[pl.ds(off, tk), :], qbd if q is None else q, preferred_element_type=F32)
        s_sc[slot] = s
        cmax_sc[slot] = jnp.max(s, axis=0, keepdims=True)

    def accumulate(slot, j):
        off = pl.multiple_of(j * tk, tk)
        m_prev = m_sc[...]
        m_new = jnp.maximum(m_prev, cmax_sc[slot])
        alpha = jnp.exp2(m_prev - m_new)
        p = jnp.exp2(s_sc[slot] - m_new)
        l_sc[...] = alpha * l_sc[...] + jnp.sum(p, axis=0, keepdims=True)
        vb = vT_ref[:, pl.ds(off, tk)]
        acc_sc[...] = alpha * acc_sc[...] + jnp.dot(vb, p.astype(BF16), preferred_element_type=F32)
        m_sc[...] = m_new

    @pl.when(pl.program_id(2) == 0)
    def _():
        put_scores(0, 0)

    def body(jj, carry):
        for u in range(unroll):
            j = unroll * jj + u
            put_scores((u + 1) % 2, j + 1)
            accumulate(u % 2, j)
        return carry

    lax.fori_loop(0, nk // unroll - 1, body, 0)
    for j in range(nk - unroll, nk):
        if j + 1 < nk:
            put_scores((j + 1) % 2, j + 1)
        else:
            put_scores(0, 0, block_diag(qTn_ref))
        accumulate(j % 2, j)

    lam = (jnp.exp(jnp.sum(lq1_ref[...] * lk1_ref[...], axis=1, keepdims=True))
           - jnp.exp(jnp.sum(lq2_ref[...] * lk2_ref[...], axis=1, keepdims=True)) + lam_init)
    on = acc_sc[...] / l_sc[...]
    o = on[:, :tq] - lam * on[:, tq:]
    ms = jnp.mean(o * o, axis=0, keepdims=True)
    o = o * lax.rsqrt(ms + RMS_EPS) * g_ref[...]
    o = o * (1.0 - lam_init)
    o_ref[...] = o.T.astype(BF16)


def _diff_attn(qT, k, vT, lq1, lk1, lq2, lk2, subln_g, lam_init):
    B, S, _ = k.shape
    tq = min(DA_TQ, S)
    tk = min(DA_TK, S // 2)
    unroll = math.gcd(DA_UNROLL, S // tk)
    assert unroll % 2 == 0 and S % tk == 0 and S % tq == 0
    nq = S // tq
    vec = pl.BlockSpec((1, DA_HEAD_DIM), lambda b, h, i: (0, 0))
    kernel = functools.partial(_diff_attn_kernel, tk=tk, unroll=unroll, lam_init=lam_init)
    return pl.pallas_call(
        kernel,
        grid=(B, DA_HEADS, nq),
        in_specs=[pl.BlockSpec((None, DA_V_DIM, tq), lambda b, h, i: (b, h, i)),
                  pl.BlockSpec((None, DA_V_DIM, tq), lambda b, h, i: (b, h, jnp.minimum(i + 1, nq - 1))),
                  pl.BlockSpec((None, S, DA_V_DIM), lambda b, h, i: (b, 0, h)),
                  pl.BlockSpec((None, DA_V_DIM, S), lambda b, h, i: (b, h, 0)),
                  vec, vec, vec, vec,
                  pl.BlockSpec((DA_V_DIM, 1), lambda b, h, i: (0, 0))],
        out_specs=pl.BlockSpec((None, tq, DA_V_DIM), lambda b, h, i: (b, i, h)),
        out_shape=jax.ShapeDtypeStruct((B, S, DA_WIDTH), BF16),
        scratch_shapes=[pltpu.VMEM((1, 2 * tq), F32), pltpu.VMEM((1, 2 * tq), F32),
                        pltpu.VMEM((DA_V_DIM, 2 * tq), F32), pltpu.VMEM((2, tk, 2 * tq), F32),
                        pltpu.VMEM((2, 1, 2 * tq), F32)],
        compiler_params=_cparams(("parallel", "parallel", "arbitrary")),
        name="diff_attn",
    )(qT, qT, k, vT, lq1, lk1, lq2, lk2, subln_g)


def _na_band_start(r0, rows):
    return jnp.clip(r0 - NA_KH // 2, 0, rows - NA_BAND)


def _na_tiles(rpb):
    cols = jnp.arange(GRID_W)
    cs = jnp.clip(cols - NA_KW // 2, 0, GRID_W - NA_KW)
    col_ok = (cols[:, None] >= cs[None, :]) & (cols[:, None] < cs[None, :] + NA_KW)
    col_off = jnp.clip(cols[:, None] - cols[None, :] + (NA_KW - 1), 0, 2 * NA_KW - 2)
    pick = col_off[:, :, None] == jnp.arange(2 * NA_KW - 1)
    tiles = jnp.sum(jnp.where(pick[None, None], rpb.astype(F32)[:, :, None, None, :], 0.0), axis=-1)
    tiles = jnp.where(col_ok[None, None], tiles, NEG_BIG)
    neg = jnp.full((rpb.shape[0], 1, GRID_W, GRID_W), NEG_BIG, F32)
    return jnp.concatenate([jnp.concatenate([tiles, neg], axis=1),
                            jnp.concatenate([neg, tiles], axis=1)], axis=3)


def _na_geometry(variant, rows):
    nblk = rows // NA_ROWS
    r0 = (0, NA_ROWS * min(1, nblk - 1), NA_ROWS * (nblk - 1))[variant]
    band = min(max(r0 - NA_KH // 2, 0), rows - NA_BAND)
    pairs = []
    for qp in range(NA_ROWS // 2):
        entries = []
        for kr in range(NA_BAND):
            ok = []
            for qr in (r0 + 2 * qp, r0 + 2 * qp + 1):
                rs = min(max(qr - NA_KH // 2, 0), rows - NA_KH)
                ok.append(rs <= band + kr < rs + NA_KH)
            if ok[0] or ok[1]:
                entries.append((kr, band + kr - (r0 + 2 * qp) + NA_KH - 1, ok[0], ok[1]))
        pairs.append(entries)
    return pairs


def _na_kernel(qT_ref, k_ref, vT_ref, tile_ref, o_ref, *, rows):
    i = pl.program_id(2)
    nblk = rows // NA_ROWS
    nk = NA_BAND * GRID_W
    off = pl.multiple_of(_na_band_start(i * NA_ROWS, rows) * GRID_W, 2 * GRID_W)

    def block(variant):
        for pair in range(NA_PAIRS):
            block_pair(variant, pair)

    def block_pair(variant, pair):
        c0 = pair * LANES
        kb = k_ref[pl.ds(off, nk), c0:c0 + LANES]
        vT = vT_ref[c0:c0 + LANES, pl.ds(off, nk)]
        lane = lax.broadcasted_iota(jnp.int32, kb.shape, 1)
        kbd = jnp.concatenate([jnp.where(lane < NA_HEAD_DIM, kb, jnp.zeros_like(kb)),
                               jnp.where(lane < NA_HEAD_DIM, jnp.zeros_like(kb), kb)], axis=0)
        row = lax.broadcasted_iota(jnp.int32, vT.shape, 0)
        vbdT = jnp.concatenate([jnp.where(row < NA_HEAD_DIM, vT, jnp.zeros_like(vT)),
                                jnp.where(row < NA_HEAD_DIM, jnp.zeros_like(vT), vT)], axis=1)
        s = jnp.dot(kbd, qT_ref[c0:c0 + LANES, :], preferred_element_type=F32)
        left = lax.broadcasted_iota(jnp.int32, (GRID_W, LANES), 1) < GRID_W
        zero_slab = jnp.zeros((GRID_W, LANES), BF16)
        geometry = _na_geometry(variant, rows)
        p_heads, l_heads = [], []
        for h in range(2):
            p_cols, l_cols = [], []
            for qp, entries in enumerate(geometry):
                logits = {}
                for kr, d, ok_l, ok_r in entries:
                    bias = tile_ref[2 * pair + h, d]
                    if not ok_r:
                        bias = jnp.where(left, bias, NEG_BIG)
                    if not ok_l:
                        bias = jnp.where(left, NEG_BIG, bias)
                    r = h * nk + kr * GRID_W
                    logits[kr] = s[r:r + GRID_W, qp * LANES:(qp + 1) * LANES] + bias
                m = functools.reduce(jnp.maximum, logits.values())
                m = jnp.max(m, axis=0, keepdims=True)
                probs = {kr: jnp.exp(x - m) for kr, x in logits.items()}
                l_cols.append(jnp.sum(functools.reduce(jnp.add, probs.values()), axis=0, keepdims=True))
                p_cols.append(jnp.concatenate(
                    [probs[kr].astype(BF16) if kr in probs else zero_slab for kr in range(NA_BAND)], axis=0))
            p_heads.append(jnp.concatenate(p_cols, axis=1))
            l_heads.append(jnp.concatenate(l_cols, axis=1))
        oT = jnp.dot(vbdT, jnp.concatenate(p_heads, axis=0), preferred_element_type=F32)
        row_o = lax.broadcasted_iota(jnp.int32, oT.shape, 0)
        oT = oT / jnp.where(row_o < NA_HEAD_DIM, l_heads[0], l_heads[1])
        o_ref[:, c0:c0 + LANES] = oT.T.astype(BF16)

    pl.when(i == 0)(lambda: block(0))
    pl.when(jnp.logical_and(i > 0, i < nblk - 1))(lambda: block(1))
    pl.when(jnp.logical_and(i > 0, i == nblk - 1))(lambda: block(2))


def _na_attn(qnT, kn, vnT, tiles):
    B, S, _ = kn.shape
    rows = S // GRID_W
    nblk = rows // NA_ROWS
    nq = NA_ROWS * GRID_W
    width = NA_PAIRS * LANES
    return pl.pallas_call(
        functools.partial(_na_kernel, rows=rows),
        grid=(B, NA_HEADS // (2 * NA_PAIRS), nblk),
        in_specs=[pl.BlockSpec((None, width, nq), lambda b, h, i: (b, h, i)),
                  pl.BlockSpec((None, S, width), lambda b, h, i: (b, 0, h)),
                  pl.BlockSpec((None, width, S), lambda b, h, i: (b, h, 0)),
                  pl.BlockSpec((2 * NA_PAIRS,) + tiles.shape[1:], lambda b, h, i: (h, 0, 0, 0))],
        out_specs=pl.BlockSpec((None, nq, width), lambda b, h, i: (b, i, h)),
        out_shape=jax.ShapeDtypeStruct((B, S, NA_WIDTH), BF16),
        compiler_params=_cparams(("parallel", "parallel", "arbitrary")),
        name="na_attn",
    )(qnT, kn, vnT, tiles)


def _layer_norm(h, g, b):
    mu = jnp.mean(h, axis=-1, keepdims=True)
    d = h - mu
    var = jnp.mean(d * d, axis=-1, keepdims=True)
    return d * lax.rsqrt(var + LN_EPS) * g + b


def _merge_kernel(x_ref, a_ref, nb_ref, g_ref, wda_ref, wna_ref, wout_ref, lng_ref, lnb_ref,
                  wr_ref, br_ref, x1_ref, x1p_ref, topi_ref, gate_ref, *, alpha):
    d = x_ref.shape[1]
    ya = jnp.dot(a_ref[...], wda_ref[...], preferred_element_type=F32)
    yb = jnp.dot(nb_ref[...], wna_ref[...], preferred_element_type=F32)
    g = g_ref[...].astype(F32)
    merged = g[:, :d] * ya + g[:, d:] * yb
    mix = jnp.dot(merged.astype(BF16), wout_ref[...], preferred_element_type=F32)
    x1 = _layer_norm(alpha * x_ref[...] + mix, lng_ref[...], lnb_ref[...])
    x1_ref[...] = x1
    x1p_ref[...] = _pack_bf16_pairs(x1)
    tm = x1.shape[0]
    xh = x1.astype(BF16)
    xl = (x1 - xh.astype(F32)).astype(BF16)
    r = jnp.dot(jnp.concatenate([xh, xl], axis=0), wr_ref[...], preferred_element_type=F32)
    logits = (r[:tm, :LANES] + r[:tm, LANES:]) + (r[tm:, :LANES] + r[tm:, LANES:]) + br_ref[...]
    lane = lax.broadcasted_iota(jnp.int32, logits.shape, 1).astype(F32)
    vals, idxs = [], []
    for _ in range(TOP_K):
        m = jnp.max(logits, axis=1, keepdims=True)
        idx = jnp.min(jnp.where(logits == m, lane, float(LANES)), axis=1, keepdims=True)
        vals.append(m)
        idxs.append(idx)
        logits = jnp.where(lane == idx, -jnp.inf, logits)
    es = [jnp.exp(v - vals[0]) for v in vals]
    denom = es[0] + es[1] + es[2] + es[3]
    gates = jnp.zeros(logits.shape, F32)
    topi = jnp.zeros(logits.shape, F32)
    for k in range(TOP_K):
        gates = jnp.where(lane == k, es[k] / denom, gates)
        topi = jnp.where(lane == k, idxs[k], topi)
    gate_ref[...] = gates
    topi_ref[...] = topi.astype(jnp.int32)


def _merge(x2, a2, nb2, g2, wda, wna, wout, ln_g, ln_b, wr, br, alpha):
    T, D = x2.shape
    tm = min(PROJ_TM, T)
    tok = lambda width: pl.BlockSpec((tm, width), lambda i: (i, 0))
    const = lambda shape: pl.BlockSpec(shape, lambda i: (0,) * len(shape))
    return pl.pallas_call(
        functools.partial(_merge_kernel, alpha=alpha),
        grid=(T // tm,),
        in_specs=[tok(D), tok(a2.shape[1]), tok(nb2.shape[1]), tok(g2.shape[1]),
                  const(wda.shape), const(wna.shape), const(wout.shape),
                  const((1, D)), const((1, D)), const(wr.shape), const((1, LANES))],
        out_specs=(tok(D), tok(D // 2), tok(LANES), tok(LANES)),
        out_shape=(jax.ShapeDtypeStruct((T, D), F32), jax.ShapeDtypeStruct((T, D // 2), jnp.uint32),
                   jax.ShapeDtypeStruct((T, LANES), jnp.int32), jax.ShapeDtypeStruct((T, LANES), F32)),
        compiler_params=_cparams(("parallel",)),
        name="merge",
    )(x2, a2, nb2, g2, wda, wna, wout, ln_g, ln_b, wr, br)


def _glu_group_bias(b1):
    e, n = b1.shape
    return b1.reshape(e, n // (2 * LANES), LANES, 2).transpose(0, 1, 3, 2).reshape(e, 1, n)


def _moe_kernel(be_ref, nused_ref, xs_ref, w1_ref, b1_ref, w2_ref, b2_ref, ys_ref, w1g_sc, w2b_sc):
    i = pl.program_id(0)
    used = i < nused_ref[0]
    fresh = jnp.logical_or(i == 0, be_ref[i] != be_ref[jnp.maximum(i - 1, 0)])
    grp = 2 * LANES

    @pl.when(jnp.logical_and(used, fresh))
    def _():
        r = lax.broadcasted_iota(jnp.int32, (grp, grp), 0)
        c = lax.broadcasted_iota(jnp.int32, (grp, grp), 1)
        src = jnp.where(c < LANES, 2 * c, 2 * (c - LANES) + 1)
        perm = jnp.where(r == src, 1.0, 0.0).astype(BF16)
        for g in range(w1_ref.shape[1] // grp):
            cols = w1_ref[:, g * grp:(g + 1) * grp].astype(BF16)
            w1g_sc[:, g * grp:(g + 1) * grp] = jnp.dot(cols, perm, preferred_element_type=F32).astype(BF16)
        w2b_sc[...] = w2_ref[...].astype(BF16)

    @pl.when(used)
    def _():
        xb = _unpack_bf16_pairs(xs_ref[...]).astype(BF16)
        h = jnp.dot(xb, w1g_sc[...], preferred_element_type=F32) + b1_ref[...]
        acts = []
        for g in range(h.shape[1] // grp):
            x_glu = jnp.minimum(h[:, g * grp:g * grp + LANES], SWIGLU_LIMIT)
            x_lin = jnp.clip(h[:, g * grp + LANES:(g + 1) * grp], -SWIGLU_LIMIT, SWIGLU_LIMIT)
            acts.append(x_glu * (1.0 / (1.0 + jnp.exp(-SWIGLU_ALPHA * x_glu))) * (x_lin + 1.0))
        act = jnp.concatenate(acts, axis=1).astype(BF16)
        y = jnp.dot(act, w2b_sc[...], preferred_element_type=F32) + b2_ref[...]
        ys_ref[...] = _pack_bf16_pairs(y)

    @pl.when(jnp.logical_not(used))
    def _():
        ys_ref[...] = jnp.zeros(ys_ref.shape, ys_ref.dtype)


def _moe_ffn(xs, block_expert, n_used, w1, b1g, w2, b2):
    P, half = xs.shape
    D = 2 * half
    de2 = w1.shape[2]
    de = w2.shape[1]
    wspec = lambda r, c: pl.BlockSpec((None, r, c), lambda i, be, nu: (be[i], 0, 0))
    grid_spec = pltpu.PrefetchScalarGridSpec(
        num_scalar_prefetch=2,
        grid=(P // MOE_BLOCK,),
        in_specs=[pl.BlockSpec((MOE_BLOCK, half), lambda i, be, nu: (i, 0)),
                  wspec(D, de2), wspec(1, de2), wspec(de, D), wspec(1, D)],
        out_specs=pl.BlockSpec((MOE_BLOCK, half), lambda i, be, nu: (i, 0)),
        scratch_shapes=[pltpu.VMEM((D, de2), BF16), pltpu.VMEM((de, D), BF16)],
    )
    return pl.pallas_call(
        _moe_kernel,
        grid_spec=grid_spec,
        out_shape=jax.ShapeDtypeStruct((P, half), jnp.uint32),
        compiler_params=_cparams(("arbitrary",)),
        name="moe_ffn",
    )(block_expert, n_used, xs, w1, b1g, w2, b2)


def _combine_kernel(x1_ref, y0_ref, y1_ref, y2_ref, y3_ref, gate_ref, lng_ref, lnb_ref, o_ref, *, alpha):
    gates = gate_ref[...]
    ffn = jnp.zeros(x1_ref.shape, F32)
    for k, y_ref in enumerate((y0_ref, y1_ref, y2_ref, y3_ref)):
        ffn = ffn + _unpack_bf16_pairs(y_ref[...]) * gates[:, k:k + 1]
    o_ref[...] = _layer_norm(alpha * x1_ref[...] + ffn, lng_ref[...], lnb_ref[...])


def _combine(x1, y_slots, gates, ln_g, ln_b, alpha):
    T, D = x1.shape
    tm = min(2 * PROJ_TM, T)
    tok = lambda width: pl.BlockSpec((tm, width), lambda i: (i, 0))
    const = lambda shape: pl.BlockSpec(shape, lambda i: (0,) * len(shape))
    slot = lambda k: pl.BlockSpec((None, tm, D // 2), lambda i: (k, i, 0))
    return pl.pallas_call(
        functools.partial(_combine_kernel, alpha=alpha),
        grid=(T // tm,),
        in_specs=[tok(D)] + [slot(k) for k in range(TOP_K)] + [tok(LANES), const((1, D)), const((1, D))],
        out_specs=tok(D),
        out_shape=jax.ShapeDtypeStruct((T, D), F32),
        compiler_params=_cparams(("parallel",)),
        name="combine",
    )(x1, y_slots, y_slots, y_slots, y_slots, gates, ln_g, ln_b)


def _route(top_i, n_experts):
    T = top_i.shape[0]
    TK = T * TOP_K
    experts = jnp.arange(n_experts, dtype=jnp.int32)
    chosen = top_i[:, :, None] == experts
    picks = jnp.sum(chosen, axis=1, dtype=jnp.int32)
    csum = jnp.cumsum(picks, axis=0)
    counts = csum[-1]
    padded = ((counts + MOE_BLOCK - 1) // MOE_BLOCK) * MOE_BLOCK
    padded_end = jnp.cumsum(padded)
    padded_start = padded_end - padded
    slot_te = padded_start[None, :] + csum - picks
    dest = jnp.sum(jnp.where(chosen, slot_te[:, None, :], 0), axis=2)
    n_blocks = -(-TK // MOE_BLOCK) + n_experts
    blk_first = jnp.arange(n_blocks, dtype=jnp.int32) * MOE_BLOCK
    block_expert = jnp.minimum(jnp.sum(padded_end[None, :] <= blk_first[:, None], axis=1),
                               n_experts - 1).astype(jnp.int32)
    n_used = (padded_end[-1] // MOE_BLOCK).astype(jnp.int32).reshape(1)
    pad = padded - counts
    pad_end = jnp.cumsum(pad)
    j = jnp.arange(n_blocks * MOE_BLOCK - TK, dtype=jnp.int32)
    owner = j[:, None] >= pad_end[None, :]
    group = jnp.sum(owner, axis=1)
    mine = group[:, None] == experts[None, :]
    in_group = jnp.sum(jnp.where(mine, (padded_start + counts - (pad_end - pad))[None, :], 0), axis=1) + j
    pad_slots = jnp.where(group < n_experts, in_group, padded_end[-1] + j - pad_end[-1])
    return block_expert, n_used, dest.T.reshape(-1), pad_slots.astype(jnp.int32)


def _rope_tables(seq):
    pos = jnp.arange(seq, dtype=F32)
    inv = ROPE_THETA ** (-jnp.arange(0, DA_HEAD_DIM, 2, dtype=F32) / DA_HEAD_DIM)
    ang = pos[:, None] * inv[None, :]
    ang = jnp.concatenate([ang, ang], axis=-1)
    cos, sin = jnp.cos(ang), jnp.sin(ang)
    half = DA_HEAD_DIM // 2
    sin_signed = jnp.concatenate([-sin[:, :half], sin[:, half:]], axis=-1)
    return jnp.tile(cos, (1, 2)), jnp.tile(sin_signed, (1, 2))


def kernel(x, w_in, b_in, lambda_q1, lambda_k1, lambda_q2, lambda_k2, subln_g, rpb, w_branch_da, w_branch_na, w_out, ln1_g, ln1_b, w_router, b_router, w_mlp1, b_mlp1, w_mlp2, b_mlp2, ln2_g, ln2_b):
    B, S, D = x.shape
    depth = w_in.shape[0]
    n_experts = w_router.shape[2]
    T = B * S
    rows = S // GRID_W
    assert S % GRID_W == 0 and rows % NA_ROWS == 0 and rows >= NA_BAND
    alpha = (2 * depth) ** 0.25
    cos, sin = _rope_tables(S)
    row = lambda v: v.reshape(1, -1)

    for l in range(depth):
        lam_init = 0.8 - 0.6 * math.exp(-0.3 * l)
        qT, k, vT, qn, kn, vn, gates_br = _in_proj(x, w_in[l].astype(BF16), row(b_in[l]), cos, sin)
        a = _diff_attn(qT, k, vT, row(lambda_q1[l]), row(lambda_k1[l]), row(lambda_q2[l]),
                       row(lambda_k2[l]), subln_g[l].reshape(-1, 1), lam_init)
        nb = _na_attn(qn, kn, vn, _na_tiles(rpb[l]))

        wr = jnp.pad(w_router[l], ((0, 0), (0, LANES - n_experts)))
        wr_h = wr.astype(BF16)
        wr = jnp.concatenate([wr_h, (wr - wr_h.astype(F32)).astype(BF16)], axis=1)
        br = jnp.pad(row(b_router[l]), ((0, 0), (0, LANES - n_experts)), constant_values=NEG_BIG)
        x1, x1p, topi, gates = _merge(
            x.reshape(T, D), a.reshape(T, -1), nb.reshape(T, -1), gates_br.reshape(T, -1),
            w_branch_da[l].astype(BF16), w_branch_na[l].astype(BF16), w_out[l].astype(BF16),
            row(ln1_g[l]), row(ln1_b[l]), wr, br, alpha)

        block_expert, n_used, dest, pad_slots = _route(topi[:, :TOP_K], n_experts)
        xs = _sc_scatter_rows(x1p, dest, pad_slots)
        ys = _moe_ffn(xs, block_expert, n_used, w_mlp1[l], _glu_group_bias(b_mlp1[l]),
                      w_mlp2[l], b_mlp2[l][:, None, :])
        y_slots = _sc_gather_rows(ys, dest).reshape(TOP_K, T, D // 2)
        x = _combine(x1, y_slots, gates, row(ln2_g[l]), row(ln2_b[l]), alpha).reshape(B, S, D)
    return x
```

```python
import functools
import math

import jax
import jax.numpy as jnp
from jax import lax
from jax.experimental import pallas as pl
from jax.experimental.pallas import tpu as pltpu
from jax.experimental.pallas import tpu_sc as plsc

F32 = jnp.float32
BF16 = jnp.bfloat16

GRID_W = 64
DA_HEADS = 4
DA_HEAD_DIM = 64
DA_V_DIM = 2 * DA_HEAD_DIM
DA_WIDTH = DA_HEADS * DA_V_DIM
ROPE_THETA = 10000.0
NA_HEADS = 8
NA_HEAD_DIM = 64
NA_WIDTH = NA_HEADS * NA_HEAD_DIM
NA_KH = 8
NA_KW = 16
TOP_K = 4
SWIGLU_ALPHA = 1.702
SWIGLU_LIMIT = 7.0
MOE_BLOCK = 512
LN_EPS = 1e-5
RMS_EPS = 1e-5

LANES = 128
VMEM_LIMIT_BYTES = 56 * 1024 * 1024
SC_CORES = 2
SC_SUBCORES = 16
SC_GATHER_ROWS = 64

LOG2E = math.log2(math.e)
NEG_BIG = -1e30

PROJ_TM = 512
DA_TQ = 512
DA_TK = 512
DA_UNROLL = 8
NA_ROWS = 8
NA_BAND = 16
NA_PAIRS = 4


def _cparams(sem):
    return pltpu.CompilerParams(dimension_semantics=sem, vmem_limit_bytes=VMEM_LIMIT_BYTES)


def _pack_bf16_pairs(y):
    w = y.shape[1] // 2
    bits = lax.bitcast_convert_type(y, jnp.uint32)
    rounded = bits + jnp.uint32(0x7FFF) + ((bits >> 16) & jnp.uint32(1))
    return (rounded[:, :w] >> 16) | (rounded[:, w:] & jnp.uint32(0xFFFF0000))


def _unpack_bf16_pairs(words):
    lo = lax.bitcast_convert_type(words << 16, F32)
    hi = lax.bitcast_convert_type(words & jnp.uint32(0xFFFF0000), F32)
    return jnp.concatenate([lo, hi], axis=1)


def _sc_gather_rows(table, idx):
    n, w = idx.shape[0], table.shape[1]
    rows = SC_GATHER_ROWS
    workers = SC_CORES * SC_SUBCORES
    per_worker = n // workers
    n_pairs = per_worker // (2 * rows)
    assert n % (workers * 2 * rows) == 0 and n_pairs >= 1
    mesh = plsc.VectorSubcoreMesh(core_axis_name="c", subcore_axis_name="s",
                                  num_cores=SC_CORES, num_subcores=SC_SUBCORES)

    def body(table_hbm, idx_hbm, out_hbm, idx0, idx1, rows0, rows1, sem0, sem1):
        base = (lax.axis_index("s") * SC_CORES + lax.axis_index("c")) * per_worker
        bufs = ((idx0, rows0, sem0), (idx1, rows1, sem1))

        def gather(b):
            idx_v, rows_v, sem = bufs[b]
            return pltpu.make_async_copy(table_hbm.at[idx_v], rows_v, sem)

        def start(c, b):
            pltpu.sync_copy(idx_hbm.at[pl.ds(base + c * rows, rows)], bufs[b][0])
            gather(b).start()

        def finish(c, b):
            gather(b).wait()
            pltpu.sync_copy(bufs[b][1], out_hbm.at[pl.ds(base + c * rows, rows)])

        start(0, 0)

        @pl.loop(0, n_pairs - 1)
        def _(g):
            c = 2 * g
            start(c + 1, 1)
            finish(c, 0)
            start(c + 2, 0)
            finish(c + 1, 1)

        last = 2 * (n_pairs - 1)
        start(last + 1, 1)
        finish(last, 0)
        finish(last + 1, 1)

    return pl.kernel(
        body, out_type=jax.ShapeDtypeStruct((n, w), table.dtype), mesh=mesh,
        scratch_types=[pltpu.VMEM((rows,), jnp.int32), pltpu.VMEM((rows,), jnp.int32),
                       pltpu.VMEM((rows, w), table.dtype), pltpu.VMEM((rows, w), table.dtype),
                       pltpu.SemaphoreType.DMA, pltpu.SemaphoreType.DMA],
        name="sc_gather_rows",
    )(table, idx)


def _sc_scatter_rows(table, dest, rest):
    t, w = table.shape
    copies = dest.shape[0] // t
    n_out = dest.shape[0] + rest.shape[0]
    rows = SC_GATHER_ROWS
    workers = SC_CORES * SC_SUBCORES
    per_worker = t // workers
    n_pairs = per_worker // (2 * rows)
    rest_per_worker = rest.shape[0] // workers
    assert t % (workers * 2 * rows) == 0 and n_pairs >= 1 and rest.shape[0] % (workers * rows) == 0
    mesh = plsc.VectorSubcoreMesh(core_axis_name="c", subcore_axis_name="s",
                                  num_cores=SC_CORES, num_subcores=SC_SUBCORES)

    n_fill = rest_per_worker // rows

    def body(table_hbm, dest_hbm, rest_hbm, zeros_hbm, out_hbm, *scratch):
        worker = lax.axis_index("s") * SC_CORES + lax.axis_index("c")
        base = worker * per_worker
        per_buf = copies + 2
        bufs = (scratch[:per_buf], scratch[per_buf:2 * per_buf])
        zero_v, zero_sem = scratch[2 * per_buf:2 * per_buf + 2]
        fill_idx = scratch[2 * per_buf + 2:]

        def fills():
            return [pltpu.make_async_copy(zero_v, out_hbm.at[idx_v], zero_sem) for idx_v in fill_idx]

        pltpu.sync_copy(zeros_hbm, zero_v)
        for c, idx_v in enumerate(fill_idx):
            pltpu.sync_copy(rest_hbm.at[pl.ds(worker * rest_per_worker + c * rows, rows)], idx_v)
        for fill in fills():
            fill.start()

        def scatters(b):
            rows_v, sem = bufs[b][0], bufs[b][1]
            return [pltpu.make_async_copy(rows_v, out_hbm.at[idx_v], sem) for idx_v in bufs[b][2:]]

        def start(c, b):
            t0 = base + c * rows
            pltpu.sync_copy(table_hbm.at[pl.ds(t0, rows)], bufs[b][0])
            for k, idx_v in enumerate(bufs[b][2:]):
                pltpu.sync_copy(dest_hbm.at[pl.ds(k * t + t0, rows)], idx_v)
            for copy in scatters(b):
                copy.start()

        def finish(b):
            for copy in scatters(b):
                copy.wait()

        start(0, 0)

        @pl.loop(0, n_pairs - 1)
        def _(g):
            start(2 * g + 1, 1)
            finish(0)
            start(2 * g + 2, 0)
            finish(1)

        start(2 * n_pairs - 1, 1)
        finish(0)
        finish(1)

        for fill in fills():
            fill.wait()

    one_buf = ([pltpu.VMEM((rows, w), table.dtype), pltpu.SemaphoreType.DMA]
               + [pltpu.VMEM((rows,), jnp.int32) for _ in range(copies)])
    zero_buf = ([pltpu.VMEM((rows, w), table.dtype), pltpu.SemaphoreType.DMA]
                + [pltpu.VMEM((rows,), jnp.int32) for _ in range(n_fill)])
    return pl.kernel(
        body, out_type=jax.ShapeDtypeStruct((n_out, w), table.dtype), mesh=mesh,
        scratch_types=one_buf + one_buf + zero_buf, name="sc_scatter_rows",
    )(table, dest, rest, jnp.zeros((rows, w), table.dtype))


def _in_proj_kernel(x_ref, w_ref, b_ref, cos_ref, sin_ref,
                    qT_ref, k_ref, vT_ref, qn_ref, kn_ref, vn_ref, g_ref):
    xb = x_ref[...].astype(BF16)

    def seg(lo, hi):
        return jnp.dot(xb, w_ref[:, lo:hi], preferred_element_type=F32) + b_ref[:, lo:hi]

    cos = cos_ref[...]
    sin = sin_ref[...]
    lane = lax.broadcasted_iota(jnp.int32, cos.shape, 1)
    first_half = (lane % DA_HEAD_DIM) < (DA_HEAD_DIM // 2)

    def rope(y):
        outs = []
        for h in range(DA_HEADS):
            yh = y[:, h * LANES:(h + 1) * LANES]
            partner = jnp.where(first_half,
                                pltpu.roll(yh, LANES - DA_HEAD_DIM // 2, 1),
                                pltpu.roll(yh, DA_HEAD_DIM // 2, 1))
            outs.append(yh * cos + partner * sin)
        return jnp.concatenate(outs, axis=1)

    w = DA_WIDTH
    q = rope(seg(0, w)) * (DA_HEAD_DIM ** -0.5 * LOG2E)
    qT_ref[...] = q.T.astype(BF16)
    k_ref[...] = rope(seg(w, 2 * w)).astype(BF16)
    vT_ref[...] = seg(2 * w, 3 * w).T.astype(BF16)
    o = 3 * w
    qn_ref[...] = (seg(o, o + NA_WIDTH) * (NA_HEAD_DIM ** -0.5)).T.astype(BF16)
    kn_ref[...] = seg(o + NA_WIDTH, o + 2 * NA_WIDTH).astype(BF16)
    vn_ref[...] = seg(o + 2 * NA_WIDTH, o + 3 * NA_WIDTH).T.astype(BF16)
    g0 = o + 3 * NA_WIDTH
    gate_pre = seg(g0, w_ref.shape[1])
    g_ref[...] = (1.0 / (1.0 + jnp.exp(-gate_pre))).astype(BF16)


def _in_proj(x, w_in, b_in, cos, sin):
    B, S, D = x.shape
    tm = min(PROJ_TM, S)
    n_cols = w_in.shape[1]
    n_gate = n_cols - 3 * DA_WIDTH - 3 * NA_WIDTH
    tok = lambda width: pl.BlockSpec((None, tm, width), lambda b, i: (b, i, 0))
    tr = pl.BlockSpec((None, DA_WIDTH, tm), lambda b, i: (b, 0, i))
    const = lambda shape: pl.BlockSpec(shape, lambda b, i: (0,) * len(shape))
    out_shape = (
        jax.ShapeDtypeStruct((B, DA_WIDTH, S), BF16),
        jax.ShapeDtypeStruct((B, S, DA_WIDTH), BF16),
        jax.ShapeDtypeStruct((B, DA_WIDTH, S), BF16),
        jax.ShapeDtypeStruct((B, NA_WIDTH, S), BF16),
        jax.ShapeDtypeStruct((B, S, NA_WIDTH), BF16),
        jax.ShapeDtypeStruct((B, NA_WIDTH, S), BF16),
        jax.ShapeDtypeStruct((B, S, n_gate), BF16),
    )
    assert DA_WIDTH == NA_WIDTH
    return pl.pallas_call(
        _in_proj_kernel,
        grid=(B, S // tm),
        in_specs=[tok(D), const((D, n_cols)), const((1, n_cols)),
                  pl.BlockSpec((tm, LANES), lambda b, i: (i, 0)),
                  pl.BlockSpec((tm, LANES), lambda b, i: (i, 0))],
        out_specs=(tr, tok(DA_WIDTH), tr, tr, tok(NA_WIDTH), tr, tok(n_gate)),
        out_shape=out_shape,
        compiler_params=_cparams(("parallel", "parallel")),
        name="in_proj",
    )(x, w_in, b_in, cos, sin)


def _diff_attn_kernel(qT_ref, qTn_ref, k_ref, vT_ref, lq1_ref, lk1_ref, lq2_ref, lk2_ref, g_ref,
                      o_ref, m_sc, l_sc, acc_sc, s_sc, cmax_sc, *, tk, unroll, lam_init):
    tq = qT_ref.shape[1]

    def block_diag(q_ref):
        qT = q_ref[...]
        row = lax.broadcasted_iota(jnp.int32, qT.shape, 0)
        zero = jnp.zeros_like(qT)
        return jnp.concatenate([jnp.where(row < DA_HEAD_DIM, qT, zero),
                                jnp.where(row < DA_HEAD_DIM, zero, qT)], axis=1)

    qbd = block_diag(qT_ref)

    m_sc[...] = jnp.full(m_sc.shape, -jnp.inf, F32)
    l_sc[...] = jnp.zeros(l_sc.shape, F32)
    acc_sc[...] = jnp.zeros(acc_sc.shape, F32)

    nk = k_ref.shape[0] // tk

    def put_scores(slot, j, q=None):
        off = pl.multiple_of(j * tk, tk)
        s = jnp.dot(k_ref[pl.ds(off, tk), :], qbd if q is None else q, preferred_element_type=F32)
        s_sc[slot] = s
        cmax_sc[slot] = jnp.max(s, axis=0, keepdims=True)

    def accumulate(slot, j):
        off = pl.multiple_of(j * tk, tk)
        m_prev = m_sc[...]
        m_new = jnp.maximum(m_prev, cmax_sc[slot])
        alpha = jnp.exp2(m_prev - m_new)
        p = jnp.exp2(s_sc[slot] - m_new)
        l_sc[...] = alpha * l_sc[...] + jnp.sum(p, axis=0, keepdims=True)
        vb = vT_ref[:, pl.ds(off, tk)]
        acc_sc[...] = alpha * acc_sc[...] + jnp.dot(vb, p.astype(BF16), preferred_element_type=F32)
        m_sc[...] = m_new

    @pl.when(pl.program_id(2) == 0)
    def _():
        put_scores(0, 0)

    def body(jj, carry):
        for u in range(unroll):
            j = unroll * jj + u
            put_scores((u + 1) % 2, j + 1)
            accumulate(u % 2, j)
        return carry

    lax.fori_loop(0, nk // unroll - 1, body, 0)
    for j in range(nk - unroll, nk):
        if j + 1 < nk:
            put_scores((j + 1) % 2, j + 1)
        else:
            put_scores(0, 0, block_diag(qTn_ref))
        accumulate(j % 2, j)

    lam = (jnp.exp(jnp.sum(lq1_ref[...] * lk1_ref[...], axis=1, keepdims=True))
           - jnp.exp(jnp.sum(lq2_ref[...] * lk2_ref[...], axis=1, keepdims=True)) + lam_init)
    on = acc_sc[...] / l_sc[...]
    o = on[:, :tq] - lam * on[:, tq:]
    ms = jnp.mean(o * o, axis=0, keepdims=True)
    o = o * lax.rsqrt(ms + RMS_EPS) * g_ref[...]
    o = o * (1.0 - lam_init)
    o_ref[...] = o.T.astype(BF16)


def _diff_attn(qT, k, vT, lq1, lk1, lq2, lk2, subln_g, lam_init):
    B, S, _ = k.shape
    tq = min(DA_TQ, S)
    tk = min(DA_TK, S // 2)
    unroll = math.gcd(DA_UNROLL, S // tk)
    assert unroll % 2 == 0 and S % tk == 0 and S % tq == 0
    nq = S // tq
    vec = pl.BlockSpec((1, DA_HEAD_DIM), lambda b, h, i: (0, 0))
    kernel = functools.partial(_diff_attn_kernel, tk=tk, unroll=unroll, lam_init=lam_init)
    return pl.pallas_call(
        kernel,
        grid=(B, DA_HEADS, nq),
        in_specs=[pl.BlockSpec((None, DA_V_DIM, tq), lambda b, h, i: (b, h, i)),
                  pl.BlockSpec((None, DA_V_DIM, tq), lambda b, h, i: (b, h, jnp.minimum(i + 1, nq - 1))),
                  pl.BlockSpec((None, S, DA_V_DIM), lambda b, h, i: (b, 0, h)),
                  pl.BlockSpec((None, DA_V_DIM, S), lambda b, h, i: (b, h, 0)),
                  vec, vec, vec, vec,
                  pl.BlockSpec((DA_V_DIM, 1), lambda b, h, i: (0, 0))],
        out_specs=pl.BlockSpec((None, tq, DA_V_DIM), lambda b, h, i: (b, i, h)),
        out_shape=jax.ShapeDtypeStruct((B, S, DA_WIDTH), BF16),
        scratch_shapes=[pltpu.VMEM((1, 2 * tq), F32), pltpu.VMEM((1, 2 * tq), F32),
                        pltpu.VMEM((DA_V_DIM, 2 * tq), F32), pltpu.VMEM((2, tk, 2 * tq), F32),
                        pltpu.VMEM((2, 1, 2 * tq), F32)],
        compiler_params=_cparams(("parallel", "parallel", "arbitrary")),
        name="diff_attn",
    )(qT, qT, k, vT, lq1, lk1, lq2, lk2, subln_g)


def _na_band_start(r0, rows):
    return jnp.clip(r0 - NA_KH // 2, 0, rows - NA_BAND)


def _na_tiles(rpb):
    cols = jnp.arange(GRID_W)
    cs = jnp.clip(cols - NA_KW // 2, 0, GRID_W - NA_KW)
    col_ok = (cols[:, None] >= cs[None, :]) & (cols[:, None] < cs[None, :] + NA_KW)
    col_off = jnp.clip(cols[:, None] - cols[None, :] + (NA_KW - 1), 0, 2 * NA_KW - 2)
    pick = col_off[:, :, None] == jnp.arange(2 * NA_KW - 1)
    tiles = jnp.sum(jnp.where(pick[None, None], rpb.astype(F32)[:, :, None, None, :], 0.0), axis=-1)
    tiles = jnp.where(col_ok[None, None], tiles, NEG_BIG)
    neg = jnp.full((rpb.shape[0], 1, GRID_W, GRID_W), NEG_BIG, F32)
    return jnp.concatenate([jnp.concatenate([tiles, neg], axis=1),
                            jnp.concatenate([neg, tiles], axis=1)], axis=3)


def _na_geometry(variant, rows):
    nblk = rows // NA_ROWS
    r0 = (0, NA_ROWS * min(1, nblk - 1), NA_ROWS * (nblk - 1))[variant]
    band = min(max(r0 - NA_KH // 2, 0), rows - NA_BAND)
    pairs = []
    for qp in range(NA_ROWS // 2):
        entries = []
        for kr in range(NA_BAND):
            ok = []
            for qr in (r0 + 2 * qp, r0 + 2 * qp + 1):
                rs = min(max(qr - NA_KH // 2, 0), rows - NA_KH)
                ok.append(rs <= band + kr < rs + NA_KH)
            if ok[0] or ok[1]:
                entries.append((kr, band + kr - (r0 + 2 * qp) + NA_KH - 1, ok[0], ok[1]))
        pairs.append(entries)
    return pairs


def _na_kernel(qT_ref, k_ref, vT_ref, tile_ref, o_ref, *, rows):
    i = pl.program_id(2)
    nblk = rows // NA_ROWS
    nk = NA_BAND * GRID_W
    off = pl.multiple_of(_na_band_start(i * NA_ROWS, rows) * GRID_W, 2 * GRID_W)

    def block(variant):
        for pair in range(NA_PAIRS):
            block_pair(variant, pair)

    def block_pair(variant, pair):
        c0 = pair * LANES
        kb = k_ref[pl.ds(off, nk), c0:c0 + LANES]
        vT = vT_ref[c0:c0 + LANES, pl.ds(off, nk)]
        lane = lax.broadcasted_iota(jnp.int32, kb.shape, 1)
        kbd = jnp.concatenate([jnp.where(lane < NA_HEAD_DIM, kb, jnp.zeros_like(kb)),
                               jnp.where(lane < NA_HEAD_DIM, jnp.zeros_like(kb), kb)], axis=0)
        row = lax.broadcasted_iota(jnp.int32, vT.shape, 0)
        vbdT = jnp.concatenate([jnp.where(row < NA_HEAD_DIM, vT, jnp.zeros_like(vT)),
                                jnp.where(row < NA_HEAD_DIM, jnp.zeros_like(vT), vT)], axis=1)
        s = jnp.dot(kbd, qT_ref[c0:c0 + LANES, :], preferred_element_type=F32)
        left = lax.broadcasted_iota(jnp.int32, (GRID_W, LANES), 1) < GRID_W
        zero_slab = jnp.zeros((GRID_W, LANES), BF16)
        geometry = _na_geometry(variant, rows)
        p_heads, l_heads = [], []
        for h in range(2):
            p_cols, l_cols = [], []
            for qp, entries in enumerate(geometry):
                logits = {}
                for kr, d, ok_l, ok_r in entries:
                    bias = tile_ref[2 * pair + h, d]
                    if not ok_r:
                        bias = jnp.where(left, bias, NEG_BIG)
                    if not ok_l:
                        bias = jnp.where(left, NEG_BIG, bias)
                    r = h * nk + kr * GRID_W
                    logits[kr] = s[r:r + GRID_W, qp * LANES:(qp + 1) * LANES] + bias
                m = functools.reduce(jnp.maximum, logits.values())
                m = jnp.max(m, axis=0, keepdims=True)
                probs = {kr: jnp.exp(x - m) for kr, x in logits.items()}
                l_cols.append(jnp.sum(functools.reduce(jnp.add, probs.values()), axis=0, keepdims=True))
                p_cols.append(jnp.concatenate(
                    [probs[kr].astype(BF16) if kr in probs else zero_slab for kr in range(NA_BAND)], axis=0))
            p_heads.append(jnp.concatenate(p_cols, axis=1))
            l_heads.append(jnp.concatenate(l_cols, axis=1))
        oT = jnp.dot(vbdT, jnp.concatenate(p_heads, axis=0), preferred_element_type=F32)
        row_o = lax.broadcasted_iota(jnp.int32, oT.shape, 0)
        oT = oT / jnp.where(row_o < NA_HEAD_DIM, l_heads[0], l_heads[1])
        o_ref[:, c0:c0 + LANES] = oT.T.astype(BF16)

    pl.when(i == 0)(lambda: block(0))
    pl.when(jnp.logical_and(i > 0, i < nblk - 1))(lambda: block(1))
    pl.when(jnp.logical_and(i > 0, i == nblk - 1))(lambda: block(2))


def _na_attn(qnT, kn, vnT, tiles):
    B, S, _ = kn.shape
    rows = S // GRID_W
    nblk = rows // NA_ROWS
    nq = NA_ROWS * GRID_W
    width = NA_PAIRS * LANES
    return pl.pallas_call(
        functools.partial(_na_kernel, rows=rows),
        grid=(B, NA_HEADS // (2 * NA_PAIRS), nblk),
        in_specs=[pl.BlockSpec((None, width, nq), lambda b, h, i: (b, h, i)),
                  pl.BlockSpec((None, S, width), lambda b, h, i: (b, 0, h)),
                  pl.BlockSpec((None, width, S), lambda b, h, i: (b, h, 0)),
                  pl.BlockSpec((2 * NA_PAIRS,) + tiles.shape[1:], lambda b, h, i: (h, 0, 0, 0))],
        out_specs=pl.BlockSpec((None, nq, width), lambda b, h, i: (b, i, h)),
        out_shape=jax.ShapeDtypeStruct((B, S, NA_WIDTH), BF16),
        compiler_params=_cparams(("parallel", "parallel", "arbitrary")),
        name="na_attn",
    )(qnT, kn, vnT, tiles)


def _layer_norm(h, g, b):
    mu = jnp.mean(h, axis=-1, keepdims=True)
    d = h - mu
    var = jnp.mean(d * d, axis=-1, keepdims=True)
    return d * lax.rsqrt(var + LN_EPS) * g + b


def _merge_kernel(x_ref, a_ref, nb_ref, g_ref, wda_ref, wna_ref, wout_ref, lng_ref, lnb_ref,
                  wr_ref, br_ref, x1_ref, x1p_ref, topi_ref, gate_ref, *, alpha):
    d = x_ref.shape[1]
    ya = jnp.dot(a_ref[...], wda_ref[...], preferred_element_type=F32)
    yb = jnp.dot(nb_ref[...], wna_ref[...], preferred_element_type=F32)
    g = g_ref[...].astype(F32)
    merged = g[:, :d] * ya + g[:, d:] * yb
    mix = jnp.dot(merged.astype(BF16), wout_ref[...], preferred_element_type=F32)
    x1 = _layer_norm(alpha * x_ref[...] + mix, lng_ref[...], lnb_ref[...])
    x1_ref[...] = x1
    x1p_ref[...] = _pack_bf16_pairs(x1)
    tm = x1.shape[0]
    xh = x1.astype(BF16)
    xl = (x1 - xh.astype(F32)).astype(BF16)
    r = jnp.dot(jnp.concatenate([xh, xl], axis=0), wr_ref[...], preferred_element_type=F32)
    logits = (r[:tm, :LANES] + r[:tm, LANES:]) + (r[tm:, :LANES] + r[tm:, LANES:]) + br_ref[...]
    lane = lax.broadcasted_iota(jnp.int32, logits.shape, 1).astype(F32)
    vals, idxs = [], []
    for _ in range(TOP_K):
        m = jnp.max(logits, axis=1, keepdims=True)
        idx = jnp.min(jnp.where(logits == m, lane, float(LANES)), axis=1, keepdims=True)
        vals.append(m)
        idxs.append(idx)
        logits = jnp.where(lane == idx, -jnp.inf, logits)
    es = [jnp.exp(v - vals[0]) for v in vals]
    denom = es[0] + es[1] + es[2] + es[3]
    gates = jnp.zeros(logits.shape, F32)
    topi = jnp.zeros(logits.shape, F32)
    for k in range(TOP_K):
        gates = jnp.where(lane == k, es[k] / denom, gates)
        topi = jnp.where(lane == k, idxs[k], topi)
    gate_ref[...] = gates
    topi_ref[...] = topi.astype(jnp.int32)


def _merge(x2, a2, nb2, g2, wda, wna, wout, ln_g, ln_b, wr, br, alpha):
    T, D = x2.shape
    tm = min(PROJ_TM, T)
    tok = lambda width: pl.BlockSpec((tm, width), lambda i: (i, 0))
    const = lambda shape: pl.BlockSpec(shape, lambda i: (0,) * len(shape))
    return pl.pallas_call(
        functools.partial(_merge_kernel, alpha=alpha),
        grid=(T // tm,),
        in_specs=[tok(D), tok(a2.shape[1]), tok(nb2.shape[1]), tok(g2.shape[1]),
                  const(wda.shape), const(wna.shape), const(wout.shape),
                  const((1, D)), const((1, D)), const(wr.shape), const((1, LANES))],
        out_specs=(tok(D), tok(D // 2), tok(LANES), tok(LANES)),
        out_shape=(jax.ShapeDtypeStruct((T, D), F32), jax.ShapeDtypeStruct((T, D // 2), jnp.uint32),
                   jax.ShapeDtypeStruct((T, LANES), jnp.int32), jax.ShapeDtypeStruct((T, LANES), F32)),
        compiler_params=_cparams(("parallel",)),
        name="merge",
    )(x2, a2, nb2, g2, wda, wna, wout, ln_g, ln_b, wr, br)


def _glu_group_bias(b1):
    e, n = b1.shape
    return b1.reshape(e, n // (2 * LANES), LANES, 2).transpose(0, 1, 3, 2).reshape(e, 1, n)


def _moe_kernel(be_ref, nused_ref, slot_ref, next_ref, xs_ref, w1_hbm, b1_ref, w2_hbm, b2_ref, ys_ref,
                w1g_sc, w2b_sc, w1_buf, w2_buf, sem):
    i = pl.program_id(0)
    used = i < nused_ref[0]
    fresh = jnp.logical_or(i == 0, be_ref[i] != be_ref[jnp.maximum(i - 1, 0)])
    grp = 2 * LANES

    def weight_copies(e, s):
        return (pltpu.make_async_copy(w1_hbm.at[e], w1_buf.at[s], sem.at[0, s]),
                pltpu.make_async_copy(w2_hbm.at[e], w2_buf.at[s], sem.at[1, s]))

    @pl.when(i == 0)
    def _():
        for copy in weight_copies(be_ref[0], 0):
            copy.start()

    def begin_run(s):
        for copy in weight_copies(be_ref[i], s):
            copy.wait()

        @pl.when(next_ref[i] >= 0)
        def _():
            for copy in weight_copies(next_ref[i], 1 - s):
                copy.start()

        @pl.when(used)
        def _():
            r = lax.broadcasted_iota(jnp.int32, (grp, grp), 0)
            c = lax.broadcasted_iota(jnp.int32, (grp, grp), 1)
            src = jnp.where(c < LANES, 2 * c, 2 * (c - LANES) + 1)
            perm = jnp.where(r == src, 1.0, 0.0).astype(BF16)
            for g in range(w1_buf.shape[2] // grp):
                cols = w1_buf[s, :, g * grp:(g + 1) * grp].astype(BF16)
                w1g_sc[:, g * grp:(g + 1) * grp] = jnp.dot(cols, perm, preferred_element_type=F32).astype(BF16)
            w2b_sc[...] = w2_buf[s].astype(BF16)

    for s in (0, 1):
        pl.when(jnp.logical_and(fresh, slot_ref[i] == s))(functools.partial(begin_run, s))

    @pl.when(used)
    def _():
        xb = _unpack_bf16_pairs(xs_ref[...]).astype(BF16)
        h = jnp.dot(xb, w1g_sc[...], preferred_element_type=F32) + b1_ref[...]
        acts = []
        for g in range(h.shape[1] // grp):
            x_glu = jnp.minimum(h[:, g * grp:g * grp + LANES], SWIGLU_LIMIT)
            x_lin = jnp.clip(h[:, g * grp + LANES:(g + 1) * grp], -SWIGLU_LIMIT, SWIGLU_LIMIT)
            acts.append(x_glu * (1.0 / (1.0 + jnp.exp(-SWIGLU_ALPHA * x_glu))) * (x_lin + 1.0))
        act = jnp.concatenate(acts, axis=1).astype(BF16)
        y = jnp.dot(act, w2b_sc[...], preferred_element_type=F32) + b2_ref[...]
        ys_ref[...] = _pack_bf16_pairs(y)

    @pl.when(jnp.logical_not(used))
    def _():
        ys_ref[...] = jnp.zeros(ys_ref.shape, ys_ref.dtype)


def _moe_ffn(xs, block_expert, n_used, w1, b1g, w2, b2):
    P, half = xs.shape
    D = 2 * half
    de2 = w1.shape[2]
    de = w2.shape[1]
    n_blocks = P // MOE_BLOCK
    fresh = jnp.concatenate([jnp.ones((1,), bool), block_expert[1:] != block_expert[:-1]])
    run = jnp.cumsum(fresh.astype(jnp.int32)) - 1
    after = jnp.searchsorted(run, run + 1, side="left")
    next_expert = jnp.where(after < n_blocks, block_expert[jnp.minimum(after, n_blocks - 1)], -1).astype(jnp.int32)
    bspec = lambda c: pl.BlockSpec((None, 1, c), lambda i, be, nu, sl, nx: (be[i], 0, 0))
    grid_spec = pltpu.PrefetchScalarGridSpec(
        num_scalar_prefetch=4,
        grid=(n_blocks,),
        in_specs=[pl.BlockSpec((MOE_BLOCK, half), lambda i, be, nu, sl, nx: (i, 0)),
                  pl.BlockSpec(memory_space=pl.ANY), bspec(de2), pl.BlockSpec(memory_space=pl.ANY), bspec(D)],
        out_specs=pl.BlockSpec((MOE_BLOCK, half), lambda i, be, nu, sl, nx: (i, 0)),
        scratch_shapes=[pltpu.VMEM((D, de2), BF16), pltpu.VMEM((de, D), BF16),
                        pltpu.VMEM((2, D, de2), w1.dtype), pltpu.VMEM((2, de, D), w2.dtype),
                        pltpu.SemaphoreType.DMA((2, 2))],
    )
    return pl.pallas_call(
        _moe_kernel,
        grid_spec=grid_spec,
        out_shape=jax.ShapeDtypeStruct((P, half), jnp.uint32),
        compiler_params=_cparams(("arbitrary",)),
        name="moe_ffn",
    )(block_expert, n_used, (run % 2).astype(jnp.int32), next_expert, xs, w1, b1g, w2, b2)


def _combine_kernel(x1_ref, y0_ref, y1_ref, y2_ref, y3_ref, gate_ref, lng_ref, lnb_ref, o_ref, *, alpha):
    gates = gate_ref[...]
    ffn = jnp.zeros(x1_ref.shape, F32)
    for k, y_ref in enumerate((y0_ref, y1_ref, y2_ref, y3_ref)):
        ffn = ffn + _unpack_bf16_pairs(y_ref[...]) * gates[:, k:k + 1]
    o_ref[...] = _layer_norm(alpha * x1_ref[...] + ffn, lng_ref[...], lnb_ref[...])


def _combine(x1, y_slots, gates, ln_g, ln_b, alpha):
    T, D = x1.shape
    tm = min(2 * PROJ_TM, T)
    tok = lambda width: pl.BlockSpec((tm, width), lambda i: (i, 0))
    const = lambda shape: pl.BlockSpec(shape, lambda i: (0,) * len(shape))
    slot = lambda k: pl.BlockSpec((None, tm, D // 2), lambda i: (k, i, 0))
    return pl.pallas_call(
        functools.partial(_combine_kernel, alpha=alpha),
        grid=(T // tm,),
        in_specs=[tok(D)] + [slot(k) for k in range(TOP_K)] + [tok(LANES), const((1, D)), const((1, D))],
        out_specs=tok(D),
        out_shape=jax.ShapeDtypeStruct((T, D), F32),
        compiler_params=_cparams(("parallel",)),
        name="combine",
    )(x1, y_slots, y_slots, y_slots, y_slots, gates, ln_g, ln_b)


def _route(top_i, n_experts):
    T = top_i.shape[0]
    TK = T * TOP_K
    experts = jnp.arange(n_experts, dtype=jnp.int32)
    chosen = top_i[:, :, None] == experts
    picks = jnp.sum(chosen, axis=1, dtype=jnp.int32)
    csum = jnp.cumsum(picks, axis=0)
    counts = csum[-1]
    padded = ((counts + MOE_BLOCK - 1) // MOE_BLOCK) * MOE_BLOCK
    padded_end = jnp.cumsum(padded)
    padded_start = padded_end - padded
    slot_te = padded_start[None, :] + csum - picks
    dest = jnp.sum(jnp.where(chosen, slot_te[:, None, :], 0), axis=2)
    n_blocks = -(-TK // MOE_BLOCK) + n_experts
    blk_first = jnp.arange(n_blocks, dtype=jnp.int32) * MOE_BLOCK
    block_expert = jnp.minimum(jnp.sum(padded_end[None, :] <= blk_first[:, None], axis=1),
                               n_experts - 1).astype(jnp.int32)
    n_used = (padded_end[-1] // MOE_BLOCK).astype(jnp.int32).reshape(1)
    pad = padded - counts
    pad_end = jnp.cumsum(pad)
    j = jnp.arange(n_blocks * MOE_BLOCK - TK, dtype=jnp.int32)
    owner = j[:, None] >= pad_end[None, :]
    group = jnp.sum(owner, axis=1)
    mine = group[:, None] == experts[None, :]
    in_group = jnp.sum(jnp.where(mine, (padded_start + counts - (pad_end - pad))[None, :], 0), axis=1) + j
    pad_slots = jnp.where(group < n_experts, in_group, padded_end[-1] + j - pad_end[-1])
    return block_expert, n_used, dest.T.reshape(-1), pad_slots.astype(jnp.int32)


def _rope_tables(seq):
    pos = jnp.arange(seq, dtype=F32)
    inv = ROPE_THETA ** (-jnp.arange(0, DA_HEAD_DIM, 2, dtype=F32) / DA_HEAD_DIM)
    ang = pos[:, None] * inv[None, :]
    ang = jnp.concatenate([ang, ang], axis=-1)
    cos, sin = jnp.cos(ang), jnp.sin(ang)
    half = DA_HEAD_DIM // 2
    sin_signed = jnp.concatenate([-sin[:, :half], sin[:, half:]], axis=-1)
    return jnp.tile(cos, (1, 2)), jnp.tile(sin_signed, (1, 2))


def kernel(x, w_in, b_in, lambda_q1, lambda_k1, lambda_q2, lambda_k2, subln_g, rpb, w_branch_da, w_branch_na, w_out, ln1_g, ln1_b, w_router, b_router, w_mlp1, b_mlp1, w_mlp2, b_mlp2, ln2_g, ln2_b):
    B, S, D = x.shape
    depth = w_in.shape[0]
    n_experts = w_router.shape[2]
    T = B * S
    rows = S // GRID_W
    assert S % GRID_W == 0 and rows % NA_ROWS == 0 and rows >= NA_BAND
    alpha = (2 * depth) ** 0.25
    cos, sin = _rope_tables(S)
    row = lambda v: v.reshape(1, -1)

    for l in range(depth):
        lam_init = 0.8 - 0.6 * math.exp(-0.3 * l)
        qT, k, vT, qn, kn, vn, gates_br = _in_proj(x, w_in[l].astype(BF16), row(b_in[l]), cos, sin)
        a = _diff_attn(qT, k, vT, row(lambda_q1[l]), row(lambda_k1[l]), row(lambda_q2[l]),
                       row(lambda_k2[l]), subln_g[l].reshape(-1, 1), lam_init)
        nb = _na_attn(qn, kn, vn, _na_tiles(rpb[l]))

        wr = jnp.pad(w_router[l], ((0, 0), (0, LANES - n_experts)))
        wr_h = wr.astype(BF16)
        wr = jnp.concatenate([wr_h, (wr - wr_h.astype(F32)).astype(BF16)], axis=1)
        br = jnp.pad(row(b_router[l]), ((0, 0), (0, LANES - n_experts)), constant_values=NEG_BIG)
        x1, x1p, topi, gates = _merge(
            x.reshape(T, D), a.reshape(T, -1), nb.reshape(T, -1), gates_br.reshape(T, -1),
            w_branch_da[l].astype(BF16), w_branch_na[l].astype(BF16), w_out[l].astype(BF16),
            row(ln1_g[l]), row(ln1_b[l]), wr, br, alpha)

        block_expert, n_used, dest, pad_slots = _route(topi[:, :TOP_K], n_experts)
        xs = _sc_scatter_rows(x1p, dest, pad_slots)
        ys = _moe_ffn(xs, block_expert, n_used, w_mlp1[l], _glu_group_bias(b_mlp1[l]),
                      w_mlp2[l], b_mlp2[l][:, None, :])
        y_slots = _sc_gather_rows(ys, dest).reshape(TOP_K, T, D // 2)
        x = _combine(x1, y_slots, gates, row(ln2_g[l]), row(ln2_b[l]), alpha).reshape(B, S, D)
    return x
```

```python
import functools
import math

import jax
import jax.numpy as jnp
from jax import lax
from jax.experimental import pallas as pl
from jax.experimental.pallas import tpu as pltpu
from jax.experimental.pallas import tpu_sc as plsc

F32 = jnp.float32
BF16 = jnp.bfloat16

GRID_W = 64
DA_HEADS = 4
DA_HEAD_DIM = 64
DA_V_DIM = 2 * DA_HEAD_DIM
DA_WIDTH = DA_HEADS * DA_V_DIM
ROPE_THETA = 10000.0
NA_HEADS = 8
NA_HEAD_DIM = 64
NA_WIDTH = NA_HEADS * NA_HEAD_DIM
NA_KH = 8
NA_KW = 16
TOP_K = 4
SWIGLU_ALPHA = 1.702
SWIGLU_LIMIT = 7.0
MOE_BLOCK = 512
LN_EPS = 1e-5
RMS_EPS = 1e-5

LANES = 128
VMEM_LIMIT_BYTES = 56 * 1024 * 1024
SC_CORES = 2
SC_SUBCORES = 16
SC_GATHER_ROWS = 64

LOG2E = math.log2(math.e)
NEG_BIG = -1e30

PROJ_TM = 512
DA_TQ = 512
DA_TK = 512
DA_UNROLL = 8
NA_ROWS = 8
NA_BAND = 16
NA_PAIRS = 4


def _cparams(sem):
    return pltpu.CompilerParams(dimension_semantics=sem, vmem_limit_bytes=VMEM_LIMIT_BYTES)


def _pack_bf16_pairs(y):
    w = y.shape[1] // 2
    bits = lax.bitcast_convert_type(y, jnp.uint32)
    rounded = bits + jnp.uint32(0x7FFF) + ((bits >> 16) & jnp.uint32(1))
    return (rounded[:, :w] >> 16) | (rounded[:, w:] & jnp.uint32(0xFFFF0000))


def _unpack_bf16_pairs(words):
    lo = lax.bitcast_convert_type(words << 16, F32)
    hi = lax.bitcast_convert_type(words & jnp.uint32(0xFFFF0000), F32)
    return jnp.concatenate([lo, hi], axis=1)


def _sc_gather_rows(table, idx):
    n, w = idx.shape[0], table.shape[1]
    rows = SC_GATHER_ROWS
    workers = SC_CORES * SC_SUBCORES
    per_worker = n // workers
    n_pairs = per_worker // (2 * rows)
    assert n % (workers * 2 * rows) == 0 and n_pairs >= 1
    mesh = plsc.VectorSubcoreMesh(core_axis_name="c", subcore_axis_name="s",
                                  num_cores=SC_CORES, num_subcores=SC_SUBCORES)

    def body(table_hbm, idx_hbm, out_hbm, idx0, idx1, rows0, rows1, sem0, sem1):
        base = (lax.axis_index("s") * SC_CORES + lax.axis_index("c")) * per_worker
        bufs = ((idx0, rows0, sem0), (idx1, rows1, sem1))

        def gather(b):
            idx_v, rows_v, sem = bufs[b]
            return pltpu.make_async_copy(table_hbm.at[idx_v], rows_v, sem)

        def start(c, b):
            pltpu.sync_copy(idx_hbm.at[pl.ds(base + c * rows, rows)], bufs[b][0])
            gather(b).start()

        def finish(c, b):
            gather(b).wait()
            pltpu.sync_copy(bufs[b][1], out_hbm.at[pl.ds(base + c * rows, rows)])

        start(0, 0)

        @pl.loop(0, n_pairs - 1)
        def _(g):
            c = 2 * g
            start(c + 1, 1)
            finish(c, 0)
            start(c + 2, 0)
            finish(c + 1, 1)

        last = 2 * (n_pairs - 1)
        start(last + 1, 1)
        finish(last, 0)
        finish(last + 1, 1)

    return pl.kernel(
        body, out_type=jax.ShapeDtypeStruct((n, w), table.dtype), mesh=mesh,
        scratch_types=[pltpu.VMEM((rows,), jnp.int32), pltpu.VMEM((rows,), jnp.int32),
                       pltpu.VMEM((rows, w), table.dtype), pltpu.VMEM((rows, w), table.dtype),
                       pltpu.SemaphoreType.DMA, pltpu.SemaphoreType.DMA],
        name="sc_gather_rows",
    )(table, idx)


def _sc_scatter_rows(table, dest, rest):
    t, w = table.shape
    copies = dest.shape[0] // t
    n_out = dest.shape[0] + rest.shape[0]
    rows = SC_GATHER_ROWS
    workers = SC_CORES * SC_SUBCORES
    per_worker = t // workers
    n_pairs = per_worker // (2 * rows)
    rest_per_worker = rest.shape[0] // workers
    assert t % (workers * 2 * rows) == 0 and n_pairs >= 1 and rest.shape[0] % (workers * rows) == 0
    mesh = plsc.VectorSubcoreMesh(core_axis_name="c", subcore_axis_name="s",
                                  num_cores=SC_CORES, num_subcores=SC_SUBCORES)

    n_fill = rest_per_worker // rows

    def body(table_hbm, dest_hbm, rest_hbm, zeros_hbm, out_hbm, *scratch):
        worker = lax.axis_index("s") * SC_CORES + lax.axis_index("c")
        base = worker * per_worker
        per_buf = copies + 2
        bufs = (scratch[:per_buf], scratch[per_buf:2 * per_buf])
        zero_v, zero_sem = scratch[2 * per_buf:2 * per_buf + 2]
        fill_idx = scratch[2 * per_buf + 2:]

        def fills():
            return [pltpu.make_async_copy(zero_v, out_hbm.at[idx_v], zero_sem) for idx_v in fill_idx]

        pltpu.sync_copy(zeros_hbm, zero_v)
        for c, idx_v in enumerate(fill_idx):
            pltpu.sync_copy(rest_hbm.at[pl.ds(worker * rest_per_worker + c * rows, rows)], idx_v)
        for fill in fills():
            fill.start()

        def scatters(b):
            rows_v, sem = bufs[b][0], bufs[b][1]
            return [pltpu.make_async_copy(rows_v, out_hbm.at[idx_v], sem) for idx_v in bufs[b][2:]]

        def start(c, b):
            t0 = base + c * rows
            pltpu.sync_copy(table_hbm.at[pl.ds(t0, rows)], bufs[b][0])
            for k, idx_v in enumerate(bufs[b][2:]):
                pltpu.sync_copy(dest_hbm.at[pl.ds(k * t + t0, rows)], idx_v)
            for copy in scatters(b):
                copy.start()

        def finish(b):
            for copy in scatters(b):
                copy.wait()

        start(0, 0)

        @pl.loop(0, n_pairs - 1)
        def _(g):
            start(2 * g + 1, 1)
            finish(0)
            start(2 * g + 2, 0)
            finish(1)

        start(2 * n_pairs - 1, 1)
        finish(0)
        finish(1)

        for fill in fills():
            fill.wait()

    one_buf = ([pltpu.VMEM((rows, w), table.dtype), pltpu.SemaphoreType.DMA]
               + [pltpu.VMEM((rows,), jnp.int32) for _ in range(copies)])
    zero_buf = ([pltpu.VMEM((rows, w), table.dtype), pltpu.SemaphoreType.DMA]
                + [pltpu.VMEM((rows,), jnp.int32) for _ in range(n_fill)])
    return pl.kernel(
        body, out_type=jax.ShapeDtypeStruct((n_out, w), table.dtype), mesh=mesh,
        scratch_types=one_buf + one_buf + zero_buf, name="sc_scatter_rows",
    )(table, dest, rest, jnp.zeros((rows, w), table.dtype))


def _in_proj_kernel(x_ref, w_ref, b_ref, cos_ref, sin_ref,
                    qT_ref, k_ref, vT_ref, qn_ref, kn_ref, vn_ref, g_ref):
    xb = x_ref[...].astype(BF16)

    def seg(lo, hi):
        return jnp.dot(xb, w_ref[:, lo:hi], preferred_element_type=F32) + b_ref[:, lo:hi]

    cos = cos_ref[...]
    sin = sin_ref[...]
    lane = lax.broadcasted_iota(jnp.int32, cos.shape, 1)
    first_half = (lane % DA_HEAD_DIM) < (DA_HEAD_DIM // 2)

    def rope(y):
        outs = []
        for h in range(DA_HEADS):
            yh = y[:, h * LANES:(h + 1) * LANES]
            partner = jnp.where(first_half,
                                pltpu.roll(yh, LANES - DA_HEAD_DIM // 2, 1),
                                pltpu.roll(yh, DA_HEAD_DIM // 2, 1))
            outs.append(yh * cos + partner * sin)
        return jnp.concatenate(outs, axis=1)

    w = DA_WIDTH
    q = rope(seg(0, w)) * (DA_HEAD_DIM ** -0.5 * LOG2E)
    qT_ref[...] = q.T.astype(BF16)
    k_ref[...] = rope(seg(w, 2 * w)).astype(BF16)
    vT_ref[...] = seg(2 * w, 3 * w).T.astype(BF16)
    o = 3 * w
    qn_ref[...] = (seg(o, o + NA_WIDTH) * (NA_HEAD_DIM ** -0.5)).T.astype(BF16)
    kn_ref[...] = seg(o + NA_WIDTH, o + 2 * NA_WIDTH).astype(BF16)
    vn_ref[...] = seg(o + 2 * NA_WIDTH, o + 3 * NA_WIDTH).T.astype(BF16)
    g0 = o + 3 * NA_WIDTH
    gate_pre = seg(g0, w_ref.shape[1])
    g_ref[...] = (1.0 / (1.0 + jnp.exp(-gate_pre))).astype(BF16)


def _in_proj(x, w_in, b_in, cos, sin):
    B, S, D = x.shape
    tm = min(PROJ_TM, S)
    n_cols = w_in.shape[1]
    n_gate = n_cols - 3 * DA_WIDTH - 3 * NA_WIDTH
    tok = lambda width: pl.BlockSpec((None, tm, width), lambda b, i: (b, i, 0))
    tr = pl.BlockSpec((None, DA_WIDTH, tm), lambda b, i: (b, 0, i))
    const = lambda shape: pl.BlockSpec(shape, lambda b, i: (0,) * len(shape))
    out_shape = (
        jax.ShapeDtypeStruct((B, DA_WIDTH, S), BF16),
        jax.ShapeDtypeStruct((B, S, DA_WIDTH), BF16),
        jax.ShapeDtypeStruct((B, DA_WIDTH, S), BF16),
        jax.ShapeDtypeStruct((B, NA_WIDTH, S), BF16),
        jax.ShapeDtypeStruct((B, S, NA_WIDTH), BF16),
        jax.ShapeDtypeStruct((B, NA_WIDTH, S), BF16),
        jax.ShapeDtypeStruct((B, S, n_gate), BF16),
    )
    assert DA_WIDTH == NA_WIDTH
    return pl.pallas_call(
        _in_proj_kernel,
        grid=(B, S // tm),
        in_specs=[tok(D), const((D, n_cols)), const((1, n_cols)),
                  pl.BlockSpec((tm, LANES), lambda b, i: (i, 0)),
                  pl.BlockSpec((tm, LANES), lambda b, i: (i, 0))],
        out_specs=(tr, tok(DA_WIDTH), tr, tr, tok(NA_WIDTH), tr, tok(n_gate)),
        out_shape=out_shape,
        compiler_params=_cparams(("parallel", "parallel")),
        name="in_proj",
    )(x, w_in, b_in, cos, sin)


def _diff_attn_kernel(qT_ref, qTn_ref, k_ref, vT_ref, lq1_ref, lk1_ref, lq2_ref, lk2_ref, g_ref,
                      o_ref, m_sc, l_sc, acc_sc, s_sc, cmax_sc, *, tk, unroll, lam_init):
    tq = qT_ref.shape[1]

    def block_diag(q_ref):
        qT = q_ref[...]
        row = lax.broadcasted_iota(jnp.int32, qT.shape, 0)
        zero = jnp.zeros_like(qT)
        return jnp.concatenate([jnp.where(row < DA_HEAD_DIM, qT, zero),
                                jnp.where(row < DA_HEAD_DIM, zero, qT)], axis=1)

    qbd = block_diag(qT_ref)

    m_sc[...] = jnp.full(m_sc.shape, -jnp.inf, F32)
    l_sc[...] = jnp.zeros(l_sc.shape, F32)
    acc_sc[...] = jnp.zeros(acc_sc.shape, F32)

    nk = k_ref.shape[0] // tk

    def put_scores(slot, j, q=None):
        off = pl.multiple_of(j * tk, tk)
        s = jnp.dot(k_ref[pl.ds(off, tk), :], qbd if q is None else q, preferred_element_type=F32)
        s_sc[slot] = s
        cmax_sc[slot] = jnp.max(s, axis=0, keepdims=True)

    def accumulate(slot, j):
        off = pl.multiple_of(j * tk, tk)
        m_prev = m_sc[...]
        m_new = jnp.maximum(m_prev, cmax_sc[slot])
        alpha = jnp.exp2(m_prev - m_new)
        p = jnp.exp2(s_sc[slot] - m_new)
        l_sc[...] = alpha * l_sc[...] + jnp.sum(p, axis=0, keepdims=True)
        vb = vT_ref[:, pl.ds(off, tk)]
        acc_sc[...] = alpha * acc_sc[...] + jnp.dot(vb, p.astype(BF16), preferred_element_type=F32)
        m_sc[...] = m_new

    @pl.when(pl.program_id(2) == 0)
    def _():
        put_scores(0, 0)

    def body(jj, carry):
        for u in range(unroll):
            j = unroll * jj + u
            put_scores((u + 1) % 2, j + 1)
            accumulate(u % 2, j)
        return carry

    lax.fori_loop(0, nk // unroll - 1, body, 0)
    for j in range(nk - unroll, nk):
        if j + 1 < nk:
            put_scores((j + 1) % 2, j + 1)
        else:
            put_scores(0, 0, block_diag(qTn_ref))
        accumulate(j % 2, j)

    lam = (jnp.exp(jnp.sum(lq1_ref[...] * lk1_ref[...], axis=1, keepdims=True))
           - jnp.exp(jnp.sum(lq2_ref[...] * lk2_ref[...], axis=1, keepdims=True)) + lam_init)
    on = acc_sc[...] / l_sc[...]
    o = on[:, :tq] - lam * on[:, tq:]
    ms = jnp.mean(o * o, axis=0, keepdims=True)
    o = o * lax.rsqrt(ms + RMS_EPS) * g_ref[...]
    o = o * (1.0 - lam_init)
    o_ref[...] = o.T.astype(BF16)


def _diff_attn(qT, k, vT, lq1, lk1, lq2, lk2, subln_g, lam_init):
    B, S, _ = k.shape
    tq = min(DA_TQ, S)
    tk = min(DA_TK, S // 2)
    unroll = math.gcd(DA_UNROLL, S // tk)
    assert unroll % 2 == 0 and S % tk == 0 and S % tq == 0
    nq = S // tq
    vec = pl.BlockSpec((1, DA_HEAD_DIM), lambda b, h, i: (0, 0))
    kernel = functools.partial(_diff_attn_kernel, tk=tk, unroll=unroll, lam_init=lam_init)
    return pl.pallas_call(
        kernel,
        grid=(B, DA_HEADS, nq),
        in_specs=[pl.BlockSpec((None, DA_V_DIM, tq), lambda b, h, i: (b, h, i)),
                  pl.BlockSpec((None, DA_V_DIM, tq), lambda b, h, i: (b, h, jnp.minimum(i + 1, nq - 1))),
                  pl.BlockSpec((None, S, DA_V_DIM), lambda b, h, i: (b, 0, h)),
                  pl.BlockSpec((None, DA_V_DIM, S), lambda b, h, i: (b, h, 0)),
                  vec, vec, vec, vec,
                  pl.BlockSpec((DA_V_DIM, 1), lambda b, h, i: (0, 0))],
        out_specs=pl.BlockSpec((None, tq, DA_V_DIM), lambda b, h, i: (b, i, h)),
        out_shape=jax.ShapeDtypeStruct((B, S, DA_WIDTH), BF16),
        scratch_shapes=[pltpu.VMEM((1, 2 * tq), F32), pltpu.VMEM((1, 2 * tq), F32),
                        pltpu.VMEM((DA_V_DIM, 2 * tq), F32), pltpu.VMEM((2, tk, 2 * tq), F32),
                        pltpu.VMEM((2, 1, 2 * tq), F32)],
        compiler_params=_cparams(("parallel", "parallel", "arbitrary")),
        name="diff_attn",
    )(qT, qT, k, vT, lq1, lk1, lq2, lk2, subln_g)


def _na_band_start(r0, rows):
    return jnp.clip(r0 - NA_KH // 2, 0, rows - NA_BAND)


def _na_tiles(rpb):
    cols = jnp.arange(GRID_W)
    cs = jnp.clip(cols - NA_KW // 2, 0, GRID_W - NA_KW)
    col_ok = (cols[:, None] >= cs[None, :]) & (cols[:, None] < cs[None, :] + NA_KW)
    col_off = jnp.clip(cols[:, None] - cols[None, :] + (NA_KW - 1), 0, 2 * NA_KW - 2)
    pick = col_off[:, :, None] == jnp.arange(2 * NA_KW - 1)
    tiles = jnp.sum(jnp.where(pick[None, None], rpb.astype(F32)[:, :, None, None, :], 0.0), axis=-1)
    tiles = jnp.where(col_ok[None, None], tiles, NEG_BIG)
    neg = jnp.full((rpb.shape[0], 1, GRID_W, GRID_W), NEG_BIG, F32)
    return jnp.concatenate([jnp.concatenate([tiles, neg], axis=1),
                            jnp.concatenate([neg, tiles], axis=1)], axis=3)


def _na_geometry(variant, rows):
    nblk = rows // NA_ROWS
    r0 = (0, NA_ROWS * min(1, nblk - 1), NA_ROWS * (nblk - 1))[variant]
    band = min(max(r0 - NA_KH // 2, 0), rows - NA_BAND)
    pairs = []
    for qp in range(NA_ROWS // 2):
        entries = []
        for kr in range(NA_BAND):
            ok = []
            for qr in (r0 + 2 * qp, r0 + 2 * qp + 1):
                rs = min(max(qr - NA_KH // 2, 0), rows - NA_KH)
                ok.append(rs <= band + kr < rs + NA_KH)
            if ok[0] or ok[1]:
                entries.append((kr, band + kr - (r0 + 2 * qp) + NA_KH - 1, ok[0], ok[1]))
        pairs.append(entries)
    return pairs


def _na_kernel(qT_ref, k_ref, vT_ref, tile_ref, o_ref, *, rows):
    i = pl.program_id(2)
    nblk = rows // NA_ROWS
    nk = NA_BAND * GRID_W
    off = pl.multiple_of(_na_band_start(i * NA_ROWS, rows) * GRID_W, 2 * GRID_W)

    def block(variant):
        for pair in range(NA_PAIRS):
            block_pair(variant, pair)

    def block_pair(variant, pair):
        c0 = pair * LANES
        kb = k_ref[pl.ds(off, nk), c0:c0 + LANES]
        vT = vT_ref[c0:c0 + LANES, pl.ds(off, nk)]
        lane = lax.broadcasted_iota(jnp.int32, kb.shape, 1)
        kbd = jnp.concatenate([jnp.where(lane < NA_HEAD_DIM, kb, jnp.zeros_like(kb)),
                               jnp.where(lane < NA_HEAD_DIM, jnp.zeros_like(kb), kb)], axis=0)
        row = lax.broadcasted_iota(jnp.int32, vT.shape, 0)
        vbdT = jnp.concatenate([jnp.where(row < NA_HEAD_DIM, vT, jnp.zeros_like(vT)),
                                jnp.where(row < NA_HEAD_DIM, jnp.zeros_like(vT), vT)], axis=1)
        s = jnp.dot(kbd, qT_ref[c0:c0 + LANES, :], preferred_element_type=F32)
        left = lax.broadcasted_iota(jnp.int32, (GRID_W, LANES), 1) < GRID_W
        zero_slab = jnp.zeros((GRID_W, LANES), BF16)
        geometry = _na_geometry(variant, rows)
        p_heads, l_heads = [], []
        for h in range(2):
            p_cols, l_cols = [], []
            for qp, entries in enumerate(geometry):
                logits = {}
                for kr, d, ok_l, ok_r in entries:
                    bias = tile_ref[2 * pair + h, d]
                    if not ok_r:
                        bias = jnp.where(left, bias, NEG_BIG)
                    if not ok_l:
                        bias = jnp.where(left, NEG_BIG, bias)
                    r = h * nk + kr * GRID_W
                    logits[kr] = s[r:r + GRID_W, qp * LANES:(qp + 1) * LANES] + bias
                m = functools.reduce(jnp.maximum, logits.values())
                m = jnp.max(m, axis=0, keepdims=True)
                probs = {kr: jnp.exp(x - m) for kr, x in logits.items()}
                l_cols.append(jnp.sum(functools.reduce(jnp.add, probs.values()), axis=0, keepdims=True))
                p_cols.append(jnp.concatenate(
                    [probs[kr].astype(BF16) if kr in probs else zero_slab for kr in range(NA_BAND)], axis=0))
            p_heads.append(jnp.concatenate(p_cols, axis=1))
            l_heads.append(jnp.concatenate(l_cols, axis=1))
        oT = jnp.dot(vbdT, jnp.concatenate(p_heads, axis=0), preferred_element_type=F32)
        row_o = lax.broadcasted_iota(jnp.int32, oT.shape, 0)
        oT = oT / jnp.where(row_o < NA_HEAD_DIM, l_heads[0], l_heads[1])
        o_ref[:, c0:c0 + LANES] = oT.T.astype(BF16)

    pl.when(i == 0)(lambda: block(0))
    pl.when(jnp.logical_and(i > 0, i < nblk - 1))(lambda: block(1))
    pl.when(jnp.logical_and(i > 0, i == nblk - 1))(lambda: block(2))


def _na_attn(qnT, kn, vnT, tiles):
    B, S, _ = kn.shape
    rows = S // GRID_W
    nblk = rows // NA_ROWS
    nq = NA_ROWS * GRID_W
    width = NA_PAIRS * LANES
    return pl.pallas_call(
        functools.partial(_na_kernel, rows=rows),
        grid=(B, NA_HEADS // (2 * NA_PAIRS), nblk),
        in_specs=[pl.BlockSpec((None, width, nq), lambda b, h, i: (b, h, i)),
                  pl.BlockSpec((None, S, width), lambda b, h, i: (b, 0, h)),
                  pl.BlockSpec((None, width, S), lambda b, h, i: (b, h, 0)),
                  pl.BlockSpec((2 * NA_PAIRS,) + tiles.shape[1:], lambda b, h, i: (h, 0, 0, 0))],
        out_specs=pl.BlockSpec((None, nq, width), lambda b, h, i: (b, i, h)),
        out_shape=jax.ShapeDtypeStruct((B, S, NA_WIDTH), BF16),
        compiler_params=_cparams(("parallel", "parallel", "arbitrary")),
        name="na_attn",
    )(qnT, kn, vnT, tiles)


def _layer_norm(h, g, b):
    mu = jnp.mean(h, axis=-1, keepdims=True)
    d = h - mu
    var = jnp.mean(d * d, axis=-1, keepdims=True)
    return d * lax.rsqrt(var + LN_EPS) * g + b


def _merge_kernel(x_ref, a_ref, nb_ref, g_ref, wda_ref, wna_ref, wout_ref, lng_ref, lnb_ref,
                  wr_ref, br_ref, x1_ref, x1p_ref, topi_ref, gate_ref, *, alpha):
    d = x_ref.shape[1]
    ya = jnp.dot(a_ref[...], wda_ref[...], preferred_element_type=F32)
    yb = jnp.dot(nb_ref[...], wna_ref[...], preferred_element_type=F32)
    g = g_ref[...].astype(F32)
    merged = g[:, :d] * ya + g[:, d:] * yb
    mix = jnp.dot(merged.astype(BF16), wout_ref[...], preferred_element_type=F32)
    x1 = _layer_norm(alpha * x_ref[...] + mix, lng_ref[...], lnb_ref[...])
    x1_ref[...] = x1
    x1p_ref[...] = _pack_bf16_pairs(x1)
    tm = x1.shape[0]
    xh = x1.astype(BF16)
    xl = (x1 - xh.astype(F32)).astype(BF16)
    r = jnp.dot(jnp.concatenate([xh, xl], axis=0), wr_ref[...], preferred_element_type=F32)
    logits = (r[:tm, :LANES] + r[:tm, LANES:]) + (r[tm:, :LANES] + r[tm:, LANES:]) + br_ref[...]
    lane = lax.broadcasted_iota(jnp.int32, logits.shape, 1).astype(F32)
    vals, idxs = [], []
    for _ in range(TOP_K):
        m = jnp.max(logits, axis=1, keepdims=True)
        idx = jnp.min(jnp.where(logits == m, lane, float(LANES)), axis=1, keepdims=True)
        vals.append(m)
        idxs.append(idx)
        logits = jnp.where(lane == idx, -jnp.inf, logits)
    es = [jnp.exp(v - vals[0]) for v in vals]
    denom = es[0] + es[1] + es[2] + es[3]
    gates = jnp.zeros(logits.shape, F32)
    topi = jnp.zeros(logits.shape, F32)
    for k in range(TOP_K):
        gates = jnp.where(lane == k, es[k] / denom, gates)
        topi = jnp.where(lane == k, idxs[k], topi)
    gate_ref[...] = gates
    topi_ref[...] = topi.astype(jnp.int32)


def _merge(x2, a2, nb2, g2, wda, wna, wout, ln_g, ln_b, wr, br, alpha):
    T, D = x2.shape
    tm = min(PROJ_TM, T)
    tok = lambda width: pl.BlockSpec((tm, width), lambda i: (i, 0))
    const = lambda shape: pl.BlockSpec(shape, lambda i: (0,) * len(shape))
    return pl.pallas_call(
        functools.partial(_merge_kernel, alpha=alpha),
        grid=(T // tm,),
        in_specs=[tok(D), tok(a2.shape[1]), tok(nb2.shape[1]), tok(g2.shape[1]),
                  const(wda.shape), const(wna.shape), const(wout.shape),
                  const((1, D)), const((1, D)), const(wr.shape), const((1, LANES))],
        out_specs=(tok(D), tok(D // 2), tok(LANES), tok(LANES)),
        out_shape=(jax.ShapeDtypeStruct((T, D), F32), jax.ShapeDtypeStruct((T, D // 2), jnp.uint32),
                   jax.ShapeDtypeStruct((T, LANES), jnp.int32), jax.ShapeDtypeStruct((T, LANES), F32)),
        compiler_params=_cparams(("parallel",)),
        name="merge",
    )(x2, a2, nb2, g2, wda, wna, wout, ln_g, ln_b, wr, br)


def _glu_group_bias(b1):
    e, n = b1.shape
    return b1.reshape(e, n // (2 * LANES), LANES, 2).transpose(0, 1, 3, 2).reshape(e, 1, n)


def _moe_kernel(be_ref, nused_ref, slot_ref, next_ref, xs_ref, w1_hbm, b1_ref, w2_hbm, b2_ref, ys_ref,
                w1g_sc, w2b_sc, w1_buf, w2_buf, sem):
    i = pl.program_id(0)
    used = i < nused_ref[0]
    fresh = jnp.logical_or(i == 0, be_ref[i] != be_ref[jnp.maximum(i - 1, 0)])
    grp = 2 * LANES

    def weight_copies(e, s):
        return (pltpu.make_async_copy(w1_hbm.at[e], w1_buf.at[s], sem.at[0, s]),
                pltpu.make_async_copy(w2_hbm.at[e], w2_buf.at[s], sem.at[1, s]))

    @pl.when(i == 0)
    def _():
        for copy in weight_copies(be_ref[0], 0):
            copy.start()

    def begin_run(s):
        for copy in weight_copies(be_ref[i], s):
            copy.wait()

        @pl.when(next_ref[i] >= 0)
        def _():
            for copy in weight_copies(next_ref[i], 1 - s):
                copy.start()

        @pl.when(used)
        def _():
            r = lax.broadcasted_iota(jnp.int32, (grp, grp), 0)
            c = lax.broadcasted_iota(jnp.int32, (grp, grp), 1)
            src = jnp.where(c < LANES, 2 * c, 2 * (c - LANES) + 1)
            perm = jnp.where(r == src, 1.0, 0.0).astype(BF16)
            for g in range(w1_buf.shape[2] // grp):
                cols = w1_buf[s, :, g * grp:(g + 1) * grp].astype(BF16)
                w1g_sc[:, g * grp:(g + 1) * grp] = jnp.dot(cols, perm, preferred_element_type=F32).astype(BF16)
            w2b_sc[...] = w2_buf[s].astype(BF16)

    for s in (0, 1):
        pl.when(jnp.logical_and(fresh, slot_ref[i] == s))(functools.partial(begin_run, s))

    @pl.when(used)
    def _():
        xb = _unpack_bf16_pairs(xs_ref[...]).astype(BF16)
        h = jnp.dot(xb, w1g_sc[...], preferred_element_type=F32) + b1_ref[...]
        acts = []
        for g in range(h.shape[1] // grp):
            x_glu = jnp.minimum(h[:, g * grp:g * grp + LANES], SWIGLU_LIMIT)
            x_lin = jnp.clip(h[:, g * grp + LANES:(g + 1) * grp], -SWIGLU_LIMIT, SWIGLU_LIMIT)
            acts.append(x_glu * (1.0 / (1.0 + jnp.exp(-SWIGLU_ALPHA * x_glu))) * (x_lin + 1.0))
        act = jnp.concatenate(acts, axis=1).astype(BF16)
        y = jnp.dot(act, w2b_sc[...], preferred_element_type=F32) + b2_ref[...]
        ys_ref[...] = _pack_bf16_pairs(y)

    @pl.when(jnp.logical_not(used))
    def _():
        ys_ref[...] = jnp.zeros(ys_ref.shape, ys_ref.dtype)


def _moe_ffn(xs, block_expert, n_used, w1, b1g, w2, b2):
    P, half = xs.shape
    D = 2 * half
    de2 = w1.shape[2]
    de = w2.shape[1]
    n_blocks = P // MOE_BLOCK
    fresh = jnp.concatenate([jnp.ones((1,), bool), block_expert[1:] != block_expert[:-1]])
    run = jnp.cumsum(fresh.astype(jnp.int32)) - 1
    after = jnp.sum(run[None, :] <= run[:, None], axis=1)
    next_expert = jnp.where(after < n_blocks, block_expert[jnp.minimum(after, n_blocks - 1)], -1).astype(jnp.int32)
    bspec = lambda c: pl.BlockSpec((None, 1, c), lambda i, be, nu, sl, nx: (be[i], 0, 0))
    grid_spec = pltpu.PrefetchScalarGridSpec(
        num_scalar_prefetch=4,
        grid=(n_blocks,),
        in_specs=[pl.BlockSpec((MOE_BLOCK, half), lambda i, be, nu, sl, nx: (i, 0)),
                  pl.BlockSpec(memory_space=pl.ANY), bspec(de2), pl.BlockSpec(memory_space=pl.ANY), bspec(D)],
        out_specs=pl.BlockSpec((MOE_BLOCK, half), lambda i, be, nu, sl, nx: (i, 0)),
        scratch_shapes=[pltpu.VMEM((D, de2), BF16), pltpu.VMEM((de, D), BF16),
                        pltpu.VMEM((2, D, de2), w1.dtype), pltpu.VMEM((2, de, D), w2.dtype),
                        pltpu.SemaphoreType.DMA((2, 2))],
    )
    return pl.pallas_call(
        _moe_kernel,
        grid_spec=grid_spec,
        out_shape=jax.ShapeDtypeStruct((P, half), jnp.uint32),
        compiler_params=_cparams(("arbitrary",)),
        name="moe_ffn",
    )(block_expert, n_used, (run % 2).astype(jnp.int32), next_expert, xs, w1, b1g, w2, b2)


def _combine_kernel(x1_ref, y0_ref, y1_ref, y2_ref, y3_ref, gate_ref, lng_ref, lnb_ref, o_ref, *, alpha):
    gates = gate_ref[...]
    ffn = jnp.zeros(x1_ref.shape, F32)
    for k, y_ref in enumerate((y0_ref, y1_ref, y2_ref, y3_ref)):
        ffn = ffn + _unpack_bf16_pairs(y_ref[...]) * gates[:, k:k + 1]
    o_ref[...] = _layer_norm(alpha * x1_ref[...] + ffn, lng_ref[...], lnb_ref[...])


def _combine(x1, y_slots, gates, ln_g, ln_b, alpha):
    T, D = x1.shape
    tm = min(2 * PROJ_TM, T)
    tok = lambda width: pl.BlockSpec((tm, width), lambda i: (i, 0))
    const = lambda shape: pl.BlockSpec(shape, lambda i: (0,) * len(shape))
    slot = lambda k: pl.BlockSpec((None, tm, D // 2), lambda i: (k, i, 0))
    return pl.pallas_call(
        functools.partial(_combine_kernel, alpha=alpha),
        grid=(T // tm,),
        in_specs=[tok(D)] + [slot(k) for k in range(TOP_K)] + [tok(LANES), const((1, D)), const((1, D))],
        out_specs=tok(D),
        out_shape=jax.ShapeDtypeStruct((T, D), F32),
        compiler_params=_cparams(("parallel",)),
        name="combine",
    )(x1, y_slots, y_slots, y_slots, y_slots, gates, ln_g, ln_b)


def _route(top_i, n_experts):
    T = top_i.shape[0]
    TK = T * TOP_K
    experts = jnp.arange(n_experts, dtype=jnp.int32)
    chosen = top_i[:, :, None] == experts
    picks = jnp.sum(chosen, axis=1, dtype=jnp.int32)
    csum = jnp.cumsum(picks, axis=0)
    counts = csum[-1]
    padded = ((counts + MOE_BLOCK - 1) // MOE_BLOCK) * MOE_BLOCK
    padded_end = jnp.cumsum(padded)
    padded_start = padded_end - padded
    slot_te = padded_start[None, :] + csum - picks
    dest = jnp.sum(jnp.where(chosen, slot_te[:, None, :], 0), axis=2)
    n_blocks = -(-TK // MOE_BLOCK) + n_experts
    blk_first = jnp.arange(n_blocks, dtype=jnp.int32) * MOE_BLOCK
    block_expert = jnp.minimum(jnp.sum(padded_end[None, :] <= blk_first[:, None], axis=1),
                               n_experts - 1).astype(jnp.int32)
    n_used = (padded_end[-1] // MOE_BLOCK).astype(jnp.int32).reshape(1)
    pad = padded - counts
    pad_end = jnp.cumsum(pad)
    j = jnp.arange(n_blocks * MOE_BLOCK - TK, dtype=jnp.int32)
    owner = j[:, None] >= pad_end[None, :]
    group = jnp.sum(owner, axis=1)
    mine = group[:, None] == experts[None, :]
    in_group = jnp.sum(jnp.where(mine, (padded_start + counts - (pad_end - pad))[None, :], 0), axis=1) + j
    pad_slots = jnp.where(group < n_experts, in_group, padded_end[-1] + j - pad_end[-1])
    return block_expert, n_used, dest.T.reshape(-1), pad_slots.astype(jnp.int32)


def _rope_tables(seq):
    pos = jnp.arange(seq, dtype=F32)
    inv = ROPE_THETA ** (-jnp.arange(0, DA_HEAD_DIM, 2, dtype=F32) / DA_HEAD_DIM)
    ang = pos[:, None] * inv[None, :]
    ang = jnp.concatenate([ang, ang], axis=-1)
    cos, sin = jnp.cos(ang), jnp.sin(ang)
    half = DA_HEAD_DIM // 2
    sin_signed = jnp.concatenate([-sin[:, :half], sin[:, half:]], axis=-1)
    return jnp.tile(cos, (1, 2)), jnp.tile(sin_signed, (1, 2))


def kernel(x, w_in, b_in, lambda_q1, lambda_k1, lambda_q2, lambda_k2, subln_g, rpb, w_branch_da, w_branch_na, w_out, ln1_g, ln1_b, w_router, b_router, w_mlp1, b_mlp1, w_mlp2, b_mlp2, ln2_g, ln2_b):
    B, S, D = x.shape
    depth = w_in.shape[0]
    n_experts = w_router.shape[2]
    T = B * S
    rows = S // GRID_W
    assert S % GRID_W == 0 and rows % NA_ROWS == 0 and rows >= NA_BAND
    alpha = (2 * depth) ** 0.25
    cos, sin = _rope_tables(S)
    row = lambda v: v.reshape(1, -1)

    for l in range(depth):
        lam_init = 0.8 - 0.6 * math.exp(-0.3 * l)
        qT, k, vT, qn, kn, vn, gates_br = _in_proj(x, w_in[l].astype(BF16), row(b_in[l]), cos, sin)
        a = _diff_attn(qT, k, vT, row(lambda_q1[l]), row(lambda_k1[l]), row(lambda_q2[l]),
                       row(lambda_k2[l]), subln_g[l].reshape(-1, 1), lam_init)
        nb = _na_attn(qn, kn, vn, _na_tiles(rpb[l]))

        wr = jnp.pad(w_router[l], ((0, 0), (0, LANES - n_experts)))
        wr_h = wr.astype(BF16)
        wr = jnp.concatenate([wr_h, (wr - wr_h.astype(F32)).astype(BF16)], axis=1)
        br = jnp.pad(row(b_router[l]), ((0, 0), (0, LANES - n_experts)), constant_values=NEG_BIG)
        x1, x1p, topi, gates = _merge(
            x.reshape(T, D), a.reshape(T, -1), nb.reshape(T, -1), gates_br.reshape(T, -1),
            w_branch_da[l].astype(BF16), w_branch_na[l].astype(BF16), w_out[l].astype(BF16),
            row(ln1_g[l]), row(ln1_b[l]), wr, br, alpha)

        block_expert, n_used, dest, pad_slots = _route(topi[:, :TOP_K], n_experts)
        xs = _sc_scatter_rows(x1p, dest, pad_slots)
        ys = _moe_ffn(xs, block_expert, n_used, w_mlp1[l], _glu_group_bias(b_mlp1[l]),
                      w_mlp2[l], b_mlp2[l][:, None, :])
        y_slots = _sc_gather_rows(ys, dest).reshape(TOP_K, T, D // 2)
        x = _combine(x1, y_slots, gates, row(ln2_g[l]), row(ln2_b[l]), alpha).reshape(B, S, D)
    return x
```
